```python
import math
import jax, jax.numpy as jnp
from jax import lax
import numpy as np

D_MODEL = 2048
BATCH = 16
SEQ = 2048
DEPTH = 2

N_MIXERS = 2
N_LAYERS_A = (DEPTH + 1) // 2
N_LAYERS_B = DEPTH // 2
CHUNK = 128
A_WIDTH = D_MODEL
A_GROUPS = 8
A_GROUP_DIM = A_WIDTH // A_GROUPS
B_WIDTH = D_MODEL // 2
B_CH = 16
B_GROUPS = B_WIDTH // B_CH
B_STATE = 64
DT_MIN = 1e-3
DT_MAX = 1e-1
N_EXPERTS = 16
N_EXPERT_GROUPS = 4
EXPERTS_PER_GROUP = N_EXPERTS // N_EXPERT_GROUPS
TOP_K = 2
D_FF = D_MODEL // 2
ALPHA = (2 * DEPTH) ** 0.25
BETA = (8 * DEPTH) ** -0.25
LN_EPS = 1e-5

kernel_name = "hybrid_gmlp_s5_grouped_moe_deepnorm"


def layer_norm(x, g, b):
    xf = x.astype(jnp.float32)
    mu = xf.mean(-1, keepdims=True)
    var = jnp.square(xf - mu).mean(-1, keepdims=True)
    y = (xf - mu) * lax.rsqrt(var + LN_EPS)
    return (y * g.astype(jnp.float32) + b.astype(jnp.float32)).astype(x.dtype)


def chunked_gmlp(h, w_in, b_in, ln_g, ln_b, w_s, b_s, w_out):
    bsz, s_len, _ = h.shape
    z = jax.nn.gelu(h @ w_in + b_in)
    u, v = jnp.split(z, 2, axis=-1)
    v = layer_norm(v, ln_g, ln_b)
    v = v.reshape(bsz, s_len // CHUNK, CHUNK, A_GROUPS, A_GROUP_DIM)
    causal = jnp.tril(jnp.ones((CHUNK, CHUNK), dtype=bool))
    w = jnp.where(causal, w_s, 0)
    s = jnp.einsum('gts,bcsgd->bctgd', w, v) + b_s.T[None, None, :, :, None]
    s = s.reshape(bsz, s_len, A_WIDTH)
    return (u * s) @ w_out


def _ssm_combine(lhs, rhs):
    a_l, b_l = lhs
    a_r, b_r = rhs
    return a_r * a_l, a_r * b_l + b_r


def s5_ssm(h, w_in, a_re, a_im, log_dt, b_re, b_im, c_re, c_im, d_skip, w_out):
    f32 = jnp.float32
    bsz, s_len, _ = h.shape
    uf = (h @ w_in).reshape(bsz, s_len, B_GROUPS, B_CH).astype(f32)
    a = lax.complex(a_re.astype(f32), a_im.astype(f32))
    dt = jnp.exp(log_dt.astype(f32))[:, None]
    a_bar = jnp.exp(a * dt)
    b = lax.complex(b_re.astype(f32), b_im.astype(f32))
    b_bar = ((a_bar - 1.0) / a)[..., None] * b
    bu = lax.complex(jnp.einsum('gpc,bsgc->sbgp', b_bar.real, uf),
                     jnp.einsum('gpc,bsgc->sbgp', b_bar.imag, uf))
    a_seq = jnp.broadcast_to(a_bar, (s_len, 1, B_GROUPS, B_STATE))
    _, states = lax.associative_scan(_ssm_combine, (a_seq, bu), axis=0)
    y = (jnp.einsum('gcp,sbgp->bsgc', c_re.astype(f32), states.real)
         - jnp.einsum('gcp,sbgp->bsgc', c_im.astype(f32), states.imag))
    y = y + d_skip.astype(f32) * uf
    y = jax.nn.gelu(y.reshape(bsz, s_len, B_WIDTH)).astype(h.dtype)
    val, gate = jnp.split(y @ w_out, 2, axis=-1)
    return val * jax.nn.sigmoid(gate)


def grouped_moe(h, router_w, router_b, w_in, w_out):
    f32 = jnp.float32
    bsz, s_len, d = h.shape
    t = h.reshape(-1, d)
    logits = t.astype(f32) @ router_w.astype(f32) + router_b.astype(f32)
    probs = jax.nn.softmax(logits, axis=-1)
    pg = probs.reshape(-1, N_EXPERT_GROUPS, EXPERTS_PER_GROUP)
    group_score = lax.top_k(pg, TOP_K)[0].sum(-1)
    best_group = jnp.argmax(group_score, axis=-1)
    expert_group = jnp.arange(N_EXPERTS) // EXPERTS_PER_GROUP
    in_group = expert_group[None, :] == best_group[:, None]
    masked = jnp.where(in_group, probs, -1.0)
    top_p, top_i = lax.top_k(masked, TOP_K)
    gates = top_p / top_p.sum(-1, keepdims=True)
    dense_gates = jnp.sum(jax.nn.one_hot(top_i, N_EXPERTS, dtype=f32) * gates[..., None], axis=1)
    out = jnp.zeros_like(t)
    for e in range(N_EXPERTS):
        g, u = jnp.split(t @ w_in[e], 2, axis=-1)
        y_e = (jax.nn.silu(g) * u) @ w_out[e]
        out = out + (dense_gates[:, e:e + 1] * y_e).astype(t.dtype)
    return out.reshape(bsz, s_len, d)


def setup_inputs(seed: int = 0) -> dict:
    key = jax.random.key(seed)
    ks = jax.random.split(key, 32)
    f32 = jnp.float32
    nrm = lambda k, shape, scale: jax.random.normal(k, shape, f32) * scale
    x = jax.random.normal(ks[0], (BATCH, SEQ, D_MODEL), f32)
    a_w_in = nrm(ks[1], (N_LAYERS_A, D_MODEL, 2 * A_WIDTH), D_MODEL ** -0.5)
    a_b_in = nrm(ks[2], (N_LAYERS_A, 2 * A_WIDTH), 0.02)
    a_ln_g = 1.0 + nrm(ks[3], (N_LAYERS_A, A_WIDTH), 0.02)
    a_ln_b = nrm(ks[4], (N_LAYERS_A, A_WIDTH), 0.02)
    a_w_s = nrm(ks[5], (N_LAYERS_A, A_GROUPS, CHUNK, CHUNK), 0.5 * CHUNK ** -0.5)
    a_b_s = 1.0 + nrm(ks[6], (N_LAYERS_A, A_GROUPS, CHUNK), 0.02)
    a_w_out = nrm(ks[7], (N_LAYERS_A, A_WIDTH, D_MODEL), BETA * A_WIDTH ** -0.5)
    b_w_in = nrm(ks[8], (N_LAYERS_B, D_MODEL, B_WIDTH), D_MODEL ** -0.5)
    b_a_re = -0.5 + nrm(ks[9], (N_LAYERS_B, B_GROUPS, B_STATE), 0.01)
    b_a_im = (jnp.pi * jnp.arange(B_STATE, dtype=f32))[None, None, :] + nrm(ks[10], (N_LAYERS_B, B_GROUPS, B_STATE), 0.01)
    b_log_dt = jax.random.uniform(ks[11], (N_LAYERS_B, B_GROUPS), f32,
                                  minval=math.log(DT_MIN), maxval=math.log(DT_MAX))
    b_b_re = nrm(ks[12], (N_LAYERS_B, B_GROUPS, B_STATE, B_CH), (2 * B_CH) ** -0.5)
    b_b_im = nrm(ks[13], (N_LAYERS_B, B_GROUPS, B_STATE, B_CH), (2 * B_CH) ** -0.5)
    b_c_re = nrm(ks[14], (N_LAYERS_B, B_GROUPS, B_CH, B_STATE), B_STATE ** -0.5)
    b_c_im = nrm(ks[15], (N_LAYERS_B, B_GROUPS, B_CH, B_STATE), B_STATE ** -0.5)
    b_d = nrm(ks[16], (N_LAYERS_B, B_GROUPS, B_CH), 1.0)
    w_val = nrm(ks[17], (N_LAYERS_B, B_WIDTH, D_MODEL), BETA * B_WIDTH ** -0.5)
    w_gate = nrm(ks[18], (N_LAYERS_B, B_WIDTH, D_MODEL), B_WIDTH ** -0.5)
    b_w_out = jnp.concatenate([w_val, w_gate], axis=-1)
    ln1_g = 1.0 + nrm(ks[19], (DEPTH, D_MODEL), 0.02)
    ln1_b = nrm(ks[20], (DEPTH, D_MODEL), 0.02)
    ln2_g = 1.0 + nrm(ks[21], (DEPTH, D_MODEL), 0.02)
    ln2_b = nrm(ks[22], (DEPTH, D_MODEL), 0.02)
    router_w = nrm(ks[23], (D_MODEL, N_EXPERTS), D_MODEL ** -0.5)
    router_b = nrm(ks[24], (N_EXPERTS,), 0.01)
    e_w_in = nrm(ks[25], (DEPTH, N_EXPERTS, D_MODEL, 2 * D_FF), D_MODEL ** -0.5)
    e_w_out = nrm(ks[26], (DEPTH, N_EXPERTS, D_FF, D_MODEL), BETA * D_FF ** -0.5)
    return {"x": x, "a_w_in": a_w_in, "a_b_in": a_b_in, "a_ln_g": a_ln_g, "a_ln_b": a_ln_b,
            "a_w_s": a_w_s, "a_b_s": a_b_s, "a_w_out": a_w_out,
            "b_w_in": b_w_in, "b_a_re": b_a_re, "b_a_im": b_a_im, "b_log_dt": b_log_dt,
            "b_b_re": b_b_re, "b_b_im": b_b_im, "b_c_re": b_c_re, "b_c_im": b_c_im,
            "b_d": b_d, "b_w_out": b_w_out,
            "ln1_g": ln1_g, "ln1_b": ln1_b, "router_w": router_w, "router_b": router_b,
            "e_w_in": e_w_in, "e_w_out": e_w_out, "ln2_g": ln2_g, "ln2_b": ln2_b}


def reference(x, a_w_in, a_b_in, a_ln_g, a_ln_b, a_w_s, a_b_s, a_w_out,
              b_w_in, b_a_re, b_a_im, b_log_dt, b_b_re, b_b_im, b_c_re, b_c_im,
              b_d, b_w_out, ln1_g, ln1_b, router_w, router_b, e_w_in, e_w_out,
              ln2_g, ln2_b):
    h = x
    for i in range(DEPTH):
        j = i // N_MIXERS
        if i % N_MIXERS == 0:
            mix = chunked_gmlp(h, a_w_in[j], a_b_in[j], a_ln_g[j], a_ln_b[j],
                               a_w_s[j], a_b_s[j], a_w_out[j])
        else:
            mix = s5_ssm(h, b_w_in[j], b_a_re[j], b_a_im[j], b_log_dt[j],
                         b_b_re[j], b_b_im[j], b_c_re[j], b_c_im[j], b_d[j], b_w_out[j])
        h = layer_norm(ALPHA * h + mix, ln1_g[i], ln1_b[i])
        moe = grouped_moe(h, router_w, router_b, e_w_in[i], e_w_out[i])
        h = layer_norm(ALPHA * h + moe, ln2_g[i], ln2_b[i])
    return h
```

```python
import functools
import math

import jax
import jax.numpy as jnp
from jax import lax
from jax.experimental import pallas as pl
from jax.experimental.pallas import tpu as pltpu

F32 = jnp.float32
BF16 = jnp.bfloat16
I32 = jnp.int32
U32 = jnp.uint32

LN_EPS = 1e-5
N_EXPERT_GROUPS = 4
SSM_CHUNK = 16

TM_MIX = 256
TM_PROJ = 512
TM_DISPATCH = 512
TM_EXPERT = 512
TM_COMBINE = 256

V7X_VMEM_LIMIT_BYTES = 56 * 1024 * 1024


def _dot(a, b):
    return jnp.dot(a, b, preferred_element_type=F32)


def _layer_norm(x, g, b):
    mu = jnp.mean(x, axis=-1, keepdims=True)
    xc = x - mu
    var = jnp.mean(xc * xc, axis=-1, keepdims=True)
    return xc * lax.rsqrt(var + LN_EPS) * g + b


def _const_spec(shape):
    return pl.BlockSpec(shape, lambda *_: (0,) * len(shape), pipeline_mode=pl.Buffered(1))


def _route(h1, rwt_ref, rb_ref, idx_ref, gate_ref, rank_ref, cnt_ref):
    n_exp = rwt_ref.shape[0]
    per_group = n_exp // N_EXPERT_GROUPS
    tm = h1.shape[0]
    logits = lax.dot_general(rwt_ref[...], h1, (((1,), (1,)), ((), ())),
                             precision=lax.Precision.HIGHEST,
                             preferred_element_type=F32) + rb_ref[...]
    m = jnp.max(logits, axis=0, keepdims=True)
    ex = jnp.exp(logits - m)
    probs = ex / jnp.sum(ex, axis=0, keepdims=True)

    best_score = None
    best_group = None
    for g in range(N_EXPERT_GROUPS):
        t1 = t2 = None
        for e in range(g * per_group, (g + 1) * per_group):
            v = probs[e:e + 1, :]
            if t1 is None:
                t1 = v
            elif t2 is None:
                t1, t2 = jnp.maximum(t1, v), jnp.minimum(t1, v)
            else:
                t2 = jnp.maximum(t2, jnp.minimum(t1, v))
                t1 = jnp.maximum(t1, v)
        score = t1 + t2
        if best_score is None:
            best_score, best_group = score, jnp.zeros((1, tm), I32)
        else:
            better = score > best_score
            best_group = jnp.where(better, g, best_group)
            best_score = jnp.where(better, score, best_score)

    eidx = lax.broadcasted_iota(I32, (n_exp, tm), 0)
    egrp = jnp.zeros((n_exp, tm), I32)
    for g in range(1, N_EXPERT_GROUPS):
        egrp = egrp + jnp.where(eidx >= g * per_group, 1, 0)
    in_group = egrp == best_group
    masked = jnp.where(in_group, probs, -1.0)
    m1 = jnp.max(masked, axis=0, keepdims=True)
    i1 = jnp.min(jnp.where(masked == m1, eidx, n_exp), axis=0, keepdims=True)
    sel1 = eidx == i1
    masked2 = jnp.where(sel1, -2.0, masked)
    m2 = jnp.max(masked2, axis=0, keepdims=True)
    i2 = jnp.min(jnp.where(masked2 == m2, eidx, n_exp), axis=0, keepdims=True)
    sel2 = eidx == i2
    den = m1 + m2

    cnt = jnp.where(sel1 | sel2, 1.0, 0.0)
    r_i = lax.broadcasted_iota(I32, (tm, tm), 0)
    c_i = lax.broadcasted_iota(I32, (tm, tm), 1)
    strict_upper = jnp.where(r_i < c_i, 1.0, 0.0).astype(BF16)
    prefix = _dot(cnt.astype(BF16), strict_upper) + cnt_ref[:, 0:1]
    r1 = jnp.sum(jnp.where(sel1, prefix, 0.0), axis=0, keepdims=True)
    r2 = jnp.sum(jnp.where(sel2, prefix, 0.0), axis=0, keepdims=True)
    cnt_ref[...] += jnp.sum(cnt, axis=1, keepdims=True)

    idx_ref[...] = jnp.concatenate([i1, i2], axis=0)
    gate_ref[...] = jnp.concatenate([m1 / den, m2 / den], axis=0)
    rank_ref[...] = jnp.concatenate([r1, r2], axis=0).astype(I32)


def _pack_bf16_pairs(h):
    half = h.shape[1] // 2
    bits = lax.bitcast_convert_type(h.astype(BF16).astype(F32), U32)
    return (bits[:, :half] & jnp.uint32(0xFFFF0000)) | (bits[:, half:] >> 16)


def _unpack_bf16_pairs(w):
    hi = lax.bitcast_convert_type(w & jnp.uint32(0xFFFF0000), F32).astype(BF16)
    lo = lax.bitcast_convert_type(w << 16, F32).astype(BF16)
    return hi, lo


def _router_out(n_tok, tm, n_exp):
    shapes = [jax.ShapeDtypeStruct((2, n_tok), I32),
              jax.ShapeDtypeStruct((2, n_tok), F32),
              jax.ShapeDtypeStruct((2, n_tok), I32),
              jax.ShapeDtypeStruct((n_exp, 128), F32)]
    specs = [pl.BlockSpec((2, tm), lambda i: (0, i)),
             pl.BlockSpec((2, tm), lambda i: (0, i)),
             pl.BlockSpec((2, tm), lambda i: (0, i)),
             pl.BlockSpec((n_exp, 128), lambda i: (0, 0))]
    return shapes, specs


def _gmlp_kernel(x_ref, win_ref, bin_ref, lng_ref, lnb_ref, ws_ref, bst_ref, wout_ref,
                 l1g_ref, l1b_ref, rwt_ref, rb_ref,
                 h1_ref, h1p_ref, idx_ref, gate_ref, rank_ref, cnt_ref,
                 p_scr, *, alpha, chunk, groups):
    @pl.when(pl.program_id(0) == 0)
    def _():
        cnt_ref[...] = jnp.zeros_like(cnt_ref)

    x = x_ref[...]
    xb = x.astype(BF16)
    width = win_ref.shape[1] // 2
    gdim = width // groups
    tm = x.shape[0]

    zv = jax.nn.gelu(_dot(xb, win_ref[:, width:]) + bin_ref[:, width:])
    vn = _layer_norm(zv, lng_ref[...], lnb_ref[...]).astype(BF16)
    zu = jax.nn.gelu(_dot(xb, win_ref[:, :width]) + bin_ref[:, :width])

    r_i = lax.broadcasted_iota(I32, (chunk, chunk), 0)
    c_i = lax.broadcasted_iota(I32, (chunk, chunk), 1)
    causal = r_i >= c_i
    for g in range(groups):
        w = jnp.where(causal, ws_ref[g], 0.0).astype(BF16)
        bias = bst_ref[:, g:g + 1]
        for c in range(tm // chunk):
            rows = slice(c * chunk, (c + 1) * chunk)
            cols = slice(g * gdim, (g + 1) * gdim)
            s = _dot(w, vn[rows, cols]) + bias
            p_scr[rows, cols] = (zu[rows, cols] * s).astype(BF16)

    mix = _dot(p_scr[...], wout_ref[...])
    h1 = _layer_norm(alpha * x + mix, l1g_ref[...], l1b_ref[...])
    h1_ref[...] = h1
    h1p_ref[...] = _pack_bf16_pairs(h1)
    _route(h1, rwt_ref, rb_ref, idx_ref, gate_ref, rank_ref, cnt_ref)


def _gmlp_layer(h, w_in, b_in, ln_g, ln_b, w_s, b_s, w_out, l1g, l1b, rwt, rb, *, alpha):
    n_tok, d = h.shape
    groups, chunk, _ = w_s.shape
    width = w_in.shape[1] // 2
    n_exp = rwt.shape[0]
    tm = min(TM_MIX, n_tok)
    assert n_tok % tm == 0 and tm % chunk == 0
    r_shapes, r_specs = _router_out(n_tok, tm, n_exp)
    kern = functools.partial(_gmlp_kernel, alpha=alpha, chunk=chunk, groups=groups)
    return pl.pallas_call(
        kern,
        grid=(n_tok // tm,),
        in_specs=[pl.BlockSpec((tm, d), lambda i: (i, 0)),
                  _const_spec((d, 2 * width)), _const_spec((1, 2 * width)),
                  _const_spec((1, width)), _const_spec((1, width)),
                  _const_spec((groups, chunk, chunk)), _const_spec((chunk, groups)),
                  _const_spec((width, d)), _const_spec((1, d)), _const_spec((1, d)),
                  _const_spec((n_exp, d)), _const_spec((n_exp, 1))],
        out_specs=[pl.BlockSpec((tm, d), lambda i: (i, 0)),
                   pl.BlockSpec((tm, d // 2), lambda i: (i, 0))] + r_specs,
        out_shape=[jax.ShapeDtypeStruct((n_tok, d), F32),
                   jax.ShapeDtypeStruct((n_tok, d // 2), U32)] + r_shapes,
        scratch_shapes=[pltpu.VMEM((tm, width), BF16)],
        compiler_params=pltpu.CompilerParams(dimension_semantics=("arbitrary",),
                                             vmem_limit_bytes=V7X_VMEM_LIMIT_BYTES),
        name="gmlp_mixer",
    )(h, w_in.astype(BF16), b_in.reshape(1, -1), ln_g.reshape(1, -1), ln_b.reshape(1, -1),
      w_s, b_s.T, w_out.astype(BF16), l1g.reshape(1, -1), l1b.reshape(1, -1), rwt, rb)


def _proj_kernel(x_ref, w_ref, o_ref):
    o_ref[...] = _dot(x_ref[...].astype(BF16), w_ref[...]).astype(o_ref.dtype)


def _ssm_in_proj(h, w):
    n_tok, d = h.shape
    width = w.shape[1]
    tm = min(TM_PROJ, n_tok)
    assert n_tok % tm == 0
    return pl.pallas_call(
        _proj_kernel,
        grid=(n_tok // tm,),
        in_specs=[pl.BlockSpec((tm, d), lambda i: (i, 0)), _const_spec((d, width))],
        out_specs=pl.BlockSpec((tm, width), lambda i: (i, 0)),
        out_shape=jax.ShapeDtypeStruct((n_tok, width), BF16),
        compiler_params=pltpu.CompilerParams(dimension_semantics=("arbitrary",),
                                             vmem_limit_bytes=V7X_VMEM_LIMIT_BYTES),
        name="ssm_in_proj",
    )(h, w.astype(BF16))


def _ssm_kernel(u_ref, t_ref, sin_ref, sout_ref, coef_ref, y_ref, w_scr, x_scr, *, n_batch):
    u = u_ref[...]
    w_scr[...] = _dot(u, sin_ref[...])
    p2 = x_scr.shape[1]
    ar = coef_ref[0:1, :]
    ai_a = coef_ref[1:2, :]
    ai_b = coef_ref[2:3, :]
    n_chunks = u.shape[0] // n_batch

    def step(k, carry):
        x, xsw = carry
        row = pl.multiple_of(k * n_batch, n_batch)
        x_scr[pl.ds(row, n_batch), :] = x
        wk = w_scr[pl.ds(row, n_batch), :]
        return (ar * x + ai_a * xsw + wk[:, :p2], ar * xsw + ai_b * x + wk[:, p2:])

    zero = jnp.zeros((n_batch, p2), F32)
    lax.fori_loop(0, n_chunks, step, (zero, zero))
    y = _dot(u, t_ref[...]) + _dot(x_scr[...].astype(BF16), sout_ref[...])
    y_ref[...] = jax.nn.gelu(y).astype(y_ref.dtype)


def _ssm_operators(a_re, a_im, log_dt, b_re, b_im, c_re, c_im, d_skip):
    hp = lax.Precision.HIGHEST
    n_grp, n_state, n_ch = b_re.shape
    L = SSM_CHUNK
    a = lax.complex(a_re.astype(F32), a_im.astype(F32))
    lam = a * jnp.exp(log_dt.astype(F32))[:, None]
    a_bar = jnp.exp(lam)
    b_bar = ((a_bar - 1.0) / a)[..., None] * lax.complex(b_re.astype(F32), b_im.astype(F32))
    cc = lax.complex(c_re.astype(F32), c_im.astype(F32))
    steps = jnp.arange(L + 1, dtype=F32)
    apow = jnp.exp(steps[:, None, None] * lam[None])

    kern = jnp.real(jnp.einsum('gcp,lgp,gpd->glcd', cc, apow[:L], b_bar, precision=hp))
    tau = jnp.arange(L)[None, :] - jnp.arange(L)[:, None]
    toe = kern[:, jnp.clip(tau, 0, L - 1)]
    toe = jnp.where((tau >= 0)[None, :, :, None, None], toe, 0.0)
    toe = toe.transpose(0, 1, 4, 2, 3)
    eye_t = jnp.eye(L, dtype=F32)[None, :, None, :, None]
    eye_c = jnp.eye(n_ch, dtype=F32)[None, None, :, None, :]
    toe = toe + eye_t * eye_c * d_skip.astype(F32)[:, None, None, None, :]
    t_op = toe.reshape(n_grp, L * n_ch, L * n_ch)

    s_in = apow[:L][::-1].transpose(1, 0, 2)[:, :, None, :] * b_bar.transpose(0, 2, 1)[:, None, :, :]
    s_in = s_in.reshape(n_grp, L * n_ch, n_state)
    s_in = jnp.concatenate([s_in.real, s_in.imag, s_in.imag, s_in.real], axis=-1)

    s_out = cc.transpose(0, 2, 1)[:, :, None, :] * apow[1:].transpose(1, 2, 0)[:, :, :, None]
    s_out = s_out.reshape(n_grp, n_state, L * n_ch)
    s_out = jnp.concatenate([s_out.real, -s_out.imag], axis=1)

    a_l = apow[L]
    coef = jnp.stack([jnp.concatenate([a_l.real, a_l.real], -1),
                      jnp.concatenate([-a_l.imag, a_l.imag], -1),
                      jnp.concatenate([a_l.imag, -a_l.imag], -1)], axis=1)
    coef = jnp.pad(coef, ((0, 0), (0, 5), (0, 0)))
    return t_op.astype(BF16), s_in.astype(BF16), s_out.astype(BF16), coef


def _ssm_scan(u, t_op, s_in, s_out, coef, *, n_batch):
    n_grp, rows, lc = u.shape
    p2 = coef.shape[-1]
    kern = functools.partial(_ssm_kernel, n_batch=n_batch)
    return pl.pallas_call(
        kern,
        grid=(n_grp,),
        in_specs=[pl.BlockSpec((None, rows, lc), lambda g: (g, 0, 0)),
                  pl.BlockSpec((None, lc, lc), lambda g: (g, 0, 0)),
                  pl.BlockSpec((None, lc, 2 * p2), lambda g: (g, 0, 0)),
                  pl.BlockSpec((None, p2, lc), lambda g: (g, 0, 0)),
                  pl.BlockSpec((None, 8, p2), lambda g: (g, 0, 0))],
        out_specs=pl.BlockSpec((None, rows, lc), lambda g: (g, 0, 0)),
        out_shape=jax.ShapeDtypeStruct((n_grp, rows, lc), BF16),
        scratch_shapes=[pltpu.VMEM((rows, 2 * p2), F32), pltpu.VMEM((rows, p2), F32)],
        compiler_params=pltpu.CompilerParams(dimension_semantics=("arbitrary",),
                                             vmem_limit_bytes=V7X_VMEM_LIMIT_BYTES),
        name="ssm_scan",
    )(u, t_op, s_in, s_out, coef)


def _ssm_out_kernel(y_ref, h_ref, wout_ref, l1g_ref, l1b_ref, rwt_ref, rb_ref,
                    h1_ref, h1p_ref, idx_ref, gate_ref, rank_ref, cnt_ref, *, alpha):
    @pl.when(pl.program_id(0) == 0)
    def _():
        cnt_ref[...] = jnp.zeros_like(cnt_ref)

    d = h_ref.shape[1]
    val = _dot(y_ref[...], wout_ref[:, :d])
    gate = _dot(y_ref[...], wout_ref[:, d:])
    mix = val * jax.nn.sigmoid(gate)
    h1 = _layer_norm(alpha * h_ref[...] + mix, l1g_ref[...], l1b_ref[...])
    h1_ref[...] = h1
    h1p_ref[...] = _pack_bf16_pairs(h1)
    _route(h1, rwt_ref, rb_ref, idx_ref, gate_ref, rank_ref, cnt_ref)


def _ssm_layer(h, n_batch, w_in, a_re, a_im, log_dt, b_re, b_im, c_re, c_im, d_skip, w_out,
               l1g, l1b, rwt, rb, *, alpha):
    n_tok, d = h.shape
    seq = n_tok // n_batch
    n_grp, n_state, n_ch = b_re.shape
    width = n_grp * n_ch
    n_exp = rwt.shape[0]
    L = SSM_CHUNK
    assert seq % L == 0

    uf = _ssm_in_proj(h, w_in)
    u = uf.reshape(n_batch, seq // L, L, n_grp, n_ch).transpose(3, 1, 0, 2, 4)
    u = u.reshape(n_grp, (seq // L) * n_batch, L * n_ch)
    t_op, s_in, s_out, coef = _ssm_operators(a_re, a_im, log_dt, b_re, b_im, c_re, c_im, d_skip)
    y = _ssm_scan(u, t_op, s_in, s_out, coef, n_batch=n_batch)
    y = y.reshape(n_grp, seq // L, n_batch, L, n_ch).transpose(2, 1, 3, 0, 4).reshape(n_tok, width)

    tm = min(TM_MIX, n_tok)
    assert n_tok % tm == 0
    r_shapes, r_specs = _router_out(n_tok, tm, n_exp)
    kern = functools.partial(_ssm_out_kernel, alpha=alpha)
    return pl.pallas_call(
        kern,
        grid=(n_tok // tm,),
        in_specs=[pl.BlockSpec((tm, width), lambda i: (i, 0)),
                  pl.BlockSpec((tm, d), lambda i: (i, 0)),
                  _const_spec((width, 2 * d)), _const_spec((1, d)), _const_spec((1, d)),
                  _const_spec((n_exp, d)), _const_spec((n_exp, 1))],
        out_specs=[pl.BlockSpec((tm, d), lambda i: (i, 0)),
                   pl.BlockSpec((tm, d // 2), lambda i: (i, 0))] + r_specs,
        out_shape=[jax.ShapeDtypeStruct((n_tok, d), F32),
                   jax.ShapeDtypeStruct((n_tok, d // 2), U32)] + r_shapes,
        compiler_params=pltpu.CompilerParams(dimension_semantics=("arbitrary",),
                                             vmem_limit_bytes=V7X_VMEM_LIMIT_BYTES),
        name="ssm_out_mixer",
    )(y, h, w_out.astype(BF16), l1g.reshape(1, -1), l1b.reshape(1, -1), rwt, rb)


def _moe_plan(idx, rank, cnt, tme, n_tiles):
    n_exp = cnt.shape[0]
    counts = cnt[:, 0].astype(I32)
    padded = ((counts + tme - 1) // tme) * tme
    ends = jnp.cumsum(padded)
    offs = ends - padded
    n_valid = ends[-1] // tme
    tile_start = jnp.arange(n_tiles, dtype=I32) * tme
    tile_expert = jnp.minimum(jnp.sum(tile_start[:, None] >= ends[None, :], axis=1), n_exp - 1).astype(I32)
    onehot = idx[:, :, None] == jnp.arange(n_exp, dtype=I32)[None, None, :]
    pos = jnp.sum(jnp.where(onehot, offs[None, None, :], 0), axis=-1) + rank
    meta = jnp.concatenate([counts, padded, offs, n_valid[None]]).astype(I32)
    return pos, tile_expert, meta


def _tile_pos(pos, tm):
    n_tok = pos.shape[1]
    return pos.reshape(2, n_tok // tm, tm).transpose(1, 0, 2).reshape(n_tok // tm, 1, 2 * tm)


def _dispatch_kernel(meta_ref, pos_ref, x_ref, xs_ref, zero_scr, sem, *, n_exp, tme, n_tiles):
    i = pl.program_id(0)
    tm = x_ref.shape[0]
    half = zero_scr.shape[0]

    def zero_fill(start, size):
        cp = pltpu.make_async_copy(zero_scr.at[pl.ds(0, size), :], xs_ref.at[pl.ds(start, size), :], sem)
        cp.start()
        cp.wait()

    @pl.when(i == 0)
    def _():
        zero_scr[...] = jnp.zeros_like(zero_scr)
        for e in range(n_exp):
            pad = meta_ref[n_exp + e] - meta_ref[e]
            base = meta_ref[2 * n_exp + e] + meta_ref[e]
            head = jnp.minimum((-base) & 7, pad)

            def head_row(r, c):
                zero_fill(base + r, 1)
                return c

            lax.fori_loop(0, head, head_row, 0)
            rest = pad - head
            for b in range(3, int(math.log2(tme))):
                size = 1 << b

                @pl.when(((rest >> b) & 1) == 1)
                def _():
                    zero_fill(pl.multiple_of(base + head + (rest & (size - 1)), 8), size)
        n_valid = meta_ref[3 * n_exp]
        for j in range(n_exp):
            @pl.when(n_valid + j < n_tiles)
            def _():
                for part in range(tme // half):
                    zero_fill(pl.multiple_of((n_valid + j) * tme + part * half, 8), half)

    def row_copy(r, slot):
        p = pos_ref[0, 0, slot * tm + r]
        return pltpu.make_async_copy(x_ref.at[pl.ds(r, 1), :], xs_ref.at[pl.ds(p, 1), :], sem)

    def start(r, c):
        row_copy(r, 0).start()
        row_copy(r, 1).start()
        return c

    def wait(r, c):
        row_copy(r, 0).wait()
        row_copy(r, 1).wait()
        return c

    lax.fori_loop(0, tm, start, 0)
    lax.fori_loop(0, tm, wait, 0)


def _moe_dispatch(h1p, pos, meta, *, n_exp, tme, n_tiles):
    n_tok, dh = h1p.shape
    tm = min(TM_DISPATCH, n_tok)
    assert n_tok % tm == 0
    half = max(tme // 2, 1)
    kern = functools.partial(_dispatch_kernel, n_exp=n_exp, tme=tme, n_tiles=n_tiles)
    return pl.pallas_call(
        kern,
        grid_spec=pltpu.PrefetchScalarGridSpec(
            num_scalar_prefetch=1,
            grid=(n_tok // tm,),
            in_specs=[pl.BlockSpec((1, 1, 2 * tm), lambda i, m: (i, 0, 0), memory_space=pltpu.SMEM),
                      pl.BlockSpec((tm, dh), lambda i, m: (i, 0))],
            out_specs=pl.BlockSpec(memory_space=pl.ANY),
            scratch_shapes=[pltpu.VMEM((half, dh), U32), pltpu.SemaphoreType.DMA]),
        out_shape=jax.ShapeDtypeStruct((n_tiles * tme, dh), U32),
        compiler_params=pltpu.CompilerParams(dimension_semantics=("arbitrary",),
                                             vmem_limit_bytes=V7X_VMEM_LIMIT_BYTES),
        name="moe_dispatch",
    )(meta, _tile_pos(pos, tm), h1p)


def _expert_kernel(te_ref, meta_ref, xs_ref, win_ref, wout_ref, ys_ref, *, n_exp):
    i = pl.program_id(0)
    n_valid = meta_ref[3 * n_exp]

    @pl.when(i < n_valid)
    def _():
        hi, lo = _unpack_bf16_pairs(xs_ref[...])
        dh = hi.shape[1]
        ff = wout_ref.shape[0]
        h = _dot(hi, win_ref[:dh, :]) + _dot(lo, win_ref[dh:, :])
        act = (jax.nn.silu(h[:, :ff]) * h[:, ff:]).astype(BF16)
        ys_ref[...] = _dot(act, wout_ref[...])

    @pl.when(i >= n_valid)
    def _():
        ys_ref[...] = jnp.zeros_like(ys_ref)


def _moe_experts(xs, tile_expert, meta, w_in, w_out, *, tme):
    n_rows, dh = xs.shape
    n_exp, d, ff2 = w_in.shape
    ff = w_out.shape[1]
    n_tiles = n_rows // tme

    def x_map(i, te, m):
        return (jnp.minimum(i, m[3 * n_exp] - 1), 0)

    kern = functools.partial(_expert_kernel, n_exp=n_exp)
    return pl.pallas_call(
        kern,
        grid_spec=pltpu.PrefetchScalarGridSpec(
            num_scalar_prefetch=2,
            grid=(n_tiles,),
            in_specs=[pl.BlockSpec((tme, dh), x_map),
                      pl.BlockSpec((None, d, ff2), lambda i, te, m: (te[i], 0, 0)),
                      pl.BlockSpec((None, ff, d), lambda i, te, m: (te[i], 0, 0))],
            out_specs=pl.BlockSpec((tme, d), lambda i, te, m: (i, 0))),
        out_shape=jax.ShapeDtypeStruct((n_rows, d), F32),
        compiler_params=pltpu.CompilerParams(dimension_semantics=("arbitrary",),
                                             vmem_limit_bytes=V7X_VMEM_LIMIT_BYTES),
        name="moe_experts",
    )(tile_expert, meta, xs, w_in, w_out)


def _combine_kernel(pos_ref, h1_ref, gate_ref, ys_ref, g_ref, b_ref, out_ref, buf, sem, *, alpha):
    tm = h1_ref.shape[0]

    def row_copy(r, slot):
        p = pos_ref[0, 0, slot * tm + r]
        return pltpu.make_async_copy(ys_ref.at[pl.ds(p, 1), :], buf.at[slot, pl.ds(r, 1), :], sem)

    def start(r, c):
        row_copy(r, 0).start()
        row_copy(r, 1).start()
        return c

    def wait(r, c):
        row_copy(r, 0).wait()
        row_copy(r, 1).wait()
        return c

    lax.fori_loop(0, tm, start, 0)
    lax.fori_loop(0, tm, wait, 0)
    moe = gate_ref[:, 0:1] * buf[0] + gate_ref[:, 1:2] * buf[1]
    out_ref[...] = _layer_norm(alpha * h1_ref[...] + moe, g_ref[...], b_ref[...])


def _moe_combine(h1, ys, pos, gates, l2g, l2b, *, alpha):
    n_tok, d = h1.shape
    tm = min(TM_COMBINE, n_tok)
    assert n_tok % tm == 0
    kern = functools.partial(_combine_kernel, alpha=alpha)
    return pl.pallas_call(
        kern,
        grid=(n_tok // tm,),
        in_specs=[pl.BlockSpec((1, 1, 2 * tm), lambda i: (i, 0, 0), memory_space=pltpu.SMEM),
                  pl.BlockSpec((tm, d), lambda i: (i, 0)),
                  pl.BlockSpec((tm, 2), lambda i: (i, 0)),
                  pl.BlockSpec(memory_space=pl.ANY),
                  _const_spec((1, d)), _const_spec((1, d))],
        out_specs=pl.BlockSpec((tm, d), lambda i: (i, 0)),
        out_shape=jax.ShapeDtypeStruct((n_tok, d), F32),
        scratch_shapes=[pltpu.VMEM((2, tm, d), F32), pltpu.SemaphoreType.DMA],
        compiler_params=pltpu.CompilerParams(dimension_semantics=("arbitrary",),
                                             vmem_limit_bytes=V7X_VMEM_LIMIT_BYTES),
        name="moe_combine",
    )(_tile_pos(pos, tm), h1, gates.T, ys, l2g.reshape(1, -1), l2b.reshape(1, -1))


def _moe_layer(h1, h1p, idx, gates, rank, cnt, w_in, w_out, l2g, l2b, *, alpha):
    n_tok = h1.shape[0]
    n_exp = w_in.shape[0]
    tme = min(TM_EXPERT, n_tok)
    n_tiles = (2 * n_tok) // tme + n_exp
    pos, tile_expert, meta = _moe_plan(idx, rank, cnt, tme, n_tiles)
    xs = _moe_dispatch(h1p, pos, meta, n_exp=n_exp, tme=tme, n_tiles=n_tiles)
    ys = _moe_experts(xs, tile_expert, meta, w_in, w_out, tme=tme)
    return _moe_combine(h1, ys, pos, gates, l2g, l2b, alpha=alpha)


def kernel(x, a_w_in, a_b_in, a_ln_g, a_ln_b, a_w_s, a_b_s, a_w_out, b_w_in, b_a_re, b_a_im, b_log_dt, b_b_re, b_b_im, b_c_re, b_c_im, b_d, b_w_out, ln1_g, ln1_b, router_w, router_b, e_w_in, e_w_out, ln2_g, ln2_b):
    n_batch, seq, d = x.shape
    depth = ln1_g.shape[0]
    alpha = float((2 * depth) ** 0.25)
    rwt = router_w.astype(F32).T
    rb = router_b.astype(F32).reshape(-1, 1)
    h = x.reshape(n_batch * seq, d)
    for i in range(depth):
        j = i // 2
        if i % 2 == 0:
            mixed = _gmlp_layer(h, a_w_in[j], a_b_in[j], a_ln_g[j], a_ln_b[j], a_w_s[j], a_b_s[j],
                                a_w_out[j], ln1_g[i], ln1_b[i], rwt, rb, alpha=alpha)
        else:
            mixed = _ssm_layer(h, n_batch, b_w_in[j], b_a_re[j], b_a_im[j], b_log_dt[j], b_b_re[j],
                               b_b_im[j], b_c_re[j], b_c_im[j], b_d[j], b_w_out[j],
                               ln1_g[i], ln1_b[i], rwt, rb, alpha=alpha)
        h1, h1p, idx, gates, rank, cnt = mixed
        h = _moe_layer(h1, h1p, idx, gates, rank, cnt, e_w_in[i].astype(BF16), e_w_out[i].astype(BF16),
                       ln2_g[i], ln2_b[i], alpha=alpha)
    return h.reshape(n_batch, seq, d)
```

```python
import functools
import math

import jax
import jax.numpy as jnp
from jax import lax
from jax.experimental import pallas as pl
from jax.experimental.pallas import tpu as pltpu

F32 = jnp.float32
BF16 = jnp.bfloat16
I32 = jnp.int32
U32 = jnp.uint32

LN_EPS = 1e-5
N_EXPERT_GROUPS = 4
SSM_CHUNK = 16

TM_MIX = 256
TM_PROJ = 512
TM_DISPATCH = 512
TM_EXPERT = 512
TM_COMBINE = 256

V7X_VMEM_LIMIT_BYTES = 56 * 1024 * 1024


def _dot(a, b):
    return jnp.dot(a, b, preferred_element_type=F32)


def _layer_norm(x, g, b):
    mu = jnp.mean(x, axis=-1, keepdims=True)
    xc = x - mu
    var = jnp.mean(xc * xc, axis=-1, keepdims=True)
    return xc * lax.rsqrt(var + LN_EPS) * g + b


def _const_spec(shape):
    return pl.BlockSpec(shape, lambda *_: (0,) * len(shape), pipeline_mode=pl.Buffered(1))


def _route(h1, rwt_ref, rb_ref, idx_ref, gate_ref, rank_ref, cnt_ref):
    n_exp = rwt_ref.shape[0]
    per_group = n_exp // N_EXPERT_GROUPS
    tm = h1.shape[0]
    logits = lax.dot_general(rwt_ref[...], h1, (((1,), (1,)), ((), ())),
                             precision=lax.Precision.HIGHEST,
                             preferred_element_type=F32) + rb_ref[...]
    m = jnp.max(logits, axis=0, keepdims=True)
    ex = jnp.exp(logits - m)
    probs = ex / jnp.sum(ex, axis=0, keepdims=True)

    best_score = None
    best_group = None
    for g in range(N_EXPERT_GROUPS):
        t1 = t2 = None
        for e in range(g * per_group, (g + 1) * per_group):
            v = probs[e:e + 1, :]
            if t1 is None:
                t1 = v
            elif t2 is None:
                t1, t2 = jnp.maximum(t1, v), jnp.minimum(t1, v)
            else:
                t2 = jnp.maximum(t2, jnp.minimum(t1, v))
                t1 = jnp.maximum(t1, v)
        score = t1 + t2
        if best_score is None:
            best_score, best_group = score, jnp.zeros((1, tm), I32)
        else:
            better = score > best_score
            best_group = jnp.where(better, g, best_group)
            best_score = jnp.where(better, score, best_score)

    eidx = lax.broadcasted_iota(I32, (n_exp, tm), 0)
    egrp = jnp.zeros((n_exp, tm), I32)
    for g in range(1, N_EXPERT_GROUPS):
        egrp = egrp + jnp.where(eidx >= g * per_group, 1, 0)
    in_group = egrp == best_group
    masked = jnp.where(in_group, probs, -1.0)
    m1 = jnp.max(masked, axis=0, keepdims=True)
    i1 = jnp.min(jnp.where(masked == m1, eidx, n_exp), axis=0, keepdims=True)
    sel1 = eidx == i1
    masked2 = jnp.where(sel1, -2.0, masked)
    m2 = jnp.max(masked2, axis=0, keepdims=True)
    i2 = jnp.min(jnp.where(masked2 == m2, eidx, n_exp), axis=0, keepdims=True)
    sel2 = eidx == i2
    den = m1 + m2

    cnt = jnp.where(sel1 | sel2, 1.0, 0.0)
    r_i = lax.broadcasted_iota(I32, (tm, tm), 0)
    c_i = lax.broadcasted_iota(I32, (tm, tm), 1)
    strict_upper = jnp.where(r_i < c_i, 1.0, 0.0).astype(BF16)
    prefix = _dot(cnt.astype(BF16), strict_upper) + cnt_ref[:, 0:1]
    r1 = jnp.sum(jnp.where(sel1, prefix, 0.0), axis=0, keepdims=True)
    r2 = jnp.sum(jnp.where(sel2, prefix, 0.0), axis=0, keepdims=True)
    cnt_ref[...] += jnp.sum(cnt, axis=1, keepdims=True)

    idx_ref[...] = jnp.concatenate([i1, i2], axis=0)
    gate_ref[...] = jnp.concatenate([m1 / den, m2 / den], axis=0)
    rank_ref[...] = jnp.concatenate([r1, r2], axis=0).astype(I32)


RECORD_ROWS = 8


def _store_records(ref, h):
    tm, d = h.shape
    half = d // 2
    assert half == RECORD_ROWS * 128
    bits = lax.bitcast_convert_type(h.astype(BF16).astype(F32), U32)
    packed = (bits[:, :half] & jnp.uint32(0xFFFF0000)) | (bits[:, half:] >> 16)
    for s in range(RECORD_ROWS):
        ref[pl.ds(s, tm, stride=RECORD_ROWS), :] = packed[:, s * 128:(s + 1) * 128]


def _load_records(ref, tm):
    w = jnp.concatenate([ref[pl.ds(s, tm, stride=RECORD_ROWS), :] for s in range(RECORD_ROWS)], axis=1)
    hi = lax.bitcast_convert_type(w & jnp.uint32(0xFFFF0000), F32).astype(BF16)
    lo = lax.bitcast_convert_type(w << 16, F32).astype(BF16)
    return hi, lo


def _router_out(n_tok, tm, n_exp):
    shapes = [jax.ShapeDtypeStruct((2, n_tok), I32),
              jax.ShapeDtypeStruct((2, n_tok), F32),
              jax.ShapeDtypeStruct((2, n_tok), I32),
              jax.ShapeDtypeStruct((n_exp, 128), F32)]
    specs = [pl.BlockSpec((2, tm), lambda i: (0, i)),
             pl.BlockSpec((2, tm), lambda i: (0, i)),
             pl.BlockSpec((2, tm), lambda i: (0, i)),
             pl.BlockSpec((n_exp, 128), lambda i: (0, 0))]
    return shapes, specs


def _gmlp_kernel(x_ref, win_ref, bin_ref, lng_ref, lnb_ref, ws_ref, bst_ref, wout_ref,
                 l1g_ref, l1b_ref, rwt_ref, rb_ref,
                 h1_ref, h1p_ref, idx_ref, gate_ref, rank_ref, cnt_ref,
                 p_scr, *, alpha, chunk, groups):
    @pl.when(pl.program_id(0) == 0)
    def _():
        cnt_ref[...] = jnp.zeros_like(cnt_ref)

    x = x_ref[...]
    xb = x.astype(BF16)
    width = win_ref.shape[1] // 2
    gdim = width // groups
    tm = x.shape[0]

    zv = jax.nn.gelu(_dot(xb, win_ref[:, width:]) + bin_ref[:, width:])
    vn = _layer_norm(zv, lng_ref[...], lnb_ref[...]).astype(BF16)
    zu = jax.nn.gelu(_dot(xb, win_ref[:, :width]) + bin_ref[:, :width])

    r_i = lax.broadcasted_iota(I32, (chunk, chunk), 0)
    c_i = lax.broadcasted_iota(I32, (chunk, chunk), 1)
    causal = r_i >= c_i
    for g in range(groups):
        w = jnp.where(causal, ws_ref[g], 0.0).astype(BF16)
        bias = bst_ref[:, g:g + 1]
        for c in range(tm // chunk):
            rows = slice(c * chunk, (c + 1) * chunk)
            cols = slice(g * gdim, (g + 1) * gdim)
            s = _dot(w, vn[rows, cols]) + bias
            p_scr[rows, cols] = (zu[rows, cols] * s).astype(BF16)

    mix = _dot(p_scr[...], wout_ref[...])
    h1 = _layer_norm(alpha * x + mix, l1g_ref[...], l1b_ref[...])
    h1_ref[...] = h1
    _store_records(h1p_ref, h1)
    _route(h1, rwt_ref, rb_ref, idx_ref, gate_ref, rank_ref, cnt_ref)


def _gmlp_layer(h, w_in, b_in, ln_g, ln_b, w_s, b_s, w_out, l1g, l1b, rwt, rb, *, alpha):
    n_tok, d = h.shape
    groups, chunk, _ = w_s.shape
    width = w_in.shape[1] // 2
    n_exp = rwt.shape[0]
    tm = min(TM_MIX, n_tok)
    assert n_tok % tm == 0 and tm % chunk == 0
    r_shapes, r_specs = _router_out(n_tok, tm, n_exp)
    kern = functools.partial(_gmlp_kernel, alpha=alpha, chunk=chunk, groups=groups)
    return pl.pallas_call(
        kern,
        grid=(n_tok // tm,),
        in_specs=[pl.BlockSpec((tm, d), lambda i: (i, 0)),
                  _const_spec((d, 2 * width)), _const_spec((1, 2 * width)),
                  _const_spec((1, width)), _const_spec((1, width)),
                  _const_spec((groups, chunk, chunk)), _const_spec((chunk, groups)),
                  _const_spec((width, d)), _const_spec((1, d)), _const_spec((1, d)),
                  _const_spec((n_exp, d)), _const_spec((n_exp, 1))],
        out_specs=[pl.BlockSpec((tm, d), lambda i: (i, 0)),
                   pl.BlockSpec((tm * RECORD_ROWS, 128), lambda i: (i, 0))] + r_specs,
        out_shape=[jax.ShapeDtypeStruct((n_tok, d), F32),
                   jax.ShapeDtypeStruct((n_tok * RECORD_ROWS, 128), U32)] + r_shapes,
        scratch_shapes=[pltpu.VMEM((tm, width), BF16)],
        compiler_params=pltpu.CompilerParams(dimension_semantics=("arbitrary",),
                                             vmem_limit_bytes=V7X_VMEM_LIMIT_BYTES),
        name="gmlp_mixer",
    )(h, w_in.astype(BF16), b_in.reshape(1, -1), ln_g.reshape(1, -1), ln_b.reshape(1, -1),
      w_s, b_s.T, w_out.astype(BF16), l1g.reshape(1, -1), l1b.reshape(1, -1), rwt, rb)


def _proj_kernel(x_ref, w_ref, o_ref):
    o_ref[...] = _dot(x_ref[...].astype(BF16), w_ref[...]).astype(o_ref.dtype)


def _ssm_in_proj(h, w):
    n_tok, d = h.shape
    width = w.shape[1]
    tm = min(TM_PROJ, n_tok)
    assert n_tok % tm == 0
    return pl.pallas_call(
        _proj_kernel,
        grid=(n_tok // tm,),
        in_specs=[pl.BlockSpec((tm, d), lambda i: (i, 0)), _const_spec((d, width))],
        out_specs=pl.BlockSpec((tm, width), lambda i: (i, 0)),
        out_shape=jax.ShapeDtypeStruct((n_tok, width), BF16),
        compiler_params=pltpu.CompilerParams(dimension_semantics=("arbitrary",),
                                             vmem_limit_bytes=V7X_VMEM_LIMIT_BYTES),
        name="ssm_in_proj",
    )(h, w.astype(BF16))


def _ssm_kernel(u_ref, t_ref, sin_ref, sout_ref, coef_ref, y_ref, w_scr, x_scr, *, n_batch):
    u = u_ref[...]
    w_scr[...] = _dot(u, sin_ref[...])
    p2 = x_scr.shape[1]
    ar = coef_ref[0:1, :]
    ai_a = coef_ref[1:2, :]
    ai_b = coef_ref[2:3, :]
    n_chunks = u.shape[0] // n_batch

    def step(k, carry):
        x, xsw = carry
        row = pl.multiple_of(k * n_batch, n_batch)
        x_scr[pl.ds(row, n_batch), :] = x
        wk = w_scr[pl.ds(row, n_batch), :]
        return (ar * x + ai_a * xsw + wk[:, :p2], ar * xsw + ai_b * x + wk[:, p2:])

    zero = jnp.zeros((n_batch, p2), F32)
    lax.fori_loop(0, n_chunks, step, (zero, zero))
    y = _dot(u, t_ref[...]) + _dot(x_scr[...].astype(BF16), sout_ref[...])
    y_ref[...] = jax.nn.gelu(y).astype(y_ref.dtype)


def _ssm_operators(a_re, a_im, log_dt, b_re, b_im, c_re, c_im, d_skip):
    hp = lax.Precision.HIGHEST
    n_grp, n_state, n_ch = b_re.shape
    L = SSM_CHUNK
    a = lax.complex(a_re.astype(F32), a_im.astype(F32))
    lam = a * jnp.exp(log_dt.astype(F32))[:, None]
    a_bar = jnp.exp(lam)
    b_bar = ((a_bar - 1.0) / a)[..., None] * lax.complex(b_re.astype(F32), b_im.astype(F32))
    cc = lax.complex(c_re.astype(F32), c_im.astype(F32))
    steps = jnp.arange(L + 1, dtype=F32)
    apow = jnp.exp(steps[:, None, None] * lam[None])

    kern = jnp.real(jnp.einsum('gcp,lgp,gpd->glcd', cc, apow[:L], b_bar, precision=hp))
    tau = jnp.arange(L)[None, :] - jnp.arange(L)[:, None]
    toe = kern[:, jnp.clip(tau, 0, L - 1)]
    toe = jnp.where((tau >= 0)[None, :, :, None, None], toe, 0.0)
    toe = toe.transpose(0, 1, 4, 2, 3)
    eye_t = jnp.eye(L, dtype=F32)[None, :, None, :, None]
    eye_c = jnp.eye(n_ch, dtype=F32)[None, None, :, None, :]
    toe = toe + eye_t * eye_c * d_skip.astype(F32)[:, None, None, None, :]
    t_op = toe.reshape(n_grp, L * n_ch, L * n_ch)

    s_in = apow[:L][::-1].transpose(1, 0, 2)[:, :, None, :] * b_bar.transpose(0, 2, 1)[:, None, :, :]
    s_in = s_in.reshape(n_grp, L * n_ch, n_state)
    s_in = jnp.concatenate([s_in.real, s_in.imag, s_in.imag, s_in.real], axis=-1)

    s_out = cc.transpose(0, 2, 1)[:, :, None, :] * apow[1:].transpose(1, 2, 0)[:, :, :, None]
    s_out = s_out.reshape(n_grp, n_state, L * n_ch)
    s_out = jnp.concatenate([s_out.real, -s_out.imag], axis=1)

    a_l = apow[L]
    coef = jnp.stack([jnp.concatenate([a_l.real, a_l.real], -1),
                      jnp.concatenate([-a_l.imag, a_l.imag], -1),
                      jnp.concatenate([a_l.imag, -a_l.imag], -1)], axis=1)
    coef = jnp.pad(coef, ((0, 0), (0, 5), (0, 0)))
    return t_op.astype(BF16), s_in.astype(BF16), s_out.astype(BF16), coef


def _ssm_scan(u, t_op, s_in, s_out, coef, *, n_batch):
    n_grp, rows, lc = u.shape
    p2 = coef.shape[-1]
    kern = functools.partial(_ssm_kernel, n_batch=n_batch)
    return pl.pallas_call(
        kern,
        grid=(n_grp,),
        in_specs=[pl.BlockSpec((None, rows, lc), lambda g: (g, 0, 0)),
                  pl.BlockSpec((None, lc, lc), lambda g: (g, 0, 0)),
                  pl.BlockSpec((None, lc, 2 * p2), lambda g: (g, 0, 0)),
                  pl.BlockSpec((None, p2, lc), lambda g: (g, 0, 0)),
                  pl.BlockSpec((None, 8, p2), lambda g: (g, 0, 0))],
        out_specs=pl.BlockSpec((None, rows, lc), lambda g: (g, 0, 0)),
        out_shape=jax.ShapeDtypeStruct((n_grp, rows, lc), BF16),
        scratch_shapes=[pltpu.VMEM((rows, 2 * p2), F32), pltpu.VMEM((rows, p2), F32)],
        compiler_params=pltpu.CompilerParams(dimension_semantics=("arbitrary",),
                                             vmem_limit_bytes=V7X_VMEM_LIMIT_BYTES),
        name="ssm_scan",
    )(u, t_op, s_in, s_out, coef)


def _ssm_out_kernel(y_ref, h_ref, wout_ref, l1g_ref, l1b_ref, rwt_ref, rb_ref,
                    h1_ref, h1p_ref, idx_ref, gate_ref, rank_ref, cnt_ref, *, alpha):
    @pl.when(pl.program_id(0) == 0)
    def _():
        cnt_ref[...] = jnp.zeros_like(cnt_ref)

    d = h_ref.shape[1]
    val = _dot(y_ref[...], wout_ref[:, :d])
    gate = _dot(y_ref[...], wout_ref[:, d:])
    mix = val * jax.nn.sigmoid(gate)
    h1 = _layer_norm(alpha * h_ref[...] + mix, l1g_ref[...], l1b_ref[...])
    h1_ref[...] = h1
    _store_records(h1p_ref, h1)
    _route(h1, rwt_ref, rb_ref, idx_ref, gate_ref, rank_ref, cnt_ref)


def _ssm_layer(h, n_batch, w_in, a_re, a_im, log_dt, b_re, b_im, c_re, c_im, d_skip, w_out,
               l1g, l1b, rwt, rb, *, alpha):
    n_tok, d = h.shape
    seq = n_tok // n_batch
    n_grp, n_state, n_ch = b_re.shape
    width = n_grp * n_ch
    n_exp = rwt.shape[0]
    L = SSM_CHUNK
    assert seq % L == 0

    uf = _ssm_in_proj(h, w_in)
    u = uf.reshape(n_batch, seq // L, L, n_grp, n_ch).transpose(3, 1, 0, 2, 4)
    u = u.reshape(n_grp, (seq // L) * n_batch, L * n_ch)
    t_op, s_in, s_out, coef = _ssm_operators(a_re, a_im, log_dt, b_re, b_im, c_re, c_im, d_skip)
    y = _ssm_scan(u, t_op, s_in, s_out, coef, n_batch=n_batch)
    y = y.reshape(n_grp, seq // L, n_batch, L, n_ch).transpose(2, 1, 3, 0, 4).reshape(n_tok, width)

    tm = min(TM_MIX, n_tok)
    assert n_tok % tm == 0
    r_shapes, r_specs = _router_out(n_tok, tm, n_exp)
    kern = functools.partial(_ssm_out_kernel, alpha=alpha)
    return pl.pallas_call(
        kern,
        grid=(n_tok // tm,),
        in_specs=[pl.BlockSpec((tm, width), lambda i: (i, 0)),
                  pl.BlockSpec((tm, d), lambda i: (i, 0)),
                  _const_spec((width, 2 * d)), _const_spec((1, d)), _const_spec((1, d)),
                  _const_spec((n_exp, d)), _const_spec((n_exp, 1))],
        out_specs=[pl.BlockSpec((tm, d), lambda i: (i, 0)),
                   pl.BlockSpec((tm * RECORD_ROWS, 128), lambda i: (i, 0))] + r_specs,
        out_shape=[jax.ShapeDtypeStruct((n_tok, d), F32),
                   jax.ShapeDtypeStruct((n_tok * RECORD_ROWS, 128), U32)] + r_shapes,
        compiler_params=pltpu.CompilerParams(dimension_semantics=("arbitrary",),
                                             vmem_limit_bytes=V7X_VMEM_LIMIT_BYTES),
        name="ssm_out_mixer",
    )(y, h, w_out.astype(BF16), l1g.reshape(1, -1), l1b.reshape(1, -1), rwt, rb)


def _moe_plan(idx, rank, cnt, tme, n_tiles):
    n_exp = cnt.shape[0]
    counts = cnt[:, 0].astype(I32)
    padded = ((counts + tme - 1) // tme) * tme
    ends = jnp.cumsum(padded)
    offs = ends - padded
    n_valid = ends[-1] // tme
    tile_start = jnp.arange(n_tiles, dtype=I32) * tme
    tile_expert = jnp.minimum(jnp.sum(tile_start[:, None] >= ends[None, :], axis=1), n_exp - 1).astype(I32)
    onehot = idx[:, :, None] == jnp.arange(n_exp, dtype=I32)[None, None, :]
    pos = jnp.sum(jnp.where(onehot, offs[None, None, :], 0), axis=-1) + rank
    meta = jnp.concatenate([counts, padded, offs, n_valid[None]]).astype(I32)
    return pos, tile_expert, meta


def _tile_pos(pos, tm):
    n_tok = pos.shape[1]
    return pos.reshape(2, n_tok // tm, tm).transpose(1, 0, 2).reshape(n_tok // tm, 1, 2 * tm)


def _record_slice(ref, row, n_rec=1):
    return ref.at[pl.ds(pl.multiple_of(row * RECORD_ROWS, RECORD_ROWS), n_rec * RECORD_ROWS), :]


def _dispatch_kernel(meta_ref, pos_ref, x_ref, xs_ref, zero_scr, sem, *, n_exp, tme, n_tiles):
    i = pl.program_id(0)
    tm = x_ref.shape[0] // RECORD_ROWS
    half = zero_scr.shape[0] // RECORD_ROWS

    def zero_fill(start, size):
        cp = pltpu.make_async_copy(_record_slice(zero_scr, 0, size), _record_slice(xs_ref, start, size), sem)
        cp.start()
        cp.wait()

    @pl.when(i == 0)
    def _():
        zero_scr[...] = jnp.zeros_like(zero_scr)
        for e in range(n_exp):
            pad = meta_ref[n_exp + e] - meta_ref[e]
            base = meta_ref[2 * n_exp + e] + meta_ref[e]
            for b in range(int(math.log2(tme))):
                size = 1 << b

                @pl.when(((pad >> b) & 1) == 1)
                def _():
                    zero_fill(base + (pad & (size - 1)), size)
        n_valid = meta_ref[3 * n_exp]
        for j in range(n_exp):
            @pl.when(n_valid + j < n_tiles)
            def _():
                for part in range(tme // half):
                    zero_fill((n_valid + j) * tme + part * half, half)

    def start(r, c):
        for slot in range(2):
            p = pos_ref[0, 0, slot * tm + r]
            pltpu.make_async_copy(_record_slice(x_ref, r), _record_slice(xs_ref, p), sem).start()
        return c

    lax.fori_loop(0, tm, start, 0, unroll=8)
    for slot in range(2):
        pltpu.make_async_copy(x_ref, _record_slice(xs_ref, 0, tm), sem).wait()


def _moe_dispatch(h1p, pos, meta, *, n_exp, tme, n_tiles):
    n_tok = h1p.shape[0] // RECORD_ROWS
    tm = min(TM_DISPATCH, n_tok)
    assert n_tok % tm == 0
    half = max(tme // 2, 1)
    kern = functools.partial(_dispatch_kernel, n_exp=n_exp, tme=tme, n_tiles=n_tiles)
    return pl.pallas_call(
        kern,
        grid_spec=pltpu.PrefetchScalarGridSpec(
            num_scalar_prefetch=1,
            grid=(n_tok // tm,),
            in_specs=[pl.BlockSpec((1, 1, 2 * tm), lambda i, m: (i, 0, 0), memory_space=pltpu.SMEM),
                      pl.BlockSpec((tm * RECORD_ROWS, 128), lambda i, m: (i, 0))],
            out_specs=pl.BlockSpec(memory_space=pl.ANY),
            scratch_shapes=[pltpu.VMEM((half * RECORD_ROWS, 128), U32), pltpu.SemaphoreType.DMA]),
        out_shape=jax.ShapeDtypeStruct((n_tiles * tme * RECORD_ROWS, 128), U32),
        compiler_params=pltpu.CompilerParams(dimension_semantics=("arbitrary",),
                                             vmem_limit_bytes=V7X_VMEM_LIMIT_BYTES),
        name="moe_dispatch",
    )(meta, _tile_pos(pos, tm), h1p)


def _expert_kernel(te_ref, meta_ref, xs_ref, win_ref, wout_ref, ys_ref, *, n_exp):
    i = pl.program_id(0)
    n_valid = meta_ref[3 * n_exp]
    tme = xs_ref.shape[0] // RECORD_ROWS

    @pl.when(i < n_valid)
    def _():
        hi, lo = _load_records(xs_ref, tme)
        dh = hi.shape[1]
        ff = wout_ref.shape[0]
        h = _dot(hi, win_ref[:dh, :]) + _dot(lo, win_ref[dh:, :])
        act = (jax.nn.silu(h[:, :ff]) * h[:, ff:]).astype(BF16)
        _store_records(ys_ref, _dot(act, wout_ref[...]))

    @pl.when(i >= n_valid)
    def _():
        ys_ref[...] = jnp.zeros_like(ys_ref)


def _moe_experts(xs, tile_expert, meta, w_in, w_out, *, layer, tme):
    n_rows = xs.shape[0] // RECORD_ROWS
    _, n_exp, d, ff2 = w_in.shape
    ff = w_out.shape[2]
    n_tiles = n_rows // tme

    def x_map(i, te, m):
        return (jnp.minimum(i, m[3 * n_exp] - 1), 0)

    kern = functools.partial(_expert_kernel, n_exp=n_exp)
    return pl.pallas_call(
        kern,
        grid_spec=pltpu.PrefetchScalarGridSpec(
            num_scalar_prefetch=2,
            grid=(n_tiles,),
            in_specs=[pl.BlockSpec((tme * RECORD_ROWS, 128), x_map),
                      pl.BlockSpec((None, None, d, ff2), lambda i, te, m: (layer, te[i], 0, 0)),
                      pl.BlockSpec((None, None, ff, d), lambda i, te, m: (layer, te[i], 0, 0))],
            out_specs=pl.BlockSpec((tme * RECORD_ROWS, 128), lambda i, te, m: (i, 0))),
        out_shape=jax.ShapeDtypeStruct((n_rows * RECORD_ROWS, 128), U32),
        compiler_params=pltpu.CompilerParams(dimension_semantics=("arbitrary",),
                                             vmem_limit_bytes=V7X_VMEM_LIMIT_BYTES),
        name="moe_experts",
    )(tile_expert, meta, xs, w_in, w_out)


def _combine_kernel(pos_ref, h1_ref, gate_ref, ys_ref, g_ref, b_ref, out_ref, buf, sem, *, alpha):
    tm = h1_ref.shape[0]

    def start(r, c):
        for slot in range(2):
            p = pos_ref[0, 0, slot * tm + r]
            pltpu.make_async_copy(_record_slice(ys_ref, p), _record_slice(buf.at[slot], r), sem).start()
        return c

    lax.fori_loop(0, tm, start, 0, unroll=8)
    for slot in range(2):
        pltpu.make_async_copy(_record_slice(ys_ref, 0, tm), buf.at[slot], sem).wait()
    moe = None
    for slot in range(2):
        hi, lo = _load_records(buf.at[slot], tm)
        y = jnp.concatenate([hi, lo], axis=1).astype(F32) * gate_ref[:, slot:slot + 1]
        moe = y if moe is None else moe + y
    out_ref[...] = _layer_norm(alpha * h1_ref[...] + moe, g_ref[...], b_ref[...])


def _moe_combine(h1, ys, pos, gates, l2g, l2b, *, alpha):
    n_tok, d = h1.shape
    tm = min(TM_COMBINE, n_tok)
    assert n_tok % tm == 0
    kern = functools.partial(_combine_kernel, alpha=alpha)
    return pl.pallas_call(
        kern,
        grid=(n_tok // tm,),
        in_specs=[pl.BlockSpec((1, 1, 2 * tm), lambda i: (i, 0, 0), memory_space=pltpu.SMEM),
                  pl.BlockSpec((tm, d), lambda i: (i, 0)),
                  pl.BlockSpec((tm, 2), lambda i: (i, 0)),
                  pl.BlockSpec(memory_space=pl.ANY),
                  _const_spec((1, d)), _const_spec((1, d))],
        out_specs=pl.BlockSpec((tm, d), lambda i: (i, 0)),
        out_shape=jax.ShapeDtypeStruct((n_tok, d), F32),
        scratch_shapes=[pltpu.VMEM((2, tm * RECORD_ROWS, 128), U32), pltpu.SemaphoreType.DMA],
        compiler_params=pltpu.CompilerParams(dimension_semantics=("arbitrary",),
                                             vmem_limit_bytes=V7X_VMEM_LIMIT_BYTES),
        name="moe_combine",
    )(_tile_pos(pos, tm), h1, gates.T, ys, l2g.reshape(1, -1), l2b.reshape(1, -1))


def _moe_layer(h1, h1p, idx, gates, rank, cnt, w_in, w_out, l2g, l2b, *, layer, alpha):
    n_tok = h1.shape[0]
    n_exp = w_in.shape[1]
    tme = min(TM_EXPERT, n_tok)
    n_tiles = (2 * n_tok) // tme + n_exp
    pos, tile_expert, meta = _moe_plan(idx, rank, cnt, tme, n_tiles)
    xs = _moe_dispatch(h1p, pos, meta, n_exp=n_exp, tme=tme, n_tiles=n_tiles)
    ys = _moe_experts(xs, tile_expert, meta, w_in, w_out, layer=layer, tme=tme)
    return _moe_combine(h1, ys, pos, gates, l2g, l2b, alpha=alpha)


def kernel(x, a_w_in, a_b_in, a_ln_g, a_ln_b, a_w_s, a_b_s, a_w_out, b_w_in, b_a_re, b_a_im, b_log_dt, b_b_re, b_b_im, b_c_re, b_c_im, b_d, b_w_out, ln1_g, ln1_b, router_w, router_b, e_w_in, e_w_out, ln2_g, ln2_b):
    n_batch, seq, d = x.shape
    depth = ln1_g.shape[0]
    alpha = float((2 * depth) ** 0.25)
    rwt = router_w.astype(F32).T
    rb = router_b.astype(F32).reshape(-1, 1)
    h = x.reshape(n_batch * seq, d)
    e_w_in = e_w_in.astype(BF16)
    e_w_out = e_w_out.astype(BF16)
    for i in range(depth):
        j = i // 2
        if i % 2 == 0:
            mixed = _gmlp_layer(h, a_w_in[j], a_b_in[j], a_ln_g[j], a_ln_b[j], a_w_s[j], a_b_s[j],
                                a_w_out[j], ln1_g[i], ln1_b[i], rwt, rb, alpha=alpha)
        else:
            mixed = _ssm_layer(h, n_batch, b_w_in[j], b_a_re[j], b_a_im[j], b_log_dt[j], b_b_re[j],
                               b_b_im[j], b_c_re[j], b_c_im[j], b_d[j], b_w_out[j],
                               ln1_g[i], ln1_b[i], rwt, rb, alpha=alpha)
        h1, h1p, idx, gates, rank, cnt = mixed
        h = _moe_layer(h1, h1p, idx, gates, rank, cnt, e_w_in, e_w_out, ln2_g[i], ln2_b[i],
                       layer=i, alpha=alpha)
    return h.reshape(n_batch, seq, d)
```

```python
import functools
import math

import jax
import jax.numpy as jnp
from jax import lax
from jax.experimental import pallas as pl
from jax.experimental.pallas import tpu as pltpu

F32 = jnp.float32
BF16 = jnp.bfloat16
I32 = jnp.int32
U32 = jnp.uint32

LN_EPS = 1e-5
N_EXPERT_GROUPS = 4
SSM_CHUNK = 16

TM_MIX = 256
TM_PROJ = 512
TM_DISPATCH = 512
TM_EXPERT = 512
TM_COMBINE = 256

V7X_VMEM_LIMIT_BYTES = 56 * 1024 * 1024


def _dot(a, b):
    return jnp.dot(a, b, preferred_element_type=F32)


def _layer_norm(x, g, b):
    mu = jnp.mean(x, axis=-1, keepdims=True)
    xc = x - mu
    var = jnp.mean(xc * xc, axis=-1, keepdims=True)
    return xc * lax.rsqrt(var + LN_EPS) * g + b


def _const_spec(shape):
    return pl.BlockSpec(shape, lambda *_: (0,) * len(shape), pipeline_mode=pl.Buffered(1))


def _route(h1, rwt_ref, rb_ref, idx_ref, gate_ref, rank_ref, cnt_ref):
    n_exp = rwt_ref.shape[0]
    per_group = n_exp // N_EXPERT_GROUPS
    tm = h1.shape[0]
    logits = lax.dot_general(rwt_ref[...], h1, (((1,), (1,)), ((), ())),
                             precision=lax.Precision.HIGHEST,
                             preferred_element_type=F32) + rb_ref[...]
    m = jnp.max(logits, axis=0, keepdims=True)
    ex = jnp.exp(logits - m)
    probs = ex / jnp.sum(ex, axis=0, keepdims=True)

    best_score = None
    best_group = None
    for g in range(N_EXPERT_GROUPS):
        t1 = t2 = None
        for e in range(g * per_group, (g + 1) * per_group):
            v = probs[e:e + 1, :]
            if t1 is None:
                t1 = v
            elif t2 is None:
                t1, t2 = jnp.maximum(t1, v), jnp.minimum(t1, v)
            else:
                t2 = jnp.maximum(t2, jnp.minimum(t1, v))
                t1 = jnp.maximum(t1, v)
        score = t1 + t2
        if best_score is None:
            best_score, best_group = score, jnp.zeros((1, tm), I32)
        else:
            better = score > best_score
            best_group = jnp.where(better, g, best_group)
            best_score = jnp.where(better, score, best_score)

    eidx = lax.broadcasted_iota(I32, (n_exp, tm), 0)
    egrp = jnp.zeros((n_exp, tm), I32)
    for g in range(1, N_EXPERT_GROUPS):
        egrp = egrp + jnp.where(eidx >= g * per_group, 1, 0)
    in_group = egrp == best_group
    masked = jnp.where(in_group, probs, -1.0)
    m1 = jnp.max(masked, axis=0, keepdims=True)
    i1 = jnp.min(jnp.where(masked == m1, eidx, n_exp), axis=0, keepdims=True)
    sel1 = eidx == i1
    masked2 = jnp.where(sel1, -2.0, masked)
    m2 = jnp.max(masked2, axis=0, keepdims=True)
    i2 = jnp.min(jnp.where(masked2 == m2, eidx, n_exp), axis=0, keepdims=True)
    sel2 = eidx == i2
    den = m1 + m2

    cnt = jnp.where(sel1 | sel2, 1.0, 0.0)
    r_i = lax.broadcasted_iota(I32, (tm, tm), 0)
    c_i = lax.broadcasted_iota(I32, (tm, tm), 1)
    strict_upper = jnp.where(r_i < c_i, 1.0, 0.0).astype(BF16)
    prefix = _dot(cnt.astype(BF16), strict_upper) + cnt_ref[:, 0:1]
    r1 = jnp.sum(jnp.where(sel1, prefix, 0.0), axis=0, keepdims=True)
    r2 = jnp.sum(jnp.where(sel2, prefix, 0.0), axis=0, keepdims=True)
    cnt_ref[...] += jnp.sum(cnt, axis=1, keepdims=True)

    idx_ref[...] = jnp.concatenate([i1, i2], axis=0)
    gate_ref[...] = jnp.concatenate([m1 / den, m2 / den], axis=0)
    rank_ref[...] = jnp.concatenate([r1, r2], axis=0).astype(I32)


RECORD_ROWS = 8


def _store_records(ref, h):
    tm, d = h.shape
    half = d // 2
    assert half == RECORD_ROWS * 128
    bits = lax.bitcast_convert_type(h.astype(BF16).astype(F32), U32)
    packed = (bits[:, :half] & jnp.uint32(0xFFFF0000)) | (bits[:, half:] >> 16)
    for s in range(RECORD_ROWS):
        ref[pl.ds(s, tm, stride=RECORD_ROWS), :] = packed[:, s * 128:(s + 1) * 128]


def _load_records(ref, tm):
    w = jnp.concatenate([ref[pl.ds(s, tm, stride=RECORD_ROWS), :] for s in range(RECORD_ROWS)], axis=1)
    hi = lax.bitcast_convert_type(w & jnp.uint32(0xFFFF0000), F32).astype(BF16)
    lo = lax.bitcast_convert_type(w << 16, F32).astype(BF16)
    return hi, lo


def _router_out(n_tok, tm, n_exp):
    shapes = [jax.ShapeDtypeStruct((2, n_tok), I32),
              jax.ShapeDtypeStruct((2, n_tok), F32),
              jax.ShapeDtypeStruct((2, n_tok), I32),
              jax.ShapeDtypeStruct((n_exp, 128), F32)]
    specs = [pl.BlockSpec((2, tm), lambda i: (0, i)),
             pl.BlockSpec((2, tm), lambda i: (0, i)),
             pl.BlockSpec((2, tm), lambda i: (0, i)),
             pl.BlockSpec((n_exp, 128), lambda i: (0, 0))]
    return shapes, specs


def _gmlp_kernel(x_ref, win_ref, bin_ref, lng_ref, lnb_ref, ws_ref, bst_ref, wout_ref,
                 l1g_ref, l1b_ref, rwt_ref, rb_ref,
                 h1_ref, h1p_ref, idx_ref, gate_ref, rank_ref, cnt_ref,
                 p_scr, *, alpha, chunk, groups):
    @pl.when(pl.program_id(0) == 0)
    def _():
        cnt_ref[...] = jnp.zeros_like(cnt_ref)

    x = x_ref[...]
    xb = x.astype(BF16)
    width = win_ref.shape[1] // 2
    gdim = width // groups
    tm = x.shape[0]

    zv = jax.nn.gelu(_dot(xb, win_ref[:, width:]) + bin_ref[:, width:])
    vn = _layer_norm(zv, lng_ref[...], lnb_ref[...]).astype(BF16)
    zu = jax.nn.gelu(_dot(xb, win_ref[:, :width]) + bin_ref[:, :width])

    r_i = lax.broadcasted_iota(I32, (chunk, chunk), 0)
    c_i = lax.broadcasted_iota(I32, (chunk, chunk), 1)
    causal = r_i >= c_i
    for g in range(groups):
        w = jnp.where(causal, ws_ref[g], 0.0).astype(BF16)
        bias = bst_ref[:, g:g + 1]
        for c in range(tm // chunk):
            rows = slice(c * chunk, (c + 1) * chunk)
            cols = slice(g * gdim, (g + 1) * gdim)
            s = _dot(w, vn[rows, cols]) + bias
            p_scr[rows, cols] = (zu[rows, cols] * s).astype(BF16)

    mix = _dot(p_scr[...], wout_ref[...])
    h1 = _layer_norm(alpha * x + mix, l1g_ref[...], l1b_ref[...])
    h1_ref[...] = h1
    _store_records(h1p_ref, h1)
    _route(h1, rwt_ref, rb_ref, idx_ref, gate_ref, rank_ref, cnt_ref)


def _gmlp_layer(h, w_in, b_in, ln_g, ln_b, w_s, b_s, w_out, l1g, l1b, rwt, rb, *, alpha):
    n_tok, d = h.shape
    groups, chunk, _ = w_s.shape
    width = w_in.shape[1] // 2
    n_exp = rwt.shape[0]
    tm = min(TM_MIX, n_tok)
    assert n_tok % tm == 0 and tm % chunk == 0
    r_shapes, r_specs = _router_out(n_tok, tm, n_exp)
    kern = functools.partial(_gmlp_kernel, alpha=alpha, chunk=chunk, groups=groups)
    return pl.pallas_call(
        kern,
        grid=(n_tok // tm,),
        in_specs=[pl.BlockSpec((tm, d), lambda i: (i, 0)),
                  _const_spec((d, 2 * width)), _const_spec((1, 2 * width)),
                  _const_spec((1, width)), _const_spec((1, width)),
                  _const_spec((groups, chunk, chunk)), _const_spec((chunk, groups)),
                  _const_spec((width, d)), _const_spec((1, d)), _const_spec((1, d)),
                  _const_spec((n_exp, d)), _const_spec((n_exp, 1))],
        out_specs=[pl.BlockSpec((tm, d), lambda i: (i, 0)),
                   pl.BlockSpec((tm * RECORD_ROWS, 128), lambda i: (i, 0))] + r_specs,
        out_shape=[jax.ShapeDtypeStruct((n_tok, d), F32),
                   jax.ShapeDtypeStruct((n_tok * RECORD_ROWS, 128), U32)] + r_shapes,
        scratch_shapes=[pltpu.VMEM((tm, width), BF16)],
        compiler_params=pltpu.CompilerParams(dimension_semantics=("arbitrary",),
                                             vmem_limit_bytes=V7X_VMEM_LIMIT_BYTES),
        name="gmlp_mixer",
    )(h, w_in.astype(BF16), b_in.reshape(1, -1), ln_g.reshape(1, -1), ln_b.reshape(1, -1),
      w_s, b_s.T, w_out.astype(BF16), l1g.reshape(1, -1), l1b.reshape(1, -1), rwt, rb)


SSM_LANE_GROUPS = 8
SSM_BATCH_PER_STEP = 4


def _proj_kernel(x_ref, w_ref, o_ref, res_scr, *, chunk):
    res = _dot(x_ref[...].astype(BF16), w_ref[...])
    n_slab = res_scr.shape[0]
    kb = o_ref.shape[0]
    for j in range(n_slab):
        res_scr[j] = res[:, j * 128:(j + 1) * 128]
    for t in range(chunk):
        for j in range(n_slab):
            o_ref[:, (t * n_slab + j) * 128:(t * n_slab + j + 1) * 128] = (
                res_scr[j, pl.ds(t, kb, stride=chunk), :].astype(o_ref.dtype))


def _ssm_in_proj(h, w, *, chunk):
    n_tok, d = h.shape
    width = w.shape[1]
    tm = min(TM_PROJ, n_tok)
    assert n_tok % tm == 0 and tm % chunk == 0 and width % 128 == 0
    kern = functools.partial(_proj_kernel, chunk=chunk)
    return pl.pallas_call(
        kern,
        grid=(n_tok // tm,),
        in_specs=[pl.BlockSpec((tm, d), lambda i: (i, 0)), _const_spec((d, width))],
        out_specs=pl.BlockSpec((tm // chunk, chunk * width), lambda i: (i, 0)),
        out_shape=jax.ShapeDtypeStruct((n_tok // chunk, chunk * width), BF16),
        scratch_shapes=[pltpu.VMEM((width // 128, tm, 128), F32)],
        compiler_params=pltpu.CompilerParams(dimension_semantics=("arbitrary",),
                                             vmem_limit_bytes=V7X_VMEM_LIMIT_BYTES),
        name="ssm_in_proj",
    )(h, w.astype(BF16))


def _ssm_kernel(*refs, chunk, n_batch, n_chunks):
    u_refs = refs[:chunk]
    t_ref, sin_ref, sout_ref, coef_ref, y_hbm, w_scr, x_scr, y_scr, sem = refs[chunk:]
    gblk = pl.program_id(0)
    bblk = pl.program_id(1)
    rows = n_batch * n_chunks
    n_x = x_scr.shape[0]

    u = jnp.concatenate([r[...] for r in u_refs], axis=1)
    for jb in range(n_x):
        w = _dot(u, sin_ref[:, jb * 256:(jb + 1) * 256])
        w_scr[2 * jb] = w[:, :128]
        w_scr[2 * jb + 1] = w[:, 128:]

    ar = [jnp.broadcast_to(coef_ref[0:1, j * 128:(j + 1) * 128], (n_batch, 128)) for j in range(n_x)]
    ai_a = [jnp.broadcast_to(coef_ref[1:2, j * 128:(j + 1) * 128], (n_batch, 128)) for j in range(n_x)]
    ai_b = [jnp.broadcast_to(coef_ref[2:3, j * 128:(j + 1) * 128], (n_batch, 128)) for j in range(n_x)]

    def step(k, carry):
        x, xsw = carry
        rows_k = pl.ds(k, n_batch, stride=n_chunks)
        new_x, new_xsw = [], []
        for j in range(n_x):
            x_scr[j, rows_k, :] = x[j]
            new_x.append(ar[j] * x[j] + ai_a[j] * xsw[j] + w_scr[j, rows_k, :])
            new_xsw.append(ar[j] * xsw[j] + ai_b[j] * x[j] + w_scr[n_x + j, rows_k, :])
        return tuple(new_x), tuple(new_xsw)

    zero = tuple(jnp.zeros((n_batch, 128), F32) for _ in range(n_x))
    lax.fori_loop(0, n_chunks, step, (zero, zero))

    xs = jnp.concatenate([x_scr[j] for j in range(n_x)], axis=1).astype(BF16)
    n_lane_blocks = y_hbm.shape[1] // (chunk * 128)

    def out_copy(t):
        col = (t * n_lane_blocks + gblk) * 128
        return pltpu.make_async_copy(
            y_scr.at[t],
            y_hbm.at[pl.ds(pl.multiple_of(bblk * rows, rows), rows), pl.ds(pl.multiple_of(col, 128), 128)],
            sem)

    for jb in range(chunk // 2):
        kk = (jb + 1) * 256
        cols = slice(jb * 256, (jb + 1) * 256)
        y = _dot(u[:, :kk], t_ref[:kk, cols]) + _dot(xs, sout_ref[:, cols])
        y = jax.nn.gelu(y).astype(BF16)
        for half in range(2):
            t = 2 * jb + half
            y_scr[t] = y[:, half * 128:(half + 1) * 128]
            out_copy(t).start()
    for t in range(chunk):
        out_copy(t).wait()


def _block_diag_groups(op, n_blk):
    n_grp, r1, r2, q1, q2 = op.shape
    op = op.reshape(n_grp // n_blk, n_blk, r1, r2, q1, q2)
    eye = jnp.eye(n_blk, dtype=op.dtype)
    out = jnp.einsum('Bgarcd,gh->Bagrchd', op, eye)
    return out.reshape(n_grp // n_blk, r1 * n_blk * r2, q1 * n_blk * q2)


def _ssm_operators(a_re, a_im, log_dt, b_re, b_im, c_re, c_im, d_skip):
    hp = lax.Precision.HIGHEST
    n_grp, n_state, n_ch = b_re.shape
    L = SSM_CHUNK
    nb = SSM_LANE_GROUPS
    assert n_ch * nb == 128 and 2 * n_state == 128 and n_grp % nb == 0
    a = lax.complex(a_re.astype(F32), a_im.astype(F32))
    lam = a * jnp.exp(log_dt.astype(F32))[:, None]
    a_bar = jnp.exp(lam)
    b_bar = ((a_bar - 1.0) / a)[..., None] * lax.complex(b_re.astype(F32), b_im.astype(F32))
    cc = lax.complex(c_re.astype(F32), c_im.astype(F32))
    steps = jnp.arange(L + 1, dtype=F32)
    apow = jnp.exp(steps[:, None, None] * lam[None])

    kern = jnp.real(jnp.einsum('gcp,lgp,gpd->glcd', cc, apow[:L], b_bar, precision=hp))
    toe = jnp.stack([jnp.concatenate([jnp.zeros_like(kern[:, :tp]), kern[:, :L - tp]], axis=1)
                     for tp in range(L)], axis=1)
    toe = toe.transpose(0, 1, 4, 2, 3)
    eye_t = jnp.eye(L, dtype=F32)[None, :, None, :, None]
    eye_c = jnp.eye(n_ch, dtype=F32)[None, None, :, None, :]
    toe = toe + eye_t * eye_c * d_skip.astype(F32)[:, None, None, None, :]
    t_op = _block_diag_groups(toe.astype(BF16), nb)

    s_in = apow[:L][::-1].transpose(1, 0, 2)[:, :, None, :] * b_bar.transpose(0, 2, 1)[:, None, :, :]
    s_plain = jnp.concatenate([s_in.real, s_in.imag], axis=-1).astype(BF16)[:, :, :, None, :]
    s_swap = jnp.concatenate([s_in.imag, s_in.real], axis=-1).astype(BF16)[:, :, :, None, :]
    s_in_op = jnp.concatenate([_block_diag_groups(s_plain, nb), _block_diag_groups(s_swap, nb)], axis=-1)

    s_out = cc.transpose(0, 2, 1)[:, :, None, :] * apow[1:].transpose(1, 2, 0)[:, :, :, None]
    s_out = jnp.concatenate([s_out.real, -s_out.imag], axis=1).astype(BF16)[:, None, :, :, :]
    s_out_op = _block_diag_groups(s_out, nb)

    a_l = apow[L]
    coef = jnp.stack([jnp.concatenate([a_l.real, a_l.real], -1),
                      jnp.concatenate([-a_l.imag, a_l.imag], -1),
                      jnp.concatenate([a_l.imag, -a_l.imag], -1)], axis=1)
    coef = coef.reshape(n_grp // nb, nb, 3, 2 * n_state).transpose(0, 2, 1, 3).reshape(n_grp // nb, 3, nb * 2 * n_state)
    coef = jnp.pad(coef, ((0, 0), (0, 5), (0, 0)))
    return t_op, s_in_op, s_out_op, coef


def _ssm_scan(u, t_op, s_in, s_out, coef, *, n_batch, chunk):
    rows_all, lanes_all = u.shape
    n_blk, lc, _ = t_op.shape
    n_chunks = rows_all // n_batch
    nb = min(SSM_BATCH_PER_STEP, n_batch)
    assert n_batch % nb == 0 and lanes_all == chunk * n_blk * 128
    rows = nb * n_chunks
    n_x = s_out.shape[1] // 128
    kern = functools.partial(_ssm_kernel, chunk=chunk, n_batch=nb, n_chunks=n_chunks)
    u_specs = [pl.BlockSpec((rows, 128), lambda g, b, t=t: (b, t * n_blk + g)) for t in range(chunk)]

    def op_spec(shape):
        return pl.BlockSpec((None,) + shape, lambda g, b: (g, 0, 0), pipeline_mode=pl.Buffered(1))

    return pl.pallas_call(
        kern,
        grid=(n_blk, n_batch // nb),
        in_specs=u_specs + [op_spec((lc, lc)), op_spec((lc, 2 * n_x * 128)), op_spec((n_x * 128, lc)),
                            op_spec((8, n_x * 128))],
        out_specs=pl.BlockSpec(memory_space=pl.ANY),
        out_shape=jax.ShapeDtypeStruct((rows_all, lanes_all), BF16),
        scratch_shapes=[pltpu.VMEM((2 * n_x, rows, 128), F32), pltpu.VMEM((n_x, rows, 128), F32),
                        pltpu.VMEM((chunk, rows, 128), BF16), pltpu.SemaphoreType.DMA],
        compiler_params=pltpu.CompilerParams(dimension_semantics=("arbitrary", "arbitrary"),
                                             vmem_limit_bytes=V7X_VMEM_LIMIT_BYTES),
        name="ssm_scan",
    )(*([u] * chunk), t_op, s_in, s_out, coef)


def _ssm_out_kernel(y_ref, h_ref, wout_ref, l1g_ref, l1b_ref, rwt_ref, rb_ref,
                    h1_ref, h1p_ref, idx_ref, gate_ref, rank_ref, cnt_ref, ytok_scr, *, alpha, chunk):
    @pl.when(pl.program_id(0) == 0)
    def _():
        cnt_ref[...] = jnp.zeros_like(cnt_ref)

    n_slab = ytok_scr.shape[0]
    kb = y_ref.shape[0]
    yf = y_ref[...].astype(F32)
    for t in range(chunk):
        for j in range(n_slab):
            ytok_scr[j, pl.ds(t, kb, stride=chunk), :] = yf[:, (t * n_slab + j) * 128:(t * n_slab + j + 1) * 128]
    y = jnp.concatenate([ytok_scr[j] for j in range(n_slab)], axis=1).astype(BF16)

    d = h_ref.shape[1]
    val = _dot(y, wout_ref[:, :d])
    gate = _dot(y, wout_ref[:, d:])
    mix = val * jax.nn.sigmoid(gate)
    h1 = _layer_norm(alpha * h_ref[...] + mix, l1g_ref[...], l1b_ref[...])
    h1_ref[...] = h1
    _store_records(h1p_ref, h1)
    _route(h1, rwt_ref, rb_ref, idx_ref, gate_ref, rank_ref, cnt_ref)


def _ssm_layer(h, n_batch, w_in, a_re, a_im, log_dt, b_re, b_im, c_re, c_im, d_skip, w_out,
               l1g, l1b, rwt, rb, *, alpha):
    n_tok, d = h.shape
    n_grp, n_state, n_ch = b_re.shape
    width = n_grp * n_ch
    n_exp = rwt.shape[0]
    L = SSM_CHUNK
    assert (n_tok // n_batch) % L == 0

    u = _ssm_in_proj(h, w_in, chunk=L)
    t_op, s_in, s_out, coef = _ssm_operators(a_re, a_im, log_dt, b_re, b_im, c_re, c_im, d_skip)
    y = _ssm_scan(u, t_op, s_in, s_out, coef, n_batch=n_batch, chunk=L)

    tm = min(TM_MIX, n_tok)
    assert n_tok % tm == 0 and tm % L == 0
    r_shapes, r_specs = _router_out(n_tok, tm, n_exp)
    kern = functools.partial(_ssm_out_kernel, alpha=alpha, chunk=L)
    return pl.pallas_call(
        kern,
        grid=(n_tok // tm,),
        in_specs=[pl.BlockSpec((tm // L, L * width), lambda i: (i, 0)),
                  pl.BlockSpec((tm, d), lambda i: (i, 0)),
                  _const_spec((width, 2 * d)), _const_spec((1, d)), _const_spec((1, d)),
                  _const_spec((n_exp, d)), _const_spec((n_exp, 1))],
        out_specs=[pl.BlockSpec((tm, d), lambda i: (i, 0)),
                   pl.BlockSpec((tm * RECORD_ROWS, 128), lambda i: (i, 0))] + r_specs,
        out_shape=[jax.ShapeDtypeStruct((n_tok, d), F32),
                   jax.ShapeDtypeStruct((n_tok * RECORD_ROWS, 128), U32)] + r_shapes,
        scratch_shapes=[pltpu.VMEM((width // 128, tm, 128), F32)],
        compiler_params=pltpu.CompilerParams(dimension_semantics=("arbitrary",),
                                             vmem_limit_bytes=V7X_VMEM_LIMIT_BYTES),
        name="ssm_out_mixer",
    )(y, h, w_out.astype(BF16), l1g.reshape(1, -1), l1b.reshape(1, -1), rwt, rb)


def _moe_plan(idx, rank, cnt, tme, n_tiles):
    n_exp = cnt.shape[0]
    counts = cnt[:, 0].astype(I32)
    padded = ((counts + tme - 1) // tme) * tme
    ends = jnp.cumsum(padded)
    offs = ends - padded
    n_valid = ends[-1] // tme
    tile_start = jnp.arange(n_tiles, dtype=I32) * tme
    tile_expert = jnp.minimum(jnp.sum(tile_start[:, None] >= ends[None, :], axis=1), n_exp - 1).astype(I32)
    onehot = idx[:, :, None] == jnp.arange(n_exp, dtype=I32)[None, None, :]
    pos = jnp.sum(jnp.where(onehot, offs[None, None, :], 0), axis=-1) + rank
    meta = jnp.concatenate([counts, padded, offs, n_valid[None]]).astype(I32)
    return pos, tile_expert, meta


def _tile_pos(pos, tm):
    n_tok = pos.shape[1]
    return pos.reshape(2, n_tok // tm, tm).transpose(1, 0, 2).reshape(n_tok // tm, 1, 2 * tm)


def _record_slice(ref, row, n_rec=1):
    return ref.at[pl.ds(pl.multiple_of(row * RECORD_ROWS, RECORD_ROWS), n_rec * RECORD_ROWS), :]


def _dispatch_kernel(meta_ref, pos_ref, x_ref, xs_ref, zero_scr, sem, *, n_exp, tme, n_tiles):
    i = pl.program_id(0)
    tm = x_ref.shape[0] // RECORD_ROWS
    half = zero_scr.shape[0] // RECORD_ROWS

    def zero_fill(start, size):
        cp = pltpu.make_async_copy(_record_slice(zero_scr, 0, size), _record_slice(xs_ref, start, size), sem)
        cp.start()
        cp.wait()

    @pl.when(i == 0)
    def _():
        zero_scr[...] = jnp.zeros_like(zero_scr)
        for e in range(n_exp):
            pad = meta_ref[n_exp + e] - meta_ref[e]
            base = meta_ref[2 * n_exp + e] + meta_ref[e]
            for b in range(int(math.log2(tme))):
                size = 1 << b

                @pl.when(((pad >> b) & 1) == 1)
                def _():
                    zero_fill(base + (pad & (size - 1)), size)
        n_valid = meta_ref[3 * n_exp]
        for j in range(n_exp):
            @pl.when(n_valid + j < n_tiles)
            def _():
                for part in range(tme // half):
                    zero_fill((n_valid + j) * tme + part * half, half)

    def start(r, c):
        for slot in range(2):
            p = pos_ref[0, 0, slot * tm + r]
            pltpu.make_async_copy(_record_slice(x_ref, r), _record_slice(xs_ref, p), sem).start(priority=slot)
        return c

    lax.fori_loop(0, tm, start, 0, unroll=8)
    for slot in range(2):
        pltpu.make_async_copy(x_ref, _record_slice(xs_ref, 0, tm), sem).wait()


def _moe_dispatch(h1p, pos, meta, *, n_exp, tme, n_tiles):
    n_tok = h1p.shape[0] // RECORD_ROWS
    tm = min(TM_DISPATCH, n_tok)
    assert n_tok % tm == 0
    half = max(tme // 2, 1)
    kern = functools.partial(_dispatch_kernel, n_exp=n_exp, tme=tme, n_tiles=n_tiles)
    return pl.pallas_call(
        kern,
        grid_spec=pltpu.PrefetchScalarGridSpec(
            num_scalar_prefetch=1,
            grid=(n_tok // tm,),
            in_specs=[pl.BlockSpec((1, 1, 2 * tm), lambda i, m: (i, 0, 0), memory_space=pltpu.SMEM),
                      pl.BlockSpec((tm * RECORD_ROWS, 128), lambda i, m: (i, 0))],
            out_specs=pl.BlockSpec(memory_space=pl.ANY),
            scratch_shapes=[pltpu.VMEM((half * RECORD_ROWS, 128), U32), pltpu.SemaphoreType.DMA]),
        out_shape=jax.ShapeDtypeStruct((n_tiles * tme * RECORD_ROWS, 128), U32),
        compiler_params=pltpu.CompilerParams(dimension_semantics=("arbitrary",),
                                             vmem_limit_bytes=V7X_VMEM_LIMIT_BYTES),
        name="moe_dispatch",
    )(meta, _tile_pos(pos, tm), h1p)


def _expert_kernel(te_ref, meta_ref, xs_ref, win_ref, wout_ref, ys_ref, *, n_exp):
    i = pl.program_id(0)
    n_valid = meta_ref[3 * n_exp]
    tme = xs_ref.shape[0] // RECORD_ROWS

    @pl.when(i < n_valid)
    def _():
        hi, lo = _load_records(xs_ref, tme)
        dh = hi.shape[1]
        ff = wout_ref.shape[0]
        h = _dot(hi, win_ref[:dh, :]) + _dot(lo, win_ref[dh:, :])
        act = (jax.nn.silu(h[:, :ff]) * h[:, ff:]).astype(BF16)
        _store_records(ys_ref, _dot(act, wout_ref[...]))

    @pl.when(i >= n_valid)
    def _():
        ys_ref[...] = jnp.zeros_like(ys_ref)


def _moe_experts(xs, tile_expert, meta, w_in, w_out, *, layer, tme):
    n_rows = xs.shape[0] // RECORD_ROWS
    _, n_exp, d, ff2 = w_in.shape
    ff = w_out.shape[2]
    n_tiles = n_rows // tme

    def x_map(i, te, m):
        return (jnp.minimum(i, m[3 * n_exp] - 1), 0)

    kern = functools.partial(_expert_kernel, n_exp=n_exp)
    return pl.pallas_call(
        kern,
        grid_spec=pltpu.PrefetchScalarGridSpec(
            num_scalar_prefetch=2,
            grid=(n_tiles,),
            in_specs=[pl.BlockSpec((tme * RECORD_ROWS, 128), x_map),
                      pl.BlockSpec((None, None, d, ff2), lambda i, te, m: (layer, te[i], 0, 0)),
                      pl.BlockSpec((None, None, ff, d), lambda i, te, m: (layer, te[i], 0, 0))],
            out_specs=pl.BlockSpec((tme * RECORD_ROWS, 128), lambda i, te, m: (i, 0))),
        out_shape=jax.ShapeDtypeStruct((n_rows * RECORD_ROWS, 128), U32),
        compiler_params=pltpu.CompilerParams(dimension_semantics=("arbitrary",),
                                             vmem_limit_bytes=V7X_VMEM_LIMIT_BYTES),
        name="moe_experts",
    )(tile_expert, meta, xs, w_in, w_out)


def _combine_kernel(pos_ref, h1_ref, gate_ref, ys_ref, g_ref, b_ref, out_ref, buf, sem, *, alpha):
    tm = h1_ref.shape[0]

    def start(r, c):
        for slot in range(2):
            p = pos_ref[0, 0, slot * tm + r]
            pltpu.make_async_copy(_record_slice(ys_ref, p), _record_slice(buf.at[slot], r), sem).start(priority=slot)
        return c

    lax.fori_loop(0, tm, start, 0, unroll=8)
    for slot in range(2):
        pltpu.make_async_copy(_record_slice(ys_ref, 0, tm), buf.at[slot], sem).wait()
    moe = None
    for slot in range(2):
        hi, lo = _load_records(buf.at[slot], tm)
        y = jnp.concatenate([hi, lo], axis=1).astype(F32) * gate_ref[:, slot:slot + 1]
        moe = y if moe is None else moe + y
    out_ref[...] = _layer_norm(alpha * h1_ref[...] + moe, g_ref[...], b_ref[...])


def _moe_combine(h1, ys, pos, gates, l2g, l2b, *, alpha):
    n_tok, d = h1.shape
    tm = min(TM_COMBINE, n_tok)
    assert n_tok % tm == 0
    kern = functools.partial(_combine_kernel, alpha=alpha)
    return pl.pallas_call(
        kern,
        grid=(n_tok // tm,),
        in_specs=[pl.BlockSpec((1, 1, 2 * tm), lambda i: (i, 0, 0), memory_space=pltpu.SMEM),
                  pl.BlockSpec((tm, d), lambda i: (i, 0)),
                  pl.BlockSpec((tm, 2), lambda i: (i, 0)),
                  pl.BlockSpec(memory_space=pl.ANY),
                  _const_spec((1, d)), _const_spec((1, d))],
        out_specs=pl.BlockSpec((tm, d), lambda i: (i, 0)),
        out_shape=jax.ShapeDtypeStruct((n_tok, d), F32),
        scratch_shapes=[pltpu.VMEM((2, tm * RECORD_ROWS, 128), U32), pltpu.SemaphoreType.DMA],
        compiler_params=pltpu.CompilerParams(dimension_semantics=("arbitrary",),
                                             vmem_limit_bytes=V7X_VMEM_LIMIT_BYTES),
        name="moe_combine",
    )(_tile_pos(pos, tm), h1, gates.T, ys, l2g.reshape(1, -1), l2b.reshape(1, -1))


def _moe_layer(h1, h1p, idx, gates, rank, cnt, w_in, w_out, l2g, l2b, *, layer, alpha):
    n_tok = h1.shape[0]
    n_exp = w_in.shape[1]
    tme = min(TM_EXPERT, n_tok)
    n_tiles = (2 * n_tok) // tme + n_exp
    pos, tile_expert, meta = _moe_plan(idx, rank, cnt, tme, n_tiles)
    xs = _moe_dispatch(h1p, pos, meta, n_exp=n_exp, tme=tme, n_tiles=n_tiles)
    ys = _moe_experts(xs, tile_expert, meta, w_in, w_out, layer=layer, tme=tme)
    return _moe_combine(h1, ys, pos, gates, l2g, l2b, alpha=alpha)


def kernel(x, a_w_in, a_b_in, a_ln_g, a_ln_b, a_w_s, a_b_s, a_w_out, b_w_in, b_a_re, b_a_im, b_log_dt, b_b_re, b_b_im, b_c_re, b_c_im, b_d, b_w_out, ln1_g, ln1_b, router_w, router_b, e_w_in, e_w_out, ln2_g, ln2_b):
    n_batch, seq, d = x.shape
    depth = ln1_g.shape[0]
    alpha = float((2 * depth) ** 0.25)
    rwt = router_w.astype(F32).T
    rb = router_b.astype(F32).reshape(-1, 1)
    h = x.reshape(n_batch * seq, d)
    e_w_in = e_w_in.astype(BF16)
    e_w_out = e_w_out.astype(BF16)
    for i in range(depth):
        j = i // 2
        if i % 2 == 0:
            mixed = _gmlp_layer(h, a_w_in[j], a_b_in[j], a_ln_g[j], a_ln_b[j], a_w_s[j], a_b_s[j],
                                a_w_out[j], ln1_g[i], ln1_b[i], rwt, rb, alpha=alpha)
        else:
            mixed = _ssm_layer(h, n_batch, b_w_in[j], b_a_re[j], b_a_im[j], b_log_dt[j], b_b_re[j],
                               b_b_im[j], b_c_re[j], b_c_im[j], b_d[j], b_w_out[j],
                               ln1_g[i], ln1_b[i], rwt, rb, alpha=alpha)
        h1, h1p, idx, gates, rank, cnt = mixed
        h = _moe_layer(h1, h1p, idx, gates, rank, cnt, e_w_in, e_w_out, ln2_g[i], ln2_b[i],
                       layer=i, alpha=alpha)
    return h.reshape(n_batch, seq, d)
```

```python
import functools
import math

import jax
import jax.numpy as jnp
from jax import lax
from jax.experimental import pallas as pl
from jax.experimental.pallas import tpu as pltpu

F32 = jnp.float32
BF16 = jnp.bfloat16
I32 = jnp.int32
U32 = jnp.uint32

LN_EPS = 1e-5
N_EXPERT_GROUPS = 4
SSM_CHUNK = 16

TM_MIX = 256
TM_PROJ = 512
TM_DISPATCH = 512
TM_EXPERT = 512
TM_COMBINE = 256

V7X_VMEM_LIMIT_BYTES = 56 * 1024 * 1024


def _dot(a, b):
    return jnp.dot(a, b, preferred_element_type=F32)


def _layer_norm(x, g, b):
    mu = jnp.mean(x, axis=-1, keepdims=True)
    xc = x - mu
    var = jnp.mean(xc * xc, axis=-1, keepdims=True)
    return xc * lax.rsqrt(var + LN_EPS) * g + b


def _const_spec(shape):
    return pl.BlockSpec(shape, lambda *_: (0,) * len(shape), pipeline_mode=pl.Buffered(1))


def _route(h1, rw_ref, rb_ref, idx_ref, gate_ref, rank_ref, cnt_ref):
    n_exp = rb_ref.shape[0]
    per_group = n_exp // N_EXPERT_GROUPS
    tm = h1.shape[0]
    h_hi = h1.astype(BF16)
    h_mid = (h1 - h_hi.astype(F32)).astype(BF16)
    first = _dot(h_hi, rw_ref[...])
    logits_t = first[:, :128] + first[:, 128:] + _dot(h_mid, rw_ref[:, :128])
    logits = logits_t.T[:n_exp, :] + rb_ref[...]
    m = jnp.max(logits, axis=0, keepdims=True)
    ex = jnp.exp(logits - m)
    probs = ex / jnp.sum(ex, axis=0, keepdims=True)

    best_score = None
    best_group = None
    for g in range(N_EXPERT_GROUPS):
        t1 = t2 = None
        for e in range(g * per_group, (g + 1) * per_group):
            v = probs[e:e + 1, :]
            if t1 is None:
                t1 = v
            elif t2 is None:
                t1, t2 = jnp.maximum(t1, v), jnp.minimum(t1, v)
            else:
                t2 = jnp.maximum(t2, jnp.minimum(t1, v))
                t1 = jnp.maximum(t1, v)
        score = t1 + t2
        if best_score is None:
            best_score, best_group = score, jnp.zeros((1, tm), I32)
        else:
            better = score > best_score
            best_group = jnp.where(better, g, best_group)
            best_score = jnp.where(better, score, best_score)

    eidx = lax.broadcasted_iota(I32, (n_exp, tm), 0)
    egrp = jnp.zeros((n_exp, tm), I32)
    for g in range(1, N_EXPERT_GROUPS):
        egrp = egrp + jnp.where(eidx >= g * per_group, 1, 0)
    in_group = egrp == best_group
    masked = jnp.where(in_group, probs, -1.0)
    m1 = jnp.max(masked, axis=0, keepdims=True)
    i1 = jnp.min(jnp.where(masked == m1, eidx, n_exp), axis=0, keepdims=True)
    sel1 = eidx == i1
    masked2 = jnp.where(sel1, -2.0, masked)
    m2 = jnp.max(masked2, axis=0, keepdims=True)
    i2 = jnp.min(jnp.where(masked2 == m2, eidx, n_exp), axis=0, keepdims=True)
    sel2 = eidx == i2
    den = m1 + m2

    cnt = jnp.where(sel1 | sel2, 1.0, 0.0)
    r_i = lax.broadcasted_iota(I32, (tm, tm), 0)
    c_i = lax.broadcasted_iota(I32, (tm, tm), 1)
    strict_upper = jnp.where(r_i < c_i, 1.0, 0.0).astype(BF16)
    prefix = _dot(cnt.astype(BF16), strict_upper) + cnt_ref[:, 0:1]
    r1 = jnp.sum(jnp.where(sel1, prefix, 0.0), axis=0, keepdims=True)
    r2 = jnp.sum(jnp.where(sel2, prefix, 0.0), axis=0, keepdims=True)
    cnt_ref[...] += jnp.sum(cnt, axis=1, keepdims=True)

    idx_ref[...] = jnp.concatenate([i1, i2], axis=0)
    gate_ref[...] = jnp.concatenate([m1 / den, m2 / den], axis=0)
    rank_ref[...] = jnp.concatenate([r1, r2], axis=0).astype(I32)


RECORD_ROWS = 8


def _store_records(ref, h):
    tm, d = h.shape
    half = d // 2
    assert half == RECORD_ROWS * 128
    bits = lax.bitcast_convert_type(h.astype(BF16).astype(F32), U32)
    packed = (bits[:, :half] & jnp.uint32(0xFFFF0000)) | (bits[:, half:] >> 16)
    for s in range(RECORD_ROWS):
        ref[pl.ds(s, tm, stride=RECORD_ROWS), :] = packed[:, s * 128:(s + 1) * 128]


def _load_records(ref, tm):
    w = jnp.concatenate([ref[pl.ds(s, tm, stride=RECORD_ROWS), :] for s in range(RECORD_ROWS)], axis=1)
    hi = lax.bitcast_convert_type(w & jnp.uint32(0xFFFF0000), F32).astype(BF16)
    lo = lax.bitcast_convert_type(w << 16, F32).astype(BF16)
    return hi, lo


def _router_out(n_tok, tm, n_exp):
    shapes = [jax.ShapeDtypeStruct((2, n_tok), I32),
              jax.ShapeDtypeStruct((2, n_tok), F32),
              jax.ShapeDtypeStruct((2, n_tok), I32),
              jax.ShapeDtypeStruct((n_exp, 128), F32)]
    specs = [pl.BlockSpec((2, tm), lambda i: (0, i)),
             pl.BlockSpec((2, tm), lambda i: (0, i)),
             pl.BlockSpec((2, tm), lambda i: (0, i)),
             pl.BlockSpec((n_exp, 128), lambda i: (0, 0))]
    return shapes, specs


def _gmlp_kernel(x_ref, win_ref, bin_ref, lng_ref, lnb_ref, ws_ref, bst_ref, wout_ref,
                 l1g_ref, l1b_ref, rw_ref, rb_ref,
                 h1_ref, h1p_ref, idx_ref, gate_ref, rank_ref, cnt_ref,
                 p_scr, *, alpha, chunk, groups):
    @pl.when(pl.program_id(0) == 0)
    def _():
        cnt_ref[...] = jnp.zeros_like(cnt_ref)

    x = x_ref[...]
    xb = x.astype(BF16)
    width = win_ref.shape[1] // 2
    gdim = width // groups
    tm = x.shape[0]

    zv = jax.nn.gelu(_dot(xb, win_ref[:, width:]) + bin_ref[:, width:])
    vn = _layer_norm(zv, lng_ref[...], lnb_ref[...]).astype(BF16)
    zu = jax.nn.gelu(_dot(xb, win_ref[:, :width]) + bin_ref[:, :width])

    r_i = lax.broadcasted_iota(I32, (chunk, chunk), 0)
    c_i = lax.broadcasted_iota(I32, (chunk, chunk), 1)
    causal = r_i >= c_i
    for g in range(groups):
        w = jnp.where(causal, ws_ref[g], 0.0).astype(BF16)
        bias = bst_ref[:, g:g + 1]
        for c in range(tm // chunk):
            rows = slice(c * chunk, (c + 1) * chunk)
            cols = slice(g * gdim, (g + 1) * gdim)
            s = _dot(w, vn[rows, cols]) + bias
            p_scr[rows, cols] = (zu[rows, cols] * s).astype(BF16)

    mix = _dot(p_scr[...], wout_ref[...])
    h1 = _layer_norm(alpha * x + mix, l1g_ref[...], l1b_ref[...])
    h1_ref[...] = h1
    _store_records(h1p_ref, h1)
    _route(h1, rw_ref, rb_ref, idx_ref, gate_ref, rank_ref, cnt_ref)


def _gmlp_layer(h, w_in, b_in, ln_g, ln_b, w_s, b_s, w_out, l1g, l1b, rw2, rb, *, alpha):
    n_tok, d = h.shape
    groups, chunk, _ = w_s.shape
    width = w_in.shape[1] // 2
    n_exp = rb.shape[0]
    tm = min(TM_MIX, n_tok)
    assert n_tok % tm == 0 and tm % chunk == 0
    r_shapes, r_specs = _router_out(n_tok, tm, n_exp)
    kern = functools.partial(_gmlp_kernel, alpha=alpha, chunk=chunk, groups=groups)
    return pl.pallas_call(
        kern,
        grid=(n_tok // tm,),
        in_specs=[pl.BlockSpec((tm, d), lambda i: (i, 0)),
                  _const_spec((d, 2 * width)), _const_spec((1, 2 * width)),
                  _const_spec((1, width)), _const_spec((1, width)),
                  _const_spec((groups, chunk, chunk)), _const_spec((chunk, groups)),
                  _const_spec((width, d)), _const_spec((1, d)), _const_spec((1, d)),
                  _const_spec((d, 256)), _const_spec((n_exp, 1))],
        out_specs=[pl.BlockSpec((tm, d), lambda i: (i, 0)),
                   pl.BlockSpec((tm * RECORD_ROWS, 128), lambda i: (i, 0))] + r_specs,
        out_shape=[jax.ShapeDtypeStruct((n_tok, d), F32),
                   jax.ShapeDtypeStruct((n_tok * RECORD_ROWS, 128), U32)] + r_shapes,
        scratch_shapes=[pltpu.VMEM((tm, width), BF16)],
        compiler_params=pltpu.CompilerParams(dimension_semantics=("arbitrary",),
                                             vmem_limit_bytes=V7X_VMEM_LIMIT_BYTES),
        name="gmlp_mixer",
    )(h, w_in.astype(BF16), b_in.reshape(1, -1), ln_g.reshape(1, -1), ln_b.reshape(1, -1),
      w_s, b_s.T, w_out.astype(BF16), l1g.reshape(1, -1), l1b.reshape(1, -1), rw2, rb)


SSM_LANE_GROUPS = 8
SSM_BATCH_PER_STEP = 4


def _proj_kernel(x_ref, w_ref, o_ref, res_scr, *, chunk):
    res = _dot(x_ref[...].astype(BF16), w_ref[...])
    n_slab = res_scr.shape[0]
    kb = o_ref.shape[0]
    for j in range(n_slab):
        res_scr[j] = res[:, j * 128:(j + 1) * 128]
    for t in range(chunk):
        for j in range(n_slab):
            o_ref[:, (t * n_slab + j) * 128:(t * n_slab + j + 1) * 128] = (
                res_scr[j, pl.ds(t, kb, stride=chunk), :].astype(o_ref.dtype))


def _ssm_in_proj(h, w, *, chunk):
    n_tok, d = h.shape
    width = w.shape[1]
    tm = min(TM_PROJ, n_tok)
    assert n_tok % tm == 0 and tm % chunk == 0 and width % 128 == 0
    kern = functools.partial(_proj_kernel, chunk=chunk)
    return pl.pallas_call(
        kern,
        grid=(n_tok // tm,),
        in_specs=[pl.BlockSpec((tm, d), lambda i: (i, 0)), _const_spec((d, width))],
        out_specs=pl.BlockSpec((tm // chunk, chunk * width), lambda i: (i, 0)),
        out_shape=jax.ShapeDtypeStruct((n_tok // chunk, chunk * width), BF16),
        scratch_shapes=[pltpu.VMEM((width // 128, tm, 128), F32)],
        compiler_params=pltpu.CompilerParams(dimension_semantics=("arbitrary",),
                                             vmem_limit_bytes=V7X_VMEM_LIMIT_BYTES),
        name="ssm_in_proj",
    )(h, w.astype(BF16))


def _ssm_kernel(*refs, chunk, n_batch, n_chunks):
    u_refs = refs[:chunk]
    d_ref, cin_ref, sout_ref, coef_ref, y_hbm, t_ref, sin_ref, w_scr, x_scr, y_scr, sem = refs[chunk:]
    gblk = pl.program_id(0)
    bblk = pl.program_id(1)
    rows = n_batch * n_chunks
    n_x = x_scr.shape[0]

    @pl.when(bblk == 0)
    def _():
        for tp in range(chunk):
            for t in range(chunk):
                blk = d_ref[t - tp] if t >= tp else jnp.zeros((128, 128), BF16)
                t_ref[tp * 128:(tp + 1) * 128, t * 128:(t + 1) * 128] = blk
        ch_bits = int(math.log2(128 // SSM_LANE_GROUPS))
        row = lax.broadcasted_iota(I32, (chunk * 128, 128), 0)
        row_group = lax.shift_right_logical(row, ch_bits) & (SSM_LANE_GROUPS - 1)
        for part in range(2 * n_x // SSM_LANE_GROUPS):
            src = cin_ref[:, part * 128:(part + 1) * 128]
            for g in range(SSM_LANE_GROUPS):
                col = (part * SSM_LANE_GROUPS + g) * 128
                sin_ref[:, col:col + 128] = jnp.where(row_group == g, src, jnp.zeros_like(src))

    u = jnp.concatenate([r[...] for r in u_refs], axis=1)
    for jb in range(n_x):
        w = _dot(u, sin_ref[:, jb * 256:(jb + 1) * 256])
        w_scr[2 * jb] = w[:, :128]
        w_scr[2 * jb + 1] = w[:, 128:]

    ar = [jnp.broadcast_to(coef_ref[0:1, j * 128:(j + 1) * 128], (n_batch, 128)) for j in range(n_x)]
    ai_a = [jnp.broadcast_to(coef_ref[1:2, j * 128:(j + 1) * 128], (n_batch, 128)) for j in range(n_x)]
    ai_b = [jnp.broadcast_to(coef_ref[2:3, j * 128:(j + 1) * 128], (n_batch, 128)) for j in range(n_x)]

    def step(k, carry):
        x, xsw = carry
        rows_k = pl.ds(k, n_batch, stride=n_chunks)
        new_x, new_xsw = [], []
        for j in range(n_x):
            x_scr[j, rows_k, :] = x[j]
            new_x.append(ar[j] * x[j] + ai_a[j] * xsw[j] + w_scr[j, rows_k, :])
            new_xsw.append(ar[j] * xsw[j] + ai_b[j] * x[j] + w_scr[n_x + j, rows_k, :])
        return tuple(new_x), tuple(new_xsw)

    zero = tuple(jnp.zeros((n_batch, 128), F32) for _ in range(n_x))
    lax.fori_loop(0, n_chunks, step, (zero, zero))

    xs = jnp.concatenate([x_scr[j] for j in range(n_x)], axis=1).astype(BF16)
    n_lane_blocks = y_hbm.shape[1] // (chunk * 128)

    def out_copy(t):
        col = (t * n_lane_blocks + gblk) * 128
        return pltpu.make_async_copy(
            y_scr.at[t],
            y_hbm.at[pl.ds(pl.multiple_of(bblk * rows, rows), rows), pl.ds(pl.multiple_of(col, 128), 128)],
            sem)

    for jb in range(chunk // 2):
        kk = (jb + 1) * 256
        cols = slice(jb * 256, (jb + 1) * 256)
        y = _dot(u[:, :kk], t_ref[:kk, cols]) + _dot(xs, sout_ref[:, cols])
        y = jax.nn.gelu(y).astype(BF16)
        for half in range(2):
            t = 2 * jb + half
            y_scr[t] = y[:, half * 128:(half + 1) * 128]
            out_copy(t).start()
    for t in range(chunk):
        out_copy(t).wait()


def _ssm_operators(a_re, a_im, log_dt, b_re, b_im, c_re, c_im, d_skip):
    hp = lax.Precision.HIGHEST
    n_grp, n_state, n_ch = b_re.shape
    L = SSM_CHUNK
    nb = SSM_LANE_GROUPS
    assert n_ch * nb == 128 and 2 * n_state == 128 and n_grp % nb == 0
    n_gb = n_grp // nb
    a = lax.complex(a_re.astype(F32), a_im.astype(F32))
    lam = a * jnp.exp(log_dt.astype(F32))[:, None]
    a_bar = jnp.exp(lam)
    b_bar = ((a_bar - 1.0) / a)[..., None] * lax.complex(b_re.astype(F32), b_im.astype(F32))
    cc = lax.complex(c_re.astype(F32), c_im.astype(F32))
    steps = jnp.arange(L + 1, dtype=F32)
    apow = jnp.exp(steps[:, None, None] * lam[None])
    eye_g = jnp.eye(nb, dtype=F32)

    kern = jnp.real(jnp.einsum('gcp,lgp,gpd->glcd', cc, apow[:L], b_bar, precision=hp))
    d_op = jnp.einsum('Bgtcd,gh->Btgdhc', kern.reshape(n_gb, nb, L, n_ch, n_ch), eye_g)
    d_op = d_op.reshape(n_gb, L, 128, 128)
    skip = jnp.eye(128, dtype=F32)[None] * d_skip.astype(F32).reshape(n_gb, 1, 128)
    d_op = d_op.at[:, 0].add(skip).astype(BF16)

    s_in = apow[:L][::-1].transpose(1, 0, 2)[:, :, None, :] * b_bar.transpose(0, 2, 1)[:, None, :, :]
    s_in = jnp.concatenate([s_in.real, s_in.imag, s_in.imag, s_in.real], axis=-1).astype(BF16)
    c_in = s_in.reshape(n_gb, nb, L, n_ch, 4 * n_state).transpose(0, 2, 1, 3, 4).reshape(n_gb, L * 128, 4 * n_state)

    s_out = cc.transpose(0, 2, 1)[:, :, None, :] * apow[1:].transpose(1, 2, 0)[:, :, :, None]
    s_out = jnp.concatenate([s_out.real, -s_out.imag], axis=1)
    x_idx = jnp.arange(L * n_ch)
    l_idx = jnp.arange(L * 128)
    place = ((x_idx[:, None] // n_ch == l_idx[None, :] // 128) & (x_idx[:, None] % n_ch == l_idx[None, :] % n_ch))
    place = place[None] & ((l_idx[None, None, :] // n_ch) % nb == jnp.arange(nb)[:, None, None])
    s_out = jnp.einsum('Bgqx,gxl->Bgql', s_out.reshape(n_gb, nb, 2 * n_state, L * n_ch), place.astype(F32),
                       precision=hp).astype(BF16).reshape(n_gb, nb * 2 * n_state, L * 128)

    a_l = apow[L]
    coef = jnp.stack([jnp.concatenate([a_l.real, a_l.real], -1),
                      jnp.concatenate([-a_l.imag, a_l.imag], -1),
                      jnp.concatenate([a_l.imag, -a_l.imag], -1)], axis=1)
    coef = coef.reshape(n_gb, nb, 3, 2 * n_state).transpose(0, 2, 1, 3).reshape(n_gb, 3, nb * 2 * n_state)
    coef = jnp.pad(coef, ((0, 0), (0, 5), (0, 0)))
    return d_op, c_in, s_out, coef


def _ssm_scan(u, d_op, c_in, s_out, coef, *, n_batch, chunk):
    rows_all, lanes_all = u.shape
    n_blk, lc, _ = c_in.shape
    n_chunks = rows_all // n_batch
    nb = min(SSM_BATCH_PER_STEP, n_batch)
    assert n_batch % nb == 0 and lanes_all == chunk * n_blk * 128
    rows = nb * n_chunks
    n_x = s_out.shape[1] // 128
    kern = functools.partial(_ssm_kernel, chunk=chunk, n_batch=nb, n_chunks=n_chunks)
    u_specs = [pl.BlockSpec((rows, 128), lambda g, b, t=t: (b, t * n_blk + g)) for t in range(chunk)]

    def op_spec(shape):
        return pl.BlockSpec((None,) + shape, lambda g, b: (g,) + (0,) * len(shape))

    return pl.pallas_call(
        kern,
        grid=(n_blk, n_batch // nb),
        in_specs=u_specs + [op_spec((chunk, 128, 128)), op_spec((lc, c_in.shape[2])),
                            op_spec((n_x * 128, lc)), op_spec((8, n_x * 128))],
        out_specs=pl.BlockSpec(memory_space=pl.ANY),
        out_shape=jax.ShapeDtypeStruct((rows_all, lanes_all), BF16),
        scratch_shapes=[pltpu.VMEM((lc, lc), BF16), pltpu.VMEM((lc, 2 * n_x * 128), BF16),
                        pltpu.VMEM((2 * n_x, rows, 128), F32), pltpu.VMEM((n_x, rows, 128), F32),
                        pltpu.VMEM((chunk, rows, 128), BF16), pltpu.SemaphoreType.DMA],
        compiler_params=pltpu.CompilerParams(dimension_semantics=("arbitrary", "arbitrary"),
                                             vmem_limit_bytes=V7X_VMEM_LIMIT_BYTES),
        name="ssm_scan",
    )(*([u] * chunk), d_op, c_in, s_out, coef)


def _ssm_out_kernel(y_ref, h_ref, wout_ref, l1g_ref, l1b_ref, rw_ref, rb_ref,
                    h1_ref, h1p_ref, idx_ref, gate_ref, rank_ref, cnt_ref, ytok_scr, *, alpha, chunk):
    @pl.when(pl.program_id(0) == 0)
    def _():
        cnt_ref[...] = jnp.zeros_like(cnt_ref)

    n_slab = ytok_scr.shape[0]
    kb = y_ref.shape[0]
    yf = y_ref[...].astype(F32)
    for t in range(chunk):
        for j in range(n_slab):
            ytok_scr[j, pl.ds(t, kb, stride=chunk), :] = yf[:, (t * n_slab + j) * 128:(t * n_slab + j + 1) * 128]
    y = jnp.concatenate([ytok_scr[j] for j in range(n_slab)], axis=1).astype(BF16)

    d = h_ref.shape[1]
    val = _dot(y, wout_ref[:, :d])
    gate = _dot(y, wout_ref[:, d:])
    mix = val * jax.nn.sigmoid(gate)
    h1 = _layer_norm(alpha * h_ref[...] + mix, l1g_ref[...], l1b_ref[...])
    h1_ref[...] = h1
    _store_records(h1p_ref, h1)
    _route(h1, rw_ref, rb_ref, idx_ref, gate_ref, rank_ref, cnt_ref)


def _ssm_layer(h, n_batch, w_in, a_re, a_im, log_dt, b_re, b_im, c_re, c_im, d_skip, w_out,
               l1g, l1b, rw2, rb, *, alpha):
    n_tok, d = h.shape
    n_grp, n_state, n_ch = b_re.shape
    width = n_grp * n_ch
    n_exp = rb.shape[0]
    L = SSM_CHUNK
    assert (n_tok // n_batch) % L == 0

    u = _ssm_in_proj(h, w_in, chunk=L)
    d_op, c_in, s_out, coef = _ssm_operators(a_re, a_im, log_dt, b_re, b_im, c_re, c_im, d_skip)
    y = _ssm_scan(u, d_op, c_in, s_out, coef, n_batch=n_batch, chunk=L)

    tm = min(TM_MIX, n_tok)
    assert n_tok % tm == 0 and tm % L == 0
    r_shapes, r_specs = _router_out(n_tok, tm, n_exp)
    kern = functools.partial(_ssm_out_kernel, alpha=alpha, chunk=L)
    return pl.pallas_call(
        kern,
        grid=(n_tok // tm,),
        in_specs=[pl.BlockSpec((tm // L, L * width), lambda i: (i, 0)),
                  pl.BlockSpec((tm, d), lambda i: (i, 0)),
                  _const_spec((width, 2 * d)), _const_spec((1, d)), _const_spec((1, d)),
                  _const_spec((d, 256)), _const_spec((n_exp, 1))],
        out_specs=[pl.BlockSpec((tm, d), lambda i: (i, 0)),
                   pl.BlockSpec((tm * RECORD_ROWS, 128), lambda i: (i, 0))] + r_specs,
        out_shape=[jax.ShapeDtypeStruct((n_tok, d), F32),
                   jax.ShapeDtypeStruct((n_tok * RECORD_ROWS, 128), U32)] + r_shapes,
        scratch_shapes=[pltpu.VMEM((width // 128, tm, 128), F32)],
        compiler_params=pltpu.CompilerParams(dimension_semantics=("arbitrary",),
                                             vmem_limit_bytes=V7X_VMEM_LIMIT_BYTES),
        name="ssm_out_mixer",
    )(y, h, w_out.astype(BF16), l1g.reshape(1, -1), l1b.reshape(1, -1), rw2, rb)


def _moe_plan(idx, rank, cnt, tme, n_tiles):
    n_exp = cnt.shape[0]
    counts = cnt[:, 0].astype(I32)
    padded = ((counts + tme - 1) // tme) * tme
    ends = jnp.cumsum(padded)
    offs = ends - padded
    n_valid = ends[-1] // tme
    tile_start = jnp.arange(n_tiles, dtype=I32) * tme
    tile_expert = jnp.minimum(jnp.sum(tile_start[:, None] >= ends[None, :], axis=1), n_exp - 1).astype(I32)
    onehot = idx[:, :, None] == jnp.arange(n_exp, dtype=I32)[None, None, :]
    pos = jnp.sum(jnp.where(onehot, offs[None, None, :], 0), axis=-1) + rank
    meta = jnp.concatenate([counts, padded, offs, n_valid[None]]).astype(I32)
    return pos, tile_expert, meta


def _tile_pos(pos, tm):
    n_tok = pos.shape[1]
    return pos.reshape(2, n_tok // tm, tm).transpose(1, 0, 2).reshape(n_tok // tm, 1, 2 * tm)


def _record_slice(ref, row, n_rec=1):
    return ref.at[pl.ds(pl.multiple_of(row * RECORD_ROWS, RECORD_ROWS), n_rec * RECORD_ROWS), :]


def _dispatch_kernel(meta_ref, pos_ref, x_ref, xs_ref, zero_scr, sem, *, n_exp, tme, n_tiles):
    i = pl.program_id(0)
    tm = x_ref.shape[0] // RECORD_ROWS
    half = zero_scr.shape[0] // RECORD_ROWS

    def zero_fill(start, size):
        cp = pltpu.make_async_copy(_record_slice(zero_scr, 0, size), _record_slice(xs_ref, start, size), sem)
        cp.start()
        cp.wait()

    @pl.when(i == 0)
    def _():
        zero_scr[...] = jnp.zeros_like(zero_scr)
        for e in range(n_exp):
            pad = meta_ref[n_exp + e] - meta_ref[e]
            base = meta_ref[2 * n_exp + e] + meta_ref[e]
            for b in range(int(math.log2(tme))):
                size = 1 << b

                @pl.when(((pad >> b) & 1) == 1)
                def _():
                    zero_fill(base + (pad & (size - 1)), size)
        n_valid = meta_ref[3 * n_exp]
        for j in range(n_exp):
            @pl.when(n_valid + j < n_tiles)
            def _():
                for part in range(tme // half):
                    zero_fill((n_valid + j) * tme + part * half, half)

    def start(r, c):
        for slot in range(2):
            p = pos_ref[0, 0, slot * tm + r]
            pltpu.make_async_copy(_record_slice(x_ref, r), _record_slice(xs_ref, p), sem).start(priority=slot)
        return c

    lax.fori_loop(0, tm, start, 0, unroll=8)
    for slot in range(2):
        pltpu.make_async_copy(x_ref, _record_slice(xs_ref, 0, tm), sem).wait()


def _moe_dispatch(h1p, pos, meta, *, n_exp, tme, n_tiles):
    n_tok = h1p.shape[0] // RECORD_ROWS
    tm = min(TM_DISPATCH, n_tok)
    assert n_tok % tm == 0
    half = max(tme // 2, 1)
    kern = functools.partial(_dispatch_kernel, n_exp=n_exp, tme=tme, n_tiles=n_tiles)
    return pl.pallas_call(
        kern,
        grid_spec=pltpu.PrefetchScalarGridSpec(
            num_scalar_prefetch=1,
            grid=(n_tok // tm,),
            in_specs=[pl.BlockSpec((1, 1, 2 * tm), lambda i, m: (i, 0, 0), memory_space=pltpu.SMEM),
                      pl.BlockSpec((tm * RECORD_ROWS, 128), lambda i, m: (i, 0))],
            out_specs=pl.BlockSpec(memory_space=pl.ANY),
            scratch_shapes=[pltpu.VMEM((half * RECORD_ROWS, 128), U32), pltpu.SemaphoreType.DMA]),
        out_shape=jax.ShapeDtypeStruct((n_tiles * tme * RECORD_ROWS, 128), U32),
        compiler_params=pltpu.CompilerParams(dimension_semantics=("arbitrary",),
                                             vmem_limit_bytes=V7X_VMEM_LIMIT_BYTES),
        name="moe_dispatch",
    )(meta, _tile_pos(pos, tm), h1p)


def _expert_kernel(te_ref, meta_ref, xs_ref, win_ref, wout_ref, ys_ref, *, n_exp):
    i = pl.program_id(0)
    n_valid = meta_ref[3 * n_exp]
    tme = xs_ref.shape[0] // RECORD_ROWS

    @pl.when(i < n_valid)
    def _():
        hi, lo = _load_records(xs_ref, tme)
        dh = hi.shape[1]
        ff = wout_ref.shape[0]
        h = _dot(hi, win_ref[:dh, :]) + _dot(lo, win_ref[dh:, :])
        act = (jax.nn.silu(h[:, :ff]) * h[:, ff:]).astype(BF16)
        _store_records(ys_ref, _dot(act, wout_ref[...]))

    @pl.when(i >= n_valid)
    def _():
        ys_ref[...] = jnp.zeros_like(ys_ref)


def _moe_experts(xs, tile_expert, meta, w_in, w_out, *, layer, tme):
    n_rows = xs.shape[0] // RECORD_ROWS
    _, n_exp, d, ff2 = w_in.shape
    ff = w_out.shape[2]
    n_tiles = n_rows // tme

    def x_map(i, te, m):
        return (jnp.minimum(i, m[3 * n_exp] - 1), 0)

    kern = functools.partial(_expert_kernel, n_exp=n_exp)
    return pl.pallas_call(
        kern,
        grid_spec=pltpu.PrefetchScalarGridSpec(
            num_scalar_prefetch=2,
            grid=(n_tiles,),
            in_specs=[pl.BlockSpec((tme * RECORD_ROWS, 128), x_map),
                      pl.BlockSpec((None, None, d, ff2), lambda i, te, m: (layer, te[i], 0, 0)),
                      pl.BlockSpec((None, None, ff, d), lambda i, te, m: (layer, te[i], 0, 0))],
            out_specs=pl.BlockSpec((tme * RECORD_ROWS, 128), lambda i, te, m: (i, 0))),
        out_shape=jax.ShapeDtypeStruct((n_rows * RECORD_ROWS, 128), U32),
        compiler_params=pltpu.CompilerParams(dimension_semantics=("arbitrary",),
                                             vmem_limit_bytes=V7X_VMEM_LIMIT_BYTES),
        name="moe_experts",
    )(tile_expert, meta, xs, w_in, w_out)


def _combine_kernel(pos_ref, pos_next_ref, h1_ref, gate_ref, ys_ref, g_ref, b_ref, out_ref, buf, sems, *, alpha):
    i = pl.program_id(0)
    n_steps = pl.num_programs(0)
    tm = h1_ref.shape[0]

    def gather(p_ref, half):
        def start(r, c):
            for slot in range(2):
                p = p_ref[0, 0, slot * tm + r]
                pltpu.make_async_copy(_record_slice(ys_ref, p), _record_slice(buf.at[half, slot], r),
                                      sems.at[half]).start(priority=slot)
            return c

        lax.fori_loop(0, tm, start, 0, unroll=8)

    @pl.when(i == 0)
    def _():
        gather(pos_ref, 0)

    @pl.when(i + 1 < n_steps)
    def _():
        gather(pos_next_ref, (i + 1) % 2)

    cur = i % 2
    moe = None
    for slot in range(2):
        pltpu.make_async_copy(_record_slice(ys_ref, 0, tm), buf.at[cur, slot], sems.at[cur]).wait()
    for slot in range(2):
        hi, lo = _load_records(buf.at[cur, slot], tm)
        y = jnp.concatenate([hi, lo], axis=1).astype(F32) * gate_ref[:, slot:slot + 1]
        moe = y if moe is None else moe + y
    out_ref[...] = _layer_norm(alpha * h1_ref[...] + moe, g_ref[...], b_ref[...])


def _moe_combine(h1, ys, pos, gates, l2g, l2b, *, alpha):
    n_tok, d = h1.shape
    tm = min(TM_COMBINE, n_tok)
    assert n_tok % tm == 0
    n_steps = n_tok // tm
    kern = functools.partial(_combine_kernel, alpha=alpha)
    pos_tiles = _tile_pos(pos, tm)
    return pl.pallas_call(
        kern,
        grid=(n_steps,),
        in_specs=[pl.BlockSpec((1, 1, 2 * tm), lambda i: (i, 0, 0), memory_space=pltpu.SMEM),
                  pl.BlockSpec((1, 1, 2 * tm), lambda i: (jnp.minimum(i + 1, n_steps - 1), 0, 0),
                               memory_space=pltpu.SMEM),
                  pl.BlockSpec((tm, d), lambda i: (i, 0)),
                  pl.BlockSpec((tm, 2), lambda i: (i, 0)),
                  pl.BlockSpec(memory_space=pl.ANY),
                  _const_spec((1, d)), _const_spec((1, d))],
        out_specs=pl.BlockSpec((tm, d), lambda i: (i, 0)),
        out_shape=jax.ShapeDtypeStruct((n_tok, d), F32),
        scratch_shapes=[pltpu.VMEM((2, 2, tm * RECORD_ROWS, 128), U32), pltpu.SemaphoreType.DMA((2,))],
        compiler_params=pltpu.CompilerParams(dimension_semantics=("arbitrary",),
                                             vmem_limit_bytes=V7X_VMEM_LIMIT_BYTES),
        name="moe_combine",
    )(pos_tiles, pos_tiles, h1, gates.T, ys, l2g.reshape(1, -1), l2b.reshape(1, -1))


def _moe_layer(h1, h1p, idx, gates, rank, cnt, w_in, w_out, l2g, l2b, *, layer, alpha):
    n_tok = h1.shape[0]
    n_exp = w_in.shape[1]
    tme = min(TM_EXPERT, n_tok)
    n_tiles = (2 * n_tok) // tme + n_exp
    pos, tile_expert, meta = _moe_plan(idx, rank, cnt, tme, n_tiles)
    xs = _moe_dispatch(h1p, pos, meta, n_exp=n_exp, tme=tme, n_tiles=n_tiles)
    ys = _moe_experts(xs, tile_expert, meta, w_in, w_out, layer=layer, tme=tme)
    return _moe_combine(h1, ys, pos, gates, l2g, l2b, alpha=alpha)


def kernel(x, a_w_in, a_b_in, a_ln_g, a_ln_b, a_w_s, a_b_s, a_w_out, b_w_in, b_a_re, b_a_im, b_log_dt, b_b_re, b_b_im, b_c_re, b_c_im, b_d, b_w_out, ln1_g, ln1_b, router_w, router_b, e_w_in, e_w_out, ln2_g, ln2_b):
    n_batch, seq, d = x.shape
    depth = ln1_g.shape[0]
    alpha = float((2 * depth) ** 0.25)
    n_exp = router_w.shape[1]
    rw = jnp.pad(router_w.astype(F32), ((0, 0), (0, 128 - n_exp)))
    rw_hi = rw.astype(BF16)
    rw2 = jnp.concatenate([rw_hi, (rw - rw_hi.astype(F32)).astype(BF16)], axis=1)
    rb = router_b.astype(F32).reshape(-1, 1)
    h = x.reshape(n_batch * seq, d)
    e_w_in = e_w_in.astype(BF16)
    e_w_out = e_w_out.astype(BF16)
    for i in range(depth):
        j = i // 2
        if i % 2 == 0:
            mixed = _gmlp_layer(h, a_w_in[j], a_b_in[j], a_ln_g[j], a_ln_b[j], a_w_s[j], a_b_s[j],
                                a_w_out[j], ln1_g[i], ln1_b[i], rw2, rb, alpha=alpha)
        else:
            mixed = _ssm_layer(h, n_batch, b_w_in[j], b_a_re[j], b_a_im[j], b_log_dt[j], b_b_re[j],
                               b_b_im[j], b_c_re[j], b_c_im[j], b_d[j], b_w_out[j],
                               ln1_g[i], ln1_b[i], rw2, rb, alpha=alpha)
        h1, h1p, idx, gates, rank, cnt = mixed
        h = _moe_layer(h1, h1p, idx, gates, rank, cnt, e_w_in, e_w_out, ln2_g[i], ln2_b[i],
                       layer=i, alpha=alpha)
    return h.reshape(n_batch, seq, d)
```

```python
import functools
import math

import jax
import jax.numpy as jnp
from jax import lax
from jax.experimental import pallas as pl
from jax.experimental.pallas import tpu as pltpu

F32 = jnp.float32
BF16 = jnp.bfloat16
I32 = jnp.int32
U32 = jnp.uint32

LN_EPS = 1e-5
N_EXPERT_GROUPS = 4
SSM_CHUNK = 16

TM_MIX = 256
TM_SSM_OUT = 512
TM_PROJ = 512
TM_DISPATCH = 512
TM_EXPERT = 512
TM_COMBINE = 256

V7X_VMEM_LIMIT_BYTES = 56 * 1024 * 1024


def _dot(a, b):
    return jnp.dot(a, b, preferred_element_type=F32)


def _layer_norm(x, g, b):
    mu = jnp.mean(x, axis=-1, keepdims=True)
    xc = x - mu
    var = jnp.mean(xc * xc, axis=-1, keepdims=True)
    return xc * lax.rsqrt(var + LN_EPS) * g + b


def _const_spec(shape):
    return pl.BlockSpec(shape, lambda *_: (0,) * len(shape), pipeline_mode=pl.Buffered(1))


def _route(h1, rw_ref, rb_ref, idx_ref, gate_ref, rank_ref, cnt_ref, live):
    n_exp = rb_ref.shape[0]
    per_group = n_exp // N_EXPERT_GROUPS
    tm = h1.shape[0]
    h_top = lax.bitcast_convert_type(lax.bitcast_convert_type(h1, U32) & jnp.uint32(0xFFFF0000), F32)
    h_hi = h_top.astype(BF16)
    h_mid = (h1 - h_top).astype(BF16)
    first = _dot(h_hi, rw_ref[...])
    logits_t = first[:, :128] + first[:, 128:] + _dot(h_mid, rw_ref[:, :128])
    logits = logits_t.T[:n_exp, :] + rb_ref[...]
    m = jnp.max(logits, axis=0, keepdims=True)
    ex = jnp.exp(logits - m)
    probs = ex / jnp.sum(ex, axis=0, keepdims=True)

    best_score = None
    best_group = None
    for g in range(N_EXPERT_GROUPS):
        t1 = t2 = None
        for e in range(g * per_group, (g + 1) * per_group):
            v = probs[e:e + 1, :]
            if t1 is None:
                t1 = v
            elif t2 is None:
                t1, t2 = jnp.maximum(t1, v), jnp.minimum(t1, v)
            else:
                t2 = jnp.maximum(t2, jnp.minimum(t1, v))
                t1 = jnp.maximum(t1, v)
        score = t1 + t2
        if best_score is None:
            best_score, best_group = score, jnp.zeros((1, tm), I32)
        else:
            better = score > best_score
            best_group = jnp.where(better, g, best_group)
            best_score = jnp.where(better, score, best_score)

    eidx = lax.broadcasted_iota(I32, (n_exp, tm), 0)
    egrp = jnp.zeros((n_exp, tm), I32)
    for g in range(1, N_EXPERT_GROUPS):
        egrp = egrp + jnp.where(eidx >= g * per_group, 1, 0)
    in_group = egrp == best_group
    masked = jnp.where(in_group, probs, -1.0)
    m1 = jnp.max(masked, axis=0, keepdims=True)
    i1 = jnp.min(jnp.where(masked == m1, eidx, n_exp), axis=0, keepdims=True)
    sel1 = eidx == i1
    masked2 = jnp.where(sel1, -2.0, masked)
    m2 = jnp.max(masked2, axis=0, keepdims=True)
    i2 = jnp.min(jnp.where(masked2 == m2, eidx, n_exp), axis=0, keepdims=True)
    sel2 = eidx == i2
    den = m1 + m2

    cnt = jnp.where(sel1 | sel2, live, 0.0)
    r_i = lax.broadcasted_iota(I32, (tm, tm), 0)
    c_i = lax.broadcasted_iota(I32, (tm, tm), 1)
    strict_upper = jnp.where(r_i < c_i, 1.0, 0.0).astype(BF16)
    prefix = _dot(cnt.astype(BF16), strict_upper) + cnt_ref[:, 0:1]
    r1 = jnp.sum(jnp.where(sel1, prefix, 0.0), axis=0, keepdims=True)
    r2 = jnp.sum(jnp.where(sel2, prefix, 0.0), axis=0, keepdims=True)
    cnt_ref[...] += jnp.sum(cnt, axis=1, keepdims=True)

    idx_ref[...] = jnp.concatenate([i1, i2], axis=0)
    gate_ref[...] = jnp.concatenate([m1 / den, m2 / den], axis=0)
    rank_ref[...] = jnp.concatenate([r1, r2], axis=0).astype(I32)


RECORD_ROWS = 8


def _store_records(ref, h):
    tm, d = h.shape
    half = d // 2
    assert half == RECORD_ROWS * 128
    bits = lax.bitcast_convert_type(h.astype(BF16).astype(F32), U32)
    packed = (bits[:, :half] & jnp.uint32(0xFFFF0000)) | (bits[:, half:] >> 16)
    for s in range(RECORD_ROWS):
        ref[pl.ds(s, tm, stride=RECORD_ROWS), :] = packed[:, s * 128:(s + 1) * 128]


def _load_records(ref, tm):
    w = jnp.concatenate([ref[pl.ds(s, tm, stride=RECORD_ROWS), :] for s in range(RECORD_ROWS)], axis=1)
    hi = lax.bitcast_convert_type(w & jnp.uint32(0xFFFF0000), F32).astype(BF16)
    lo = lax.bitcast_convert_type(w << 16, F32).astype(BF16)
    return hi, lo


def _router_out(n_tok, tm, n_exp):
    shapes = [jax.ShapeDtypeStruct((2, n_tok), I32),
              jax.ShapeDtypeStruct((2, n_tok), F32),
              jax.ShapeDtypeStruct((2, n_tok), I32),
              jax.ShapeDtypeStruct((n_exp, 128), F32)]
    specs = [pl.BlockSpec((2, tm), lambda i: (0, _prev(i))),
             pl.BlockSpec((2, tm), lambda i: (0, _prev(i))),
             pl.BlockSpec((2, tm), lambda i: (0, _prev(i))),
             pl.BlockSpec((n_exp, 128), lambda i: (0, 0))]
    return shapes, specs


def _prev(i):
    return jnp.maximum(i - 1, 0)


def _mixer_tail(res_scr, l1g_ref, l1b_ref, rw_ref, rb_ref, h1_ref, h1p_ref, idx_ref, gate_ref, rank_ref, cnt_ref):
    live = jnp.where(pl.program_id(0) > 0, 1.0, 0.0)
    h1 = _layer_norm(res_scr[...], l1g_ref[...], l1b_ref[...])
    h1_ref[...] = h1
    _store_records(h1p_ref, h1)
    _route(h1, rw_ref, rb_ref, idx_ref, gate_ref, rank_ref, cnt_ref, live)


def _gmlp_kernel(x_ref, win_ref, bin_ref, lng_ref, lnb_ref, ws_ref, bst_ref, wout_ref,
                 l1g_ref, l1b_ref, rw_ref, rb_ref,
                 h1_ref, h1p_ref, idx_ref, gate_ref, rank_ref, cnt_ref,
                 p_scr, res_scr, *, alpha, chunk, groups):
    @pl.when(pl.program_id(0) == 0)
    def _():
        cnt_ref[...] = jnp.zeros_like(cnt_ref)
        res_scr[...] = jnp.zeros_like(res_scr)

    _mixer_tail(res_scr, l1g_ref, l1b_ref, rw_ref, rb_ref, h1_ref, h1p_ref, idx_ref, gate_ref, rank_ref, cnt_ref)

    x = x_ref[...]
    xb = x.astype(BF16)
    width = win_ref.shape[1] // 2
    gdim = width // groups
    tm = x.shape[0]

    zv = jax.nn.gelu(_dot(xb, win_ref[:, width:]) + bin_ref[:, width:])
    vn = _layer_norm(zv, lng_ref[...], lnb_ref[...]).astype(BF16)
    zu = jax.nn.gelu(_dot(xb, win_ref[:, :width]) + bin_ref[:, :width])

    r_i = lax.broadcasted_iota(I32, (chunk, chunk), 0)
    c_i = lax.broadcasted_iota(I32, (chunk, chunk), 1)
    causal = r_i >= c_i
    for g in range(groups):
        w = jnp.where(causal, ws_ref[g], 0.0).astype(BF16)
        bias = bst_ref[:, g:g + 1]
        for c in range(tm // chunk):
            rows = slice(c * chunk, (c + 1) * chunk)
            cols = slice(g * gdim, (g + 1) * gdim)
            s = _dot(w, vn[rows, cols]) + bias
            p_scr[rows, cols] = (zu[rows, cols] * s).astype(BF16)

    res_scr[...] = alpha * x + _dot(p_scr[...], wout_ref[...])


def _gmlp_layer(h, w_in, b_in, ln_g, ln_b, w_s, b_s, w_out, l1g, l1b, rw2, rb, *, alpha):
    n_tok, d = h.shape
    groups, chunk, _ = w_s.shape
    width = w_in.shape[1] // 2
    n_exp = rb.shape[0]
    tm = min(TM_MIX, n_tok)
    assert n_tok % tm == 0 and tm % chunk == 0
    r_shapes, r_specs = _router_out(n_tok, tm, n_exp)
    kern = functools.partial(_gmlp_kernel, alpha=alpha, chunk=chunk, groups=groups)
    n_steps = n_tok // tm
    return pl.pallas_call(
        kern,
        grid=(n_steps + 1,),
        in_specs=[pl.BlockSpec((tm, d), lambda i: (jnp.minimum(i, n_steps - 1), 0)),
                  _const_spec((d, 2 * width)), _const_spec((1, 2 * width)),
                  _const_spec((1, width)), _const_spec((1, width)),
                  _const_spec((groups, chunk, chunk)), _const_spec((chunk, groups)),
                  _const_spec((width, d)), _const_spec((1, d)), _const_spec((1, d)),
                  _const_spec((d, 256)), _const_spec((n_exp, 1))],
        out_specs=[pl.BlockSpec((tm, d), lambda i: (_prev(i), 0)),
                   pl.BlockSpec((tm * RECORD_ROWS, 128), lambda i: (_prev(i), 0))] + r_specs,
        out_shape=[jax.ShapeDtypeStruct((n_tok, d), F32),
                   jax.ShapeDtypeStruct((n_tok * RECORD_ROWS, 128), U32)] + r_shapes,
        scratch_shapes=[pltpu.VMEM((tm, width), BF16), pltpu.VMEM((tm, d), F32)],
        compiler_params=pltpu.CompilerParams(dimension_semantics=("arbitrary",),
                                             vmem_limit_bytes=V7X_VMEM_LIMIT_BYTES),
        name="gmlp_mixer",
    )(h, w_in.astype(BF16), b_in.reshape(1, -1), ln_g.reshape(1, -1), ln_b.reshape(1, -1),
      w_s, b_s.T, w_out.astype(BF16), l1g.reshape(1, -1), l1b.reshape(1, -1), rw2, rb)


SSM_LANE_GROUPS = 8
SSM_BATCH_PER_STEP = 4


def _proj_kernel(x_ref, w_ref, o_ref, res_scr, *, chunk):
    res = _dot(x_ref[...].astype(BF16), w_ref[...])
    n_slab = res_scr.shape[0]
    kb = o_ref.shape[0]
    for j in range(n_slab):
        res_scr[j] = res[:, j * 128:(j + 1) * 128]
    for t in range(chunk):
        for j in range(n_slab):
            o_ref[:, (t * n_slab + j) * 128:(t * n_slab + j + 1) * 128] = (
                res_scr[j, pl.ds(t, kb, stride=chunk), :].astype(o_ref.dtype))


def _ssm_in_proj(h, w, *, chunk):
    n_tok, d = h.shape
    width = w.shape[1]
    tm = min(TM_PROJ, n_tok)
    assert n_tok % tm == 0 and tm % chunk == 0 and width % 128 == 0
    kern = functools.partial(_proj_kernel, chunk=chunk)
    return pl.pallas_call(
        kern,
        grid=(n_tok // tm,),
        in_specs=[pl.BlockSpec((tm, d), lambda i: (i, 0)), _const_spec((d, width))],
        out_specs=pl.BlockSpec((tm // chunk, chunk * width), lambda i: (i, 0)),
        out_shape=jax.ShapeDtypeStruct((n_tok // chunk, chunk * width), BF16),
        scratch_shapes=[pltpu.VMEM((width // 128, tm, 128), F32)],
        compiler_params=pltpu.CompilerParams(dimension_semantics=("arbitrary",),
                                             vmem_limit_bytes=V7X_VMEM_LIMIT_BYTES),
        name="ssm_in_proj",
    )(h, w.astype(BF16))


def _ssm_kernel(*refs, chunk, n_batch, n_chunks):
    u_refs = refs[:chunk]
    d_ref, cin_ref, sout_ref, coef_ref, y_ref, t_ref, sin_ref, w_scr, x_scr, ytok_scr = refs[chunk:]
    bblk = pl.program_id(1)
    rows = n_batch * n_chunks
    n_x = x_scr.shape[0]

    @pl.when(bblk == 0)
    def _():
        for tp in range(chunk):
            for t in range(chunk):
                blk = d_ref[t - tp] if t >= tp else jnp.zeros((128, 128), BF16)
                t_ref[tp * 128:(tp + 1) * 128, t * 128:(t + 1) * 128] = blk
        ch_bits = int(math.log2(128 // SSM_LANE_GROUPS))
        row = lax.broadcasted_iota(I32, (chunk * 128, 128), 0)
        row_group = lax.shift_right_logical(row, ch_bits) & (SSM_LANE_GROUPS - 1)
        for part in range(2 * n_x // SSM_LANE_GROUPS):
            src = cin_ref[:, part * 128:(part + 1) * 128]
            for g in range(SSM_LANE_GROUPS):
                col = (part * SSM_LANE_GROUPS + g) * 128
                sin_ref[:, col:col + 128] = jnp.where(row_group == g, src, jnp.zeros_like(src))

    u = jnp.concatenate([r[...] for r in u_refs], axis=1)
    for jb in range(n_x):
        w = _dot(u, sin_ref[:, jb * 256:(jb + 1) * 256])
        w_scr[2 * jb] = w[:, :128]
        w_scr[2 * jb + 1] = w[:, 128:]

    ar = [jnp.broadcast_to(coef_ref[0:1, j * 128:(j + 1) * 128], (n_batch, 128)) for j in range(n_x)]
    ai_a = [jnp.broadcast_to(coef_ref[1:2, j * 128:(j + 1) * 128], (n_batch, 128)) for j in range(n_x)]
    ai_b = [jnp.broadcast_to(coef_ref[2:3, j * 128:(j + 1) * 128], (n_batch, 128)) for j in range(n_x)]

    def step(k, carry):
        x, xsw = carry
        rows_k = pl.ds(k, n_batch, stride=n_chunks)
        new_x, new_xsw = [], []
        for j in range(n_x):
            x_scr[j, rows_k, :] = x[j]
            new_x.append(ar[j] * x[j] + ai_a[j] * xsw[j] + w_scr[j, rows_k, :])
            new_xsw.append(ar[j] * xsw[j] + ai_b[j] * x[j] + w_scr[n_x + j, rows_k, :])
        return tuple(new_x), tuple(new_xsw)

    zero = tuple(jnp.zeros((n_batch, 128), F32) for _ in range(n_x))
    lax.fori_loop(0, n_chunks, step, (zero, zero))

    xs = jnp.concatenate([x_scr[j] for j in range(n_x)], axis=1).astype(BF16)
    for jb in range(chunk // 2):
        kk = (jb + 1) * 256
        cols = slice(jb * 256, (jb + 1) * 256)
        y = jax.nn.gelu(_dot(u[:, :kk], t_ref[:kk, cols]) + _dot(xs, sout_ref[:, cols]))
        for half in range(2):
            ytok_scr[pl.ds(2 * jb + half, rows, stride=chunk), :] = y[:, half * 128:(half + 1) * 128]
    y_ref[...] = ytok_scr[...].astype(y_ref.dtype)


def _ssm_operators(a_re, a_im, log_dt, b_re, b_im, c_re, c_im, d_skip):
    hp = lax.Precision.HIGHEST
    n_grp, n_state, n_ch = b_re.shape
    L = SSM_CHUNK
    nb = SSM_LANE_GROUPS
    assert n_ch * nb == 128 and 2 * n_state == 128 and n_grp % nb == 0
    n_gb = n_grp // nb
    a = lax.complex(a_re.astype(F32), a_im.astype(F32))
    lam = a * jnp.exp(log_dt.astype(F32))[:, None]
    a_bar = jnp.exp(lam)
    b_bar = ((a_bar - 1.0) / a)[..., None] * lax.complex(b_re.astype(F32), b_im.astype(F32))
    cc = lax.complex(c_re.astype(F32), c_im.astype(F32))
    steps = jnp.arange(L + 1, dtype=F32)
    apow = jnp.exp(steps[:, None, None] * lam[None])
    eye_g = jnp.eye(nb, dtype=F32)

    kern = jnp.real(jnp.einsum('gcp,lgp,gpd->glcd', cc, apow[:L], b_bar, precision=hp))
    d_op = jnp.einsum('Bgtcd,gh->Btgdhc', kern.reshape(n_gb, nb, L, n_ch, n_ch), eye_g)
    d_op = d_op.reshape(n_gb, L, 128, 128)
    skip = jnp.eye(128, dtype=F32)[None] * d_skip.astype(F32).reshape(n_gb, 1, 128)
    d_op = d_op.at[:, 0].add(skip).astype(BF16)

    s_in = apow[:L][::-1].transpose(1, 0, 2)[:, :, None, :] * b_bar.transpose(0, 2, 1)[:, None, :, :]
    s_in = jnp.concatenate([s_in.real, s_in.imag, s_in.imag, s_in.real], axis=-1).astype(BF16)
    c_in = s_in.reshape(n_gb, nb, L, n_ch, 4 * n_state).transpose(0, 2, 1, 3, 4).reshape(n_gb, L * 128, 4 * n_state)

    s_out = cc.transpose(0, 2, 1)[:, :, None, :] * apow[1:].transpose(1, 2, 0)[:, :, :, None]
    s_out = jnp.concatenate([s_out.real, -s_out.imag], axis=1)
    x_idx = jnp.arange(L * n_ch)
    l_idx = jnp.arange(L * 128)
    place = ((x_idx[:, None] // n_ch == l_idx[None, :] // 128) & (x_idx[:, None] % n_ch == l_idx[None, :] % n_ch))
    place = place[None] & ((l_idx[None, None, :] // n_ch) % nb == jnp.arange(nb)[:, None, None])
    s_out = jnp.einsum('Bgqx,gxl->Bgql', s_out.reshape(n_gb, nb, 2 * n_state, L * n_ch), place.astype(F32),
                       precision=hp).astype(BF16).reshape(n_gb, nb * 2 * n_state, L * 128)

    a_l = apow[L]
    coef = jnp.stack([jnp.concatenate([a_l.real, a_l.real], -1),
                      jnp.concatenate([-a_l.imag, a_l.imag], -1),
                      jnp.concatenate([a_l.imag, -a_l.imag], -1)], axis=1)
    coef = coef.reshape(n_gb, nb, 3, 2 * n_state).transpose(0, 2, 1, 3).reshape(n_gb, 3, nb * 2 * n_state)
    coef = jnp.pad(coef, ((0, 0), (0, 5), (0, 0)))
    return d_op, c_in, s_out, coef


def _ssm_scan(u, d_op, c_in, s_out, coef, *, n_batch, chunk):
    rows_all, lanes_all = u.shape
    n_blk, lc, _ = c_in.shape
    n_chunks = rows_all // n_batch
    nb = min(SSM_BATCH_PER_STEP, n_batch)
    assert n_batch % nb == 0 and lanes_all == chunk * n_blk * 128
    rows = nb * n_chunks
    n_x = s_out.shape[1] // 128
    kern = functools.partial(_ssm_kernel, chunk=chunk, n_batch=nb, n_chunks=n_chunks)
    u_specs = [pl.BlockSpec((rows, 128), lambda g, b, t=t: (b, t * n_blk + g)) for t in range(chunk)]

    def op_spec(shape):
        return pl.BlockSpec((None,) + shape, lambda g, b: (g,) + (0,) * len(shape))

    return pl.pallas_call(
        kern,
        grid=(n_blk, n_batch // nb),
        in_specs=u_specs + [op_spec((chunk, 128, 128)), op_spec((lc, c_in.shape[2])),
                            op_spec((n_x * 128, lc)), op_spec((8, n_x * 128))],
        out_specs=pl.BlockSpec((rows * chunk, 128), lambda g, b: (b, g)),
        out_shape=jax.ShapeDtypeStruct((rows_all * chunk, n_blk * 128), BF16),
        scratch_shapes=[pltpu.VMEM((lc, lc), BF16), pltpu.VMEM((lc, 2 * n_x * 128), BF16),
                        pltpu.VMEM((2 * n_x, rows, 128), F32), pltpu.VMEM((n_x, rows, 128), F32),
                        pltpu.VMEM((rows * chunk, 128), F32)],
        compiler_params=pltpu.CompilerParams(dimension_semantics=("arbitrary", "arbitrary"),
                                             vmem_limit_bytes=V7X_VMEM_LIMIT_BYTES),
        name="ssm_scan",
    )(*([u] * chunk), d_op, c_in, s_out, coef)


def _ssm_out_kernel(y_ref, h_ref, wout_ref, l1g_ref, l1b_ref, rw_ref, rb_ref,
                    h1_ref, h1p_ref, idx_ref, gate_ref, rank_ref, cnt_ref, res_scr, *, alpha):
    @pl.when(pl.program_id(0) == 0)
    def _():
        cnt_ref[...] = jnp.zeros_like(cnt_ref)
        res_scr[...] = jnp.zeros_like(res_scr)

    _mixer_tail(res_scr, l1g_ref, l1b_ref, rw_ref, rb_ref, h1_ref, h1p_ref, idx_ref, gate_ref, rank_ref, cnt_ref)

    y = y_ref[...]
    d = h_ref.shape[1]
    val = _dot(y, wout_ref[:, :d])
    gate = _dot(y, wout_ref[:, d:])
    res_scr[...] = alpha * h_ref[...] + val * jax.nn.sigmoid(gate)


def _ssm_layer(h, n_batch, w_in, a_re, a_im, log_dt, b_re, b_im, c_re, c_im, d_skip, w_out,
               l1g, l1b, rw2, rb, *, alpha):
    n_tok, d = h.shape
    n_grp, n_state, n_ch = b_re.shape
    width = n_grp * n_ch
    n_exp = rb.shape[0]
    L = SSM_CHUNK
    assert (n_tok // n_batch) % L == 0

    u = _ssm_in_proj(h, w_in, chunk=L)
    d_op, c_in, s_out, coef = _ssm_operators(a_re, a_im, log_dt, b_re, b_im, c_re, c_im, d_skip)
    y = _ssm_scan(u, d_op, c_in, s_out, coef, n_batch=n_batch, chunk=L)

    tm = min(TM_SSM_OUT, n_tok)
    assert n_tok % tm == 0
    r_shapes, r_specs = _router_out(n_tok, tm, n_exp)
    kern = functools.partial(_ssm_out_kernel, alpha=alpha)
    n_steps = n_tok // tm
    return pl.pallas_call(
        kern,
        grid=(n_steps + 1,),
        in_specs=[pl.BlockSpec((tm, width), lambda i: (jnp.minimum(i, n_steps - 1), 0)),
                  pl.BlockSpec((tm, d), lambda i: (jnp.minimum(i, n_steps - 1), 0)),
                  _const_spec((width, 2 * d)), _const_spec((1, d)), _const_spec((1, d)),
                  _const_spec((d, 256)), _const_spec((n_exp, 1))],
        out_specs=[pl.BlockSpec((tm, d), lambda i: (_prev(i), 0)),
                   pl.BlockSpec((tm * RECORD_ROWS, 128), lambda i: (_prev(i), 0))] + r_specs,
        out_shape=[jax.ShapeDtypeStruct((n_tok, d), F32),
                   jax.ShapeDtypeStruct((n_tok * RECORD_ROWS, 128), U32)] + r_shapes,
        scratch_shapes=[pltpu.VMEM((tm, d), F32)],
        compiler_params=pltpu.CompilerParams(dimension_semantics=("arbitrary",),
                                             vmem_limit_bytes=V7X_VMEM_LIMIT_BYTES),
        name="ssm_out_mixer",
    )(y, h, w_out.astype(BF16), l1g.reshape(1, -1), l1b.reshape(1, -1), rw2, rb)


def _moe_plan(idx, rank, cnt, tme, n_tiles):
    n_exp = cnt.shape[0]
    counts = cnt[:, 0].astype(I32)
    padded = ((counts + tme - 1) // tme) * tme
    ends = jnp.cumsum(padded)
    offs = ends - padded
    n_valid = ends[-1] // tme
    tile_start = jnp.arange(n_tiles, dtype=I32) * tme
    tile_expert = jnp.minimum(jnp.sum(tile_start[:, None] >= ends[None, :], axis=1), n_exp - 1).astype(I32)
    onehot = idx[:, :, None] == jnp.arange(n_exp, dtype=I32)[None, None, :]
    pos = jnp.sum(jnp.where(onehot, offs[None, None, :], 0), axis=-1) + rank
    meta = jnp.concatenate([counts, padded, offs, n_valid[None]]).astype(I32)
    return pos, tile_expert, meta


def _tile_pos(pos, tm):
    n_tok = pos.shape[1]
    return pos.reshape(2, n_tok // tm, tm).transpose(1, 0, 2).reshape(n_tok // tm, 1, 2 * tm)


def _record_slice(ref, row, n_rec=1):
    return ref.at[pl.ds(pl.multiple_of(row * RECORD_ROWS, RECORD_ROWS), n_rec * RECORD_ROWS), :]


def _dispatch_kernel(meta_ref, pos_ref, x_ref, xs_ref, zero_scr, sem, *, n_exp, tme, n_tiles):
    i = pl.program_id(0)
    tm = x_ref.shape[0] // RECORD_ROWS
    half = zero_scr.shape[0] // RECORD_ROWS

    def zero_fill(start, size):
        cp = pltpu.make_async_copy(_record_slice(zero_scr, 0, size), _record_slice(xs_ref, start, size), sem)
        cp.start()
        cp.wait()

    @pl.when(i == 0)
    def _():
        zero_scr[...] = jnp.zeros_like(zero_scr)
        for e in range(n_exp):
            pad = meta_ref[n_exp + e] - meta_ref[e]
            base = meta_ref[2 * n_exp + e] + meta_ref[e]
            for b in range(int(math.log2(tme))):
                size = 1 << b

                @pl.when(((pad >> b) & 1) == 1)
                def _():
                    zero_fill(base + (pad & (size - 1)), size)
        n_valid = meta_ref[3 * n_exp]
        for j in range(n_exp):
            @pl.when(n_valid + j < n_tiles)
            def _():
                for part in range(tme // half):
                    zero_fill((n_valid + j) * tme + part * half, half)

    def start(r, c):
        for slot in range(2):
            p = pos_ref[0, 0, slot * tm + r]
            pltpu.make_async_copy(_record_slice(x_ref, r), _record_slice(xs_ref, p), sem).start(priority=slot)
        return c

    lax.fori_loop(0, tm, start, 0, unroll=8)
    for slot in range(2):
        pltpu.make_async_copy(x_ref, _record_slice(xs_ref, 0, tm), sem).wait()


def _moe_dispatch(h1p, pos, meta, *, n_exp, tme, n_tiles):
    n_tok = h1p.shape[0] // RECORD_ROWS
    tm = min(TM_DISPATCH, n_tok)
    assert n_tok % tm == 0
    half = max(tme // 2, 1)
    kern = functools.partial(_dispatch_kernel, n_exp=n_exp, tme=tme, n_tiles=n_tiles)
    return pl.pallas_call(
        kern,
        grid_spec=pltpu.PrefetchScalarGridSpec(
            num_scalar_prefetch=1,
            grid=(n_tok // tm,),
            in_specs=[pl.BlockSpec((1, 1, 2 * tm), lambda i, m: (i, 0, 0), memory_space=pltpu.SMEM),
                      pl.BlockSpec((tm * RECORD_ROWS, 128), lambda i, m: (i, 0))],
            out_specs=pl.BlockSpec(memory_space=pl.ANY),
            scratch_shapes=[pltpu.VMEM((half * RECORD_ROWS, 128), U32), pltpu.SemaphoreType.DMA]),
        out_shape=jax.ShapeDtypeStruct((n_tiles * tme * RECORD_ROWS, 128), U32),
        compiler_params=pltpu.CompilerParams(dimension_semantics=("arbitrary",),
                                             vmem_limit_bytes=V7X_VMEM_LIMIT_BYTES),
        name="moe_dispatch",
    )(meta, _tile_pos(pos, tm), h1p)


def _expert_kernel(te_ref, meta_ref, xs_ref, win_ref, wout_ref, ys_ref, *, n_exp):
    i = pl.program_id(0)
    n_valid = meta_ref[3 * n_exp]
    tme = xs_ref.shape[0] // RECORD_ROWS

    @pl.when(i < n_valid)
    def _():
        hi, lo = _load_records(xs_ref, tme)
        dh = hi.shape[1]
        ff = wout_ref.shape[0]
        h = _dot(hi, win_ref[:dh, :]) + _dot(lo, win_ref[dh:, :])
        act = (jax.nn.silu(h[:, :ff]) * h[:, ff:]).astype(BF16)
        _store_records(ys_ref, _dot(act, wout_ref[...]))

    @pl.when(i >= n_valid)
    def _():
        ys_ref[...] = jnp.zeros_like(ys_ref)


def _moe_experts(xs, tile_expert, meta, w_in, w_out, *, layer, tme):
    n_rows = xs.shape[0] // RECORD_ROWS
    _, n_exp, d, ff2 = w_in.shape
    ff = w_out.shape[2]
    n_tiles = n_rows // tme

    def x_map(i, te, m):
        return (jnp.minimum(i, m[3 * n_exp] - 1), 0)

    kern = functools.partial(_expert_kernel, n_exp=n_exp)
    return pl.pallas_call(
        kern,
        grid_spec=pltpu.PrefetchScalarGridSpec(
            num_scalar_prefetch=2,
            grid=(n_tiles,),
            in_specs=[pl.BlockSpec((tme * RECORD_ROWS, 128), x_map),
                      pl.BlockSpec((None, None, d, ff2), lambda i, te, m: (layer, te[i], 0, 0)),
                      pl.BlockSpec((None, None, ff, d), lambda i, te, m: (layer, te[i], 0, 0))],
            out_specs=pl.BlockSpec((tme * RECORD_ROWS, 128), lambda i, te, m: (i, 0))),
        out_shape=jax.ShapeDtypeStruct((n_rows * RECORD_ROWS, 128), U32),
        compiler_params=pltpu.CompilerParams(dimension_semantics=("arbitrary",),
                                             vmem_limit_bytes=V7X_VMEM_LIMIT_BYTES),
        name="moe_experts",
    )(tile_expert, meta, xs, w_in, w_out)


def _combine_kernel(pos_ref, pos_next_ref, h1_ref, gate_ref, ys_ref, g_ref, b_ref, out_ref, buf, sems, *, alpha):
    i = pl.program_id(0)
    n_steps = pl.num_programs(0)
    tm = h1_ref.shape[0]

    def gather(p_ref, half):
        def start(r, c):
            for slot in range(2):
                p = p_ref[0, 0, slot * tm + r]
                pltpu.make_async_copy(_record_slice(ys_ref, p), _record_slice(buf.at[half, slot], r),
                                      sems.at[half]).start(priority=slot)
            return c

        lax.fori_loop(0, tm, start, 0, unroll=8)

    @pl.when(i == 0)
    def _():
        gather(pos_ref, 0)

    @pl.when(i + 1 < n_steps)
    def _():
        gather(pos_next_ref, (i + 1) % 2)

    cur = i % 2
    moe = None
    for slot in range(2):
        pltpu.make_async_copy(_record_slice(ys_ref, 0, tm), buf.at[cur, slot], sems.at[cur]).wait()
    for slot in range(2):
        hi, lo = _load_records(buf.at[cur, slot], tm)
        y = jnp.concatenate([hi, lo], axis=1).astype(F32) * gate_ref[:, slot:slot + 1]
        moe = y if moe is None else moe + y
    out_ref[...] = _layer_norm(alpha * h1_ref[...] + moe, g_ref[...], b_ref[...])


def _moe_combine(h1, ys, pos, gates, l2g, l2b, *, alpha):
    n_tok, d = h1.shape
    tm = min(TM_COMBINE, n_tok)
    assert n_tok % tm == 0
    n_steps = n_tok // tm
    kern = functools.partial(_combine_kernel, alpha=alpha)
    pos_tiles = _tile_pos(pos, tm)
    return pl.pallas_call(
        kern,
        grid=(n_steps,),
        in_specs=[pl.BlockSpec((1, 1, 2 * tm), lambda i: (i, 0, 0), memory_space=pltpu.SMEM),
                  pl.BlockSpec((1, 1, 2 * tm), lambda i: (jnp.minimum(i + 1, n_steps - 1), 0, 0),
                               memory_space=pltpu.SMEM),
                  pl.BlockSpec((tm, d), lambda i: (i, 0)),
                  pl.BlockSpec((tm, 2), lambda i: (i, 0)),
                  pl.BlockSpec(memory_space=pl.ANY),
                  _const_spec((1, d)), _const_spec((1, d))],
        out_specs=pl.BlockSpec((tm, d), lambda i: (i, 0)),
        out_shape=jax.ShapeDtypeStruct((n_tok, d), F32),
        scratch_shapes=[pltpu.VMEM((2, 2, tm * RECORD_ROWS, 128), U32), pltpu.SemaphoreType.DMA((2,))],
        compiler_params=pltpu.CompilerParams(dimension_semantics=("arbitrary",),
                                             vmem_limit_bytes=V7X_VMEM_LIMIT_BYTES),
        name="moe_combine",
    )(pos_tiles, pos_tiles, h1, gates.T, ys, l2g.reshape(1, -1), l2b.reshape(1, -1))


def _moe_layer(h1, h1p, idx, gates, rank, cnt, w_in, w_out, l2g, l2b, *, layer, alpha):
    n_tok = h1.shape[0]
    n_exp = w_in.shape[1]
    tme = min(TM_EXPERT, n_tok)
    n_tiles = (2 * n_tok) // tme + n_exp
    pos, tile_expert, meta = _moe_plan(idx, rank, cnt, tme, n_tiles)
    xs = _moe_dispatch(h1p, pos, meta, n_exp=n_exp, tme=tme, n_tiles=n_tiles)
    ys = _moe_experts(xs, tile_expert, meta, w_in, w_out, layer=layer, tme=tme)
    return _moe_combine(h1, ys, pos, gates, l2g, l2b, alpha=alpha)


def kernel(x, a_w_in, a_b_in, a_ln_g, a_ln_b, a_w_s, a_b_s, a_w_out, b_w_in, b_a_re, b_a_im, b_log_dt, b_b_re, b_b_im, b_c_re, b_c_im, b_d, b_w_out, ln1_g, ln1_b, router_w, router_b, e_w_in, e_w_out, ln2_g, ln2_b):
    n_batch, seq, d = x.shape
    depth = ln1_g.shape[0]
    alpha = float((2 * depth) ** 0.25)
    n_exp = router_w.shape[1]
    rw = jnp.pad(router_w.astype(F32), ((0, 0), (0, 128 - n_exp)))
    rw_hi = lax.reduce_precision(rw, exponent_bits=8, mantissa_bits=7)
    rw2 = jnp.concatenate([rw_hi, rw - rw_hi], axis=1).astype(BF16)
    rb = router_b.astype(F32).reshape(-1, 1)
    h = x.reshape(n_batch * seq, d)
    e_w_in = e_w_in.astype(BF16)
    e_w_out = e_w_out.astype(BF16)
    for i in range(depth):
        j = i // 2
        if i % 2 == 0:
            mixed = _gmlp_layer(h, a_w_in[j], a_b_in[j], a_ln_g[j], a_ln_b[j], a_w_s[j], a_b_s[j],
                                a_w_out[j], ln1_g[i], ln1_b[i], rw2, rb, alpha=alpha)
        else:
            mixed = _ssm_layer(h, n_batch, b_w_in[j], b_a_re[j], b_a_im[j], b_log_dt[j], b_b_re[j],
                               b_b_im[j], b_c_re[j], b_c_im[j], b_d[j], b_w_out[j],
                               ln1_g[i], ln1_b[i], rw2, rb, alpha=alpha)
        h1, h1p, idx, gates, rank, cnt = mixed
        h = _moe_layer(h1, h1p, idx, gates, rank, cnt, e_w_in, e_w_out, ln2_g[i], ln2_b[i],
                       layer=i, alpha=alpha)
    return h.reshape(n_batch, seq, d)
```

```python
import functools
import math

import jax
import jax.numpy as jnp
from jax import lax
from jax.experimental import pallas as pl
from jax.experimental.pallas import tpu as pltpu

F32 = jnp.float32
BF16 = jnp.bfloat16
I32 = jnp.int32
U32 = jnp.uint32

LN_EPS = 1e-5
N_EXPERT_GROUPS = 4
SSM_CHUNK = 16

TM_MIX = 256
TM_SSM_OUT = 512
TM_PROJ = 512
TM_DISPATCH = 512
TM_EXPERT = 512
TM_COMBINE = 256

V7X_VMEM_LIMIT_BYTES = 56 * 1024 * 1024


def _dot(a, b):
    return jnp.dot(a, b, preferred_element_type=F32)


def _layer_norm(x, g, b):
    mu = jnp.mean(x, axis=-1, keepdims=True)
    xc = x - mu
    var = jnp.mean(xc * xc, axis=-1, keepdims=True)
    return xc * lax.rsqrt(var + LN_EPS) * g + b


def _const_spec(shape):
    return pl.BlockSpec(shape, lambda *_: (0,) * len(shape), pipeline_mode=pl.Buffered(1))


def _route(h1, rw_ref, rb_ref, idx_ref, gate_ref, rank_ref, cnt_ref, live):
    n_exp = rb_ref.shape[0]
    per_group = n_exp // N_EXPERT_GROUPS
    tm = h1.shape[0]
    h_top = lax.bitcast_convert_type(lax.bitcast_convert_type(h1, U32) & jnp.uint32(0xFFFF0000), F32)
    h_hi = h_top.astype(BF16)
    h_mid = (h1 - h_top).astype(BF16)
    first = _dot(h_hi, rw_ref[...])
    logits_t = first[:, :128] + first[:, 128:] + _dot(h_mid, rw_ref[:, :128])
    logits = logits_t.T[:n_exp, :] + rb_ref[...]
    m = jnp.max(logits, axis=0, keepdims=True)
    ex = jnp.exp(logits - m)
    probs = ex / jnp.sum(ex, axis=0, keepdims=True)

    best_score = None
    best_group = None
    for g in range(N_EXPERT_GROUPS):
        t1 = t2 = None
        for e in range(g * per_group, (g + 1) * per_group):
            v = probs[e:e + 1, :]
            if t1 is None:
                t1 = v
            elif t2 is None:
                t1, t2 = jnp.maximum(t1, v), jnp.minimum(t1, v)
            else:
                t2 = jnp.maximum(t2, jnp.minimum(t1, v))
                t1 = jnp.maximum(t1, v)
        score = t1 + t2
        if best_score is None:
            best_score, best_group = score, jnp.zeros((1, tm), I32)
        else:
            better = score > best_score
            best_group = jnp.where(better, g, best_group)
            best_score = jnp.where(better, score, best_score)

    eidx = lax.broadcasted_iota(I32, (n_exp, tm), 0)
    egrp = jnp.zeros((n_exp, tm), I32)
    for g in range(1, N_EXPERT_GROUPS):
        egrp = egrp + jnp.where(eidx >= g * per_group, 1, 0)
    in_group = egrp == best_group
    masked = jnp.where(in_group, probs, -1.0)
    m1 = jnp.max(masked, axis=0, keepdims=True)
    i1 = jnp.min(jnp.where(masked == m1, eidx, n_exp), axis=0, keepdims=True)
    sel1 = eidx == i1
    masked2 = jnp.where(sel1, -2.0, masked)
    m2 = jnp.max(masked2, axis=0, keepdims=True)
    i2 = jnp.min(jnp.where(masked2 == m2, eidx, n_exp), axis=0, keepdims=True)
    sel2 = eidx == i2
    den = m1 + m2

    cnt = jnp.where(sel1 | sel2, live, 0.0)
    r_i = lax.broadcasted_iota(I32, (tm, tm), 0)
    c_i = lax.broadcasted_iota(I32, (tm, tm), 1)
    strict_upper = jnp.where(r_i < c_i, 1.0, 0.0).astype(BF16)
    prefix = _dot(cnt.astype(BF16), strict_upper) + cnt_ref[:, 0:1]
    r1 = jnp.sum(jnp.where(sel1, prefix, 0.0), axis=0, keepdims=True)
    r2 = jnp.sum(jnp.where(sel2, prefix, 0.0), axis=0, keepdims=True)
    cnt_ref[...] += jnp.sum(cnt, axis=1, keepdims=True)

    idx_ref[...] = jnp.concatenate([i1, i2], axis=0)
    gate_ref[...] = jnp.concatenate([m1 / den, m2 / den], axis=0)
    rank_ref[...] = jnp.concatenate([r1, r2], axis=0).astype(I32)


RECORD_ROWS = 8


def _store_records(ref, h):
    tm, d = h.shape
    half = d // 2
    assert half == RECORD_ROWS * 128
    bits = lax.bitcast_convert_type(h.astype(BF16).astype(F32), U32)
    packed = (bits[:, :half] & jnp.uint32(0xFFFF0000)) | (bits[:, half:] >> 16)
    for s in range(RECORD_ROWS):
        ref[pl.ds(s, tm, stride=RECORD_ROWS), :] = packed[:, s * 128:(s + 1) * 128]


def _load_records(ref, tm):
    w = jnp.concatenate([ref[pl.ds(s, tm, stride=RECORD_ROWS), :] for s in range(RECORD_ROWS)], axis=1)
    hi = lax.bitcast_convert_type(w & jnp.uint32(0xFFFF0000), F32)
    lo = lax.bitcast_convert_type(w << 16, F32)
    return hi, lo


def _router_out(n_tok, tm, n_exp):
    shapes = [jax.ShapeDtypeStruct((2, n_tok), I32),
              jax.ShapeDtypeStruct((2, n_tok), F32),
              jax.ShapeDtypeStruct((2, n_tok), I32),
              jax.ShapeDtypeStruct((n_exp, 128), F32)]
    specs = [pl.BlockSpec((2, tm), lambda i: (0, _prev(i))),
             pl.BlockSpec((2, tm), lambda i: (0, _prev(i))),
             pl.BlockSpec((2, tm), lambda i: (0, _prev(i))),
             pl.BlockSpec((n_exp, 128), lambda i: (0, 0))]
    return shapes, specs


def _prev(i):
    return jnp.maximum(i - 1, 0)


def _mixer_tail(res_scr, l1g_ref, l1b_ref, rw_ref, rb_ref, h1_ref, h1p_ref, idx_ref, gate_ref, rank_ref, cnt_ref):
    live = jnp.where(pl.program_id(0) > 0, 1.0, 0.0)
    h1 = _layer_norm(res_scr[...], l1g_ref[...], l1b_ref[...])
    h1_ref[...] = h1
    _store_records(h1p_ref, h1)
    _route(h1, rw_ref, rb_ref, idx_ref, gate_ref, rank_ref, cnt_ref, live)


def _gmlp_kernel(x_ref, win_ref, bin_ref, lng_ref, lnb_ref, ws_ref, bst_ref, wout_ref,
                 l1g_ref, l1b_ref, rw_ref, rb_ref,
                 h1_ref, h1p_ref, idx_ref, gate_ref, rank_ref, cnt_ref,
                 p_scr, res_scr, *, alpha, chunk, groups):
    @pl.when(pl.program_id(0) == 0)
    def _():
        cnt_ref[...] = jnp.zeros_like(cnt_ref)
        res_scr[...] = jnp.zeros_like(res_scr)

    _mixer_tail(res_scr, l1g_ref, l1b_ref, rw_ref, rb_ref, h1_ref, h1p_ref, idx_ref, gate_ref, rank_ref, cnt_ref)

    x = x_ref[...]
    xb = x.astype(BF16)
    width = win_ref.shape[1] // 2
    gdim = width // groups
    tm = x.shape[0]

    zv = jax.nn.gelu(_dot(xb, win_ref[:, width:]) + bin_ref[:, width:])
    vn = _layer_norm(zv, lng_ref[...], lnb_ref[...]).astype(BF16)
    zu = jax.nn.gelu(_dot(xb, win_ref[:, :width]) + bin_ref[:, :width])

    r_i = lax.broadcasted_iota(I32, (chunk, chunk), 0)
    c_i = lax.broadcasted_iota(I32, (chunk, chunk), 1)
    causal = r_i >= c_i
    for g in range(groups):
        w = jnp.where(causal, ws_ref[g], 0.0).astype(BF16)
        bias = bst_ref[:, g:g + 1]
        for c in range(tm // chunk):
            rows = slice(c * chunk, (c + 1) * chunk)
            cols = slice(g * gdim, (g + 1) * gdim)
            s = _dot(w, vn[rows, cols]) + bias
            p_scr[rows, cols] = (zu[rows, cols] * s).astype(BF16)

    res_scr[...] = alpha * x + _dot(p_scr[...], wout_ref[...])


def _gmlp_layer(h, w_in, b_in, ln_g, ln_b, w_s, b_s, w_out, l1g, l1b, rw2, rb, *, alpha):
    n_tok, d = h.shape
    groups, chunk, _ = w_s.shape
    width = w_in.shape[1] // 2
    n_exp = rb.shape[0]
    tm = min(TM_MIX, n_tok)
    assert n_tok % tm == 0 and tm % chunk == 0
    r_shapes, r_specs = _router_out(n_tok, tm, n_exp)
    kern = functools.partial(_gmlp_kernel, alpha=alpha, chunk=chunk, groups=groups)
    n_steps = n_tok // tm
    return pl.pallas_call(
        kern,
        grid=(n_steps + 1,),
        in_specs=[pl.BlockSpec((tm, d), lambda i: (jnp.minimum(i, n_steps - 1), 0)),
                  _const_spec((d, 2 * width)), _const_spec((1, 2 * width)),
                  _const_spec((1, width)), _const_spec((1, width)),
                  _const_spec((groups, chunk, chunk)), _const_spec((chunk, groups)),
                  _const_spec((width, d)), _const_spec((1, d)), _const_spec((1, d)),
                  _const_spec((d, 256)), _const_spec((n_exp, 1))],
        out_specs=[pl.BlockSpec((tm, d), lambda i: (_prev(i), 0)),
                   pl.BlockSpec((tm * RECORD_ROWS, 128), lambda i: (_prev(i), 0))] + r_specs,
        out_shape=[jax.ShapeDtypeStruct((n_tok, d), F32),
                   jax.ShapeDtypeStruct((n_tok * RECORD_ROWS, 128), U32)] + r_shapes,
        scratch_shapes=[pltpu.VMEM((tm, width), BF16), pltpu.VMEM((tm, d), F32)],
        compiler_params=pltpu.CompilerParams(dimension_semantics=("arbitrary",),
                                             vmem_limit_bytes=V7X_VMEM_LIMIT_BYTES),
        name="gmlp_mixer",
    )(h, w_in.astype(BF16), b_in.reshape(1, -1), ln_g.reshape(1, -1), ln_b.reshape(1, -1),
      w_s, b_s.T, w_out.astype(BF16), l1g.reshape(1, -1), l1b.reshape(1, -1), rw2, rb)


SSM_LANE_GROUPS = 8
SSM_BATCH_PER_STEP = 4


def _proj_kernel(x_ref, w_ref, o_ref, res_scr, *, chunk):
    res = _dot(x_ref[...].astype(BF16), w_ref[...])
    n_slab = res_scr.shape[0]
    kb = o_ref.shape[0]
    for j in range(n_slab):
        res_scr[j] = res[:, j * 128:(j + 1) * 128]
    for t in range(chunk):
        for j in range(n_slab):
            o_ref[:, (t * n_slab + j) * 128:(t * n_slab + j + 1) * 128] = (
                res_scr[j, pl.ds(t, kb, stride=chunk), :].astype(o_ref.dtype))


def _ssm_in_proj(h, w, *, chunk):
    n_tok, d = h.shape
    width = w.shape[1]
    tm = min(TM_PROJ, n_tok)
    assert n_tok % tm == 0 and tm % chunk == 0 and width % 128 == 0
    kern = functools.partial(_proj_kernel, chunk=chunk)
    return pl.pallas_call(
        kern,
        grid=(n_tok // tm,),
        in_specs=[pl.BlockSpec((tm, d), lambda i: (i, 0)), _const_spec((d, width))],
        out_specs=pl.BlockSpec((tm // chunk, chunk * width), lambda i: (i, 0)),
        out_shape=jax.ShapeDtypeStruct((n_tok // chunk, chunk * width), BF16),
        scratch_shapes=[pltpu.VMEM((width // 128, tm, 128), F32)],
        compiler_params=pltpu.CompilerParams(dimension_semantics=("arbitrary",),
                                             vmem_limit_bytes=V7X_VMEM_LIMIT_BYTES),
        name="ssm_in_proj",
    )(h, w.astype(BF16))


def _ssm_kernel(*refs, chunk, n_batch, n_chunks):
    u_refs = refs[:chunk]
    d_ref, cin_ref, sout_ref, coef_ref, y_ref, t_ref, sin_ref, w_scr, x_scr, ytok_scr = refs[chunk:]
    bblk = pl.program_id(1)
    rows = n_batch * n_chunks
    n_x = x_scr.shape[0]

    @pl.when(bblk == 0)
    def _():
        for tp in range(chunk):
            for t in range(chunk):
                blk = d_ref[t - tp] if t >= tp else jnp.zeros((128, 128), BF16)
                t_ref[tp * 128:(tp + 1) * 128, t * 128:(t + 1) * 128] = blk
        ch_bits = int(math.log2(128 // SSM_LANE_GROUPS))
        row = lax.broadcasted_iota(I32, (chunk * 128, 128), 0)
        row_group = lax.shift_right_logical(row, ch_bits) & (SSM_LANE_GROUPS - 1)
        for part in range(2 * n_x // SSM_LANE_GROUPS):
            src = cin_ref[:, part * 128:(part + 1) * 128]
            for g in range(SSM_LANE_GROUPS):
                col = (part * SSM_LANE_GROUPS + g) * 128
                sin_ref[:, col:col + 128] = jnp.where(row_group == g, src, jnp.zeros_like(src))

    u = jnp.concatenate([r[...] for r in u_refs], axis=1)
    for jb in range(n_x):
        w = _dot(u, sin_ref[:, jb * 256:(jb + 1) * 256])
        w_scr[2 * jb] = w[:, :128]
        w_scr[2 * jb + 1] = w[:, 128:]

    ar = [jnp.broadcast_to(coef_ref[0:1, j * 128:(j + 1) * 128], (n_batch, 128)) for j in range(n_x)]
    ai_a = [jnp.broadcast_to(coef_ref[1:2, j * 128:(j + 1) * 128], (n_batch, 128)) for j in range(n_x)]
    ai_b = [jnp.broadcast_to(coef_ref[2:3, j * 128:(j + 1) * 128], (n_batch, 128)) for j in range(n_x)]

    def step(k, carry):
        x, xsw = carry
        rows_k = pl.ds(k, n_batch, stride=n_chunks)
        new_x, new_xsw = [], []
        for j in range(n_x):
            x_scr[j, rows_k, :] = x[j]
            new_x.append(ar[j] * x[j] + ai_a[j] * xsw[j] + w_scr[j, rows_k, :])
            new_xsw.append(ar[j] * xsw[j] + ai_b[j] * x[j] + w_scr[n_x + j, rows_k, :])
        return tuple(new_x), tuple(new_xsw)

    zero = tuple(jnp.zeros((n_batch, 128), F32) for _ in range(n_x))
    lax.fori_loop(0, n_chunks, step, (zero, zero))

    xs = jnp.concatenate([x_scr[j] for j in range(n_x)], axis=1).astype(BF16)
    for jb in range(chunk // 2):
        kk = (jb + 1) * 256
        cols = slice(jb * 256, (jb + 1) * 256)
        y = jax.nn.gelu(_dot(u[:, :kk], t_ref[:kk, cols]) + _dot(xs, sout_ref[:, cols]))
        for half in range(2):
            ytok_scr[pl.ds(2 * jb + half, rows, stride=chunk), :] = y[:, half * 128:(half + 1) * 128]
    y_ref[...] = ytok_scr[...].astype(y_ref.dtype)


def _ssm_operators(a_re, a_im, log_dt, b_re, b_im, c_re, c_im, d_skip):
    hp = lax.Precision.HIGHEST
    n_grp, n_state, n_ch = b_re.shape
    L = SSM_CHUNK
    nb = SSM_LANE_GROUPS
    assert n_ch * nb == 128 and 2 * n_state == 128 and n_grp % nb == 0
    n_gb = n_grp // nb
    a = lax.complex(a_re.astype(F32), a_im.astype(F32))
    lam = a * jnp.exp(log_dt.astype(F32))[:, None]
    a_bar = jnp.exp(lam)
    b_bar = ((a_bar - 1.0) / a)[..., None] * lax.complex(b_re.astype(F32), b_im.astype(F32))
    cc = lax.complex(c_re.astype(F32), c_im.astype(F32))
    steps = jnp.arange(L + 1, dtype=F32)
    apow = jnp.exp(steps[:, None, None] * lam[None])
    eye_g = jnp.eye(nb, dtype=F32)

    kern = jnp.real(jnp.einsum('gcp,lgp,gpd->glcd', cc, apow[:L], b_bar, precision=hp))
    d_op = jnp.einsum('Bgtcd,gh->Btgdhc', kern.reshape(n_gb, nb, L, n_ch, n_ch), eye_g)
    d_op = d_op.reshape(n_gb, L, 128, 128)
    skip = jnp.eye(128, dtype=F32)[None] * d_skip.astype(F32).reshape(n_gb, 1, 128)
    d_op = d_op.at[:, 0].add(skip).astype(BF16)

    s_in = apow[:L][::-1].transpose(1, 0, 2)[:, :, None, :] * b_bar.transpose(0, 2, 1)[:, None, :, :]
    s_in = jnp.concatenate([s_in.real, s_in.imag, s_in.imag, s_in.real], axis=-1).astype(BF16)
    c_in = s_in.reshape(n_gb, nb, L, n_ch, 4 * n_state).transpose(0, 2, 1, 3, 4).reshape(n_gb, L * 128, 4 * n_state)

    s_out = cc.transpose(0, 2, 1)[:, :, None, :] * apow[1:].transpose(1, 2, 0)[:, :, :, None]
    s_out = jnp.concatenate([s_out.real, -s_out.imag], axis=1)
    x_idx = jnp.arange(L * n_ch)
    l_idx = jnp.arange(L * 128)
    place = ((x_idx[:, None] // n_ch == l_idx[None, :] // 128) & (x_idx[:, None] % n_ch == l_idx[None, :] % n_ch))
    place = place[None] & ((l_idx[None, None, :] // n_ch) % nb == jnp.arange(nb)[:, None, None])
    s_out = jnp.einsum('Bgqx,gxl->Bgql', s_out.reshape(n_gb, nb, 2 * n_state, L * n_ch), place.astype(F32),
                       precision=hp).astype(BF16).reshape(n_gb, nb * 2 * n_state, L * 128)

    a_l = apow[L]
    coef = jnp.stack([jnp.concatenate([a_l.real, a_l.real], -1),
                      jnp.concatenate([-a_l.imag, a_l.imag], -1),
                      jnp.concatenate([a_l.imag, -a_l.imag], -1)], axis=1)
    coef = coef.reshape(n_gb, nb, 3, 2 * n_state).transpose(0, 2, 1, 3).reshape(n_gb, 3, nb * 2 * n_state)
    coef = jnp.pad(coef, ((0, 0), (0, 5), (0, 0)))
    return d_op, c_in, s_out, coef


def _ssm_scan(u, d_op, c_in, s_out, coef, *, n_batch, chunk):
    rows_all, lanes_all = u.shape
    n_blk, lc, _ = c_in.shape
    n_chunks = rows_all // n_batch
    nb = min(SSM_BATCH_PER_STEP, n_batch)
    assert n_batch % nb == 0 and lanes_all == chunk * n_blk * 128
    rows = nb * n_chunks
    n_x = s_out.shape[1] // 128
    kern = functools.partial(_ssm_kernel, chunk=chunk, n_batch=nb, n_chunks=n_chunks)
    u_specs = [pl.BlockSpec((rows, 128), lambda g, b, t=t: (b, t * n_blk + g)) for t in range(chunk)]

    def op_spec(shape):
        return pl.BlockSpec((None,) + shape, lambda g, b: (g,) + (0,) * len(shape))

    return pl.pallas_call(
        kern,
        grid=(n_blk, n_batch // nb),
        in_specs=u_specs + [op_spec((chunk, 128, 128)), op_spec((lc, c_in.shape[2])),
                            op_spec((n_x * 128, lc)), op_spec((8, n_x * 128))],
        out_specs=pl.BlockSpec((rows * chunk, 128), lambda g, b: (b, g)),
        out_shape=jax.ShapeDtypeStruct((rows_all * chunk, n_blk * 128), BF16),
        scratch_shapes=[pltpu.VMEM((lc, lc), BF16), pltpu.VMEM((lc, 2 * n_x * 128), BF16),
                        pltpu.VMEM((2 * n_x, rows, 128), F32), pltpu.VMEM((n_x, rows, 128), F32),
                        pltpu.VMEM((rows * chunk, 128), F32)],
        compiler_params=pltpu.CompilerParams(dimension_semantics=("arbitrary", "arbitrary"),
                                             vmem_limit_bytes=V7X_VMEM_LIMIT_BYTES),
        name="ssm_scan",
    )(*([u] * chunk), d_op, c_in, s_out, coef)


def _ssm_out_kernel(y_ref, h_ref, wout_ref, l1g_ref, l1b_ref, rw_ref, rb_ref,
                    h1_ref, h1p_ref, idx_ref, gate_ref, rank_ref, cnt_ref, res_scr, *, alpha):
    @pl.when(pl.program_id(0) == 0)
    def _():
        cnt_ref[...] = jnp.zeros_like(cnt_ref)
        res_scr[...] = jnp.zeros_like(res_scr)

    _mixer_tail(res_scr, l1g_ref, l1b_ref, rw_ref, rb_ref, h1_ref, h1p_ref, idx_ref, gate_ref, rank_ref, cnt_ref)

    y = y_ref[...]
    d = h_ref.shape[1]
    val = _dot(y, wout_ref[:, :d])
    gate = _dot(y, wout_ref[:, d:])
    res_scr[...] = alpha * h_ref[...] + val * jax.nn.sigmoid(gate)


def _ssm_layer(h, n_batch, w_in, a_re, a_im, log_dt, b_re, b_im, c_re, c_im, d_skip, w_out,
               l1g, l1b, rw2, rb, *, alpha):
    n_tok, d = h.shape
    n_grp, n_state, n_ch = b_re.shape
    width = n_grp * n_ch
    n_exp = rb.shape[0]
    L = SSM_CHUNK
    assert (n_tok // n_batch) % L == 0

    u = _ssm_in_proj(h, w_in, chunk=L)
    d_op, c_in, s_out, coef = _ssm_operators(a_re, a_im, log_dt, b_re, b_im, c_re, c_im, d_skip)
    y = _ssm_scan(u, d_op, c_in, s_out, coef, n_batch=n_batch, chunk=L)

    tm = min(TM_SSM_OUT, n_tok)
    assert n_tok % tm == 0
    r_shapes, r_specs = _router_out(n_tok, tm, n_exp)
    kern = functools.partial(_ssm_out_kernel, alpha=alpha)
    n_steps = n_tok // tm
    return pl.pallas_call(
        kern,
        grid=(n_steps + 1,),
        in_specs=[pl.BlockSpec((tm, width), lambda i: (jnp.minimum(i, n_steps - 1), 0)),
                  pl.BlockSpec((tm, d), lambda i: (jnp.minimum(i, n_steps - 1), 0)),
                  _const_spec((width, 2 * d)), _const_spec((1, d)), _const_spec((1, d)),
                  _const_spec((d, 256)), _const_spec((n_exp, 1))],
        out_specs=[pl.BlockSpec((tm, d), lambda i: (_prev(i), 0)),
                   pl.BlockSpec((tm * RECORD_ROWS, 128), lambda i: (_prev(i), 0))] + r_specs,
        out_shape=[jax.ShapeDtypeStruct((n_tok, d), F32),
                   jax.ShapeDtypeStruct((n_tok * RECORD_ROWS, 128), U32)] + r_shapes,
        scratch_shapes=[pltpu.VMEM((tm, d), F32)],
        compiler_params=pltpu.CompilerParams(dimension_semantics=("arbitrary",),
                                             vmem_limit_bytes=V7X_VMEM_LIMIT_BYTES),
        name="ssm_out_mixer",
    )(y, h, w_out.astype(BF16), l1g.reshape(1, -1), l1b.reshape(1, -1), rw2, rb)


def _moe_plan(idx, rank, cnt, tme, n_tiles):
    n_exp = cnt.shape[0]
    counts = cnt[:, 0].astype(I32)
    padded = ((counts + tme - 1) // tme) * tme
    ends = jnp.cumsum(padded)
    offs = ends - padded
    n_valid = ends[-1] // tme
    tile_start = jnp.arange(n_tiles, dtype=I32) * tme
    tile_expert = jnp.minimum(jnp.sum(tile_start[:, None] >= ends[None, :], axis=1), n_exp - 1).astype(I32)
    onehot = idx[:, :, None] == jnp.arange(n_exp, dtype=I32)[None, None, :]
    pos = jnp.sum(jnp.where(onehot, offs[None, None, :], 0), axis=-1) + rank
    meta = jnp.concatenate([counts, padded, offs, n_valid[None]]).astype(I32)
    return pos, tile_expert, meta


def _tile_pos(pos, tm):
    n_tok = pos.shape[1]
    return pos.reshape(2, n_tok // tm, tm).transpose(1, 0, 2).reshape(n_tok // tm, 1, 2 * tm)


def _record_slice(ref, row, n_rec=1):
    return ref.at[pl.ds(pl.multiple_of(row * RECORD_ROWS, RECORD_ROWS), n_rec * RECORD_ROWS), :]


def _dispatch_kernel(meta_ref, pos_ref, x_ref, win_ref, wout_ref, xs_ref, win_b_ref, wout_b_ref, zero_scr, sem,
                     *, n_exp, tme, n_tiles):
    i = pl.program_id(0)
    tm = x_ref.shape[0] // RECORD_ROWS
    half = zero_scr.shape[0] // RECORD_ROWS

    def zero_fill(start, size):
        cp = pltpu.make_async_copy(_record_slice(zero_scr, 0, size), _record_slice(xs_ref, start, size), sem)
        cp.start()
        cp.wait()

    @pl.when(i == 0)
    def _():
        zero_scr[...] = jnp.zeros_like(zero_scr)
        for e in range(n_exp):
            pad = meta_ref[n_exp + e] - meta_ref[e]
            base = meta_ref[2 * n_exp + e] + meta_ref[e]
            for b in range(int(math.log2(tme))):
                size = 1 << b

                @pl.when(((pad >> b) & 1) == 1)
                def _():
                    zero_fill(base + (pad & (size - 1)), size)
        n_valid = meta_ref[3 * n_exp]
        for j in range(n_exp):
            @pl.when(n_valid + j < n_tiles)
            def _():
                for part in range(tme // half):
                    zero_fill((n_valid + j) * tme + part * half, half)

    def start(r, c):
        for slot in range(2):
            p = pos_ref[0, 0, slot * tm + r]
            pltpu.make_async_copy(_record_slice(x_ref, r), _record_slice(xs_ref, p), sem).start(priority=slot)
        return c

    lax.fori_loop(0, tm, start, 0, unroll=8)
    win_b_ref[...] = win_ref[...].astype(BF16)
    wout_b_ref[...] = wout_ref[...].astype(BF16)
    for slot in range(2):
        pltpu.make_async_copy(x_ref, _record_slice(xs_ref, 0, tm), sem).wait()


def _moe_dispatch(h1p, pos, meta, w_in, w_out, *, layer, n_exp, tme, n_tiles):
    n_tok = h1p.shape[0] // RECORD_ROWS
    tm = min(TM_DISPATCH, n_tok)
    assert n_tok % tm == 0
    n_steps = n_tok // tm
    half = max(tme // 2, 1)
    n_layers, _, d, ff2 = w_in.shape
    ff = w_out.shape[2]
    assert (n_exp * d) % n_steps == 0 and (n_exp * ff) % n_steps == 0
    rin, rout = n_exp * d // n_steps, n_exp * ff // n_steps
    kern = functools.partial(_dispatch_kernel, n_exp=n_exp, tme=tme, n_tiles=n_tiles)
    xs, win_b, wout_b = pl.pallas_call(
        kern,
        grid_spec=pltpu.PrefetchScalarGridSpec(
            num_scalar_prefetch=1,
            grid=(n_steps,),
            in_specs=[pl.BlockSpec((1, 1, 2 * tm), lambda i, m: (i, 0, 0), memory_space=pltpu.SMEM),
                      pl.BlockSpec((tm * RECORD_ROWS, 128), lambda i, m: (i, 0)),
                      pl.BlockSpec((None, rin, ff2), lambda i, m: (layer, i, 0)),
                      pl.BlockSpec((None, rout, d), lambda i, m: (layer, i, 0))],
            out_specs=[pl.BlockSpec(memory_space=pl.ANY),
                       pl.BlockSpec((rin, ff2), lambda i, m: (i, 0)),
                       pl.BlockSpec((rout, d), lambda i, m: (i, 0))],
            scratch_shapes=[pltpu.VMEM((half * RECORD_ROWS, 128), U32), pltpu.SemaphoreType.DMA]),
        out_shape=[jax.ShapeDtypeStruct((n_tiles * tme * RECORD_ROWS, 128), U32),
                   jax.ShapeDtypeStruct((n_exp * d, ff2), BF16),
                   jax.ShapeDtypeStruct((n_exp * ff, d), BF16)],
        compiler_params=pltpu.CompilerParams(dimension_semantics=("arbitrary",),
                                             vmem_limit_bytes=V7X_VMEM_LIMIT_BYTES),
        name="moe_dispatch",
    )(meta, _tile_pos(pos, tm), h1p, w_in.reshape(n_layers, n_exp * d, ff2), w_out.reshape(n_layers, n_exp * ff, d))
    return xs, win_b.reshape(n_exp, d, ff2), wout_b.reshape(n_exp, ff, d)


def _expert_kernel(te_ref, meta_ref, xs_ref, win_ref, wout_ref, ys_ref, *, n_exp):
    i = pl.program_id(0)
    n_valid = meta_ref[3 * n_exp]
    tme = xs_ref.shape[0] // RECORD_ROWS

    @pl.when(i < n_valid)
    def _():
        hi, lo = _load_records(xs_ref, tme)
        dh = hi.shape[1]
        ff = wout_ref.shape[0]
        h = _dot(hi.astype(BF16), win_ref[:dh, :]) + _dot(lo.astype(BF16), win_ref[dh:, :])
        act = (jax.nn.silu(h[:, :ff]) * h[:, ff:]).astype(BF16)
        _store_records(ys_ref, _dot(act, wout_ref[...]))

    @pl.when(i >= n_valid)
    def _():
        ys_ref[...] = jnp.zeros_like(ys_ref)


def _moe_experts(xs, tile_expert, meta, w_in, w_out, *, tme):
    n_rows = xs.shape[0] // RECORD_ROWS
    n_exp, d, ff2 = w_in.shape
    ff = w_out.shape[1]
    n_tiles = n_rows // tme

    def x_map(i, te, m):
        return (jnp.minimum(i, m[3 * n_exp] - 1), 0)

    kern = functools.partial(_expert_kernel, n_exp=n_exp)
    return pl.pallas_call(
        kern,
        grid_spec=pltpu.PrefetchScalarGridSpec(
            num_scalar_prefetch=2,
            grid=(n_tiles,),
            in_specs=[pl.BlockSpec((tme * RECORD_ROWS, 128), x_map),
                      pl.BlockSpec((None, d, ff2), lambda i, te, m: (te[i], 0, 0)),
                      pl.BlockSpec((None, ff, d), lambda i, te, m: (te[i], 0, 0))],
            out_specs=pl.BlockSpec((tme * RECORD_ROWS, 128), lambda i, te, m: (i, 0))),
        out_shape=jax.ShapeDtypeStruct((n_rows * RECORD_ROWS, 128), U32),
        compiler_params=pltpu.CompilerParams(dimension_semantics=("arbitrary",),
                                             vmem_limit_bytes=V7X_VMEM_LIMIT_BYTES),
        name="moe_experts",
    )(tile_expert, meta, xs, w_in, w_out)


def _combine_kernel(pos_ref, pos_next_ref, h1_ref, gate_ref, ys_ref, g_ref, b_ref, out_ref, buf, sems, *, alpha):
    i = pl.program_id(0)
    n_steps = pl.num_programs(0)
    tm = h1_ref.shape[0]

    def gather(p_ref, half):
        def start(r, c):
            for slot in range(2):
                p = p_ref[0, 0, slot * tm + r]
                pltpu.make_async_copy(_record_slice(ys_ref, p), _record_slice(buf.at[half, slot], r),
                                      sems.at[half]).start(priority=slot)
            return c

        lax.fori_loop(0, tm, start, 0, unroll=8)

    @pl.when(i == 0)
    def _():
        gather(pos_ref, 0)

    @pl.when(i + 1 < n_steps)
    def _():
        gather(pos_next_ref, (i + 1) % 2)

    cur = i % 2
    moe = None
    for slot in range(2):
        pltpu.make_async_copy(_record_slice(ys_ref, 0, tm), buf.at[cur, slot], sems.at[cur]).wait()
    for slot in range(2):
        hi, lo = _load_records(buf.at[cur, slot], tm)
        y = jnp.concatenate([hi, lo], axis=1) * gate_ref[:, slot:slot + 1]
        moe = y if moe is None else moe + y
    out_ref[...] = _layer_norm(alpha * h1_ref[...] + moe, g_ref[...], b_ref[...])


def _moe_combine(h1, ys, pos, gates, l2g, l2b, *, alpha):
    n_tok, d = h1.shape
    tm = min(TM_COMBINE, n_tok)
    assert n_tok % tm == 0
    n_steps = n_tok // tm
    kern = functools.partial(_combine_kernel, alpha=alpha)
    pos_tiles = _tile_pos(pos, tm)
    return pl.pallas_call(
        kern,
        grid=(n_steps,),
        in_specs=[pl.BlockSpec((1, 1, 2 * tm), lambda i: (i, 0, 0), memory_space=pltpu.SMEM),
                  pl.BlockSpec((1, 1, 2 * tm), lambda i: (jnp.minimum(i + 1, n_steps - 1), 0, 0),
                               memory_space=pltpu.SMEM),
                  pl.BlockSpec((tm, d), lambda i: (i, 0)),
                  pl.BlockSpec((tm, 2), lambda i: (i, 0)),
                  pl.BlockSpec(memory_space=pl.ANY),
                  _const_spec((1, d)), _const_spec((1, d))],
        out_specs=pl.BlockSpec((tm, d), lambda i: (i, 0)),
        out_shape=jax.ShapeDtypeStruct((n_tok, d), F32),
        scratch_shapes=[pltpu.VMEM((2, 2, tm * RECORD_ROWS, 128), U32), pltpu.SemaphoreType.DMA((2,))],
        compiler_params=pltpu.CompilerParams(dimension_semantics=("arbitrary",),
                                             vmem_limit_bytes=V7X_VMEM_LIMIT_BYTES),
        name="moe_combine",
    )(pos_tiles, pos_tiles, h1, gates.T, ys, l2g.reshape(1, -1), l2b.reshape(1, -1))


def _moe_layer(h1, h1p, idx, gates, rank, cnt, w_in, w_out, l2g, l2b, *, layer, alpha):
    n_tok = h1.shape[0]
    n_exp = w_in.shape[1]
    tme = min(TM_EXPERT, n_tok)
    n_tiles = (2 * n_tok) // tme + n_exp
    pos, tile_expert, meta = _moe_plan(idx, rank, cnt, tme, n_tiles)
    xs, w_in_b, w_out_b = _moe_dispatch(h1p, pos, meta, w_in, w_out, layer=layer, n_exp=n_exp, tme=tme,
                                        n_tiles=n_tiles)
    ys = _moe_experts(xs, tile_expert, meta, w_in_b, w_out_b, tme=tme)
    return _moe_combine(h1, ys, pos, gates, l2g, l2b, alpha=alpha)


def kernel(x, a_w_in, a_b_in, a_ln_g, a_ln_b, a_w_s, a_b_s, a_w_out, b_w_in, b_a_re, b_a_im, b_log_dt, b_b_re, b_b_im, b_c_re, b_c_im, b_d, b_w_out, ln1_g, ln1_b, router_w, router_b, e_w_in, e_w_out, ln2_g, ln2_b):
    n_batch, seq, d = x.shape
    depth = ln1_g.shape[0]
    alpha = float((2 * depth) ** 0.25)
    n_exp = router_w.shape[1]
    rw = jnp.pad(router_w.astype(F32), ((0, 0), (0, 128 - n_exp)))
    rw_hi = lax.reduce_precision(rw, exponent_bits=8, mantissa_bits=7)
    rw2 = jnp.concatenate([rw_hi, rw - rw_hi], axis=1).astype(BF16)
    rb = router_b.astype(F32).reshape(-1, 1)
    h = x.reshape(n_batch * seq, d)
    for i in range(depth):
        j = i // 2
        if i % 2 == 0:
            mixed = _gmlp_layer(h, a_w_in[j], a_b_in[j], a_ln_g[j], a_ln_b[j], a_w_s[j], a_b_s[j],
                                a_w_out[j], ln1_g[i], ln1_b[i], rw2, rb, alpha=alpha)
        else:
            mixed = _ssm_layer(h, n_batch, b_w_in[j], b_a_re[j], b_a_im[j], b_log_dt[j], b_b_re[j],
                               b_b_im[j], b_c_re[j], b_c_im[j], b_d[j], b_w_out[j],
                               ln1_g[i], ln1_b[i], rw2, rb, alpha=alpha)
        h1, h1p, idx, gates, rank, cnt = mixed
        h = _moe_layer(h1, h1p, idx, gates, rank, cnt, e_w_in, e_w_out, ln2_g[i], ln2_b[i],
                       layer=i, alpha=alpha)
    return h.reshape(n_batch, seq, d)
```

```python
import functools
import math

import jax
import jax.numpy as jnp
from jax import lax
from jax.experimental import pallas as pl
from jax.experimental.pallas import tpu as pltpu

F32 = jnp.float32
BF16 = jnp.bfloat16
I32 = jnp.int32
U32 = jnp.uint32

LN_EPS = 1e-5
N_EXPERT_GROUPS = 4
SSM_CHUNK = 16

TM_MIX = 256
TM_SSM_OUT = 512
TM_PROJ = 512
TM_DISPATCH = 512
TM_EXPERT = 512
TM_COMBINE = 256

V7X_VMEM_LIMIT_BYTES = 56 * 1024 * 1024


def _dot(a, b):
    return jnp.dot(a, b, preferred_element_type=F32)


def _layer_norm(x, g, b):
    mu = jnp.mean(x, axis=-1, keepdims=True)
    xc = x - mu
    var = jnp.mean(xc * xc, axis=-1, keepdims=True)
    return xc * lax.rsqrt(var + LN_EPS) * g + b


def _const_spec(shape):
    return pl.BlockSpec(shape, lambda *_: (0,) * len(shape), pipeline_mode=pl.Buffered(1))


def _route(h1, rw_ref, rb_ref, idx_ref, gate_ref, rank_ref, cnt_ref, live):
    n_exp = rb_ref.shape[0]
    per_group = n_exp // N_EXPERT_GROUPS
    tm = h1.shape[0]
    h_top = lax.bitcast_convert_type(lax.bitcast_convert_type(h1, U32) & jnp.uint32(0xFFFF0000), F32)
    h_hi = h_top.astype(BF16)
    h_mid = (h1 - h_top).astype(BF16)
    first = _dot(h_hi, rw_ref[...])
    logits_t = first[:, :128] + first[:, 128:] + _dot(h_mid, rw_ref[:, :128])
    logits = logits_t.T[:n_exp, :] + rb_ref[...]
    m = jnp.max(logits, axis=0, keepdims=True)
    ex = jnp.exp(logits - m)
    probs = ex / jnp.sum(ex, axis=0, keepdims=True)

    best_score = None
    best_group = None
    for g in range(N_EXPERT_GROUPS):
        t1 = t2 = None
        for e in range(g * per_group, (g + 1) * per_group):
            v = probs[e:e + 1, :]
            if t1 is None:
                t1 = v
            elif t2 is None:
                t1, t2 = jnp.maximum(t1, v), jnp.minimum(t1, v)
            else:
                t2 = jnp.maximum(t2, jnp.minimum(t1, v))
                t1 = jnp.maximum(t1, v)
        score = t1 + t2
        if best_score is None:
            best_score, best_group = score, jnp.zeros((1, tm), I32)
        else:
            better = score > best_score
            best_group = jnp.where(better, g, best_group)
            best_score = jnp.where(better, score, best_score)

    eidx = lax.broadcasted_iota(I32, (n_exp, tm), 0)
    egrp = jnp.zeros((n_exp, tm), I32)
    for g in range(1, N_EXPERT_GROUPS):
        egrp = egrp + jnp.where(eidx >= g * per_group, 1, 0)
    in_group = egrp == best_group
    masked = jnp.where(in_group, probs, -1.0)
    m1 = jnp.max(masked, axis=0, keepdims=True)
    i1 = jnp.min(jnp.where(masked == m1, eidx, n_exp), axis=0, keepdims=True)
    sel1 = eidx == i1
    masked2 = jnp.where(sel1, -2.0, masked)
    m2 = jnp.max(masked2, axis=0, keepdims=True)
    i2 = jnp.min(jnp.where(masked2 == m2, eidx, n_exp), axis=0, keepdims=True)
    sel2 = eidx == i2
    den = m1 + m2

    cnt = jnp.where(sel1 | sel2, live, 0.0)
    r_i = lax.broadcasted_iota(I32, (tm, tm), 0)
    c_i = lax.broadcasted_iota(I32, (tm, tm), 1)
    strict_upper = jnp.where(r_i < c_i, 1.0, 0.0).astype(BF16)
    prefix = _dot(cnt.astype(BF16), strict_upper) + cnt_ref[:, 0:1]
    r1 = jnp.sum(jnp.where(sel1, prefix, 0.0), axis=0, keepdims=True)
    r2 = jnp.sum(jnp.where(sel2, prefix, 0.0), axis=0, keepdims=True)
    cnt_ref[...] += jnp.sum(cnt, axis=1, keepdims=True)

    idx_ref[...] = jnp.concatenate([i1, i2], axis=0)
    gate_ref[...] = jnp.concatenate([m1 / den, m2 / den], axis=0)
    rank_ref[...] = jnp.concatenate([r1, r2], axis=0).astype(I32)


RECORD_ROWS = 8


def _store_records(ref, h):
    tm, d = h.shape
    half = d // 2
    assert half == RECORD_ROWS * 128
    bits = lax.bitcast_convert_type(h.astype(BF16).astype(F32), U32)
    packed = (bits[:, :half] & jnp.uint32(0xFFFF0000)) | (bits[:, half:] >> 16)
    for s in range(RECORD_ROWS):
        ref[pl.ds(s, tm, stride=RECORD_ROWS), :] = packed[:, s * 128:(s + 1) * 128]


def _load_records(ref, tm):
    w = jnp.concatenate([ref[pl.ds(s, tm, stride=RECORD_ROWS), :] for s in range(RECORD_ROWS)], axis=1)
    hi = lax.bitcast_convert_type(w & jnp.uint32(0xFFFF0000), F32)
    lo = lax.bitcast_convert_type(w << 16, F32)
    return hi, lo


def _router_out(n_tok, tm, n_exp):
    shapes = [jax.ShapeDtypeStruct((2, n_tok), I32),
              jax.ShapeDtypeStruct((2, n_tok), F32),
              jax.ShapeDtypeStruct((2, n_tok), I32),
              jax.ShapeDtypeStruct((n_exp, 128), F32)]
    specs = [pl.BlockSpec((2, tm), lambda i: (0, _prev(i))),
             pl.BlockSpec((2, tm), lambda i: (0, _prev(i))),
             pl.BlockSpec((2, tm), lambda i: (0, _prev(i))),
             pl.BlockSpec((n_exp, 128), lambda i: (0, 0))]
    return shapes, specs


def _prev(i):
    return jnp.maximum(i - 1, 0)


def _mixer_tail(res_scr, l1g_ref, l1b_ref, rw_ref, rb_ref, h1_ref, h1p_ref, idx_ref, gate_ref, rank_ref, cnt_ref):
    live = jnp.where(pl.program_id(0) > 0, 1.0, 0.0)
    h1 = _layer_norm(res_scr[...], l1g_ref[...], l1b_ref[...])
    h1_ref[...] = h1
    _store_records(h1p_ref, h1)
    _route(h1, rw_ref, rb_ref, idx_ref, gate_ref, rank_ref, cnt_ref, live)


def _gmlp_kernel(x_ref, win_ref, bin_ref, lng_ref, lnb_ref, ws_ref, bst_ref, wout_ref,
                 l1g_ref, l1b_ref, rw_ref, rb_ref,
                 h1_ref, h1p_ref, idx_ref, gate_ref, rank_ref, cnt_ref,
                 p_scr, res_scr, *, alpha, chunk, groups):
    @pl.when(pl.program_id(0) == 0)
    def _():
        cnt_ref[...] = jnp.zeros_like(cnt_ref)
        res_scr[...] = jnp.zeros_like(res_scr)

    _mixer_tail(res_scr, l1g_ref, l1b_ref, rw_ref, rb_ref, h1_ref, h1p_ref, idx_ref, gate_ref, rank_ref, cnt_ref)

    x = x_ref[...]
    xb = x.astype(BF16)
    width = win_ref.shape[1] // 2
    gdim = width // groups
    tm = x.shape[0]

    zv = jax.nn.gelu(_dot(xb, win_ref[:, width:]) + bin_ref[:, width:])
    vn = _layer_norm(zv, lng_ref[...], lnb_ref[...]).astype(BF16)
    zu = jax.nn.gelu(_dot(xb, win_ref[:, :width]) + bin_ref[:, :width])

    r_i = lax.broadcasted_iota(I32, (chunk, chunk), 0)
    c_i = lax.broadcasted_iota(I32, (chunk, chunk), 1)
    causal = r_i >= c_i
    for g in range(groups):
        w = jnp.where(causal, ws_ref[g], 0.0).astype(BF16)
        bias = bst_ref[:, g:g + 1]
        for c in range(tm // chunk):
            rows = slice(c * chunk, (c + 1) * chunk)
            cols = slice(g * gdim, (g + 1) * gdim)
            s = _dot(w, vn[rows, cols]) + bias
            p_scr[rows, cols] = (zu[rows, cols] * s).astype(BF16)

    res_scr[...] = alpha * x + _dot(p_scr[...], wout_ref[...])


def _gmlp_layer(h, w_in, b_in, ln_g, ln_b, w_s, b_s, w_out, l1g, l1b, rw2, rb, *, alpha):
    n_tok, d = h.shape
    groups, chunk, _ = w_s.shape
    width = w_in.shape[1] // 2
    n_exp = rb.shape[0]
    tm = min(TM_MIX, n_tok)
    assert n_tok % tm == 0 and tm % chunk == 0
    r_shapes, r_specs = _router_out(n_tok, tm, n_exp)
    kern = functools.partial(_gmlp_kernel, alpha=alpha, chunk=chunk, groups=groups)
    n_steps = n_tok // tm
    return pl.pallas_call(
        kern,
        grid=(n_steps + 1,),
        in_specs=[pl.BlockSpec((tm, d), lambda i: (jnp.minimum(i, n_steps - 1), 0)),
                  _const_spec((d, 2 * width)), _const_spec((1, 2 * width)),
                  _const_spec((1, width)), _const_spec((1, width)),
                  _const_spec((groups, chunk, chunk)), _const_spec((chunk, groups)),
                  _const_spec((width, d)), _const_spec((1, d)), _const_spec((1, d)),
                  _const_spec((d, 256)), _const_spec((n_exp, 1))],
        out_specs=[pl.BlockSpec((tm, d), lambda i: (_prev(i), 0)),
                   pl.BlockSpec((tm * RECORD_ROWS, 128), lambda i: (_prev(i), 0))] + r_specs,
        out_shape=[jax.ShapeDtypeStruct((n_tok, d), F32),
                   jax.ShapeDtypeStruct((n_tok * RECORD_ROWS, 128), U32)] + r_shapes,
        scratch_shapes=[pltpu.VMEM((tm, width), BF16), pltpu.VMEM((tm, d), F32)],
        compiler_params=pltpu.CompilerParams(dimension_semantics=("arbitrary",),
                                             vmem_limit_bytes=V7X_VMEM_LIMIT_BYTES),
        name="gmlp_mixer",
    )(h, w_in.astype(BF16), b_in.reshape(1, -1), ln_g.reshape(1, -1), ln_b.reshape(1, -1),
      w_s, b_s.T, w_out.astype(BF16), l1g.reshape(1, -1), l1b.reshape(1, -1), rw2, rb)


SSM_LANE_GROUPS = 8
SSM_BATCH_PER_STEP = 4


def _proj_kernel(x_ref, w_ref, o_ref, res_scr, *, chunk):
    res = _dot(x_ref[...].astype(BF16), w_ref[...])
    n_slab = res_scr.shape[0]
    kb = o_ref.shape[0]
    for j in range(n_slab):
        res_scr[j] = res[:, j * 128:(j + 1) * 128]
    for t in range(chunk):
        for j in range(n_slab):
            o_ref[:, (t * n_slab + j) * 128:(t * n_slab + j + 1) * 128] = (
                res_scr[j, pl.ds(t, kb, stride=chunk), :].astype(o_ref.dtype))


def _ssm_in_proj(h, w, *, chunk):
    n_tok, d = h.shape
    width = w.shape[1]
    tm = min(TM_PROJ, n_tok)
    assert n_tok % tm == 0 and tm % chunk == 0 and width % 128 == 0
    kern = functools.partial(_proj_kernel, chunk=chunk)
    return pl.pallas_call(
        kern,
        grid=(n_tok // tm,),
        in_specs=[pl.BlockSpec((tm, d), lambda i: (i, 0)), _const_spec((d, width))],
        out_specs=pl.BlockSpec((tm // chunk, chunk * width), lambda i: (i, 0)),
        out_shape=jax.ShapeDtypeStruct((n_tok // chunk, chunk * width), BF16),
        scratch_shapes=[pltpu.VMEM((width // 128, tm, 128), F32)],
        compiler_params=pltpu.CompilerParams(dimension_semantics=("arbitrary",),
                                             vmem_limit_bytes=V7X_VMEM_LIMIT_BYTES),
        name="ssm_in_proj",
    )(h, w.astype(BF16))


def _ssm_kernel(*refs, chunk, n_batch, n_chunks):
    u_refs = refs[:chunk]
    d_ref, cin_ref, sout_ref, coef_ref, y_ref, t_ref, sin_ref, w_scr, x_scr, ytok_scr = refs[chunk:]
    bblk = pl.program_id(1)
    rows = n_batch * n_chunks
    n_x = x_scr.shape[0]

    @pl.when(bblk == 0)
    def _():
        for tp in range(chunk):
            for t in range(chunk):
                blk = d_ref[t - tp] if t >= tp else jnp.zeros((128, 128), BF16)
                t_ref[tp * 128:(tp + 1) * 128, t * 128:(t + 1) * 128] = blk
        ch_bits = int(math.log2(128 // SSM_LANE_GROUPS))
        row = lax.broadcasted_iota(I32, (chunk * 128, 128), 0)
        row_group = lax.shift_right_logical(row, ch_bits) & (SSM_LANE_GROUPS - 1)
        for part in range(2 * n_x // SSM_LANE_GROUPS):
            src = cin_ref[:, part * 128:(part + 1) * 128]
            for g in range(SSM_LANE_GROUPS):
                col = (part * SSM_LANE_GROUPS + g) * 128
                sin_ref[:, col:col + 128] = jnp.where(row_group == g, src, jnp.zeros_like(src))

    pitch = w_scr.shape[1] // n_batch

    u = jnp.concatenate([r[...] for r in u_refs], axis=1)
    for jb in range(n_x):
        w = _dot(u, sin_ref[:, jb * 256:(jb + 1) * 256])
        for b in range(n_batch):
            seq = slice(b * n_chunks, (b + 1) * n_chunks)
            w_scr[2 * jb, b * pitch:b * pitch + n_chunks, :] = w[seq, :128]
            w_scr[2 * jb + 1, b * pitch:b * pitch + n_chunks, :] = w[seq, 128:]

    ar = [jnp.broadcast_to(coef_ref[0:1, j * 128:(j + 1) * 128], (n_batch, 128)) for j in range(n_x)]
    ai_a = [jnp.broadcast_to(coef_ref[1:2, j * 128:(j + 1) * 128], (n_batch, 128)) for j in range(n_x)]
    ai_b = [jnp.broadcast_to(coef_ref[2:3, j * 128:(j + 1) * 128], (n_batch, 128)) for j in range(n_x)]

    def step(k, carry):
        x, xsw = carry
        rows_k = pl.ds(k, n_batch, stride=pitch)
        new_x, new_xsw = [], []
        for j in range(n_x):
            x_scr[j, rows_k, :] = x[j]
            new_x.append(ar[j] * x[j] + ai_a[j] * xsw[j] + w_scr[j, rows_k, :])
            new_xsw.append(ar[j] * xsw[j] + ai_b[j] * x[j] + w_scr[n_x + j, rows_k, :])
        return tuple(new_x), tuple(new_xsw)

    zero = tuple(jnp.zeros((n_batch, 128), F32) for _ in range(n_x))
    lax.fori_loop(0, n_chunks, step, (zero, zero))

    xs = jnp.concatenate(
        [jnp.concatenate([x_scr[j, b * pitch:b * pitch + n_chunks, :] for b in range(n_batch)], axis=0)
         for j in range(n_x)], axis=1).astype(BF16)
    for jb in range(chunk // 2):
        kk = (jb + 1) * 256
        cols = slice(jb * 256, (jb + 1) * 256)
        y = jax.nn.gelu(_dot(u[:, :kk], t_ref[:kk, cols]) + _dot(xs, sout_ref[:, cols]))
        for half in range(2):
            ytok_scr[pl.ds(2 * jb + half, rows, stride=chunk), :] = y[:, half * 128:(half + 1) * 128]
    y_ref[...] = ytok_scr[...].astype(y_ref.dtype)


def _ssm_operators(a_re, a_im, log_dt, b_re, b_im, c_re, c_im, d_skip):
    hp = lax.Precision.HIGHEST
    n_grp, n_state, n_ch = b_re.shape
    L = SSM_CHUNK
    nb = SSM_LANE_GROUPS
    assert n_ch * nb == 128 and 2 * n_state == 128 and n_grp % nb == 0
    n_gb = n_grp // nb
    a = lax.complex(a_re.astype(F32), a_im.astype(F32))
    lam = a * jnp.exp(log_dt.astype(F32))[:, None]
    a_bar = jnp.exp(lam)
    b_bar = ((a_bar - 1.0) / a)[..., None] * lax.complex(b_re.astype(F32), b_im.astype(F32))
    cc = lax.complex(c_re.astype(F32), c_im.astype(F32))
    steps = jnp.arange(L + 1, dtype=F32)
    apow = jnp.exp(steps[:, None, None] * lam[None])
    eye_g = jnp.eye(nb, dtype=F32)

    kern = jnp.real(jnp.einsum('gcp,lgp,gpd->glcd', cc, apow[:L], b_bar, precision=hp))
    d_op = jnp.einsum('Bgtcd,gh->Btgdhc', kern.reshape(n_gb, nb, L, n_ch, n_ch), eye_g)
    d_op = d_op.reshape(n_gb, L, 128, 128)
    skip = jnp.eye(128, dtype=F32)[None] * d_skip.astype(F32).reshape(n_gb, 1, 128)
    d_op = d_op.at[:, 0].add(skip).astype(BF16)

    s_in = apow[:L][::-1].transpose(1, 0, 2)[:, :, None, :] * b_bar.transpose(0, 2, 1)[:, None, :, :]
    s_in = jnp.concatenate([s_in.real, s_in.imag, s_in.imag, s_in.real], axis=-1).astype(BF16)
    c_in = s_in.reshape(n_gb, nb, L, n_ch, 4 * n_state).transpose(0, 2, 1, 3, 4).reshape(n_gb, L * 128, 4 * n_state)

    s_out = cc.transpose(0, 2, 1)[:, :, None, :] * apow[1:].transpose(1, 2, 0)[:, :, :, None]
    s_out = jnp.concatenate([s_out.real, -s_out.imag], axis=1)
    x_idx = jnp.arange(L * n_ch)
    l_idx = jnp.arange(L * 128)
    place = ((x_idx[:, None] // n_ch == l_idx[None, :] // 128) & (x_idx[:, None] % n_ch == l_idx[None, :] % n_ch))
    place = place[None] & ((l_idx[None, None, :] // n_ch) % nb == jnp.arange(nb)[:, None, None])
    s_out = jnp.einsum('Bgqx,gxl->Bgql', s_out.reshape(n_gb, nb, 2 * n_state, L * n_ch), place.astype(F32),
                       precision=hp).astype(BF16).reshape(n_gb, nb * 2 * n_state, L * 128)

    a_l = apow[L]
    coef = jnp.stack([jnp.concatenate([a_l.real, a_l.real], -1),
                      jnp.concatenate([-a_l.imag, a_l.imag], -1),
                      jnp.concatenate([a_l.imag, -a_l.imag], -1)], axis=1)
    coef = coef.reshape(n_gb, nb, 3, 2 * n_state).transpose(0, 2, 1, 3).reshape(n_gb, 3, nb * 2 * n_state)
    coef = jnp.pad(coef, ((0, 0), (0, 5), (0, 0)))
    return d_op, c_in, s_out, coef


def _ssm_scan(u, d_op, c_in, s_out, coef, *, n_batch, chunk):
    rows_all, lanes_all = u.shape
    n_blk, lc, _ = c_in.shape
    n_chunks = rows_all // n_batch
    nb = min(SSM_BATCH_PER_STEP, n_batch)
    assert n_batch % nb == 0 and lanes_all == chunk * n_blk * 128
    rows = nb * n_chunks
    pitch = n_chunks + 4 if n_chunks % 8 == 0 else n_chunks
    n_x = s_out.shape[1] // 128
    kern = functools.partial(_ssm_kernel, chunk=chunk, n_batch=nb, n_chunks=n_chunks)
    u_specs = [pl.BlockSpec((rows, 128), lambda g, b, t=t: (b, t * n_blk + g)) for t in range(chunk)]

    def op_spec(shape):
        return pl.BlockSpec((None,) + shape, lambda g, b: (g,) + (0,) * len(shape))

    return pl.pallas_call(
        kern,
        grid=(n_blk, n_batch // nb),
        in_specs=u_specs + [op_spec((chunk, 128, 128)), op_spec((lc, c_in.shape[2])),
                            op_spec((n_x * 128, lc)), op_spec((8, n_x * 128))],
        out_specs=pl.BlockSpec((rows * chunk, 128), lambda g, b: (b, g)),
        out_shape=jax.ShapeDtypeStruct((rows_all * chunk, n_blk * 128), BF16),
        scratch_shapes=[pltpu.VMEM((lc, lc), BF16), pltpu.VMEM((lc, 2 * n_x * 128), BF16),
                        pltpu.VMEM((2 * n_x, nb * pitch, 128), F32), pltpu.VMEM((n_x, nb * pitch, 128), F32),
                        pltpu.VMEM((rows * chunk, 128), F32)],
        compiler_params=pltpu.CompilerParams(dimension_semantics=("arbitrary", "arbitrary"),
                                             vmem_limit_bytes=V7X_VMEM_LIMIT_BYTES),
        name="ssm_scan",
    )(*([u] * chunk), d_op, c_in, s_out, coef)


def _ssm_out_kernel(y_ref, h_ref, wout_ref, l1g_ref, l1b_ref, rw_ref, rb_ref,
                    h1_ref, h1p_ref, idx_ref, gate_ref, rank_ref, cnt_ref, res_scr, *, alpha):
    @pl.when(pl.program_id(0) == 0)
    def _():
        cnt_ref[...] = jnp.zeros_like(cnt_ref)
        res_scr[...] = jnp.zeros_like(res_scr)

    _mixer_tail(res_scr, l1g_ref, l1b_ref, rw_ref, rb_ref, h1_ref, h1p_ref, idx_ref, gate_ref, rank_ref, cnt_ref)

    y = y_ref[...]
    d = h_ref.shape[1]
    val = _dot(y, wout_ref[:, :d])
    gate = _dot(y, wout_ref[:, d:])
    res_scr[...] = alpha * h_ref[...] + val * jax.nn.sigmoid(gate)


def _ssm_layer(h, n_batch, w_in, a_re, a_im, log_dt, b_re, b_im, c_re, c_im, d_skip, w_out,
               l1g, l1b, rw2, rb, *, alpha):
    n_tok, d = h.shape
    n_grp, n_state, n_ch = b_re.shape
    width = n_grp * n_ch
    n_exp = rb.shape[0]
    L = SSM_CHUNK
    assert (n_tok // n_batch) % L == 0

    u = _ssm_in_proj(h, w_in, chunk=L)
    d_op, c_in, s_out, coef = _ssm_operators(a_re, a_im, log_dt, b_re, b_im, c_re, c_im, d_skip)
    y = _ssm_scan(u, d_op, c_in, s_out, coef, n_batch=n_batch, chunk=L)

    tm = min(TM_SSM_OUT, n_tok)
    assert n_tok % tm == 0
    r_shapes, r_specs = _router_out(n_tok, tm, n_exp)
    kern = functools.partial(_ssm_out_kernel, alpha=alpha)
    n_steps = n_tok // tm
    return pl.pallas_call(
        kern,
        grid=(n_steps + 1,),
        in_specs=[pl.BlockSpec((tm, width), lambda i: (jnp.minimum(i, n_steps - 1), 0)),
                  pl.BlockSpec((tm, d), lambda i: (jnp.minimum(i, n_steps - 1), 0)),
                  _const_spec((width, 2 * d)), _const_spec((1, d)), _const_spec((1, d)),
                  _const_spec((d, 256)), _const_spec((n_exp, 1))],
        out_specs=[pl.BlockSpec((tm, d), lambda i: (_prev(i), 0)),
                   pl.BlockSpec((tm * RECORD_ROWS, 128), lambda i: (_prev(i), 0))] + r_specs,
        out_shape=[jax.ShapeDtypeStruct((n_tok, d), F32),
                   jax.ShapeDtypeStruct((n_tok * RECORD_ROWS, 128), U32)] + r_shapes,
        scratch_shapes=[pltpu.VMEM((tm, d), F32)],
        compiler_params=pltpu.CompilerParams(dimension_semantics=("arbitrary",),
                                             vmem_limit_bytes=V7X_VMEM_LIMIT_BYTES),
        name="ssm_out_mixer",
    )(y, h, w_out.astype(BF16), l1g.reshape(1, -1), l1b.reshape(1, -1), rw2, rb)


def _moe_plan(idx, rank, cnt, tme, n_tiles):
    n_exp = cnt.shape[0]
    counts = cnt[:, 0].astype(I32)
    padded = ((counts + tme - 1) // tme) * tme
    ends = jnp.cumsum(padded)
    offs = ends - padded
    n_valid = ends[-1] // tme
    tile_start = jnp.arange(n_tiles, dtype=I32) * tme
    tile_expert = jnp.minimum(jnp.sum(tile_start[:, None] >= ends[None, :], axis=1), n_exp - 1).astype(I32)
    onehot = idx[:, :, None] == jnp.arange(n_exp, dtype=I32)[None, None, :]
    pos = jnp.sum(jnp.where(onehot, offs[None, None, :], 0), axis=-1) + rank
    meta = jnp.concatenate([counts, padded, offs, n_valid[None]]).astype(I32)
    return pos, tile_expert, meta


def _tile_pos(pos, tm):
    n_tok = pos.shape[1]
    return pos.reshape(2, n_tok // tm, tm).transpose(1, 0, 2).reshape(n_tok // tm, 1, 2 * tm)


def _record_slice(ref, row, n_rec=1):
    return ref.at[pl.ds(pl.multiple_of(row * RECORD_ROWS, RECORD_ROWS), n_rec * RECORD_ROWS), :]


def _dispatch_kernel(meta_ref, pos_ref, x_ref, *refs, n_exp, tme, n_tiles, cast):
    if cast:
        win_ref, wout_ref, xs_ref, win_b_ref, wout_b_ref, zero_scr, sem = refs
    else:
        xs_ref, zero_scr, sem = refs
    i = pl.program_id(0)
    tm = x_ref.shape[0] // RECORD_ROWS
    half = zero_scr.shape[0] // RECORD_ROWS

    def zero_fill(start, size):
        cp = pltpu.make_async_copy(_record_slice(zero_scr, 0, size), _record_slice(xs_ref, start, size), sem)
        cp.start()
        cp.wait()

    @pl.when(i == 0)
    def _():
        zero_scr[...] = jnp.zeros_like(zero_scr)
        for e in range(n_exp):
            pad = meta_ref[n_exp + e] - meta_ref[e]
            base = meta_ref[2 * n_exp + e] + meta_ref[e]
            for b in range(int(math.log2(tme))):
                size = 1 << b

                @pl.when(((pad >> b) & 1) == 1)
                def _():
                    zero_fill(base + (pad & (size - 1)), size)
        n_valid = meta_ref[3 * n_exp]
        for j in range(n_exp):
            @pl.when(n_valid + j < n_tiles)
            def _():
                for part in range(tme // half):
                    zero_fill((n_valid + j) * tme + part * half, half)

    def start(r, c):
        for slot in range(2):
            p = pos_ref[0, 0, slot * tm + r]
            pltpu.make_async_copy(_record_slice(x_ref, r), _record_slice(xs_ref, p), sem).start(priority=slot)
        return c

    lax.fori_loop(0, tm, start, 0, unroll=8)
    if cast:
        win_b_ref[...] = win_ref[...].astype(BF16)
        wout_b_ref[...] = wout_ref[...].astype(BF16)
    for slot in range(2):
        pltpu.make_async_copy(x_ref, _record_slice(xs_ref, 0, tm), sem).wait()


def _cast_specs(w_in, w_out, layer, n_steps, index):
    n_layers, n_exp, d, ff2 = w_in.shape
    ff = w_out.shape[2]
    assert (n_exp * d) % n_steps == 0 and (n_exp * ff) % n_steps == 0
    rin, rout = n_exp * d // n_steps, n_exp * ff // n_steps
    in_specs = [pl.BlockSpec((None, rin, ff2), lambda i, *_: (layer, index(i), 0)),
                pl.BlockSpec((None, rout, d), lambda i, *_: (layer, index(i), 0))]
    out_specs = [pl.BlockSpec((rin, ff2), lambda i, *_: (index(i), 0)),
                 pl.BlockSpec((rout, d), lambda i, *_: (index(i), 0))]
    out_shape = [jax.ShapeDtypeStruct((n_exp * d, ff2), BF16), jax.ShapeDtypeStruct((n_exp * ff, d), BF16)]
    operands = (w_in.reshape(n_layers, n_exp * d, ff2), w_out.reshape(n_layers, n_exp * ff, d))
    return in_specs, out_specs, out_shape, operands


def _moe_dispatch(h1p, pos, meta, cast_weights, *, n_exp, tme, n_tiles):
    n_tok = h1p.shape[0] // RECORD_ROWS
    tm = min(TM_DISPATCH, n_tok)
    assert n_tok % tm == 0
    n_steps = n_tok // tm
    half = max(tme // 2, 1)
    cast = cast_weights is not None
    c_in, c_out, c_shape, c_ops = ([], [], [], ())
    if cast:
        c_in, c_out, c_shape, c_ops = _cast_specs(*cast_weights, n_steps, lambda i: i)
    kern = functools.partial(_dispatch_kernel, n_exp=n_exp, tme=tme, n_tiles=n_tiles, cast=cast)
    return pl.pallas_call(
        kern,
        grid_spec=pltpu.PrefetchScalarGridSpec(
            num_scalar_prefetch=1,
            grid=(n_steps,),
            in_specs=[pl.BlockSpec((1, 1, 2 * tm), lambda i, m: (i, 0, 0), memory_space=pltpu.SMEM),
                      pl.BlockSpec((tm * RECORD_ROWS, 128), lambda i, m: (i, 0))] + c_in,
            out_specs=[pl.BlockSpec(memory_space=pl.ANY)] + c_out,
            scratch_shapes=[pltpu.VMEM((half * RECORD_ROWS, 128), U32), pltpu.SemaphoreType.DMA]),
        out_shape=[jax.ShapeDtypeStruct((n_tiles * tme * RECORD_ROWS, 128), U32)] + c_shape,
        compiler_params=pltpu.CompilerParams(dimension_semantics=("arbitrary",),
                                             vmem_limit_bytes=V7X_VMEM_LIMIT_BYTES),
        name="moe_dispatch",
    )(meta, _tile_pos(pos, tm), h1p, *c_ops)


def _expert_kernel(te_ref, meta_ref, xs_ref, win_ref, wout_ref, *refs, n_exp, cast):
    if cast:
        nwin_ref, nwout_ref, ys_ref, nwin_b_ref, nwout_b_ref = refs
        nwin_b_ref[...] = nwin_ref[...].astype(BF16)
        nwout_b_ref[...] = nwout_ref[...].astype(BF16)
    else:
        (ys_ref,) = refs
    i = pl.program_id(0)
    n_valid = meta_ref[3 * n_exp]
    tme = xs_ref.shape[0] // RECORD_ROWS

    @pl.when(i < n_valid)
    def _():
        hi, lo = _load_records(xs_ref, tme)
        dh = hi.shape[1]
        ff = wout_ref.shape[0]
        h = _dot(hi.astype(BF16), win_ref[:dh, :]) + _dot(lo.astype(BF16), win_ref[dh:, :])
        act = (jax.nn.silu(h[:, :ff]) * h[:, ff:]).astype(BF16)
        _store_records(ys_ref, _dot(act, wout_ref[...]))

    @pl.when(i >= n_valid)
    def _():
        ys_ref[...] = jnp.zeros_like(ys_ref)


def _moe_experts(xs, tile_expert, meta, w_in, w_out, cast_weights, *, tme):
    n_rows = xs.shape[0] // RECORD_ROWS
    n_exp, d, ff2 = w_in.shape
    ff = w_out.shape[1]
    n_tiles = n_rows // tme

    def x_map(i, te, m):
        return (jnp.minimum(i, m[3 * n_exp] - 1), 0)

    cast = cast_weights is not None
    c_in, c_out, c_shape, c_ops = ([], [], [], ())
    if cast:
        n_cast = 1 << (n_tiles.bit_length() - 1)
        c_in, c_out, c_shape, c_ops = _cast_specs(*cast_weights, n_cast, lambda i: jnp.minimum(i, n_cast - 1))
    kern = functools.partial(_expert_kernel, n_exp=n_exp, cast=cast)
    return pl.pallas_call(
        kern,
        grid_spec=pltpu.PrefetchScalarGridSpec(
            num_scalar_prefetch=2,
            grid=(n_tiles,),
            in_specs=[pl.BlockSpec((tme * RECORD_ROWS, 128), x_map),
                      pl.BlockSpec((None, d, ff2), lambda i, te, m: (te[i], 0, 0)),
                      pl.BlockSpec((None, ff, d), lambda i, te, m: (te[i], 0, 0))] + c_in,
            out_specs=[pl.BlockSpec((tme * RECORD_ROWS, 128), lambda i, te, m: (i, 0))] + c_out),
        out_shape=[jax.ShapeDtypeStruct((n_rows * RECORD_ROWS, 128), U32)] + c_shape,
        compiler_params=pltpu.CompilerParams(dimension_semantics=("arbitrary",),
                                             vmem_limit_bytes=V7X_VMEM_LIMIT_BYTES),
        name="moe_experts",
    )(tile_expert, meta, xs, w_in, w_out, *c_ops)


def _combine_kernel(pos_ref, pos_next_ref, h1_ref, gate_ref, ys_ref, g_ref, b_ref, out_ref, buf, sems, *, alpha):
    i = pl.program_id(0)
    n_steps = pl.num_programs(0)
    tm = h1_ref.shape[0]

    def gather(p_ref, half):
        def start(r, c):
            for slot in range(2):
                p = p_ref[0, 0, slot * tm + r]
                pltpu.make_async_copy(_record_slice(ys_ref, p), _record_slice(buf.at[half, slot], r),
                                      sems.at[half]).start(priority=slot)
            return c

        lax.fori_loop(0, tm, start, 0, unroll=8)

    @pl.when(i == 0)
    def _():
        gather(pos_ref, 0)

    @pl.when(i + 1 < n_steps)
    def _():
        gather(pos_next_ref, (i + 1) % 2)

    cur = i % 2
    moe = None
    for slot in range(2):
        pltpu.make_async_copy(_record_slice(ys_ref, 0, tm), buf.at[cur, slot], sems.at[cur]).wait()
    for slot in range(2):
        hi, lo = _load_records(buf.at[cur, slot], tm)
        y = jnp.concatenate([hi, lo], axis=1) * gate_ref[:, slot:slot + 1]
        moe = y if moe is None else moe + y
    out_ref[...] = _layer_norm(alpha * h1_ref[...] + moe, g_ref[...], b_ref[...])


def _moe_combine(h1, ys, pos, gates, l2g, l2b, *, alpha):
    n_tok, d = h1.shape
    tm = min(TM_COMBINE, n_tok)
    assert n_tok % tm == 0
    n_steps = n_tok // tm
    kern = functools.partial(_combine_kernel, alpha=alpha)
    pos_tiles = _tile_pos(pos, tm)
    return pl.pallas_call(
        kern,
        grid=(n_steps,),
        in_specs=[pl.BlockSpec((1, 1, 2 * tm), lambda i: (i, 0, 0), memory_space=pltpu.SMEM),
                  pl.BlockSpec((1, 1, 2 * tm), lambda i: (jnp.minimum(i + 1, n_steps - 1), 0, 0),
                               memory_space=pltpu.SMEM),
                  pl.BlockSpec((tm, d), lambda i: (i, 0)),
                  pl.BlockSpec((tm, 2), lambda i: (i, 0)),
                  pl.BlockSpec(memory_space=pl.ANY),
                  _const_spec((1, d)), _const_spec((1, d))],
        out_specs=pl.BlockSpec((tm, d), lambda i: (i, 0)),
        out_shape=jax.ShapeDtypeStruct((n_tok, d), F32),
        scratch_shapes=[pltpu.VMEM((2, 2, tm * RECORD_ROWS, 128), U32), pltpu.SemaphoreType.DMA((2,))],
        compiler_params=pltpu.CompilerParams(dimension_semantics=("arbitrary",),
                                             vmem_limit_bytes=V7X_VMEM_LIMIT_BYTES),
        name="moe_combine",
    )(pos_tiles, pos_tiles, h1, gates.T, ys, l2g.reshape(1, -1), l2b.reshape(1, -1))


def _moe_layer(h1, h1p, idx, gates, rank, cnt, expert_weights, l2g, l2b, *, layer, alpha):
    w_in, w_out, ready = expert_weights
    n_tok = h1.shape[0]
    n_layers, n_exp = w_in.shape[:2]
    d, ff2 = w_in.shape[2:]
    ff = w_out.shape[2]
    tme = min(TM_EXPERT, n_tok)
    n_tiles = (2 * n_tok) // tme + n_exp
    pos, tile_expert, meta = _moe_plan(idx, rank, cnt, tme, n_tiles)
    if ready is None:
        xs, w_in_b, w_out_b = _moe_dispatch(h1p, pos, meta, (w_in, w_out, layer), n_exp=n_exp, tme=tme,
                                            n_tiles=n_tiles)
    else:
        (xs,) = _moe_dispatch(h1p, pos, meta, None, n_exp=n_exp, tme=tme, n_tiles=n_tiles)
        w_in_b, w_out_b = ready
    nxt = (w_in, w_out, layer + 1) if layer + 1 < n_layers else None
    res = _moe_experts(xs, tile_expert, meta, w_in_b.reshape(n_exp, d, ff2), w_out_b.reshape(n_exp, ff, d), nxt,
                       tme=tme)
    nxt_ready = tuple(res[1:]) if nxt is not None else None
    return _moe_combine(h1, res[0], pos, gates, l2g, l2b, alpha=alpha), nxt_ready


def kernel(x, a_w_in, a_b_in, a_ln_g, a_ln_b, a_w_s, a_b_s, a_w_out, b_w_in, b_a_re, b_a_im, b_log_dt, b_b_re, b_b_im, b_c_re, b_c_im, b_d, b_w_out, ln1_g, ln1_b, router_w, router_b, e_w_in, e_w_out, ln2_g, ln2_b):
    n_batch, seq, d = x.shape
    depth = ln1_g.shape[0]
    alpha = float((2 * depth) ** 0.25)
    n_exp = router_w.shape[1]
    rw = jnp.pad(router_w.astype(F32), ((0, 0), (0, 128 - n_exp)))
    rw_hi = lax.reduce_precision(rw, exponent_bits=8, mantissa_bits=7)
    rw2 = jnp.concatenate([rw_hi, rw - rw_hi], axis=1).astype(BF16)
    rb = router_b.astype(F32).reshape(-1, 1)
    h = x.reshape(n_batch * seq, d)
    ready = None
    for i in range(depth):
        j = i // 2
        if i % 2 == 0:
            mixed = _gmlp_layer(h, a_w_in[j], a_b_in[j], a_ln_g[j], a_ln_b[j], a_w_s[j], a_b_s[j],
                                a_w_out[j], ln1_g[i], ln1_b[i], rw2, rb, alpha=alpha)
        else:
            mixed = _ssm_layer(h, n_batch, b_w_in[j], b_a_re[j], b_a_im[j], b_log_dt[j], b_b_re[j],
                               b_b_im[j], b_c_re[j], b_c_im[j], b_d[j], b_w_out[j],
                               ln1_g[i], ln1_b[i], rw2, rb, alpha=alpha)
        h1, h1p, idx, gates, rank, cnt = mixed
        h, ready = _moe_layer(h1, h1p, idx, gates, rank, cnt, (e_w_in, e_w_out, ready), ln2_g[i], ln2_b[i],
                              layer=i, alpha=alpha)
    return h.reshape(n_batch, seq, d)
```

```python
import functools
import math

import jax
import jax.numpy as jnp
from jax import lax
from jax.experimental import pallas as pl
from jax.experimental.pallas import tpu as pltpu

F32 = jnp.float32
BF16 = jnp.bfloat16
I32 = jnp.int32
U32 = jnp.uint32

LN_EPS = 1e-5
N_EXPERT_GROUPS = 4
SSM_CHUNK = 16

TM_MIX = 256
TM_SSM_OUT = 512
TM_PROJ = 512
TM_DISPATCH = 512
TM_EXPERT = 512
TM_COMBINE = 512

V7X_VMEM_LIMIT_BYTES = 56 * 1024 * 1024


def _dot(a, b):
    return jnp.dot(a, b, preferred_element_type=F32)


def _layer_norm(x, g, b):
    mu = jnp.mean(x, axis=-1, keepdims=True)
    xc = x - mu
    var = jnp.mean(xc * xc, axis=-1, keepdims=True)
    return xc * lax.rsqrt(var + LN_EPS) * g + b


def _const_spec(shape):
    return pl.BlockSpec(shape, lambda *_: (0,) * len(shape), pipeline_mode=pl.Buffered(1))


def _route(h1, rw_ref, rb_ref, idx_ref, gate_ref, rank_ref, cnt_ref, live):
    n_exp = rb_ref.shape[0]
    per_group = n_exp // N_EXPERT_GROUPS
    tm = h1.shape[0]
    h_top = lax.bitcast_convert_type(lax.bitcast_convert_type(h1, U32) & jnp.uint32(0xFFFF0000), F32)
    h_hi = h_top.astype(BF16)
    h_mid = (h1 - h_top).astype(BF16)
    first = _dot(h_hi, rw_ref[...])
    logits_t = first[:, :128] + first[:, 128:] + _dot(h_mid, rw_ref[:, :128])
    logits = logits_t.T[:n_exp, :] + rb_ref[...]
    m = jnp.max(logits, axis=0, keepdims=True)
    ex = jnp.exp(logits - m)
    probs = ex / jnp.sum(ex, axis=0, keepdims=True)

    best_score = None
    best_group = None
    for g in range(N_EXPERT_GROUPS):
        t1 = t2 = None
        for e in range(g * per_group, (g + 1) * per_group):
            v = probs[e:e + 1, :]
            if t1 is None:
                t1 = v
            elif t2 is None:
                t1, t2 = jnp.maximum(t1, v), jnp.minimum(t1, v)
            else:
                t2 = jnp.maximum(t2, jnp.minimum(t1, v))
                t1 = jnp.maximum(t1, v)
        score = t1 + t2
        if best_score is None:
            best_score, best_group = score, jnp.zeros((1, tm), I32)
        else:
            better = score > best_score
            best_group = jnp.where(better, g, best_group)
            best_score = jnp.where(better, score, best_score)

    eidx = lax.broadcasted_iota(I32, (n_exp, tm), 0)
    egrp = jnp.zeros((n_exp, tm), I32)
    for g in range(1, N_EXPERT_GROUPS):
        egrp = egrp + jnp.where(eidx >= g * per_group, 1, 0)
    in_group = egrp == best_group
    masked = jnp.where(in_group, probs, -1.0)
    m1 = jnp.max(masked, axis=0, keepdims=True)
    i1 = jnp.min(jnp.where(masked == m1, eidx, n_exp), axis=0, keepdims=True)
    sel1 = eidx == i1
    masked2 = jnp.where(sel1, -2.0, masked)
    m2 = jnp.max(masked2, axis=0, keepdims=True)
    i2 = jnp.min(jnp.where(masked2 == m2, eidx, n_exp), axis=0, keepdims=True)
    sel2 = eidx == i2
    den = m1 + m2

    cnt = jnp.where(sel1 | sel2, live, 0.0)
    r_i = lax.broadcasted_iota(I32, (tm, tm), 0)
    c_i = lax.broadcasted_iota(I32, (tm, tm), 1)
    strict_upper = jnp.where(r_i < c_i, 1.0, 0.0).astype(BF16)
    prefix = _dot(cnt.astype(BF16), strict_upper) + cnt_ref[:, 0:1]
    r1 = jnp.sum(jnp.where(sel1, prefix, 0.0), axis=0, keepdims=True)
    r2 = jnp.sum(jnp.where(sel2, prefix, 0.0), axis=0, keepdims=True)
    cnt_ref[...] += jnp.sum(cnt, axis=1, keepdims=True)

    idx_ref[...] = jnp.concatenate([i1, i2], axis=0)
    gate_ref[...] = jnp.concatenate([m1 / den, m2 / den], axis=0)
    rank_ref[...] = jnp.concatenate([r1, r2], axis=0).astype(I32)


RECORD_ROWS = 8


def _store_records(ref, h):
    tm, d = h.shape
    half = d // 2
    assert half == RECORD_ROWS * 128
    bits = lax.bitcast_convert_type(h.astype(BF16).astype(F32), U32)
    packed = (bits[:, :half] & jnp.uint32(0xFFFF0000)) | (bits[:, half:] >> 16)
    for s in range(RECORD_ROWS):
        ref[pl.ds(s, tm, stride=RECORD_ROWS), :] = packed[:, s * 128:(s + 1) * 128]


def _load_records(ref, tm):
    w = jnp.concatenate([ref[pl.ds(s, tm, stride=RECORD_ROWS), :] for s in range(RECORD_ROWS)], axis=1)
    hi = lax.bitcast_convert_type(w & jnp.uint32(0xFFFF0000), F32)
    lo = lax.bitcast_convert_type(w << 16, F32)
    return hi, lo


def _router_out(n_tok, tm, n_exp):
    shapes = [jax.ShapeDtypeStruct((2, n_tok), I32),
              jax.ShapeDtypeStruct((2, n_tok), F32),
              jax.ShapeDtypeStruct((2, n_tok), I32),
              jax.ShapeDtypeStruct((n_exp, 128), F32)]
    specs = [pl.BlockSpec((2, tm), lambda i: (0, _prev(i))),
             pl.BlockSpec((2, tm), lambda i: (0, _prev(i))),
             pl.BlockSpec((2, tm), lambda i: (0, _prev(i))),
             pl.BlockSpec((n_exp, 128), lambda i: (0, 0))]
    return shapes, specs


def _prev(i):
    return jnp.maximum(i - 1, 0)


def _mixer_tail(res_scr, l1g_ref, l1b_ref, rw_ref, rb_ref, h1_ref, h1p_ref, idx_ref, gate_ref, rank_ref, cnt_ref):
    live = jnp.where(pl.program_id(0) > 0, 1.0, 0.0)
    h1 = _layer_norm(res_scr[...], l1g_ref[...], l1b_ref[...])
    h1_ref[...] = h1
    _store_records(h1p_ref, h1)
    _route(h1, rw_ref, rb_ref, idx_ref, gate_ref, rank_ref, cnt_ref, live)


def _gmlp_kernel(x_ref, win_ref, bin_ref, lng_ref, lnb_ref, ws_ref, bst_ref, wout_ref,
                 l1g_ref, l1b_ref, rw_ref, rb_ref, ew_in_ref, ew_out_ref,
                 h1_ref, h1p_ref, idx_ref, gate_ref, rank_ref, cnt_ref, ew_in_b_ref, ew_out_b_ref,
                 p_scr, res_scr, *, alpha, chunk, groups):
    @pl.when(pl.program_id(0) == 0)
    def _():
        cnt_ref[...] = jnp.zeros_like(cnt_ref)
        res_scr[...] = jnp.zeros_like(res_scr)

    ew_in_b_ref[...] = ew_in_ref[...].astype(BF16)
    ew_out_b_ref[...] = ew_out_ref[...].astype(BF16)

    _mixer_tail(res_scr, l1g_ref, l1b_ref, rw_ref, rb_ref, h1_ref, h1p_ref, idx_ref, gate_ref, rank_ref, cnt_ref)

    x = x_ref[...]
    xb = x.astype(BF16)
    width = win_ref.shape[1] // 2
    gdim = width // groups
    tm = x.shape[0]

    zv = jax.nn.gelu(_dot(xb, win_ref[:, width:]) + bin_ref[:, width:])
    vn = _layer_norm(zv, lng_ref[...], lnb_ref[...]).astype(BF16)
    zu = jax.nn.gelu(_dot(xb, win_ref[:, :width]) + bin_ref[:, :width])

    r_i = lax.broadcasted_iota(I32, (chunk, chunk), 0)
    c_i = lax.broadcasted_iota(I32, (chunk, chunk), 1)
    causal = r_i >= c_i
    for g in range(groups):
        w = jnp.where(causal, ws_ref[g], 0.0).astype(BF16)
        bias = bst_ref[:, g:g + 1]
        for c in range(tm // chunk):
            rows = slice(c * chunk, (c + 1) * chunk)
            cols = slice(g * gdim, (g + 1) * gdim)
            s = _dot(w, vn[rows, cols]) + bias
            p_scr[rows, cols] = (zu[rows, cols] * s).astype(BF16)

    res_scr[...] = alpha * x + _dot(p_scr[...], wout_ref[...])


def _gmlp_layer(h, w_in, b_in, ln_g, ln_b, w_s, b_s, w_out, l1g, l1b, rw2, rb, expert_weights, *, alpha):
    n_tok, d = h.shape
    groups, chunk, _ = w_s.shape
    width = w_in.shape[1] // 2
    n_exp = rb.shape[0]
    tm = min(TM_MIX, n_tok)
    assert n_tok % tm == 0 and tm % chunk == 0
    r_shapes, r_specs = _router_out(n_tok, tm, n_exp)
    kern = functools.partial(_gmlp_kernel, alpha=alpha, chunk=chunk, groups=groups)
    n_steps = n_tok // tm
    c_in, c_out, c_shape, c_ops = _cast_specs(*expert_weights, n_steps, lambda i: jnp.minimum(i, n_steps - 1))
    return pl.pallas_call(
        kern,
        grid=(n_steps + 1,),
        in_specs=[pl.BlockSpec((tm, d), lambda i: (jnp.minimum(i, n_steps - 1), 0)),
                  _const_spec((d, 2 * width)), _const_spec((1, 2 * width)),
                  _const_spec((1, width)), _const_spec((1, width)),
                  _const_spec((groups, chunk, chunk)), _const_spec((chunk, groups)),
                  _const_spec((width, d)), _const_spec((1, d)), _const_spec((1, d)),
                  _const_spec((d, 256)), _const_spec((n_exp, 1))] + c_in,
        out_specs=[pl.BlockSpec((tm, d), lambda i: (_prev(i), 0)),
                   pl.BlockSpec((tm * RECORD_ROWS, 128), lambda i: (_prev(i), 0))] + r_specs + c_out,
        out_shape=[jax.ShapeDtypeStruct((n_tok, d), F32),
                   jax.ShapeDtypeStruct((n_tok * RECORD_ROWS, 128), U32)] + r_shapes + c_shape,
        scratch_shapes=[pltpu.VMEM((tm, width), BF16), pltpu.VMEM((tm, d), F32)],
        compiler_params=pltpu.CompilerParams(dimension_semantics=("arbitrary",),
                                             vmem_limit_bytes=V7X_VMEM_LIMIT_BYTES),
        name="gmlp_mixer",
    )(h, w_in.astype(BF16), b_in.reshape(1, -1), ln_g.reshape(1, -1), ln_b.reshape(1, -1),
      w_s, b_s.T, w_out.astype(BF16), l1g.reshape(1, -1), l1b.reshape(1, -1), rw2, rb, *c_ops)


SSM_LANE_GROUPS = 8
SSM_BATCH_PER_STEP = 4


def _proj_kernel(x_ref, w_ref, o_ref, res_scr, *, chunk):
    res = _dot(x_ref[...].astype(BF16), w_ref[...])
    n_slab = res_scr.shape[0]
    kb = o_ref.shape[0]
    for j in range(n_slab):
        res_scr[j] = res[:, j * 128:(j + 1) * 128]
    for t in range(chunk):
        for j in range(n_slab):
            o_ref[:, (t * n_slab + j) * 128:(t * n_slab + j + 1) * 128] = (
                res_scr[j, pl.ds(t, kb, stride=chunk), :].astype(o_ref.dtype))


def _ssm_in_proj(h, w, *, chunk):
    n_tok, d = h.shape
    width = w.shape[1]
    tm = min(TM_PROJ, n_tok)
    assert n_tok % tm == 0 and tm % chunk == 0 and width % 128 == 0
    kern = functools.partial(_proj_kernel, chunk=chunk)
    return pl.pallas_call(
        kern,
        grid=(n_tok // tm,),
        in_specs=[pl.BlockSpec((tm, d), lambda i: (i, 0)), _const_spec((d, width))],
        out_specs=pl.BlockSpec((tm // chunk, chunk * width), lambda i: (i, 0)),
        out_shape=jax.ShapeDtypeStruct((n_tok // chunk, chunk * width), BF16),
        scratch_shapes=[pltpu.VMEM((width // 128, tm, 128), F32)],
        compiler_params=pltpu.CompilerParams(dimension_semantics=("arbitrary",),
                                             vmem_limit_bytes=V7X_VMEM_LIMIT_BYTES),
        name="ssm_in_proj",
    )(h, w.astype(BF16))


def _ssm_kernel(*refs, chunk, n_batch, n_chunks):
    u_refs = refs[:chunk]
    d_ref, cin_ref, sout_ref, coef_ref, y_ref, t_ref, sin_ref, w_scr, x_scr, ytok_scr = refs[chunk:]
    bblk = pl.program_id(1)
    rows = n_batch * n_chunks
    n_x = x_scr.shape[0]

    @pl.when(bblk == 0)
    def _():
        for tp in range(chunk):
            for t in range(chunk):
                blk = d_ref[t - tp] if t >= tp else jnp.zeros((128, 128), BF16)
                t_ref[tp * 128:(tp + 1) * 128, t * 128:(t + 1) * 128] = blk
        ch_bits = int(math.log2(128 // SSM_LANE_GROUPS))
        row = lax.broadcasted_iota(I32, (chunk * 128, 128), 0)
        row_group = lax.shift_right_logical(row, ch_bits) & (SSM_LANE_GROUPS - 1)
        for part in range(2 * n_x // SSM_LANE_GROUPS):
            src = cin_ref[:, part * 128:(part + 1) * 128]
            for g in range(SSM_LANE_GROUPS):
                col = (part * SSM_LANE_GROUPS + g) * 128
                sin_ref[:, col:col + 128] = jnp.where(row_group == g, src, jnp.zeros_like(src))

    pitch = w_scr.shape[1] // n_batch

    u = jnp.concatenate([r[...] for r in u_refs], axis=1)
    for jb in range(n_x):
        w = _dot(u, sin_ref[:, jb * 256:(jb + 1) * 256])
        for b in range(n_batch):
            seq = slice(b * n_chunks, (b + 1) * n_chunks)
            w_scr[2 * jb, b * pitch:b * pitch + n_chunks, :] = w[seq, :128]
            w_scr[2 * jb + 1, b * pitch:b * pitch + n_chunks, :] = w[seq, 128:]

    ar = [jnp.broadcast_to(coef_ref[0:1, j * 128:(j + 1) * 128], (n_batch, 128)) for j in range(n_x)]
    ai_a = [jnp.broadcast_to(coef_ref[1:2, j * 128:(j + 1) * 128], (n_batch, 128)) for j in range(n_x)]
    ai_b = [jnp.broadcast_to(coef_ref[2:3, j * 128:(j + 1) * 128], (n_batch, 128)) for j in range(n_x)]

    def step(k, carry):
        x, xsw = carry
        rows_k = pl.ds(k, n_batch, stride=pitch)
        new_x, new_xsw = [], []
        for j in range(n_x):
            x_scr[j, rows_k, :] = x[j]
            new_x.append(ar[j] * x[j] + ai_a[j] * xsw[j] + w_scr[j, rows_k, :])
            new_xsw.append(ar[j] * xsw[j] + ai_b[j] * x[j] + w_scr[n_x + j, rows_k, :])
        return tuple(new_x), tuple(new_xsw)

    zero = tuple(jnp.zeros((n_batch, 128), F32) for _ in range(n_x))
    lax.fori_loop(0, n_chunks, step, (zero, zero))

    xs = jnp.concatenate(
        [jnp.concatenate([x_scr[j, b * pitch:b * pitch + n_chunks, :] for b in range(n_batch)], axis=0)
         for j in range(n_x)], axis=1).astype(BF16)
    for jb in range(chunk // 2):
        kk = (jb + 1) * 256
        cols = slice(jb * 256, (jb + 1) * 256)
        y = jax.nn.gelu(_dot(u[:, :kk], t_ref[:kk, cols]) + _dot(xs, sout_ref[:, cols]))
        for half in range(2):
            ytok_scr[pl.ds(2 * jb + half, rows, stride=chunk), :] = y[:, half * 128:(half + 1) * 128]
    y_ref[...] = ytok_scr[...].astype(y_ref.dtype)


def _ssm_operators(a_re, a_im, log_dt, b_re, b_im, c_re, c_im, d_skip):
    hp = lax.Precision.HIGHEST
    n_grp, n_state, n_ch = b_re.shape
    L = SSM_CHUNK
    nb = SSM_LANE_GROUPS
    assert n_ch * nb == 128 and 2 * n_state == 128 and n_grp % nb == 0
    n_gb = n_grp // nb
    a = lax.complex(a_re.astype(F32), a_im.astype(F32))
    lam = a * jnp.exp(log_dt.astype(F32))[:, None]
    a_bar = jnp.exp(lam)
    b_bar = ((a_bar - 1.0) / a)[..., None] * lax.complex(b_re.astype(F32), b_im.astype(F32))
    cc = lax.complex(c_re.astype(F32), c_im.astype(F32))
    steps = jnp.arange(L + 1, dtype=F32)
    apow = jnp.exp(steps[:, None, None] * lam[None])
    lane = jnp.arange(128)
    put = ((lane[None, None, :] // n_ch == jnp.arange(nb)[:, None, None])
           & (lane[None, None, :] % n_ch == jnp.arange(n_ch)[None, :, None])).astype(F32)

    kern = jnp.real(jnp.einsum('gcp,lgp,gpd->glcd', cc, apow[:L], b_bar, precision=hp))
    d_op = jnp.einsum('Bgtcd,gdr,gcl->Btrl', kern.reshape(n_gb, nb, L, n_ch, n_ch), put, put, precision=hp)
    skip = jnp.eye(128, dtype=F32)[None] * d_skip.astype(F32).reshape(n_gb, 1, 128)
    d_op = d_op.at[:, 0].add(skip).astype(BF16)

    s_in = apow[:L][::-1].transpose(1, 0, 2)[:, :, None, :] * b_bar.transpose(0, 2, 1)[:, None, :, :]
    s_in = jnp.concatenate([s_in.real, s_in.imag, s_in.imag, s_in.real], axis=-1).astype(BF16)
    c_in = s_in.reshape(n_gb, nb, L, n_ch, 4 * n_state).transpose(0, 2, 1, 3, 4).reshape(n_gb, L * 128, 4 * n_state)

    s_out = cc.transpose(0, 2, 1)[:, :, None, :] * apow[1:].transpose(1, 2, 0)[:, :, :, None]
    s_out = jnp.concatenate([s_out.real, -s_out.imag], axis=1)
    x_idx = jnp.arange(L * n_ch)
    l_idx = jnp.arange(L * 128)
    place = ((x_idx[:, None] // n_ch == l_idx[None, :] // 128) & (x_idx[:, None] % n_ch == l_idx[None, :] % n_ch))
    place = place[None] & ((l_idx[None, None, :] // n_ch) % nb == jnp.arange(nb)[:, None, None])
    s_out = jnp.einsum('Bgqx,gxl->Bgql', s_out.reshape(n_gb, nb, 2 * n_state, L * n_ch), place.astype(F32),
                       precision=hp).astype(BF16).reshape(n_gb, nb * 2 * n_state, L * 128)

    a_l = apow[L]
    coef = jnp.stack([jnp.concatenate([a_l.real, a_l.real], -1),
                      jnp.concatenate([-a_l.imag, a_l.imag], -1),
                      jnp.concatenate([a_l.imag, -a_l.imag], -1)], axis=1)
    coef = coef.reshape(n_gb, nb, 3, 2 * n_state).transpose(0, 2, 1, 3).reshape(n_gb, 3, nb * 2 * n_state)
    coef = jnp.pad(coef, ((0, 0), (0, 5), (0, 0)))
    return d_op, c_in, s_out, coef


def _ssm_scan(u, d_op, c_in, s_out, coef, *, n_batch, chunk):
    rows_all, lanes_all = u.shape
    n_blk, lc, _ = c_in.shape
    n_chunks = rows_all // n_batch
    nb = min(SSM_BATCH_PER_STEP, n_batch)
    assert n_batch % nb == 0 and lanes_all == chunk * n_blk * 128
    rows = nb * n_chunks
    pitch = n_chunks + 4 if n_chunks % 8 == 0 else n_chunks
    n_x = s_out.shape[1] // 128
    kern = functools.partial(_ssm_kernel, chunk=chunk, n_batch=nb, n_chunks=n_chunks)
    u_specs = [pl.BlockSpec((rows, 128), lambda g, b, t=t: (b, t * n_blk + g)) for t in range(chunk)]

    def op_spec(shape):
        return pl.BlockSpec((None,) + shape, lambda g, b: (g,) + (0,) * len(shape))

    return pl.pallas_call(
        kern,
        grid=(n_blk, n_batch // nb),
        in_specs=u_specs + [op_spec((chunk, 128, 128)), op_spec((lc, c_in.shape[2])),
                            op_spec((n_x * 128, lc)), op_spec((8, n_x * 128))],
        out_specs=pl.BlockSpec((rows * chunk, 128), lambda g, b: (b, g)),
        out_shape=jax.ShapeDtypeStruct((rows_all * chunk, n_blk * 128), BF16),
        scratch_shapes=[pltpu.VMEM((lc, lc), BF16), pltpu.VMEM((lc, 2 * n_x * 128), BF16),
                        pltpu.VMEM((2 * n_x, nb * pitch, 128), F32), pltpu.VMEM((n_x, nb * pitch, 128), F32),
                        pltpu.VMEM((rows * chunk, 128), F32)],
        compiler_params=pltpu.CompilerParams(dimension_semantics=("arbitrary", "arbitrary"),
                                             vmem_limit_bytes=V7X_VMEM_LIMIT_BYTES),
        name="ssm_scan",
    )(*([u] * chunk), d_op, c_in, s_out, coef)


def _ssm_out_kernel(y_ref, h_ref, wout_ref, l1g_ref, l1b_ref, rw_ref, rb_ref,
                    h1_ref, h1p_ref, idx_ref, gate_ref, rank_ref, cnt_ref, res_scr, *, alpha):
    @pl.when(pl.program_id(0) == 0)
    def _():
        cnt_ref[...] = jnp.zeros_like(cnt_ref)
        res_scr[...] = jnp.zeros_like(res_scr)

    _mixer_tail(res_scr, l1g_ref, l1b_ref, rw_ref, rb_ref, h1_ref, h1p_ref, idx_ref, gate_ref, rank_ref, cnt_ref)

    y = y_ref[...]
    d = h_ref.shape[1]
    val = _dot(y, wout_ref[:, :d])
    gate = _dot(y, wout_ref[:, d:])
    res_scr[...] = alpha * h_ref[...] + val * jax.nn.sigmoid(gate)


def _ssm_layer(h, n_batch, w_in, a_re, a_im, log_dt, b_re, b_im, c_re, c_im, d_skip, w_out,
               l1g, l1b, rw2, rb, *, alpha):
    n_tok, d = h.shape
    n_grp, n_state, n_ch = b_re.shape
    width = n_grp * n_ch
    n_exp = rb.shape[0]
    L = SSM_CHUNK
    assert (n_tok // n_batch) % L == 0

    u = _ssm_in_proj(h, w_in, chunk=L)
    d_op, c_in, s_out, coef = _ssm_operators(a_re, a_im, log_dt, b_re, b_im, c_re, c_im, d_skip)
    y = _ssm_scan(u, d_op, c_in, s_out, coef, n_batch=n_batch, chunk=L)

    tm = min(TM_SSM_OUT, n_tok)
    assert n_tok % tm == 0
    r_shapes, r_specs = _router_out(n_tok, tm, n_exp)
    kern = functools.partial(_ssm_out_kernel, alpha=alpha)
    n_steps = n_tok // tm
    return pl.pallas_call(
        kern,
        grid=(n_steps + 1,),
        in_specs=[pl.BlockSpec((tm, width), lambda i: (jnp.minimum(i, n_steps - 1), 0)),
                  pl.BlockSpec((tm, d), lambda i: (jnp.minimum(i, n_steps - 1), 0)),
                  _const_spec((width, 2 * d)), _const_spec((1, d)), _const_spec((1, d)),
                  _const_spec((d, 256)), _const_spec((n_exp, 1))],
        out_specs=[pl.BlockSpec((tm, d), lambda i: (_prev(i), 0)),
                   pl.BlockSpec((tm * RECORD_ROWS, 128), lambda i: (_prev(i), 0))] + r_specs,
        out_shape=[jax.ShapeDtypeStruct((n_tok, d), F32),
                   jax.ShapeDtypeStruct((n_tok * RECORD_ROWS, 128), U32)] + r_shapes,
        scratch_shapes=[pltpu.VMEM((tm, d), F32)],
        compiler_params=pltpu.CompilerParams(dimension_semantics=("arbitrary",),
                                             vmem_limit_bytes=V7X_VMEM_LIMIT_BYTES),
        name="ssm_out_mixer",
    )(y, h, w_out.astype(BF16), l1g.reshape(1, -1), l1b.reshape(1, -1), rw2, rb)


def _moe_plan(idx, rank, cnt, tme, n_tiles):
    n_exp = cnt.shape[0]
    counts = cnt[:, 0].astype(I32)
    padded = ((counts + tme - 1) // tme) * tme
    ends = jnp.cumsum(padded)
    offs = ends - padded
    n_valid = ends[-1] // tme
    tile_start = jnp.arange(n_tiles, dtype=I32) * tme
    tile_expert = jnp.minimum(jnp.sum(tile_start[:, None] >= ends[None, :], axis=1), n_exp - 1).astype(I32)
    onehot = idx[:, :, None] == jnp.arange(n_exp, dtype=I32)[None, None, :]
    pos = jnp.sum(jnp.where(onehot, offs[None, None, :], 0), axis=-1) + rank
    meta = jnp.concatenate([counts, padded, offs, n_valid[None]]).astype(I32)
    return pos, tile_expert, meta


def _tile_pos(pos, tm):
    n_tok = pos.shape[1]
    return pos.reshape(2, n_tok // tm, tm).transpose(1, 0, 2).reshape(n_tok // tm, 1, 2 * tm)


def _record_slice(ref, row, n_rec=1):
    return ref.at[pl.ds(pl.multiple_of(row * RECORD_ROWS, RECORD_ROWS), n_rec * RECORD_ROWS), :]


def _dispatch_kernel(meta_ref, pos_ref, x_ref, xs_ref, zero_scr, sem, *, n_exp, tme, n_tiles):
    i = pl.program_id(0)
    tm = x_ref.shape[0] // RECORD_ROWS
    half = zero_scr.shape[0] // RECORD_ROWS

    def zero_fill(start, size):
        cp = pltpu.make_async_copy(_record_slice(zero_scr, 0, size), _record_slice(xs_ref, start, size), sem)
        cp.start()
        cp.wait()

    @pl.when(i == 0)
    def _():
        zero_scr[...] = jnp.zeros_like(zero_scr)
        for e in range(n_exp):
            pad = meta_ref[n_exp + e] - meta_ref[e]
            base = meta_ref[2 * n_exp + e] + meta_ref[e]
            for b in range(int(math.log2(tme))):
                size = 1 << b

                @pl.when(((pad >> b) & 1) == 1)
                def _():
                    zero_fill(base + (pad & (size - 1)), size)
        n_valid = meta_ref[3 * n_exp]
        for j in range(n_exp):
            @pl.when(n_valid + j < n_tiles)
            def _():
                for part in range(tme // half):
                    zero_fill((n_valid + j) * tme + part * half, half)

    def start(r, c):
        for slot in range(2):
            p = pos_ref[0, 0, slot * tm + r]
            pltpu.make_async_copy(_record_slice(x_ref, r), _record_slice(xs_ref, p), sem).start(priority=slot)
        return c

    lax.fori_loop(0, tm, start, 0, unroll=8)
    for slot in range(2):
        pltpu.make_async_copy(x_ref, _record_slice(xs_ref, 0, tm), sem).wait()


def _cast_specs(w_in, w_out, layer, n_steps, index):
    n_layers, n_exp, d, ff2 = w_in.shape
    ff = w_out.shape[2]
    assert (n_exp * d) % n_steps == 0 and (n_exp * ff) % n_steps == 0
    rin, rout = n_exp * d // n_steps, n_exp * ff // n_steps
    in_specs = [pl.BlockSpec((None, rin, ff2), lambda i, *_: (layer, index(i), 0)),
                pl.BlockSpec((None, rout, d), lambda i, *_: (layer, index(i), 0))]
    out_specs = [pl.BlockSpec((rin, ff2), lambda i, *_: (index(i), 0)),
                 pl.BlockSpec((rout, d), lambda i, *_: (index(i), 0))]
    out_shape = [jax.ShapeDtypeStruct((n_exp * d, ff2), BF16), jax.ShapeDtypeStruct((n_exp * ff, d), BF16)]
    operands = (w_in.reshape(n_layers, n_exp * d, ff2), w_out.reshape(n_layers, n_exp * ff, d))
    return in_specs, out_specs, out_shape, operands


def _moe_dispatch(h1p, pos, meta, *, n_exp, tme, n_tiles):
    n_tok = h1p.shape[0] // RECORD_ROWS
    tm = min(TM_DISPATCH, n_tok)
    assert n_tok % tm == 0
    half = max(tme // 2, 1)
    kern = functools.partial(_dispatch_kernel, n_exp=n_exp, tme=tme, n_tiles=n_tiles)
    return pl.pallas_call(
        kern,
        grid_spec=pltpu.PrefetchScalarGridSpec(
            num_scalar_prefetch=1,
            grid=(n_tok // tm,),
            in_specs=[pl.BlockSpec((1, 1, 2 * tm), lambda i, m: (i, 0, 0), memory_space=pltpu.SMEM),
                      pl.BlockSpec((tm * RECORD_ROWS, 128), lambda i, m: (i, 0))],
            out_specs=pl.BlockSpec(memory_space=pl.ANY),
            scratch_shapes=[pltpu.VMEM((half * RECORD_ROWS, 128), U32), pltpu.SemaphoreType.DMA]),
        out_shape=jax.ShapeDtypeStruct((n_tiles * tme * RECORD_ROWS, 128), U32),
        compiler_params=pltpu.CompilerParams(dimension_semantics=("arbitrary",),
                                             vmem_limit_bytes=V7X_VMEM_LIMIT_BYTES),
        name="moe_dispatch",
    )(meta, _tile_pos(pos, tm), h1p)


def _expert_kernel(te_ref, meta_ref, xs_ref, win_ref, wout_ref, *refs, n_exp, cast):
    if cast:
        nwin_ref, nwout_ref, ys_ref, nwin_b_ref, nwout_b_ref = refs
        nwin_b_ref[...] = nwin_ref[...].astype(BF16)
        nwout_b_ref[...] = nwout_ref[...].astype(BF16)
    else:
        (ys_ref,) = refs
    i = pl.program_id(0)
    n_valid = meta_ref[3 * n_exp]
    tme = xs_ref.shape[0] // RECORD_ROWS

    @pl.when(i < n_valid)
    def _():
        hi, lo = _load_records(xs_ref, tme)
        dh = hi.shape[1]
        ff = wout_ref.shape[0]
        h = _dot(hi.astype(BF16), win_ref[:dh, :]) + _dot(lo.astype(BF16), win_ref[dh:, :])
        act = (jax.nn.silu(h[:, :ff]) * h[:, ff:]).astype(BF16)
        _store_records(ys_ref, _dot(act, wout_ref[...]))

    @pl.when(i >= n_valid)
    def _():
        ys_ref[...] = jnp.zeros_like(ys_ref)


def _moe_experts(xs, tile_expert, meta, w_in, w_out, cast_weights, *, tme):
    n_rows = xs.shape[0] // RECORD_ROWS
    n_exp, d, ff2 = w_in.shape
    ff = w_out.shape[1]
    n_tiles = n_rows // tme

    def x_map(i, te, m):
        return (jnp.minimum(i, m[3 * n_exp] - 1), 0)

    cast = cast_weights is not None
    c_in, c_out, c_shape, c_ops = ([], [], [], ())
    if cast:
        n_cast = 1 << (n_tiles.bit_length() - 1)
        c_in, c_out, c_shape, c_ops = _cast_specs(*cast_weights, n_cast, lambda i: jnp.minimum(i, n_cast - 1))
    kern = functools.partial(_expert_kernel, n_exp=n_exp, cast=cast)
    return pl.pallas_call(
        kern,
        grid_spec=pltpu.PrefetchScalarGridSpec(
            num_scalar_prefetch=2,
            grid=(n_tiles,),
            in_specs=[pl.BlockSpec((tme * RECORD_ROWS, 128), x_map),
                      pl.BlockSpec((None, d, ff2), lambda i, te, m: (te[i], 0, 0)),
                      pl.BlockSpec((None, ff, d), lambda i, te, m: (te[i], 0, 0))] + c_in,
            out_specs=[pl.BlockSpec((tme * RECORD_ROWS, 128), lambda i, te, m: (i, 0))] + c_out),
        out_shape=[jax.ShapeDtypeStruct((n_rows * RECORD_ROWS, 128), U32)] + c_shape,
        compiler_params=pltpu.CompilerParams(dimension_semantics=("arbitrary",),
                                             vmem_limit_bytes=V7X_VMEM_LIMIT_BYTES),
        name="moe_experts",
    )(tile_expert, meta, xs, w_in, w_out, *c_ops)


def _combine_kernel(pos_ref, pos_next_ref, h1_ref, gate_ref, ys_ref, g_ref, b_ref, out_ref, buf, sems, *, alpha):
    i = pl.program_id(0)
    n_steps = pl.num_programs(0)
    tm = h1_ref.shape[0]

    def gather(p_ref, half):
        def start(r, c):
            for slot in range(2):
                p = p_ref[0, 0, slot * tm + r]
                pltpu.make_async_copy(_record_slice(ys_ref, p), _record_slice(buf.at[half, slot], r),
                                      sems.at[half]).start(priority=slot)
            return c

        lax.fori_loop(0, tm, start, 0, unroll=8)

    @pl.when(i == 0)
    def _():
        gather(pos_ref, 0)

    @pl.when(i + 1 < n_steps)
    def _():
        gather(pos_next_ref, (i + 1) % 2)

    cur = i % 2
    moe = None
    for slot in range(2):
        pltpu.make_async_copy(_record_slice(ys_ref, 0, tm), buf.at[cur, slot], sems.at[cur]).wait()
    for slot in range(2):
        hi, lo = _load_records(buf.at[cur, slot], tm)
        y = jnp.concatenate([hi, lo], axis=1) * gate_ref[:, slot:slot + 1]
        moe = y if moe is None else moe + y
    out_ref[...] = _layer_norm(alpha * h1_ref[...] + moe, g_ref[...], b_ref[...])


def _moe_combine(h1, ys, pos, gates, l2g, l2b, *, alpha):
    n_tok, d = h1.shape
    tm = min(TM_COMBINE, n_tok)
    assert n_tok % tm == 0
    n_steps = n_tok // tm
    kern = functools.partial(_combine_kernel, alpha=alpha)
    pos_tiles = _tile_pos(pos, tm)
    return pl.pallas_call(
        kern,
        grid=(n_steps,),
        in_specs=[pl.BlockSpec((1, 1, 2 * tm), lambda i: (i, 0, 0), memory_space=pltpu.SMEM),
                  pl.BlockSpec((1, 1, 2 * tm), lambda i: (jnp.minimum(i + 1, n_steps - 1), 0, 0),
                               memory_space=pltpu.SMEM),
                  pl.BlockSpec((tm, d), lambda i: (i, 0)),
                  pl.BlockSpec((tm, 2), lambda i: (i, 0)),
                  pl.BlockSpec(memory_space=pl.ANY),
                  _const_spec((1, d)), _const_spec((1, d))],
        out_specs=pl.BlockSpec((tm, d), lambda i: (i, 0)),
        out_shape=jax.ShapeDtypeStruct((n_tok, d), F32),
        scratch_shapes=[pltpu.VMEM((2, 2, tm * RECORD_ROWS, 128), U32), pltpu.SemaphoreType.DMA((2,))],
        compiler_params=pltpu.CompilerParams(dimension_semantics=("arbitrary",),
                                             vmem_limit_bytes=V7X_VMEM_LIMIT_BYTES),
        name="moe_combine",
    )(pos_tiles, pos_tiles, h1, gates.T, ys, l2g.reshape(1, -1), l2b.reshape(1, -1))


def _moe_layer(h1, h1p, idx, gates, rank, cnt, w_in_b, w_out_b, cast_next, l2g, l2b, *, alpha):
    n_tok, d = h1.shape
    n_exp = cnt.shape[0]
    ff = w_out_b.shape[0] // n_exp
    tme = min(TM_EXPERT, n_tok)
    n_tiles = (2 * n_tok) // tme + n_exp
    pos, tile_expert, meta = _moe_plan(idx, rank, cnt, tme, n_tiles)
    xs = _moe_dispatch(h1p, pos, meta, n_exp=n_exp, tme=tme, n_tiles=n_tiles)
    ys, *cast = _moe_experts(xs, tile_expert, meta, w_in_b.reshape(n_exp, d, 2 * ff), w_out_b.reshape(n_exp, ff, d),
                             cast_next, tme=tme)
    return _moe_combine(h1, ys, pos, gates, l2g, l2b, alpha=alpha), cast


def kernel(x, a_w_in, a_b_in, a_ln_g, a_ln_b, a_w_s, a_b_s, a_w_out, b_w_in, b_a_re, b_a_im, b_log_dt, b_b_re, b_b_im, b_c_re, b_c_im, b_d, b_w_out, ln1_g, ln1_b, router_w, router_b, e_w_in, e_w_out, ln2_g, ln2_b):
    n_batch, seq, d = x.shape
    depth = ln1_g.shape[0]
    alpha = float((2 * depth) ** 0.25)
    n_exp = router_w.shape[1]
    rw = jnp.pad(router_w.astype(F32), ((0, 0), (0, 128 - n_exp)))
    rw_hi = lax.reduce_precision(rw, exponent_bits=8, mantissa_bits=7)
    rw2 = jnp.concatenate([rw_hi, rw - rw_hi], axis=1).astype(BF16)
    rb = router_b.astype(F32).reshape(-1, 1)
    h = x.reshape(n_batch * seq, d)
    e_bf16 = None
    for i in range(depth):
        j = i // 2
        if i % 2 == 0:
            *mixed, e_in_b, e_out_b = _gmlp_layer(h, a_w_in[j], a_b_in[j], a_ln_g[j], a_ln_b[j], a_w_s[j], a_b_s[j],
                                                  a_w_out[j], ln1_g[i], ln1_b[i], rw2, rb, (e_w_in, e_w_out, i),
                                                  alpha=alpha)
            e_bf16 = [e_in_b, e_out_b]
        else:
            mixed = _ssm_layer(h, n_batch, b_w_in[j], b_a_re[j], b_a_im[j], b_log_dt[j], b_b_re[j],
                               b_b_im[j], b_c_re[j], b_c_im[j], b_d[j], b_w_out[j],
                               ln1_g[i], ln1_b[i], rw2, rb, alpha=alpha)
        cast_next = (e_w_in, e_w_out, i + 1) if (i % 2 == 0 and i + 1 < depth) else None
        h, e_bf16 = _moe_layer(*mixed, *e_bf16, cast_next, ln2_g[i], ln2_b[i], alpha=alpha)
    return h.reshape(n_batch, seq, d)
```

```python
import functools
import math

import jax
import jax.numpy as jnp
from jax import lax
from jax.experimental import pallas as pl
from jax.experimental.pallas import tpu as pltpu

F32 = jnp.float32
BF16 = jnp.bfloat16
I32 = jnp.int32
U32 = jnp.uint32

LN_EPS = 1e-5
N_EXPERT_GROUPS = 4
SSM_CHUNK = 16

TM_MIX = 256
TM_SSM_OUT = 512
TM_PROJ = 512
TM_DISPATCH = 512
TM_EXPERT = 512
TM_COMBINE = 256

V7X_VMEM_LIMIT_BYTES = 56 * 1024 * 1024


def _dot(a, b):
    return jnp.dot(a, b, preferred_element_type=F32)


def _layer_norm(x, g, b):
    mu = jnp.mean(x, axis=-1, keepdims=True)
    xc = x - mu
    var = jnp.sum(xc * xc, axis=-1, keepdims=True) * (1.0 / x.shape[-1])
    return xc * lax.rsqrt(var + LN_EPS) * g + b


def _const_spec(shape):
    return pl.BlockSpec(shape, lambda *_: (0,) * len(shape), pipeline_mode=pl.Buffered(1))


def _split_bf16(h1):
    h_top = lax.bitcast_convert_type(lax.bitcast_convert_type(h1, U32) & jnp.uint32(0xFFFF0000), F32)
    return h_top.astype(BF16), (h1 - h_top).astype(BF16)


def _route(h_hi, h_mid, rw_ref, rb_ref, idx_ref, gate_ref, rank_ref, cnt_ref, live):
    n_exp = rb_ref.shape[0]
    per_group = n_exp // N_EXPERT_GROUPS
    tm = h_hi.shape[0]
    first = _dot(h_hi, rw_ref[...])
    logits_t = first[:, :128] + first[:, 128:] + _dot(h_mid, rw_ref[:, :128])
    logits = logits_t.T[:n_exp, :] + rb_ref[...]
    m = jnp.max(logits, axis=0, keepdims=True)
    ex = jnp.exp(logits - m)
    probs = ex / jnp.sum(ex, axis=0, keepdims=True)

    best_score = None
    best_group = None
    for g in range(N_EXPERT_GROUPS):
        t1 = t2 = None
        for e in range(g * per_group, (g + 1) * per_group):
            v = probs[e:e + 1, :]
            if t1 is None:
                t1 = v
            elif t2 is None:
                t1, t2 = jnp.maximum(t1, v), jnp.minimum(t1, v)
            else:
                t2 = jnp.maximum(t2, jnp.minimum(t1, v))
                t1 = jnp.maximum(t1, v)
        score = t1 + t2
        if best_score is None:
            best_score, best_group = score, jnp.zeros((1, tm), I32)
        else:
            better = score > best_score
            best_group = jnp.where(better, g, best_group)
            best_score = jnp.where(better, score, best_score)

    eidx = lax.broadcasted_iota(I32, (n_exp, tm), 0)
    egrp = jnp.zeros((n_exp, tm), I32)
    for g in range(1, N_EXPERT_GROUPS):
        egrp = egrp + jnp.where(eidx >= g * per_group, 1, 0)
    in_group = egrp == best_group
    masked = jnp.where(in_group, probs, -1.0)
    m1 = jnp.max(masked, axis=0, keepdims=True)
    i1 = jnp.min(jnp.where(masked == m1, eidx, n_exp), axis=0, keepdims=True)
    sel1 = eidx == i1
    masked2 = jnp.where(sel1, -2.0, masked)
    m2 = jnp.max(masked2, axis=0, keepdims=True)
    i2 = jnp.min(jnp.where(masked2 == m2, eidx, n_exp), axis=0, keepdims=True)
    sel2 = eidx == i2
    den = m1 + m2

    cnt = jnp.where(sel1 | sel2, live, 0.0)
    r_i = lax.broadcasted_iota(I32, (tm, tm), 0)
    c_i = lax.broadcasted_iota(I32, (tm, tm), 1)
    strict_upper = jnp.where(r_i < c_i, 1.0, 0.0).astype(BF16)
    prefix = _dot(cnt.astype(BF16), strict_upper) + cnt_ref[:, 0:1]
    r1 = jnp.sum(jnp.where(sel1, prefix, 0.0), axis=0, keepdims=True)
    r2 = jnp.sum(jnp.where(sel2, prefix, 0.0), axis=0, keepdims=True)
    cnt_ref[...] += jnp.sum(cnt, axis=1, keepdims=True)

    idx_ref[...] = jnp.concatenate([i1, i2], axis=0)
    gate_ref[...] = jnp.concatenate([m1 / den, m2 / den], axis=0)
    rank_ref[...] = jnp.concatenate([r1, r2], axis=0).astype(I32)


RECORD_ROWS = 8


def _store_records(ref, h):
    tm, d = h.shape
    half = d // 2
    assert half == RECORD_ROWS * 128
    bits = lax.bitcast_convert_type(h.astype(BF16).astype(F32), U32)
    packed = (bits[:, :half] & jnp.uint32(0xFFFF0000)) | (bits[:, half:] >> 16)
    for s in range(RECORD_ROWS):
        ref[pl.ds(s, tm, stride=RECORD_ROWS), :] = packed[:, s * 128:(s + 1) * 128]


def _load_records(ref, tm):
    w = jnp.concatenate([ref[pl.ds(s, tm, stride=RECORD_ROWS), :] for s in range(RECORD_ROWS)], axis=1)
    hi = lax.bitcast_convert_type(w & jnp.uint32(0xFFFF0000), F32)
    lo = lax.bitcast_convert_type(w << 16, F32)
    return hi, lo


def _router_out(n_tok, tm, n_exp):
    shapes = [jax.ShapeDtypeStruct((2, n_tok), I32),
              jax.ShapeDtypeStruct((2, n_tok), F32),
              jax.ShapeDtypeStruct((2, n_tok), I32),
              jax.ShapeDtypeStruct((n_exp, 128), F32)]
    specs = [pl.BlockSpec((2, tm), lambda i: (0, _prev(i))),
             pl.BlockSpec((2, tm), lambda i: (0, _prev(i))),
             pl.BlockSpec((2, tm), lambda i: (0, _prev(i))),
             pl.BlockSpec((n_exp, 128), lambda i: (0, 0))]
    return shapes, specs


def _prev(i):
    return jnp.maximum(i - 1, 0)


def _tail_norm(res_scr, l1g_ref, l1b_ref, h1_ref, h1p_ref):
    h1 = _layer_norm(res_scr[...], l1g_ref[...], l1b_ref[...])
    h1_ref[...] = h1
    _store_records(h1p_ref, h1)
    return _split_bf16(h1)


def _tail_route(h_split, rw_ref, rb_ref, idx_ref, gate_ref, rank_ref, cnt_ref):
    live = jnp.where(pl.program_id(0) > 0, 1.0, 0.0)
    _route(*h_split, rw_ref, rb_ref, idx_ref, gate_ref, rank_ref, cnt_ref, live)


def _gmlp_kernel(x_ref, win_ref, bin_ref, lng_ref, lnb_ref, ws_ref, bst_ref, wout_ref,
                 l1g_ref, l1b_ref, rw_ref, rb_ref, ew_in_ref, ew_out_ref,
                 h1_ref, h1p_ref, idx_ref, gate_ref, rank_ref, cnt_ref, ew_in_b_ref, ew_out_b_ref,
                 p_scr, res_scr, *, alpha, chunk, groups):
    @pl.when(pl.program_id(0) == 0)
    def _():
        cnt_ref[...] = jnp.zeros_like(cnt_ref)
        res_scr[...] = jnp.zeros_like(res_scr)

    ew_in_b_ref[...] = ew_in_ref[...].astype(BF16)
    ew_out_b_ref[...] = ew_out_ref[...].astype(BF16)

    h_split = _tail_norm(res_scr, l1g_ref, l1b_ref, h1_ref, h1p_ref)

    x = x_ref[...]
    xb = x.astype(BF16)
    width = win_ref.shape[1] // 2
    gdim = width // groups
    tm = x.shape[0]

    zv = jax.nn.gelu(_dot(xb, win_ref[:, width:]) + bin_ref[:, width:])
    vn = _layer_norm(zv, lng_ref[...], lnb_ref[...]).astype(BF16)
    zu = jax.nn.gelu(_dot(xb, win_ref[:, :width]) + bin_ref[:, :width])

    r_i = lax.broadcasted_iota(I32, (chunk, chunk), 0)
    c_i = lax.broadcasted_iota(I32, (chunk, chunk), 1)
    causal = r_i >= c_i
    for g in range(groups):
        w = jnp.where(causal, ws_ref[g], 0.0).astype(BF16)
        bias = bst_ref[:, g:g + 1]
        for c in range(tm // chunk):
            rows = slice(c * chunk, (c + 1) * chunk)
            cols = slice(g * gdim, (g + 1) * gdim)
            s = _dot(w, vn[rows, cols]) + bias
            p_scr[rows, cols] = (zu[rows, cols] * s).astype(BF16)

    res_scr[...] = alpha * x + _dot(p_scr[...], wout_ref[...])
    _tail_route(h_split, rw_ref, rb_ref, idx_ref, gate_ref, rank_ref, cnt_ref)


def _gmlp_layer(h, w_in, b_in, ln_g, ln_b, w_s, b_s, w_out, l1g, l1b, rw2, rb, expert_weights, *, alpha):
    n_tok, d = h.shape
    groups, chunk, _ = w_s.shape
    width = w_in.shape[1] // 2
    n_exp = rb.shape[0]
    tm = min(TM_MIX, n_tok)
    assert n_tok % tm == 0 and tm % chunk == 0
    r_shapes, r_specs = _router_out(n_tok, tm, n_exp)
    kern = functools.partial(_gmlp_kernel, alpha=alpha, chunk=chunk, groups=groups)
    n_steps = n_tok // tm
    c_in, c_out, c_shape, c_ops = _cast_specs(*expert_weights, n_steps, lambda i: jnp.minimum(i, n_steps - 1))
    return pl.pallas_call(
        kern,
        grid=(n_steps + 1,),
        in_specs=[pl.BlockSpec((tm, d), lambda i: (jnp.minimum(i, n_steps - 1), 0)),
                  _const_spec((d, 2 * width)), _const_spec((1, 2 * width)),
                  _const_spec((1, width)), _const_spec((1, width)),
                  _const_spec((groups, chunk, chunk)), _const_spec((chunk, groups)),
                  _const_spec((width, d)), _const_spec((1, d)), _const_spec((1, d)),
                  _const_spec((d, 256)), _const_spec((n_exp, 1))] + c_in,
        out_specs=[pl.BlockSpec((tm, d), lambda i: (_prev(i), 0)),
                   pl.BlockSpec((tm * RECORD_ROWS, 128), lambda i: (_prev(i), 0))] + r_specs + c_out,
        out_shape=[jax.ShapeDtypeStruct((n_tok, d), F32),
                   jax.ShapeDtypeStruct((n_tok * RECORD_ROWS, 128), U32)] + r_shapes + c_shape,
        scratch_shapes=[pltpu.VMEM((tm, width), BF16), pltpu.VMEM((tm, d), F32)],
        compiler_params=pltpu.CompilerParams(dimension_semantics=("arbitrary",),
                                             vmem_limit_bytes=V7X_VMEM_LIMIT_BYTES),
        name="gmlp_mixer",
    )(h, w_in.astype(BF16), b_in.reshape(1, -1), ln_g.reshape(1, -1), ln_b.reshape(1, -1),
      w_s, b_s.T, w_out.astype(BF16), l1g.reshape(1, -1), l1b.reshape(1, -1), rw2, rb, *c_ops)


SSM_LANE_GROUPS = 8
SSM_BATCH_PER_STEP = 4


def _proj_kernel(x_ref, w_ref, o_ref, res_scr, *, chunk):
    res = _dot(x_ref[...].astype(BF16), w_ref[...])
    n_slab = res_scr.shape[0]
    kb = o_ref.shape[0]
    for j in range(n_slab):
        res_scr[j] = res[:, j * 128:(j + 1) * 128]
    for t in range(chunk):
        for j in range(n_slab):
            o_ref[:, (t * n_slab + j) * 128:(t * n_slab + j + 1) * 128] = (
                res_scr[j, pl.ds(t, kb, stride=chunk), :].astype(o_ref.dtype))


def _ssm_in_proj(h, w, *, chunk):
    n_tok, d = h.shape
    width = w.shape[1]
    tm = min(TM_PROJ, n_tok)
    assert n_tok % tm == 0 and tm % chunk == 0 and width % 128 == 0
    kern = functools.partial(_proj_kernel, chunk=chunk)
    return pl.pallas_call(
        kern,
        grid=(n_tok // tm,),
        in_specs=[pl.BlockSpec((tm, d), lambda i: (i, 0)), _const_spec((d, width))],
        out_specs=pl.BlockSpec((tm // chunk, chunk * width), lambda i: (i, 0)),
        out_shape=jax.ShapeDtypeStruct((n_tok // chunk, chunk * width), BF16),
        scratch_shapes=[pltpu.VMEM((width // 128, tm, 128), F32)],
        compiler_params=pltpu.CompilerParams(dimension_semantics=("arbitrary",),
                                             vmem_limit_bytes=V7X_VMEM_LIMIT_BYTES),
        name="ssm_in_proj",
    )(h, w.astype(BF16))


def _ssm_kernel(*refs, chunk, n_batch, n_chunks):
    u_refs = refs[:chunk]
    d_ref, cin_ref, sout_ref, coef_ref, y_ref, t_ref, sin_ref, w_scr, x_scr, ytok_scr = refs[chunk:]
    bblk = pl.program_id(1)
    rows = n_batch * n_chunks
    n_x = x_scr.shape[0]

    @pl.when(bblk == 0)
    def _():
        for tp in range(chunk):
            for t in range(chunk):
                blk = d_ref[t - tp] if t >= tp else jnp.zeros((128, 128), BF16)
                t_ref[tp * 128:(tp + 1) * 128, t * 128:(t + 1) * 128] = blk
        ch_bits = int(math.log2(128 // SSM_LANE_GROUPS))
        row = lax.broadcasted_iota(I32, (chunk * 128, 128), 0)
        row_group = lax.shift_right_logical(row, ch_bits) & (SSM_LANE_GROUPS - 1)
        for part in range(2 * n_x // SSM_LANE_GROUPS):
            src = cin_ref[:, part * 128:(part + 1) * 128]
            for g in range(SSM_LANE_GROUPS):
                col = (part * SSM_LANE_GROUPS + g) * 128
                sin_ref[:, col:col + 128] = jnp.where(row_group == g, src, jnp.zeros_like(src))

    pitch = w_scr.shape[1] // n_batch

    u = jnp.concatenate([r[...] for r in u_refs], axis=1)
    for jb in range(n_x):
        w = _dot(u, sin_ref[:, jb * 256:(jb + 1) * 256])
        for b in range(n_batch):
            seq = slice(b * n_chunks, (b + 1) * n_chunks)
            w_scr[2 * jb, b * pitch:b * pitch + n_chunks, :] = w[seq, :128]
            w_scr[2 * jb + 1, b * pitch:b * pitch + n_chunks, :] = w[seq, 128:]

    ar = [jnp.broadcast_to(coef_ref[0:1, j * 128:(j + 1) * 128], (n_batch, 128)) for j in range(n_x)]
    ai_a = [jnp.broadcast_to(coef_ref[1:2, j * 128:(j + 1) * 128], (n_batch, 128)) for j in range(n_x)]
    ai_b = [jnp.broadcast_to(coef_ref[2:3, j * 128:(j + 1) * 128], (n_batch, 128)) for j in range(n_x)]

    def step(k, carry):
        x, xsw = carry
        rows_k = pl.ds(k, n_batch, stride=pitch)
        new_x, new_xsw = [], []
        for j in range(n_x):
            x_scr[j, rows_k, :] = x[j]
            new_x.append(ar[j] * x[j] + ai_a[j] * xsw[j] + w_scr[j, rows_k, :])
            new_xsw.append(ar[j] * xsw[j] + ai_b[j] * x[j] + w_scr[n_x + j, rows_k, :])
        return tuple(new_x), tuple(new_xsw)

    zero = tuple(jnp.zeros((n_batch, 128), F32) for _ in range(n_x))
    lax.fori_loop(0, n_chunks, step, (zero, zero))

    xs = jnp.concatenate(
        [jnp.concatenate([x_scr[j, b * pitch:b * pitch + n_chunks, :] for b in range(n_batch)], axis=0)
         for j in range(n_x)], axis=1).astype(BF16)
    for jb in range(chunk // 2):
        kk = (jb + 1) * 256
        cols = slice(jb * 256, (jb + 1) * 256)
        y = jax.nn.gelu(_dot(u[:, :kk], t_ref[:kk, cols]) + _dot(xs, sout_ref[:, cols]))
        for half in range(2):
            ytok_scr[pl.ds(2 * jb + half, rows, stride=chunk), :] = y[:, half * 128:(half + 1) * 128]
    y_ref[...] = ytok_scr[...].astype(y_ref.dtype)


def _ssm_operators(a_re, a_im, log_dt, b_re, b_im, c_re, c_im, d_skip):
    hp = lax.Precision.HIGHEST
    n_grp, n_state, n_ch = b_re.shape
    L = SSM_CHUNK
    nb = SSM_LANE_GROUPS
    assert n_ch * nb == 128 and 2 * n_state == 128 and n_grp % nb == 0
    n_gb = n_grp // nb
    a = lax.complex(a_re.astype(F32), a_im.astype(F32))
    lam = a * jnp.exp(log_dt.astype(F32))[:, None]
    a_bar = jnp.exp(lam)
    b_bar = ((a_bar - 1.0) / a)[..., None] * lax.complex(b_re.astype(F32), b_im.astype(F32))
    cc = lax.complex(c_re.astype(F32), c_im.astype(F32))
    steps = jnp.arange(L + 1, dtype=F32)
    apow = jnp.exp(steps[:, None, None] * lam[None])
    lane = jnp.arange(128)
    put = ((lane[None, None, :] // n_ch == jnp.arange(nb)[:, None, None])
           & (lane[None, None, :] % n_ch == jnp.arange(n_ch)[None, :, None])).astype(F32)

    kern = jnp.real(jnp.einsum('gcp,lgp,gpd->glcd', cc, apow[:L], b_bar, precision=hp))
    d_op = jnp.einsum('Bgtcd,gdr,gcl->Btrl', kern.reshape(n_gb, nb, L, n_ch, n_ch), put, put, precision=hp)
    skip = jnp.eye(128, dtype=F32)[None] * d_skip.astype(F32).reshape(n_gb, 1, 128)
    d_op = d_op.at[:, 0].add(skip).astype(BF16)

    s_in = apow[:L][::-1].transpose(1, 0, 2)[:, :, None, :] * b_bar.transpose(0, 2, 1)[:, None, :, :]
    s_in = jnp.concatenate([s_in.real, s_in.imag, s_in.imag, s_in.real], axis=-1).astype(BF16)
    c_in = s_in.reshape(n_gb, nb, L, n_ch, 4 * n_state).transpose(0, 2, 1, 3, 4).reshape(n_gb, L * 128, 4 * n_state)

    s_out = cc.transpose(0, 2, 1)[:, :, None, :] * apow[1:].transpose(1, 2, 0)[:, :, :, None]
    s_out = jnp.concatenate([s_out.real, -s_out.imag], axis=1)
    x_idx = jnp.arange(L * n_ch)
    l_idx = jnp.arange(L * 128)
    place = ((x_idx[:, None] // n_ch == l_idx[None, :] // 128) & (x_idx[:, None] % n_ch == l_idx[None, :] % n_ch))
    place = place[None] & ((l_idx[None, None, :] // n_ch) % nb == jnp.arange(nb)[:, None, None])
    s_out = jnp.einsum('Bgqx,gxl->Bgql', s_out.reshape(n_gb, nb, 2 * n_state, L * n_ch), place.astype(F32),
                       precision=hp).astype(BF16).reshape(n_gb, nb * 2 * n_state, L * 128)

    a_l = apow[L]
    coef = jnp.stack([jnp.concatenate([a_l.real, a_l.real], -1),
                      jnp.concatenate([-a_l.imag, a_l.imag], -1),
                      jnp.concatenate([a_l.imag, -a_l.imag], -1)], axis=1)
    coef = coef.reshape(n_gb, nb, 3, 2 * n_state).transpose(0, 2, 1, 3).reshape(n_gb, 3, nb * 2 * n_state)
    coef = jnp.pad(coef, ((0, 0), (0, 5), (0, 0)))
    return d_op, c_in, s_out, coef


def _ssm_scan(u, d_op, c_in, s_out, coef, *, n_batch, chunk):
    rows_all, lanes_all = u.shape
    n_blk, lc, _ = c_in.shape
    n_chunks = rows_all // n_batch
    nb = min(SSM_BATCH_PER_STEP, n_batch)
    assert n_batch % nb == 0 and lanes_all == chunk * n_blk * 128
    rows = nb * n_chunks
    pitch = n_chunks + 4 if n_chunks % 8 == 0 else n_chunks
    n_x = s_out.shape[1] // 128
    kern = functools.partial(_ssm_kernel, chunk=chunk, n_batch=nb, n_chunks=n_chunks)
    u_specs = [pl.BlockSpec((rows, 128), lambda g, b, t=t: (b, t * n_blk + g)) for t in range(chunk)]

    def op_spec(shape):
        return pl.BlockSpec((None,) + shape, lambda g, b: (g,) + (0,) * len(shape))

    return pl.pallas_call(
        kern,
        grid=(n_blk, n_batch // nb),
        in_specs=u_specs + [op_spec((chunk, 128, 128)), op_spec((lc, c_in.shape[2])),
                            op_spec((n_x * 128, lc)), op_spec((8, n_x * 128))],
        out_specs=pl.BlockSpec((rows * chunk, 128), lambda g, b: (b, g)),
        out_shape=jax.ShapeDtypeStruct((rows_all * chunk, n_blk * 128), BF16),
        scratch_shapes=[pltpu.VMEM((lc, lc), BF16), pltpu.VMEM((lc, 2 * n_x * 128), BF16),
                        pltpu.VMEM((2 * n_x, nb * pitch, 128), F32), pltpu.VMEM((n_x, nb * pitch, 128), F32),
                        pltpu.VMEM((rows * chunk, 128), F32)],
        compiler_params=pltpu.CompilerParams(dimension_semantics=("arbitrary", "arbitrary"),
                                             vmem_limit_bytes=V7X_VMEM_LIMIT_BYTES),
        name="ssm_scan",
    )(*([u] * chunk), d_op, c_in, s_out, coef)


def _ssm_out_kernel(y_ref, h_ref, wout_ref, l1g_ref, l1b_ref, rw_ref, rb_ref,
                    h1_ref, h1p_ref, idx_ref, gate_ref, rank_ref, cnt_ref, res_scr, *, alpha):
    @pl.when(pl.program_id(0) == 0)
    def _():
        cnt_ref[...] = jnp.zeros_like(cnt_ref)
        res_scr[...] = jnp.zeros_like(res_scr)

    h_split = _tail_norm(res_scr, l1g_ref, l1b_ref, h1_ref, h1p_ref)

    y = y_ref[...]
    d = h_ref.shape[1]
    val = _dot(y, wout_ref[:, :d])
    gate = _dot(y, wout_ref[:, d:])
    res_scr[...] = alpha * h_ref[...] + val * jax.nn.sigmoid(gate)
    _tail_route(h_split, rw_ref, rb_ref, idx_ref, gate_ref, rank_ref, cnt_ref)


def _ssm_layer(h, n_batch, w_in, a_re, a_im, log_dt, b_re, b_im, c_re, c_im, d_skip, w_out,
               l1g, l1b, rw2, rb, *, alpha):
    n_tok, d = h.shape
    n_grp, n_state, n_ch = b_re.shape
    width = n_grp * n_ch
    n_exp = rb.shape[0]
    L = SSM_CHUNK
    assert (n_tok // n_batch) % L == 0

    u = _ssm_in_proj(h, w_in, chunk=L)
    d_op, c_in, s_out, coef = _ssm_operators(a_re, a_im, log_dt, b_re, b_im, c_re, c_im, d_skip)
    y = _ssm_scan(u, d_op, c_in, s_out, coef, n_batch=n_batch, chunk=L)

    tm = min(TM_SSM_OUT, n_tok)
    assert n_tok % tm == 0
    r_shapes, r_specs = _router_out(n_tok, tm, n_exp)
    kern = functools.partial(_ssm_out_kernel, alpha=alpha)
    n_steps = n_tok // tm
    return pl.pallas_call(
        kern,
        grid=(n_steps + 1,),
        in_specs=[pl.BlockSpec((tm, width), lambda i: (jnp.minimum(i, n_steps - 1), 0)),
                  pl.BlockSpec((tm, d), lambda i: (jnp.minimum(i, n_steps - 1), 0)),
                  _const_spec((width, 2 * d)), _const_spec((1, d)), _const_spec((1, d)),
                  _const_spec((d, 256)), _const_spec((n_exp, 1))],
        out_specs=[pl.BlockSpec((tm, d), lambda i: (_prev(i), 0)),
                   pl.BlockSpec((tm * RECORD_ROWS, 128), lambda i: (_prev(i), 0))] + r_specs,
        out_shape=[jax.ShapeDtypeStruct((n_tok, d), F32),
                   jax.ShapeDtypeStruct((n_tok * RECORD_ROWS, 128), U32)] + r_shapes,
        scratch_shapes=[pltpu.VMEM((tm, d), F32)],
        compiler_params=pltpu.CompilerParams(dimension_semantics=("arbitrary",),
                                             vmem_limit_bytes=V7X_VMEM_LIMIT_BYTES),
        name="ssm_out_mixer",
    )(y, h, w_out.astype(BF16), l1g.reshape(1, -1), l1b.reshape(1, -1), rw2, rb)


def _moe_plan(idx, rank, cnt, tme, n_tiles):
    n_exp = cnt.shape[0]
    counts = cnt[:, 0].astype(I32)
    padded = ((counts + tme - 1) // tme) * tme
    ends = jnp.cumsum(padded)
    offs = ends - padded
    n_valid = ends[-1] // tme
    tile_start = jnp.arange(n_tiles, dtype=I32) * tme
    tile_expert = jnp.minimum(jnp.sum(tile_start[:, None] >= ends[None, :], axis=1), n_exp - 1).astype(I32)
    onehot = idx[:, :, None] == jnp.arange(n_exp, dtype=I32)[None, None, :]
    pos = jnp.sum(jnp.where(onehot, offs[None, None, :], 0), axis=-1) + rank
    meta = jnp.concatenate([counts, padded, offs, n_valid[None]]).astype(I32)
    return pos, tile_expert, meta


def _tile_pos(pos, tm):
    n_tok = pos.shape[1]
    return pos.reshape(2, n_tok // tm, tm).transpose(1, 0, 2).reshape(n_tok // tm, 1, 2 * tm)


def _record_slice(ref, row, n_rec=1):
    return ref.at[pl.ds(pl.multiple_of(row * RECORD_ROWS, RECORD_ROWS), n_rec * RECORD_ROWS), :]


def _dispatch_kernel(meta_ref, pos_ref, x_ref, xs_ref, zero_scr, sem, *, n_exp, tme, n_tiles):
    i = pl.program_id(0)
    tm = x_ref.shape[0] // RECORD_ROWS
    half = zero_scr.shape[0] // RECORD_ROWS

    def zero_fill(start, size):
        cp = pltpu.make_async_copy(_record_slice(zero_scr, 0, size), _record_slice(xs_ref, start, size), sem)
        cp.start()
        cp.wait()

    @pl.when(i == 0)
    def _():
        zero_scr[...] = jnp.zeros_like(zero_scr)
        for e in range(n_exp):
            pad = meta_ref[n_exp + e] - meta_ref[e]
            base = meta_ref[2 * n_exp + e] + meta_ref[e]
            for b in range(int(math.log2(tme))):
                size = 1 << b

                @pl.when(((pad >> b) & 1) == 1)
                def _():
                    zero_fill(base + (pad & (size - 1)), size)
        n_valid = meta_ref[3 * n_exp]
        for j in range(n_exp):
            @pl.when(n_valid + j < n_tiles)
            def _():
                for part in range(tme // half):
                    zero_fill((n_valid + j) * tme + part * half, half)

    def start(r, c):
        for slot in range(2):
            p = pos_ref[0, 0, slot * tm + r]
            pltpu.make_async_copy(_record_slice(x_ref, r), _record_slice(xs_ref, p), sem).start(priority=slot)
        return c

    lax.fori_loop(0, tm, start, 0, unroll=8)
    for slot in range(2):
        pltpu.make_async_copy(x_ref, _record_slice(xs_ref, 0, tm), sem).wait()


def _cast_specs(w_in, w_out, layer, n_steps, index):
    n_layers, n_exp, d, ff2 = w_in.shape
    ff = w_out.shape[2]
    assert (n_exp * d) % n_steps == 0 and (n_exp * ff) % n_steps == 0
    rin, rout = n_exp * d // n_steps, n_exp * ff // n_steps
    in_specs = [pl.BlockSpec((None, rin, ff2), lambda i, *_: (layer, index(i), 0)),
                pl.BlockSpec((None, rout, d), lambda i, *_: (layer, index(i), 0))]
    out_specs = [pl.BlockSpec((rin, ff2), lambda i, *_: (index(i), 0)),
                 pl.BlockSpec((rout, d), lambda i, *_: (index(i), 0))]
    out_shape = [jax.ShapeDtypeStruct((n_exp * d, ff2), BF16), jax.ShapeDtypeStruct((n_exp * ff, d), BF16)]
    operands = (w_in.reshape(n_layers, n_exp * d, ff2), w_out.reshape(n_layers, n_exp * ff, d))
    return in_specs, out_specs, out_shape, operands


def _moe_dispatch(h1p, pos, meta, *, n_exp, tme, n_tiles):
    n_tok = h1p.shape[0] // RECORD_ROWS
    tm = min(TM_DISPATCH, n_tok)
    assert n_tok % tm == 0
    half = max(tme // 2, 1)
    kern = functools.partial(_dispatch_kernel, n_exp=n_exp, tme=tme, n_tiles=n_tiles)
    return pl.pallas_call(
        kern,
        grid_spec=pltpu.PrefetchScalarGridSpec(
            num_scalar_prefetch=1,
            grid=(n_tok // tm,),
            in_specs=[pl.BlockSpec((1, 1, 2 * tm), lambda i, m: (i, 0, 0), memory_space=pltpu.SMEM),
                      pl.BlockSpec((tm * RECORD_ROWS, 128), lambda i, m: (i, 0))],
            out_specs=pl.BlockSpec(memory_space=pl.ANY),
            scratch_shapes=[pltpu.VMEM((half * RECORD_ROWS, 128), U32), pltpu.SemaphoreType.DMA]),
        out_shape=jax.ShapeDtypeStruct((n_tiles * tme * RECORD_ROWS, 128), U32),
        compiler_params=pltpu.CompilerParams(dimension_semantics=("arbitrary",),
                                             vmem_limit_bytes=V7X_VMEM_LIMIT_BYTES),
        name="moe_dispatch",
    )(meta, _tile_pos(pos, tm), h1p)


def _expert_kernel(te_ref, meta_ref, xs_ref, win_ref, wout_ref, *refs, n_exp, cast):
    if cast:
        nwin_ref, nwout_ref, ys_ref, nwin_b_ref, nwout_b_ref = refs
        nwin_b_ref[...] = nwin_ref[...].astype(BF16)
        nwout_b_ref[...] = nwout_ref[...].astype(BF16)
    else:
        (ys_ref,) = refs
    i = pl.program_id(0)
    n_valid = meta_ref[3 * n_exp]
    tme = xs_ref.shape[0] // RECORD_ROWS

    @pl.when(i < n_valid)
    def _():
        hi, lo = _load_records(xs_ref, tme)
        dh = hi.shape[1]
        ff = wout_ref.shape[0]
        h = _dot(hi.astype(BF16), win_ref[:dh, :]) + _dot(lo.astype(BF16), win_ref[dh:, :])
        act = (jax.nn.silu(h[:, :ff]) * h[:, ff:]).astype(BF16)
        _store_records(ys_ref, _dot(act, wout_ref[...]))

    @pl.when(i >= n_valid)
    def _():
        ys_ref[...] = jnp.zeros_like(ys_ref)


def _moe_experts(xs, tile_expert, meta, w_in, w_out, cast_weights, *, tme):
    n_rows = xs.shape[0] // RECORD_ROWS
    n_exp, d, ff2 = w_in.shape
    ff = w_out.shape[1]
    n_tiles = n_rows // tme

    def x_map(i, te, m):
        return (jnp.minimum(i, m[3 * n_exp] - 1), 0)

    cast = cast_weights is not None
    c_in, c_out, c_shape, c_ops = ([], [], [], ())
    if cast:
        n_cast = 1 << (n_tiles.bit_length() - 1)
        c_in, c_out, c_shape, c_ops = _cast_specs(*cast_weights, n_cast, lambda i: jnp.minimum(i, n_cast - 1))
    kern = functools.partial(_expert_kernel, n_exp=n_exp, cast=cast)
    return pl.pallas_call(
        kern,
        grid_spec=pltpu.PrefetchScalarGridSpec(
            num_scalar_prefetch=2,
            grid=(n_tiles,),
            in_specs=[pl.BlockSpec((tme * RECORD_ROWS, 128), x_map),
                      pl.BlockSpec((None, d, ff2), lambda i, te, m: (te[i], 0, 0)),
                      pl.BlockSpec((None, ff, d), lambda i, te, m: (te[i], 0, 0))] + c_in,
            out_specs=[pl.BlockSpec((tme * RECORD_ROWS, 128), lambda i, te, m: (i, 0))] + c_out),
        out_shape=[jax.ShapeDtypeStruct((n_rows * RECORD_ROWS, 128), U32)] + c_shape,
        compiler_params=pltpu.CompilerParams(dimension_semantics=("arbitrary",),
                                             vmem_limit_bytes=V7X_VMEM_LIMIT_BYTES),
        name="moe_experts",
    )(tile_expert, meta, xs, w_in, w_out, *c_ops)


def _combine_kernel(pos_ref, pos_next_ref, h1_ref, gate_ref, ys_ref, g_ref, b_ref, out_ref, buf, sems, *, alpha):
    i = pl.program_id(0)
    n_steps = pl.num_programs(0)
    tm = h1_ref.shape[0]

    def gather(p_ref, half):
        def start(r, c):
            for slot in range(2):
                p = p_ref[0, 0, slot * tm + r]
                pltpu.make_async_copy(_record_slice(ys_ref, p), _record_slice(buf.at[half, slot], r),
                                      sems.at[half]).start(priority=slot)
            return c

        lax.fori_loop(0, tm, start, 0, unroll=8)

    @pl.when(i == 0)
    def _():
        gather(pos_ref, 0)

    @pl.when(i + 1 < n_steps)
    def _():
        gather(pos_next_ref, (i + 1) % 2)

    cur = i % 2
    moe = None
    for slot in range(2):
        pltpu.make_async_copy(_record_slice(ys_ref, 0, tm), buf.at[cur, slot], sems.at[cur]).wait()
    for slot in range(2):
        hi, lo = _load_records(buf.at[cur, slot], tm)
        y = jnp.concatenate([hi, lo], axis=1) * gate_ref[:, slot:slot + 1]
        moe = y if moe is None else moe + y
    out_ref[...] = _layer_norm(alpha * h1_ref[...] + moe, g_ref[...], b_ref[...])


def _moe_combine(h1, ys, pos, gates, l2g, l2b, *, alpha):
    n_tok, d = h1.shape
    tm = min(TM_COMBINE, n_tok)
    assert n_tok % tm == 0
    n_steps = n_tok // tm
    kern = functools.partial(_combine_kernel, alpha=alpha)
    pos_tiles = _tile_pos(pos, tm)
    return pl.pallas_call(
        kern,
        grid=(n_steps,),
        in_specs=[pl.BlockSpec((1, 1, 2 * tm), lambda i: (i, 0, 0), memory_space=pltpu.SMEM),
                  pl.BlockSpec((1, 1, 2 * tm), lambda i: (jnp.minimum(i + 1, n_steps - 1), 0, 0),
                               memory_space=pltpu.SMEM),
                  pl.BlockSpec((tm, d), lambda i: (i, 0)),
                  pl.BlockSpec((tm, 2), lambda i: (i, 0)),
                  pl.BlockSpec(memory_space=pl.ANY),
                  _const_spec((1, d)), _const_spec((1, d))],
        out_specs=pl.BlockSpec((tm, d), lambda i: (i, 0)),
        out_shape=jax.ShapeDtypeStruct((n_tok, d), F32),
        scratch_shapes=[pltpu.VMEM((2, 2, tm * RECORD_ROWS, 128), U32), pltpu.SemaphoreType.DMA((2,))],
        compiler_params=pltpu.CompilerParams(dimension_semantics=("arbitrary",),
                                             vmem_limit_bytes=V7X_VMEM_LIMIT_BYTES),
        name="moe_combine",
    )(pos_tiles, pos_tiles, h1, gates.T, ys, l2g.reshape(1, -1), l2b.reshape(1, -1))


def _moe_layer(h1, h1p, idx, gates, rank, cnt, w_in_b, w_out_b, cast_next, l2g, l2b, *, alpha):
    n_tok, d = h1.shape
    n_exp = cnt.shape[0]
    ff = w_out_b.shape[0] // n_exp
    tme = min(TM_EXPERT, n_tok)
    n_tiles = (2 * n_tok) // tme + n_exp
    pos, tile_expert, meta = _moe_plan(idx, rank, cnt, tme, n_tiles)
    xs = _moe_dispatch(h1p, pos, meta, n_exp=n_exp, tme=tme, n_tiles=n_tiles)
    ys, *cast = _moe_experts(xs, tile_expert, meta, w_in_b.reshape(n_exp, d, 2 * ff), w_out_b.reshape(n_exp, ff, d),
                             cast_next, tme=tme)
    return _moe_combine(h1, ys, pos, gates, l2g, l2b, alpha=alpha), cast


def kernel(x, a_w_in, a_b_in, a_ln_g, a_ln_b, a_w_s, a_b_s, a_w_out, b_w_in, b_a_re, b_a_im, b_log_dt, b_b_re, b_b_im, b_c_re, b_c_im, b_d, b_w_out, ln1_g, ln1_b, router_w, router_b, e_w_in, e_w_out, ln2_g, ln2_b):
    n_batch, seq, d = x.shape
    depth = ln1_g.shape[0]
    alpha = float((2 * depth) ** 0.25)
    n_exp = router_w.shape[1]
    rw = jnp.pad(router_w.astype(F32), ((0, 0), (0, 128 - n_exp)))
    rw_hi = lax.reduce_precision(rw, exponent_bits=8, mantissa_bits=7)
    rw2 = jnp.concatenate([rw_hi, rw - rw_hi], axis=1).astype(BF16)
    rb = router_b.astype(F32).reshape(-1, 1)
    h = x.reshape(n_batch * seq, d)
    e_bf16 = None
    for i in range(depth):
        j = i // 2
        if i % 2 == 0:
            *mixed, e_in_b, e_out_b = _gmlp_layer(h, a_w_in[j], a_b_in[j], a_ln_g[j], a_ln_b[j], a_w_s[j], a_b_s[j],
                                                  a_w_out[j], ln1_g[i], ln1_b[i], rw2, rb, (e_w_in, e_w_out, i),
                                                  alpha=alpha)
            e_bf16 = [e_in_b, e_out_b]
        else:
            mixed = _ssm_layer(h, n_batch, b_w_in[j], b_a_re[j], b_a_im[j], b_log_dt[j], b_b_re[j],
                               b_b_im[j], b_c_re[j], b_c_im[j], b_d[j], b_w_out[j],
                               ln1_g[i], ln1_b[i], rw2, rb, alpha=alpha)
        cast_next = (e_w_in, e_w_out, i + 1) if (i % 2 == 0 and i + 1 < depth) else None
        h, e_bf16 = _moe_layer(*mixed, *e_bf16, cast_next, ln2_g[i], ln2_b[i], alpha=alpha)
    return h.reshape(n_batch, seq, d)
```

```python
import functools
import math

import jax
import jax.numpy as jnp
from jax import lax
from jax.experimental import pallas as pl
from jax.experimental.pallas import tpu as pltpu

F32 = jnp.float32
BF16 = jnp.bfloat16
I32 = jnp.int32
U32 = jnp.uint32

LN_EPS = 1e-5
N_EXPERT_GROUPS = 4
SSM_CHUNK = 16

TM_MIX = 256
TM_SSM_OUT = 512
TM_PROJ = 512
TM_DISPATCH = 512
TM_EXPERT = 512
TM_COMBINE = 256

V7X_VMEM_LIMIT_BYTES = 56 * 1024 * 1024


def _dot(a, b):
    return jnp.dot(a, b, preferred_element_type=F32)


def _layer_norm(x, g, b):
    mu = jnp.mean(x, axis=-1, keepdims=True)
    xc = x - mu
    var = jnp.sum(xc * xc, axis=-1, keepdims=True) * (1.0 / x.shape[-1])
    return xc * lax.rsqrt(var + LN_EPS) * g + b


def _const_spec(shape):
    return pl.BlockSpec(shape, lambda *_: (0,) * len(shape), pipeline_mode=pl.Buffered(1))


def _split_bf16(h1):
    h_top = lax.bitcast_convert_type(lax.bitcast_convert_type(h1, U32) & jnp.uint32(0xFFFF0000), F32)
    return h_top.astype(BF16), (h1 - h_top).astype(BF16)


def _route(h_hi, h_mid, rw_ref, rb_ref, idx_ref, gate_ref, rank_ref, cnt_ref, live):
    n_exp = rb_ref.shape[0]
    per_group = n_exp // N_EXPERT_GROUPS
    tm = h_hi.shape[0]
    first = _dot(h_hi, rw_ref[...])
    logits_t = first[:, :128] + first[:, 128:] + _dot(h_mid, rw_ref[:, :128])
    logits = logits_t.T[:n_exp, :] + rb_ref[...]
    m = jnp.max(logits, axis=0, keepdims=True)
    ex = jnp.exp(logits - m)
    probs = ex / jnp.sum(ex, axis=0, keepdims=True)

    best_score = None
    best_group = None
    for g in range(N_EXPERT_GROUPS):
        t1 = t2 = None
        for e in range(g * per_group, (g + 1) * per_group):
            v = probs[e:e + 1, :]
            if t1 is None:
                t1 = v
            elif t2 is None:
                t1, t2 = jnp.maximum(t1, v), jnp.minimum(t1, v)
            else:
                t2 = jnp.maximum(t2, jnp.minimum(t1, v))
                t1 = jnp.maximum(t1, v)
        score = t1 + t2
        if best_score is None:
            best_score, best_group = score, jnp.zeros((1, tm), I32)
        else:
            better = score > best_score
            best_group = jnp.where(better, g, best_group)
            best_score = jnp.where(better, score, best_score)

    eidx = lax.broadcasted_iota(I32, (n_exp, tm), 0)
    egrp = jnp.zeros((n_exp, tm), I32)
    for g in range(1, N_EXPERT_GROUPS):
        egrp = egrp + jnp.where(eidx >= g * per_group, 1, 0)
    in_group = egrp == best_group
    masked = jnp.where(in_group, probs, -1.0)
    m1 = jnp.max(masked, axis=0, keepdims=True)
    i1 = jnp.min(jnp.where(masked == m1, eidx, n_exp), axis=0, keepdims=True)
    sel1 = eidx == i1
    masked2 = jnp.where(sel1, -2.0, masked)
    m2 = jnp.max(masked2, axis=0, keepdims=True)
    i2 = jnp.min(jnp.where(masked2 == m2, eidx, n_exp), axis=0, keepdims=True)
    sel2 = eidx == i2
    den = m1 + m2

    cnt = jnp.where(sel1 | sel2, live, 0.0)
    r_i = lax.broadcasted_iota(I32, (tm, tm), 0)
    c_i = lax.broadcasted_iota(I32, (tm, tm), 1)
    strict_upper = jnp.where(r_i < c_i, 1.0, 0.0).astype(BF16)
    prefix = _dot(cnt.astype(BF16), strict_upper) + cnt_ref[:, 0:1]
    r1 = jnp.sum(jnp.where(sel1, prefix, 0.0), axis=0, keepdims=True)
    r2 = jnp.sum(jnp.where(sel2, prefix, 0.0), axis=0, keepdims=True)
    cnt_ref[...] += jnp.sum(cnt, axis=1, keepdims=True)

    idx_ref[...] = jnp.concatenate([i1, i2], axis=0)
    gate_ref[...] = jnp.concatenate([m1 / den, m2 / den], axis=0)
    rank_ref[...] = jnp.concatenate([r1, r2], axis=0).astype(I32)


RECORD_ROWS = 8


def _store_records(ref, h):
    tm, d = h.shape
    half = d // 2
    assert half == RECORD_ROWS * 128
    bits = lax.bitcast_convert_type(h.astype(BF16).astype(F32), U32)
    packed = (bits[:, :half] & jnp.uint32(0xFFFF0000)) | (bits[:, half:] >> 16)
    for s in range(RECORD_ROWS):
        ref[pl.ds(s, tm, stride=RECORD_ROWS), :] = packed[:, s * 128:(s + 1) * 128]


def _load_records(ref, tm):
    w = jnp.concatenate([ref[pl.ds(s, tm, stride=RECORD_ROWS), :] for s in range(RECORD_ROWS)], axis=1)
    hi = lax.bitcast_convert_type(w & jnp.uint32(0xFFFF0000), F32)
    lo = lax.bitcast_convert_type(w << 16, F32)
    return hi, lo


def _router_out(n_tok, tm, n_exp):
    shapes = [jax.ShapeDtypeStruct((2, n_tok), I32),
              jax.ShapeDtypeStruct((2, n_tok), F32),
              jax.ShapeDtypeStruct((2, n_tok), I32),
              jax.ShapeDtypeStruct((n_exp, 128), F32)]
    specs = [pl.BlockSpec((2, tm), lambda i: (0, _prev(i))),
             pl.BlockSpec((2, tm), lambda i: (0, _prev(i))),
             pl.BlockSpec((2, tm), lambda i: (0, _prev(i))),
             pl.BlockSpec((n_exp, 128), lambda i: (0, 0))]
    return shapes, specs


def _prev(i):
    return jnp.maximum(i - 1, 0)


def _tail_norm(res_scr, l1g_ref, l1b_ref, h1_ref, h1p_ref):
    h1 = _layer_norm(res_scr[...], l1g_ref[...], l1b_ref[...])
    h1_ref[...] = h1
    _store_records(h1p_ref, h1)
    return _split_bf16(h1)


def _tail_route(h_split, rw_ref, rb_ref, idx_ref, gate_ref, rank_ref, cnt_ref):
    live = jnp.where(pl.program_id(0) > 0, 1.0, 0.0)
    _route(*h_split, rw_ref, rb_ref, idx_ref, gate_ref, rank_ref, cnt_ref, live)


def _gmlp_kernel(x_ref, win_ref, bin_ref, lng_ref, lnb_ref, ws_ref, bst_ref, wout_ref,
                 l1g_ref, l1b_ref, rw_ref, rb_ref, ew_in_ref, ew_out_ref,
                 h1_ref, h1p_ref, idx_ref, gate_ref, rank_ref, cnt_ref, ew_in_b_ref, ew_out_b_ref,
                 p_scr, res_scr, *, alpha, chunk, groups):
    @pl.when(pl.program_id(0) == 0)
    def _():
        cnt_ref[...] = jnp.zeros_like(cnt_ref)
        res_scr[...] = jnp.zeros_like(res_scr)

    ew_in_b_ref[...] = ew_in_ref[...].astype(BF16)
    ew_out_b_ref[...] = ew_out_ref[...].astype(BF16)

    _tail_route(_tail_norm(res_scr, l1g_ref, l1b_ref, h1_ref, h1p_ref), rw_ref, rb_ref, idx_ref, gate_ref, rank_ref,
                cnt_ref)

    x = x_ref[...]
    xb = x.astype(BF16)
    width = win_ref.shape[1] // 2
    gdim = width // groups
    tm = x.shape[0]

    zv = jax.nn.gelu(_dot(xb, win_ref[:, width:]) + bin_ref[:, width:])
    vn = _layer_norm(zv, lng_ref[...], lnb_ref[...]).astype(BF16)
    zu = jax.nn.gelu(_dot(xb, win_ref[:, :width]) + bin_ref[:, :width])

    r_i = lax.broadcasted_iota(I32, (chunk, chunk), 0)
    c_i = lax.broadcasted_iota(I32, (chunk, chunk), 1)
    causal = r_i >= c_i
    for g in range(groups):
        w = jnp.where(causal, ws_ref[g], 0.0).astype(BF16)
        bias = bst_ref[:, g:g + 1]
        for c in range(tm // chunk):
            rows = slice(c * chunk, (c + 1) * chunk)
            cols = slice(g * gdim, (g + 1) * gdim)
            s = _dot(w, vn[rows, cols]) + bias
            p_scr[rows, cols] = (zu[rows, cols] * s).astype(BF16)

    res_scr[...] = alpha * x + _dot(p_scr[...], wout_ref[...])


def _gmlp_layer(h, w_in, b_in, ln_g, ln_b, w_s, b_s, w_out, l1g, l1b, rw2, rb, expert_weights, *, alpha):
    n_tok, d = h.shape
    groups, chunk, _ = w_s.shape
    width = w_in.shape[1] // 2
    n_exp = rb.shape[0]
    tm = min(TM_MIX, n_tok)
    assert n_tok % tm == 0 and tm % chunk == 0
    r_shapes, r_specs = _router_out(n_tok, tm, n_exp)
    kern = functools.partial(_gmlp_kernel, alpha=alpha, chunk=chunk, groups=groups)
    n_steps = n_tok // tm
    c_in, c_out, c_shape, c_ops = _cast_specs(*expert_weights, n_steps, lambda i: jnp.minimum(i, n_steps - 1))
    return pl.pallas_call(
        kern,
        grid=(n_steps + 1,),
        in_specs=[pl.BlockSpec((tm, d), lambda i: (jnp.minimum(i, n_steps - 1), 0)),
                  _const_spec((d, 2 * width)), _const_spec((1, 2 * width)),
                  _const_spec((1, width)), _const_spec((1, width)),
                  _const_spec((groups, chunk, chunk)), _const_spec((chunk, groups)),
                  _const_spec((width, d)), _const_spec((1, d)), _const_spec((1, d)),
                  _const_spec((d, 256)), _const_spec((n_exp, 1))] + c_in,
        out_specs=[pl.BlockSpec((tm, d), lambda i: (_prev(i), 0)),
                   pl.BlockSpec((tm * RECORD_ROWS, 128), lambda i: (_prev(i), 0))] + r_specs + c_out,
        out_shape=[jax.ShapeDtypeStruct((n_tok, d), F32),
                   jax.ShapeDtypeStruct((n_tok * RECORD_ROWS, 128), U32)] + r_shapes + c_shape,
        scratch_shapes=[pltpu.VMEM((tm, width), BF16), pltpu.VMEM((tm, d), F32)],
        compiler_params=pltpu.CompilerParams(dimension_semantics=("arbitrary",),
                                             vmem_limit_bytes=V7X_VMEM_LIMIT_BYTES),
        name="gmlp_mixer",
    )(h, w_in.astype(BF16), b_in.reshape(1, -1), ln_g.reshape(1, -1), ln_b.reshape(1, -1),
      w_s, b_s.T, w_out.astype(BF16), l1g.reshape(1, -1), l1b.reshape(1, -1), rw2, rb, *c_ops)


SSM_LANE_GROUPS = 8
SSM_BATCH_PER_STEP = 4


def _proj_kernel(x_ref, w_ref, o_ref, res_scr, *, chunk):
    res = _dot(x_ref[...].astype(BF16), w_ref[...])
    n_slab = res_scr.shape[0]
    kb = o_ref.shape[0]
    for j in range(n_slab):
        res_scr[j] = res[:, j * 128:(j + 1) * 128]
    for t in range(chunk):
        for j in range(n_slab):
            o_ref[:, (t * n_slab + j) * 128:(t * n_slab + j + 1) * 128] = (
                res_scr[j, pl.ds(t, kb, stride=chunk), :].astype(o_ref.dtype))


def _ssm_in_proj(h, w, *, chunk):
    n_tok, d = h.shape
    width = w.shape[1]
    tm = min(TM_PROJ, n_tok)
    assert n_tok % tm == 0 and tm % chunk == 0 and width % 128 == 0
    kern = functools.partial(_proj_kernel, chunk=chunk)
    return pl.pallas_call(
        kern,
        grid=(n_tok // tm,),
        in_specs=[pl.BlockSpec((tm, d), lambda i: (i, 0)), _const_spec((d, width))],
        out_specs=pl.BlockSpec((tm // chunk, chunk * width), lambda i: (i, 0)),
        out_shape=jax.ShapeDtypeStruct((n_tok // chunk, chunk * width), BF16),
        scratch_shapes=[pltpu.VMEM((width // 128, tm, 128), F32)],
        compiler_params=pltpu.CompilerParams(dimension_semantics=("arbitrary",),
                                             vmem_limit_bytes=V7X_VMEM_LIMIT_BYTES),
        name="ssm_in_proj",
    )(h, w.astype(BF16))


def _ssm_kernel(*refs, chunk, n_batch, n_chunks):
    u_refs = refs[:chunk]
    d_ref, cin_ref, sout_ref, coef_ref, y_ref, t_ref, sin_ref, w_scr, x_scr, ytok_scr = refs[chunk:]
    bblk = pl.program_id(1)
    rows = n_batch * n_chunks
    n_x = x_scr.shape[0]

    @pl.when(bblk == 0)
    def _():
        for tp in range(chunk):
            for t in range(chunk):
                blk = d_ref[t - tp] if t >= tp else jnp.zeros((128, 128), BF16)
                t_ref[tp * 128:(tp + 1) * 128, t * 128:(t + 1) * 128] = blk
        ch_bits = int(math.log2(128 // SSM_LANE_GROUPS))
        row = lax.broadcasted_iota(I32, (chunk * 128, 128), 0)
        row_group = lax.shift_right_logical(row, ch_bits) & (SSM_LANE_GROUPS - 1)
        for part in range(2 * n_x // SSM_LANE_GROUPS):
            src = cin_ref[:, part * 128:(part + 1) * 128]
            for g in range(SSM_LANE_GROUPS):
                col = (part * SSM_LANE_GROUPS + g) * 128
                sin_ref[:, col:col + 128] = jnp.where(row_group == g, src, jnp.zeros_like(src))

    pitch = w_scr.shape[1] // n_batch

    u = jnp.concatenate([r[...] for r in u_refs], axis=1)
    for jb in range(n_x):
        w = _dot(u, sin_ref[:, jb * 256:(jb + 1) * 256])
        for b in range(n_batch):
            seq = slice(b * n_chunks, (b + 1) * n_chunks)
            w_scr[2 * jb, b * pitch:b * pitch + n_chunks, :] = w[seq, :128]
            w_scr[2 * jb + 1, b * pitch:b * pitch + n_chunks, :] = w[seq, 128:]

    ar = [jnp.broadcast_to(coef_ref[0:1, j * 128:(j + 1) * 128], (n_batch, 128)) for j in range(n_x)]
    ai_a = [jnp.broadcast_to(coef_ref[1:2, j * 128:(j + 1) * 128], (n_batch, 128)) for j in range(n_x)]
    ai_b = [jnp.broadcast_to(coef_ref[2:3, j * 128:(j + 1) * 128], (n_batch, 128)) for j in range(n_x)]

    def step(k, carry):
        x, xsw = carry
        rows_k = pl.ds(k, n_batch, stride=pitch)
        new_x, new_xsw = [], []
        for j in range(n_x):
            x_scr[j, rows_k, :] = x[j]
            new_x.append(ar[j] * x[j] + ai_a[j] * xsw[j] + w_scr[j, rows_k, :])
            new_xsw.append(ar[j] * xsw[j] + ai_b[j] * x[j] + w_scr[n_x + j, rows_k, :])
        return tuple(new_x), tuple(new_xsw)

    zero = tuple(jnp.zeros((n_batch, 128), F32) for _ in range(n_x))
    lax.fori_loop(0, n_chunks, step, (zero, zero))

    xs = jnp.concatenate(
        [jnp.concatenate([x_scr[j, b * pitch:b * pitch + n_chunks, :] for b in range(n_batch)], axis=0)
         for j in range(n_x)], axis=1).astype(BF16)
    for jb in range(chunk // 2):
        kk = (jb + 1) * 256
        cols = slice(jb * 256, (jb + 1) * 256)
        y = jax.nn.gelu(_dot(u[:, :kk], t_ref[:kk, cols]) + _dot(xs, sout_ref[:, cols]))
        for half in range(2):
            ytok_scr[pl.ds(2 * jb + half, rows, stride=chunk), :] = y[:, half * 128:(half + 1) * 128]
    y_ref[...] = ytok_scr[...].astype(y_ref.dtype)


def _ssm_operators(a_re, a_im, log_dt, b_re, b_im, c_re, c_im, d_skip):
    hp = lax.Precision.HIGHEST
    n_grp, n_state, n_ch = b_re.shape
    L = SSM_CHUNK
    nb = SSM_LANE_GROUPS
    assert n_ch * nb == 128 and 2 * n_state == 128 and n_grp % nb == 0
    n_gb = n_grp // nb
    a = lax.complex(a_re.astype(F32), a_im.astype(F32))
    lam = a * jnp.exp(log_dt.astype(F32))[:, None]
    a_bar = jnp.exp(lam)
    b_bar = ((a_bar - 1.0) / a)[..., None] * lax.complex(b_re.astype(F32), b_im.astype(F32))
    cc = lax.complex(c_re.astype(F32), c_im.astype(F32))
    steps = jnp.arange(L + 1, dtype=F32)
    apow = jnp.exp(steps[:, None, None] * lam[None])
    lane = jnp.arange(128)
    put = ((lane[None, None, :] // n_ch == jnp.arange(nb)[:, None, None])
           & (lane[None, None, :] % n_ch == jnp.arange(n_ch)[None, :, None])).astype(F32)

    kern = jnp.real(jnp.einsum('gcp,lgp,gpd->glcd', cc, apow[:L], b_bar, precision=hp))
    d_op = jnp.einsum('Bgtcd,gdr,gcl->Btrl', kern.reshape(n_gb, nb, L, n_ch, n_ch), put, put, precision=hp)
    skip = jnp.eye(128, dtype=F32)[None] * d_skip.astype(F32).reshape(n_gb, 1, 128)
    d_op = d_op.at[:, 0].add(skip).astype(BF16)

    s_in = apow[:L][::-1].transpose(1, 0, 2)[:, :, None, :] * b_bar.transpose(0, 2, 1)[:, None, :, :]
    s_in = jnp.concatenate([s_in.real, s_in.imag, s_in.imag, s_in.real], axis=-1).astype(BF16)
    c_in = s_in.reshape(n_gb, nb, L, n_ch, 4 * n_state).transpose(0, 2, 1, 3, 4).reshape(n_gb, L * 128, 4 * n_state)

    s_out = cc.transpose(0, 2, 1)[:, :, None, :] * apow[1:].transpose(1, 2, 0)[:, :, :, None]
    s_out = jnp.concatenate([s_out.real, -s_out.imag], axis=1)
    x_idx = jnp.arange(L * n_ch)
    l_idx = jnp.arange(L * 128)
    place = ((x_idx[:, None] // n_ch == l_idx[None, :] // 128) & (x_idx[:, None] % n_ch == l_idx[None, :] % n_ch))
    place = place[None] & ((l_idx[None, None, :] // n_ch) % nb == jnp.arange(nb)[:, None, None])
    s_out = jnp.einsum('Bgqx,gxl->Bgql', s_out.reshape(n_gb, nb, 2 * n_state, L * n_ch), place.astype(F32),
                       precision=hp).astype(BF16).reshape(n_gb, nb * 2 * n_state, L * 128)

    a_l = apow[L]
    coef = jnp.stack([jnp.concatenate([a_l.real, a_l.real], -1),
                      jnp.concatenate([-a_l.imag, a_l.imag], -1),
                      jnp.concatenate([a_l.imag, -a_l.imag], -1)], axis=1)
    coef = coef.reshape(n_gb, nb, 3, 2 * n_state).transpose(0, 2, 1, 3).reshape(n_gb, 3, nb * 2 * n_state)
    coef = jnp.pad(coef, ((0, 0), (0, 5), (0, 0)))
    return d_op, c_in, s_out, coef


def _ssm_scan(u, d_op, c_in, s_out, coef, *, n_batch, chunk):
    rows_all, lanes_all = u.shape
    n_blk, lc, _ = c_in.shape
    n_chunks = rows_all // n_batch
    nb = min(SSM_BATCH_PER_STEP, n_batch)
    assert n_batch % nb == 0 and lanes_all == chunk * n_blk * 128
    rows = nb * n_chunks
    pitch = n_chunks + 4 if n_chunks % 8 == 0 else n_chunks
    n_x = s_out.shape[1] // 128
    kern = functools.partial(_ssm_kernel, chunk=chunk, n_batch=nb, n_chunks=n_chunks)
    u_specs = [pl.BlockSpec((rows, 128), lambda g, b, t=t: (b, t * n_blk + g)) for t in range(chunk)]

    def op_spec(shape):
        return pl.BlockSpec((None,) + shape, lambda g, b: (g,) + (0,) * len(shape))

    return pl.pallas_call(
        kern,
        grid=(n_blk, n_batch // nb),
        in_specs=u_specs + [op_spec((chunk, 128, 128)), op_spec((lc, c_in.shape[2])),
                            op_spec((n_x * 128, lc)), op_spec((8, n_x * 128))],
        out_specs=pl.BlockSpec((rows * chunk, 128), lambda g, b: (b, g)),
        out_shape=jax.ShapeDtypeStruct((rows_all * chunk, n_blk * 128), BF16),
        scratch_shapes=[pltpu.VMEM((lc, lc), BF16), pltpu.VMEM((lc, 2 * n_x * 128), BF16),
                        pltpu.VMEM((2 * n_x, nb * pitch, 128), F32), pltpu.VMEM((n_x, nb * pitch, 128), F32),
                        pltpu.VMEM((rows * chunk, 128), F32)],
        compiler_params=pltpu.CompilerParams(dimension_semantics=("arbitrary", "arbitrary"),
                                             vmem_limit_bytes=V7X_VMEM_LIMIT_BYTES),
        name="ssm_scan",
    )(*([u] * chunk), d_op, c_in, s_out, coef)


def _ssm_out_kernel(y_ref, h_ref, wout_ref, l1g_ref, l1b_ref, rw_ref, rb_ref,
                    h1_ref, h1p_ref, idx_ref, gate_ref, rank_ref, cnt_ref, res_scr, *, alpha):
    @pl.when(pl.program_id(0) == 0)
    def _():
        cnt_ref[...] = jnp.zeros_like(cnt_ref)
        res_scr[...] = jnp.zeros_like(res_scr)

    h_split = _tail_norm(res_scr, l1g_ref, l1b_ref, h1_ref, h1p_ref)

    y = y_ref[...]
    d = h_ref.shape[1]
    val = _dot(y, wout_ref[:, :d])
    gate = _dot(y, wout_ref[:, d:])
    res_scr[...] = alpha * h_ref[...] + val * jax.nn.sigmoid(gate)
    _tail_route(h_split, rw_ref, rb_ref, idx_ref, gate_ref, rank_ref, cnt_ref)


def _ssm_layer(h, n_batch, w_in, a_re, a_im, log_dt, b_re, b_im, c_re, c_im, d_skip, w_out,
               l1g, l1b, rw2, rb, *, alpha):
    n_tok, d = h.shape
    n_grp, n_state, n_ch = b_re.shape
    width = n_grp * n_ch
    n_exp = rb.shape[0]
    L = SSM_CHUNK
    assert (n_tok // n_batch) % L == 0

    u = _ssm_in_proj(h, w_in, chunk=L)
    d_op, c_in, s_out, coef = _ssm_operators(a_re, a_im, log_dt, b_re, b_im, c_re, c_im, d_skip)
    y = _ssm_scan(u, d_op, c_in, s_out, coef, n_batch=n_batch, chunk=L)

    tm = min(TM_SSM_OUT, n_tok)
    assert n_tok % tm == 0
    r_shapes, r_specs = _router_out(n_tok, tm, n_exp)
    kern = functools.partial(_ssm_out_kernel, alpha=alpha)
    n_steps = n_tok // tm
    return pl.pallas_call(
        kern,
        grid=(n_steps + 1,),
        in_specs=[pl.BlockSpec((tm, width), lambda i: (jnp.minimum(i, n_steps - 1), 0)),
                  pl.BlockSpec((tm, d), lambda i: (jnp.minimum(i, n_steps - 1), 0)),
                  _const_spec((width, 2 * d)), _const_spec((1, d)), _const_spec((1, d)),
                  _const_spec((d, 256)), _const_spec((n_exp, 1))],
        out_specs=[pl.BlockSpec((tm, d), lambda i: (_prev(i), 0)),
                   pl.BlockSpec((tm * RECORD_ROWS, 128), lambda i: (_prev(i), 0))] + r_specs,
        out_shape=[jax.ShapeDtypeStruct((n_tok, d), F32),
                   jax.ShapeDtypeStruct((n_tok * RECORD_ROWS, 128), U32)] + r_shapes,
        scratch_shapes=[pltpu.VMEM((tm, d), F32)],
        compiler_params=pltpu.CompilerParams(dimension_semantics=("arbitrary",),
                                             vmem_limit_bytes=V7X_VMEM_LIMIT_BYTES),
        name="ssm_out_mixer",
    )(y, h, w_out.astype(BF16), l1g.reshape(1, -1), l1b.reshape(1, -1), rw2, rb)


def _moe_plan(idx, rank, cnt, tme, n_tiles):
    n_exp = cnt.shape[0]
    counts = cnt[:, 0].astype(I32)
    padded = ((counts + tme - 1) // tme) * tme
    ends = jnp.cumsum(padded)
    offs = ends - padded
    n_valid = ends[-1] // tme
    tile_start = jnp.arange(n_tiles, dtype=I32) * tme
    tile_expert = jnp.minimum(jnp.sum(tile_start[:, None] >= ends[None, :], axis=1), n_exp - 1).astype(I32)
    onehot = idx[:, :, None] == jnp.arange(n_exp, dtype=I32)[None, None, :]
    pos = jnp.sum(jnp.where(onehot, offs[None, None, :], 0), axis=-1) + rank
    meta = jnp.concatenate([counts, padded, offs, n_valid[None]]).astype(I32)
    return pos, tile_expert, meta


def _tile_pos(pos, tm):
    n_tok = pos.shape[1]
    return pos.reshape(2, n_tok // tm, tm).transpose(1, 0, 2).reshape(n_tok // tm, 1, 2 * tm)


def _record_slice(ref, row, n_rec=1):
    return ref.at[pl.ds(pl.multiple_of(row * RECORD_ROWS, RECORD_ROWS), n_rec * RECORD_ROWS), :]


def _dispatch_kernel(meta_ref, pos_ref, x_ref, xs_ref, zero_scr, sem, *, n_exp, tme, n_tiles):
    i = pl.program_id(0)
    tm = x_ref.shape[0] // RECORD_ROWS
    half = zero_scr.shape[0] // RECORD_ROWS

    def zero_fill(start, size):
        cp = pltpu.make_async_copy(_record_slice(zero_scr, 0, size), _record_slice(xs_ref, start, size), sem)
        cp.start()
        cp.wait()

    @pl.when(i == 0)
    def _():
        zero_scr[...] = jnp.zeros_like(zero_scr)
        for e in range(n_exp):
            pad = meta_ref[n_exp + e] - meta_ref[e]
            base = meta_ref[2 * n_exp + e] + meta_ref[e]
            for b in range(int(math.log2(tme))):
                size = 1 << b

                @pl.when(((pad >> b) & 1) == 1)
                def _():
                    zero_fill(base + (pad & (size - 1)), size)
        n_valid = meta_ref[3 * n_exp]
        for j in range(n_exp):
            @pl.when(n_valid + j < n_tiles)
            def _():
                for part in range(tme // half):
                    zero_fill((n_valid + j) * tme + part * half, half)

    def start(r, c):
        for slot in range(2):
            p = pos_ref[0, 0, slot * tm + r]
            pltpu.make_async_copy(_record_slice(x_ref, r), _record_slice(xs_ref, p), sem).start(priority=slot)
        return c

    lax.fori_loop(0, tm, start, 0, unroll=8)
    for slot in range(2):
        pltpu.make_async_copy(x_ref, _record_slice(xs_ref, 0, tm), sem).wait()


def _cast_specs(w_in, w_out, layer, n_steps, index):
    n_layers, n_exp, d, ff2 = w_in.shape
    ff = w_out.shape[2]
    assert (n_exp * d) % n_steps == 0 and (n_exp * ff) % n_steps == 0
    rin, rout = n_exp * d // n_steps, n_exp * ff // n_steps
    in_specs = [pl.BlockSpec((None, rin, ff2), lambda i, *_: (layer, index(i), 0)),
                pl.BlockSpec((None, rout, d), lambda i, *_: (layer, index(i), 0))]
    out_specs = [pl.BlockSpec((rin, ff2), lambda i, *_: (index(i), 0)),
                 pl.BlockSpec((rout, d), lambda i, *_: (index(i), 0))]
    out_shape = [jax.ShapeDtypeStruct((n_exp * d, ff2), BF16), jax.ShapeDtypeStruct((n_exp * ff, d), BF16)]
    operands = (w_in.reshape(n_layers, n_exp * d, ff2), w_out.reshape(n_layers, n_exp * ff, d))
    return in_specs, out_specs, out_shape, operands


def _moe_dispatch(h1p, pos, meta, *, n_exp, tme, n_tiles):
    n_tok = h1p.shape[0] // RECORD_ROWS
    tm = min(TM_DISPATCH, n_tok)
    assert n_tok % tm == 0
    half = max(tme // 2, 1)
    kern = functools.partial(_dispatch_kernel, n_exp=n_exp, tme=tme, n_tiles=n_tiles)
    return pl.pallas_call(
        kern,
        grid_spec=pltpu.PrefetchScalarGridSpec(
            num_scalar_prefetch=1,
            grid=(n_tok // tm,),
            in_specs=[pl.BlockSpec((1, 1, 2 * tm), lambda i, m: (i, 0, 0), memory_space=pltpu.SMEM),
                      pl.BlockSpec((tm * RECORD_ROWS, 128), lambda i, m: (i, 0))],
            out_specs=pl.BlockSpec(memory_space=pl.ANY),
            scratch_shapes=[pltpu.VMEM((half * RECORD_ROWS, 128), U32), pltpu.SemaphoreType.DMA]),
        out_shape=jax.ShapeDtypeStruct((n_tiles * tme * RECORD_ROWS, 128), U32),
        compiler_params=pltpu.CompilerParams(dimension_semantics=("arbitrary",),
                                             vmem_limit_bytes=V7X_VMEM_LIMIT_BYTES),
        name="moe_dispatch",
    )(meta, _tile_pos(pos, tm), h1p)


def _expert_kernel(te_ref, meta_ref, xs_ref, win_ref, wout_ref, *refs, n_exp, cast):
    if cast:
        nwin_ref, nwout_ref, ys_ref, nwin_b_ref, nwout_b_ref = refs
        nwin_b_ref[...] = nwin_ref[...].astype(BF16)
        nwout_b_ref[...] = nwout_ref[...].astype(BF16)
    else:
        (ys_ref,) = refs
    i = pl.program_id(0)
    n_valid = meta_ref[3 * n_exp]
    tme = xs_ref.shape[0] // RECORD_ROWS
    half = tme // 2
    expert = te_ref[i]
    rows_used = meta_ref[expert] - (i * tme - meta_ref[2 * n_exp + expert])

    def mlp(rows):
        hi, lo = _load_records(xs_ref, rows)
        dh = hi.shape[1]
        ff = wout_ref.shape[0]
        h = _dot(hi.astype(BF16), win_ref[:dh, :]) + _dot(lo.astype(BF16), win_ref[dh:, :])
        act = (jax.nn.silu(h[:, :ff]) * h[:, ff:]).astype(BF16)
        _store_records(ys_ref, _dot(act, wout_ref[...]))

    @pl.when((i < n_valid) & (rows_used > half))
    def _():
        mlp(tme)

    @pl.when((i < n_valid) & (rows_used <= half))
    def _():
        mlp(half)
        ys_ref[pl.ds(half * RECORD_ROWS, (tme - half) * RECORD_ROWS), :] = jnp.zeros(
            ((tme - half) * RECORD_ROWS, 128), ys_ref.dtype)

    @pl.when(i >= n_valid)
    def _():
        ys_ref[...] = jnp.zeros_like(ys_ref)


def _moe_experts(xs, tile_expert, meta, w_in, w_out, cast_weights, *, tme):
    n_rows = xs.shape[0] // RECORD_ROWS
    n_exp, d, ff2 = w_in.shape
    ff = w_out.shape[1]
    n_tiles = n_rows // tme

    def x_map(i, te, m):
        return (jnp.minimum(i, m[3 * n_exp] - 1), 0)

    cast = cast_weights is not None
    c_in, c_out, c_shape, c_ops = ([], [], [], ())
    if cast:
        n_cast = 1 << (n_tiles.bit_length() - 1)
        c_in, c_out, c_shape, c_ops = _cast_specs(*cast_weights, n_cast, lambda i: jnp.minimum(i, n_cast - 1))
    kern = functools.partial(_expert_kernel, n_exp=n_exp, cast=cast)
    return pl.pallas_call(
        kern,
        grid_spec=pltpu.PrefetchScalarGridSpec(
            num_scalar_prefetch=2,
            grid=(n_tiles,),
            in_specs=[pl.BlockSpec((tme * RECORD_ROWS, 128), x_map),
                      pl.BlockSpec((None, d, ff2), lambda i, te, m: (te[i], 0, 0)),
                      pl.BlockSpec((None, ff, d), lambda i, te, m: (te[i], 0, 0))] + c_in,
            out_specs=[pl.BlockSpec((tme * RECORD_ROWS, 128), lambda i, te, m: (i, 0))] + c_out),
        out_shape=[jax.ShapeDtypeStruct((n_rows * RECORD_ROWS, 128), U32)] + c_shape,
        compiler_params=pltpu.CompilerParams(dimension_semantics=("arbitrary",),
                                             vmem_limit_bytes=V7X_VMEM_LIMIT_BYTES),
        name="moe_experts",
    )(tile_expert, meta, xs, w_in, w_out, *c_ops)


def _combine_kernel(pos_ref, pos_next_ref, h1_ref, gate_ref, ys_ref, g_ref, b_ref, out_ref, buf, sems, *, alpha):
    i = pl.program_id(0)
    n_steps = pl.num_programs(0)
    tm = h1_ref.shape[0]

    def gather(p_ref, half):
        def start(r, c):
            for slot in range(2):
                p = p_ref[0, 0, slot * tm + r]
                pltpu.make_async_copy(_record_slice(ys_ref, p), _record_slice(buf.at[half, slot], r),
                                      sems.at[half]).start(priority=slot)
            return c

        lax.fori_loop(0, tm, start, 0, unroll=8)

    @pl.when(i == 0)
    def _():
        gather(pos_ref, 0)

    @pl.when(i + 1 < n_steps)
    def _():
        gather(pos_next_ref, (i + 1) % 2)

    cur = i % 2
    moe = None
    for slot in range(2):
        pltpu.make_async_copy(_record_slice(ys_ref, 0, tm), buf.at[cur, slot], sems.at[cur]).wait()
    for slot in range(2):
        hi, lo = _load_records(buf.at[cur, slot], tm)
        y = jnp.concatenate([hi, lo], axis=1) * gate_ref[:, slot:slot + 1]
        moe = y if moe is None else moe + y
    out_ref[...] = _layer_norm(alpha * h1_ref[...] + moe, g_ref[...], b_ref[...])


def _moe_combine(h1, ys, pos, gates, l2g, l2b, *, alpha):
    n_tok, d = h1.shape
    tm = min(TM_COMBINE, n_tok)
    assert n_tok % tm == 0
    n_steps = n_tok // tm
    kern = functools.partial(_combine_kernel, alpha=alpha)
    pos_tiles = _tile_pos(pos, tm)
    return pl.pallas_call(
        kern,
        grid=(n_steps,),
        in_specs=[pl.BlockSpec((1, 1, 2 * tm), lambda i: (i, 0, 0), memory_space=pltpu.SMEM),
                  pl.BlockSpec((1, 1, 2 * tm), lambda i: (jnp.minimum(i + 1, n_steps - 1), 0, 0),
                               memory_space=pltpu.SMEM),
                  pl.BlockSpec((tm, d), lambda i: (i, 0)),
                  pl.BlockSpec((tm, 2), lambda i: (i, 0)),
                  pl.BlockSpec(memory_space=pl.ANY),
                  _const_spec((1, d)), _const_spec((1, d))],
        out_specs=pl.BlockSpec((tm, d), lambda i: (i, 0)),
        out_shape=jax.ShapeDtypeStruct((n_tok, d), F32),
        scratch_shapes=[pltpu.VMEM((2, 2, tm * RECORD_ROWS, 128), U32), pltpu.SemaphoreType.DMA((2,))],
        compiler_params=pltpu.CompilerParams(dimension_semantics=("arbitrary",),
                                             vmem_limit_bytes=V7X_VMEM_LIMIT_BYTES),
        name="moe_combine",
    )(pos_tiles, pos_tiles, h1, gates.T, ys, l2g.reshape(1, -1), l2b.reshape(1, -1))


def _moe_layer(h1, h1p, idx, gates, rank, cnt, w_in_b, w_out_b, cast_next, l2g, l2b, *, alpha):
    n_tok, d = h1.shape
    n_exp = cnt.shape[0]
    ff = w_out_b.shape[0] // n_exp
    tme = min(TM_EXPERT, n_tok)
    n_tiles = (2 * n_tok) // tme + n_exp
    pos, tile_expert, meta = _moe_plan(idx, rank, cnt, tme, n_tiles)
    xs = _moe_dispatch(h1p, pos, meta, n_exp=n_exp, tme=tme, n_tiles=n_tiles)
    ys, *cast = _moe_experts(xs, tile_expert, meta, w_in_b.reshape(n_exp, d, 2 * ff), w_out_b.reshape(n_exp, ff, d),
                             cast_next, tme=tme)
    return _moe_combine(h1, ys, pos, gates, l2g, l2b, alpha=alpha), cast


def kernel(x, a_w_in, a_b_in, a_ln_g, a_ln_b, a_w_s, a_b_s, a_w_out, b_w_in, b_a_re, b_a_im, b_log_dt, b_b_re, b_b_im, b_c_re, b_c_im, b_d, b_w_out, ln1_g, ln1_b, router_w, router_b, e_w_in, e_w_out, ln2_g, ln2_b):
    n_batch, seq, d = x.shape
    depth = ln1_g.shape[0]
    alpha = float((2 * depth) ** 0.25)
    n_exp = router_w.shape[1]
    rw = jnp.pad(router_w.astype(F32), ((0, 0), (0, 128 - n_exp)))
    rw_hi = lax.reduce_precision(rw, exponent_bits=8, mantissa_bits=7)
    rw2 = jnp.concatenate([rw_hi, rw - rw_hi], axis=1).astype(BF16)
    rb = router_b.astype(F32).reshape(-1, 1)
    h = x.reshape(n_batch * seq, d)
    e_bf16 = None
    for i in range(depth):
        j = i // 2
        if i % 2 == 0:
            *mixed, e_in_b, e_out_b = _gmlp_layer(h, a_w_in[j], a_b_in[j], a_ln_g[j], a_ln_b[j], a_w_s[j], a_b_s[j],
                                                  a_w_out[j], ln1_g[i], ln1_b[i], rw2, rb, (e_w_in, e_w_out, i),
                                                  alpha=alpha)
            e_bf16 = [e_in_b, e_out_b]
        else:
            mixed = _ssm_layer(h, n_batch, b_w_in[j], b_a_re[j], b_a_im[j], b_log_dt[j], b_b_re[j],
                               b_b_im[j], b_c_re[j], b_c_im[j], b_d[j], b_w_out[j],
                               ln1_g[i], ln1_b[i], rw2, rb, alpha=alpha)
        cast_next = (e_w_in, e_w_out, i + 1) if (i % 2 == 0 and i + 1 < depth) else None
        h, e_bf16 = _moe_layer(*mixed, *e_bf16, cast_next, ln2_g[i], ln2_b[i], alpha=alpha)
    return h.reshape(n_batch, seq, d)
```

```python
import functools
import math

import jax
import jax.numpy as jnp
from jax import lax
from jax.experimental import pallas as pl
from jax.experimental.pallas import tpu as pltpu

F32 = jnp.float32
BF16 = jnp.bfloat16
I32 = jnp.int32
U32 = jnp.uint32

LN_EPS = 1e-5
N_EXPERT_GROUPS = 4
SSM_CHUNK = 16

TM_MIX = 256
TM_SSM_OUT = 512
TM_PROJ = 1024
TM_DISPATCH = 512
TM_EXPERT = 512
TM_COMBINE = 256

V7X_VMEM_LIMIT_BYTES = 56 * 1024 * 1024


def _dot(a, b):
    return jnp.dot(a, b, preferred_element_type=F32)


def _layer_norm(x, g, b):
    mu = jnp.mean(x, axis=-1, keepdims=True)
    xc = x - mu
    var = jnp.sum(xc * xc, axis=-1, keepdims=True) * (1.0 / x.shape[-1])
    return xc * lax.rsqrt(var + LN_EPS) * g + b


def _const_spec(shape):
    return pl.BlockSpec(shape, lambda *_: (0,) * len(shape), pipeline_mode=pl.Buffered(1))


def _split_bf16(h1):
    h_top = lax.bitcast_convert_type(lax.bitcast_convert_type(h1, U32) & jnp.uint32(0xFFFF0000), F32)
    return h_top.astype(BF16), (h1 - h_top).astype(BF16)


def _route(h_hi, h_mid, rw_ref, rb_ref, idx_ref, gate_ref, rank_ref, cnt_ref, live):
    n_exp = rb_ref.shape[0]
    per_group = n_exp // N_EXPERT_GROUPS
    tm = h_hi.shape[0]
    first = _dot(h_hi, rw_ref[...])
    logits_t = first[:, :128] + first[:, 128:] + _dot(h_mid, rw_ref[:, :128])
    logits = logits_t.T[:n_exp, :] + rb_ref[...]
    m = jnp.max(logits, axis=0, keepdims=True)
    ex = jnp.exp(logits - m)
    probs = ex / jnp.sum(ex, axis=0, keepdims=True)

    best_score = None
    best_group = None
    for g in range(N_EXPERT_GROUPS):
        t1 = t2 = None
        for e in range(g * per_group, (g + 1) * per_group):
            v = probs[e:e + 1, :]
            if t1 is None:
                t1 = v
            elif t2 is None:
                t1, t2 = jnp.maximum(t1, v), jnp.minimum(t1, v)
            else:
                t2 = jnp.maximum(t2, jnp.minimum(t1, v))
                t1 = jnp.maximum(t1, v)
        score = t1 + t2
        if best_score is None:
            best_score, best_group = score, jnp.zeros((1, tm), I32)
        else:
            better = score > best_score
            best_group = jnp.where(better, g, best_group)
            best_score = jnp.where(better, score, best_score)

    eidx = lax.broadcasted_iota(I32, (n_exp, tm), 0)
    egrp = jnp.zeros((n_exp, tm), I32)
    for g in range(1, N_EXPERT_GROUPS):
        egrp = egrp + jnp.where(eidx >= g * per_group, 1, 0)
    in_group = egrp == best_group
    masked = jnp.where(in_group, probs, -1.0)
    m1 = jnp.max(masked, axis=0, keepdims=True)
    i1 = jnp.min(jnp.where(masked == m1, eidx, n_exp), axis=0, keepdims=True)
    sel1 = eidx == i1
    masked2 = jnp.where(sel1, -2.0, masked)
    m2 = jnp.max(masked2, axis=0, keepdims=True)
    i2 = jnp.min(jnp.where(masked2 == m2, eidx, n_exp), axis=0, keepdims=True)
    sel2 = eidx == i2
    den = m1 + m2

    cnt = jnp.where(sel1 | sel2, live, 0.0)
    r_i = lax.broadcasted_iota(I32, (tm, tm), 0)
    c_i = lax.broadcasted_iota(I32, (tm, tm), 1)
    strict_upper = jnp.where(r_i < c_i, 1.0, 0.0).astype(BF16)
    prefix = _dot(cnt.astype(BF16), strict_upper) + cnt_ref[:, 0:1]
    r1 = jnp.sum(jnp.where(sel1, prefix, 0.0), axis=0, keepdims=True)
    r2 = jnp.sum(jnp.where(sel2, prefix, 0.0), axis=0, keepdims=True)
    cnt_ref[...] += jnp.sum(cnt, axis=1, keepdims=True)

    idx_ref[...] = jnp.concatenate([i1, i2], axis=0)
    gate_ref[...] = jnp.concatenate([m1 / den, m2 / den], axis=0)
    rank_ref[...] = jnp.concatenate([r1, r2], axis=0).astype(I32)


RECORD_ROWS = 8


def _store_records(ref, h):
    tm, d = h.shape
    half = d // 2
    assert half == RECORD_ROWS * 128
    bits = lax.bitcast_convert_type(h.astype(BF16).astype(F32), U32)
    packed = (bits[:, :half] & jnp.uint32(0xFFFF0000)) | (bits[:, half:] >> 16)
    for s in range(RECORD_ROWS):
        ref[pl.ds(s, tm, stride=RECORD_ROWS), :] = packed[:, s * 128:(s + 1) * 128]


def _load_records(ref, tm):
    w = jnp.concatenate([ref[pl.ds(s, tm, stride=RECORD_ROWS), :] for s in range(RECORD_ROWS)], axis=1)
    hi = lax.bitcast_convert_type(w & jnp.uint32(0xFFFF0000), F32)
    lo = lax.bitcast_convert_type(w << 16, F32)
    return hi, lo


def _router_out(n_tok, tm, n_exp):
    shapes = [jax.ShapeDtypeStruct((2, n_tok), I32),
              jax.ShapeDtypeStruct((2, n_tok), F32),
              jax.ShapeDtypeStruct((2, n_tok), I32),
              jax.ShapeDtypeStruct((n_exp, 128), F32)]
    specs = [pl.BlockSpec((2, tm), lambda i: (0, _prev(i))),
             pl.BlockSpec((2, tm), lambda i: (0, _prev(i))),
             pl.BlockSpec((2, tm), lambda i: (0, _prev(i))),
             pl.BlockSpec((n_exp, 128), lambda i: (0, 0))]
    return shapes, specs


def _prev(i):
    return jnp.maximum(i - 1, 0)


def _tail_norm(res_scr, l1g_ref, l1b_ref, h1_ref, h1p_ref):
    h1 = _layer_norm(res_scr[...], l1g_ref[...], l1b_ref[...])
    h1_ref[...] = h1
    _store_records(h1p_ref, h1)
    return _split_bf16(h1)


def _tail_route(h_split, rw_ref, rb_ref, idx_ref, gate_ref, rank_ref, cnt_ref):
    live = jnp.where(pl.program_id(0) > 0, 1.0, 0.0)
    _route(*h_split, rw_ref, rb_ref, idx_ref, gate_ref, rank_ref, cnt_ref, live)


def _gmlp_kernel(x_ref, win_ref, bin_ref, lng_ref, lnb_ref, ws_ref, bst_ref, wout_ref,
                 l1g_ref, l1b_ref, rw_ref, rb_ref, ew_in_ref, ew_out_ref,
                 h1_ref, h1p_ref, idx_ref, gate_ref, rank_ref, cnt_ref, ew_in_b_ref, ew_out_b_ref,
                 p_scr, res_scr, *, alpha, chunk, groups):
    @pl.when(pl.program_id(0) == 0)
    def _():
        cnt_ref[...] = jnp.zeros_like(cnt_ref)
        res_scr[...] = jnp.zeros_like(res_scr)

    ew_in_b_ref[...] = ew_in_ref[...].astype(BF16)
    ew_out_b_ref[...] = ew_out_ref[...].astype(BF16)

    _tail_route(_tail_norm(res_scr, l1g_ref, l1b_ref, h1_ref, h1p_ref), rw_ref, rb_ref, idx_ref, gate_ref, rank_ref,
                cnt_ref)

    x = x_ref[...]
    xb = x.astype(BF16)
    width = win_ref.shape[1] // 2
    gdim = width // groups
    tm = x.shape[0]

    zv = jax.nn.gelu(_dot(xb, win_ref[:, width:]) + bin_ref[:, width:])
    vn = _layer_norm(zv, lng_ref[...], lnb_ref[...]).astype(BF16)
    zu = jax.nn.gelu(_dot(xb, win_ref[:, :width]) + bin_ref[:, :width])

    r_i = lax.broadcasted_iota(I32, (chunk, chunk), 0)
    c_i = lax.broadcasted_iota(I32, (chunk, chunk), 1)
    causal = r_i >= c_i
    for g in range(groups):
        w = jnp.where(causal, ws_ref[g], 0.0).astype(BF16)
        bias = bst_ref[:, g:g + 1]
        for c in range(tm // chunk):
            rows = slice(c * chunk, (c + 1) * chunk)
            cols = slice(g * gdim, (g + 1) * gdim)
            s = _dot(w, vn[rows, cols]) + bias
            p_scr[rows, cols] = (zu[rows, cols] * s).astype(BF16)

    res_scr[...] = alpha * x + _dot(p_scr[...], wout_ref[...])


def _gmlp_layer(h, w_in, b_in, ln_g, ln_b, w_s, b_s, w_out, l1g, l1b, rw2, rb, expert_weights, *, alpha):
    n_tok, d = h.shape
    groups, chunk, _ = w_s.shape
    width = w_in.shape[1] // 2
    n_exp = rb.shape[0]
    tm = min(TM_MIX, n_tok)
    assert n_tok % tm == 0 and tm % chunk == 0
    r_shapes, r_specs = _router_out(n_tok, tm, n_exp)
    kern = functools.partial(_gmlp_kernel, alpha=alpha, chunk=chunk, groups=groups)
    n_steps = n_tok // tm
    c_in, c_out, c_shape, c_ops = _cast_specs(*expert_weights, n_steps, lambda i: jnp.minimum(i, n_steps - 1))
    return pl.pallas_call(
        kern,
        grid=(n_steps + 1,),
        in_specs=[pl.BlockSpec((tm, d), lambda i: (jnp.minimum(i, n_steps - 1), 0)),
                  _const_spec((d, 2 * width)), _const_spec((1, 2 * width)),
                  _const_spec((1, width)), _const_spec((1, width)),
                  _const_spec((groups, chunk, chunk)), _const_spec((chunk, groups)),
                  _const_spec((width, d)), _const_spec((1, d)), _const_spec((1, d)),
                  _const_spec((d, 256)), _const_spec((n_exp, 1))] + c_in,
        out_specs=[pl.BlockSpec((tm, d), lambda i: (_prev(i), 0)),
                   pl.BlockSpec((tm * RECORD_ROWS, 128), lambda i: (_prev(i), 0))] + r_specs + c_out,
        out_shape=[jax.ShapeDtypeStruct((n_tok, d), F32),
                   jax.ShapeDtypeStruct((n_tok * RECORD_ROWS, 128), U32)] + r_shapes + c_shape,
        scratch_shapes=[pltpu.VMEM((tm, width), BF16), pltpu.VMEM((tm, d), F32)],
        compiler_params=pltpu.CompilerParams(dimension_semantics=("arbitrary",),
                                             vmem_limit_bytes=V7X_VMEM_LIMIT_BYTES),
        name="gmlp_mixer",
    )(h, w_in.astype(BF16), b_in.reshape(1, -1), ln_g.reshape(1, -1), ln_b.reshape(1, -1),
      w_s, b_s.T, w_out.astype(BF16), l1g.reshape(1, -1), l1b.reshape(1, -1), rw2, rb, *c_ops)


SSM_LANE_GROUPS = 8
SSM_BATCH_PER_STEP = 4


def _proj_kernel(x_ref, w_ref, o_ref, res_scr, *, chunk):
    res = _dot(x_ref[...].astype(BF16), w_ref[...])
    n_slab = res_scr.shape[0]
    kb = o_ref.shape[0]
    for j in range(n_slab):
        res_scr[j] = res[:, j * 128:(j + 1) * 128]
    for t in range(chunk):
        for j in range(n_slab):
            o_ref[:, (t * n_slab + j) * 128:(t * n_slab + j + 1) * 128] = (
                res_scr[j, pl.ds(t, kb, stride=chunk), :].astype(o_ref.dtype))


def _ssm_in_proj(h, w, *, chunk):
    n_tok, d = h.shape
    width = w.shape[1]
    tm = min(TM_PROJ, n_tok)
    assert n_tok % tm == 0 and tm % chunk == 0 and width % 128 == 0
    kern = functools.partial(_proj_kernel, chunk=chunk)
    return pl.pallas_call(
        kern,
        grid=(n_tok // tm,),
        in_specs=[pl.BlockSpec((tm, d), lambda i: (i, 0)), _const_spec((d, width))],
        out_specs=pl.BlockSpec((tm // chunk, chunk * width), lambda i: (i, 0)),
        out_shape=jax.ShapeDtypeStruct((n_tok // chunk, chunk * width), BF16),
        scratch_shapes=[pltpu.VMEM((width // 128, tm, 128), F32)],
        compiler_params=pltpu.CompilerParams(dimension_semantics=("arbitrary",),
                                             vmem_limit_bytes=V7X_VMEM_LIMIT_BYTES),
        name="ssm_in_proj",
    )(h, w.astype(BF16))


def _ssm_kernel(*refs, chunk, n_batch, n_chunks):
    u_refs = refs[:chunk]
    d_ref, cin_ref, sout_ref, coef_ref, y_ref, t_ref, sin_ref, w_scr, x_scr, ytok_scr = refs[chunk:]
    bblk = pl.program_id(1)
    rows = n_batch * n_chunks
    n_x = x_scr.shape[0]

    @pl.when(bblk == 0)
    def _():
        for tp in range(chunk):
            for t in range(chunk):
                blk = d_ref[t - tp] if t >= tp else jnp.zeros((128, 128), BF16)
                t_ref[tp * 128:(tp + 1) * 128, t * 128:(t + 1) * 128] = blk
        ch_bits = int(math.log2(128 // SSM_LANE_GROUPS))
        row = lax.broadcasted_iota(I32, (chunk * 128, 128), 0)
        row_group = lax.shift_right_logical(row, ch_bits) & (SSM_LANE_GROUPS - 1)
        for part in range(2 * n_x // SSM_LANE_GROUPS):
            src = cin_ref[:, part * 128:(part + 1) * 128]
            for g in range(SSM_LANE_GROUPS):
                col = (part * SSM_LANE_GROUPS + g) * 128
                sin_ref[:, col:col + 128] = jnp.where(row_group == g, src, jnp.zeros_like(src))

    pitch = w_scr.shape[1] // n_batch

    u = jnp.concatenate([r[...] for r in u_refs], axis=1)
    for jb in range(n_x):
        w = _dot(u, sin_ref[:, jb * 256:(jb + 1) * 256])
        for b in range(n_batch):
            seq = slice(b * n_chunks, (b + 1) * n_chunks)
            w_scr[2 * jb, b * pitch:b * pitch + n_chunks, :] = w[seq, :128]
            w_scr[2 * jb + 1, b * pitch:b * pitch + n_chunks, :] = w[seq, 128:]

    ar = [jnp.broadcast_to(coef_ref[0:1, j * 128:(j + 1) * 128], (n_batch, 128)) for j in range(n_x)]
    ai_a = [jnp.broadcast_to(coef_ref[1:2, j * 128:(j + 1) * 128], (n_batch, 128)) for j in range(n_x)]
    ai_b = [jnp.broadcast_to(coef_ref[2:3, j * 128:(j + 1) * 128], (n_batch, 128)) for j in range(n_x)]

    def step(k, carry):
        x, xsw = carry
        rows_k = pl.ds(k, n_batch, stride=pitch)
        new_x, new_xsw = [], []
        for j in range(n_x):
            x_scr[j, rows_k, :] = x[j]
            new_x.append(ar[j] * x[j] + ai_a[j] * xsw[j] + w_scr[j, rows_k, :])
            new_xsw.append(ar[j] * xsw[j] + ai_b[j] * x[j] + w_scr[n_x + j, rows_k, :])
        return tuple(new_x), tuple(new_xsw)

    zero = tuple(jnp.zeros((n_batch, 128), F32) for _ in range(n_x))
    lax.fori_loop(0, n_chunks, step, (zero, zero))

    xs = jnp.concatenate(
        [jnp.concatenate([x_scr[j, b * pitch:b * pitch + n_chunks, :] for b in range(n_batch)], axis=0)
         for j in range(n_x)], axis=1).astype(BF16)
    for jb in range(chunk // 2):
        kk = (jb + 1) * 256
        cols = slice(jb * 256, (jb + 1) * 256)
        y = jax.nn.gelu(_dot(u[:, :kk], t_ref[:kk, cols]) + _dot(xs, sout_ref[:, cols]))
        for half in range(2):
            ytok_scr[pl.ds(2 * jb + half, rows, stride=chunk), :] = y[:, half * 128:(half + 1) * 128]
    y_ref[...] = ytok_scr[...].astype(y_ref.dtype)


def _ssm_operators(a_re, a_im, log_dt, b_re, b_im, c_re, c_im, d_skip):
    hp = lax.Precision.HIGHEST
    n_grp, n_state, n_ch = b_re.shape
    L = SSM_CHUNK
    nb = SSM_LANE_GROUPS
    assert n_ch * nb == 128 and 2 * n_state == 128 and n_grp % nb == 0
    n_gb = n_grp // nb
    a = lax.complex(a_re.astype(F32), a_im.astype(F32))
    lam = a * jnp.exp(log_dt.astype(F32))[:, None]
    a_bar = jnp.exp(lam)
    b_bar = ((a_bar - 1.0) / a)[..., None] * lax.complex(b_re.astype(F32), b_im.astype(F32))
    cc = lax.complex(c_re.astype(F32), c_im.astype(F32))
    steps = jnp.arange(L + 1, dtype=F32)
    apow = jnp.exp(steps[:, None, None] * lam[None])
    lane = jnp.arange(128)
    put = ((lane[None, None, :] // n_ch == jnp.arange(nb)[:, None, None])
           & (lane[None, None, :] % n_ch == jnp.arange(n_ch)[None, :, None])).astype(F32)

    kern = jnp.real(jnp.einsum('gcp,lgp,gpd->glcd', cc, apow[:L], b_bar, precision=hp))
    d_op = jnp.einsum('Bgtcd,gdr,gcl->Btrl', kern.reshape(n_gb, nb, L, n_ch, n_ch), put, put, precision=hp)
    skip = jnp.eye(128, dtype=F32)[None] * d_skip.astype(F32).reshape(n_gb, 1, 128)
    d_op = d_op.at[:, 0].add(skip).astype(BF16)

    s_in = apow[:L][::-1].transpose(1, 0, 2)[:, :, None, :] * b_bar.transpose(0, 2, 1)[:, None, :, :]
    s_in = jnp.concatenate([s_in.real, s_in.imag, s_in.imag, s_in.real], axis=-1).astype(BF16)
    c_in = s_in.reshape(n_gb, nb, L, n_ch, 4 * n_state).transpose(0, 2, 1, 3, 4).reshape(n_gb, L * 128, 4 * n_state)

    s_out = cc.transpose(0, 2, 1)[:, :, None, :] * apow[1:].transpose(1, 2, 0)[:, :, :, None]
    s_out = jnp.concatenate([s_out.real, -s_out.imag], axis=1)
    x_idx = jnp.arange(L * n_ch)
    l_idx = jnp.arange(L * 128)
    place = ((x_idx[:, None] // n_ch == l_idx[None, :] // 128) & (x_idx[:, None] % n_ch == l_idx[None, :] % n_ch))
    place = place[None] & ((l_idx[None, None, :] // n_ch) % nb == jnp.arange(nb)[:, None, None])
    s_out = jnp.einsum('Bgqx,gxl->Bgql', s_out.reshape(n_gb, nb, 2 * n_state, L * n_ch), place.astype(F32),
                       precision=hp).astype(BF16).reshape(n_gb, nb * 2 * n_state, L * 128)

    a_l = apow[L]
    coef = jnp.stack([jnp.concatenate([a_l.real, a_l.real], -1),
                      jnp.concatenate([-a_l.imag, a_l.imag], -1),
                      jnp.concatenate([a_l.imag, -a_l.imag], -1)], axis=1)
    coef = coef.reshape(n_gb, nb, 3, 2 * n_state).transpose(0, 2, 1, 3).reshape(n_gb, 3, nb * 2 * n_state)
    coef = jnp.pad(coef, ((0, 0), (0, 5), (0, 0)))
    return d_op, c_in, s_out, coef


def _ssm_scan(u, d_op, c_in, s_out, coef, *, n_batch, chunk):
    rows_all, lanes_all = u.shape
    n_blk, lc, _ = c_in.shape
    n_chunks = rows_all // n_batch
    nb = min(SSM_BATCH_PER_STEP, n_batch)
    assert n_batch % nb == 0 and lanes_all == chunk * n_blk * 128
    rows = nb * n_chunks
    pitch = n_chunks + 4 if n_chunks % 8 == 0 else n_chunks
    n_x = s_out.shape[1] // 128
    kern = functools.partial(_ssm_kernel, chunk=chunk, n_batch=nb, n_chunks=n_chunks)
    u_specs = [pl.BlockSpec((rows, 128), lambda g, b, t=t: (b, t * n_blk + g)) for t in range(chunk)]

    def op_spec(shape):
        return pl.BlockSpec((None,) + shape, lambda g, b: (g,) + (0,) * len(shape))

    return pl.pallas_call(
        kern,
        grid=(n_blk, n_batch // nb),
        in_specs=u_specs + [op_spec((chunk, 128, 128)), op_spec((lc, c_in.shape[2])),
                            op_spec((n_x * 128, lc)), op_spec((8, n_x * 128))],
        out_specs=pl.BlockSpec((rows * chunk, 128), lambda g, b: (b, g)),
        out_shape=jax.ShapeDtypeStruct((rows_all * chunk, n_blk * 128), BF16),
        scratch_shapes=[pltpu.VMEM((lc, lc), BF16), pltpu.VMEM((lc, 2 * n_x * 128), BF16),
                        pltpu.VMEM((2 * n_x, nb * pitch, 128), F32), pltpu.VMEM((n_x, nb * pitch, 128), F32),
                        pltpu.VMEM((rows * chunk, 128), F32)],
        compiler_params=pltpu.CompilerParams(dimension_semantics=("arbitrary", "arbitrary"),
                                             vmem_limit_bytes=V7X_VMEM_LIMIT_BYTES),
        name="ssm_scan",
    )(*([u] * chunk), d_op, c_in, s_out, coef)


def _ssm_out_kernel(y_ref, h_ref, wout_ref, l1g_ref, l1b_ref, rw_ref, rb_ref,
                    h1_ref, h1p_ref, idx_ref, gate_ref, rank_ref, cnt_ref, res_scr, *, alpha):
    @pl.when(pl.program_id(0) == 0)
    def _():
        cnt_ref[...] = jnp.zeros_like(cnt_ref)
        res_scr[...] = jnp.zeros_like(res_scr)

    h_split = _tail_norm(res_scr, l1g_ref, l1b_ref, h1_ref, h1p_ref)

    y = y_ref[...]
    d = h_ref.shape[1]
    val = _dot(y, wout_ref[:, :d])
    gate = _dot(y, wout_ref[:, d:])
    res_scr[...] = alpha * h_ref[...] + val * jax.nn.sigmoid(gate)
    _tail_route(h_split, rw_ref, rb_ref, idx_ref, gate_ref, rank_ref, cnt_ref)


def _ssm_layer(h, n_batch, w_in, a_re, a_im, log_dt, b_re, b_im, c_re, c_im, d_skip, w_out,
               l1g, l1b, rw2, rb, *, alpha):
    n_tok, d = h.shape
    n_grp, n_state, n_ch = b_re.shape
    width = n_grp * n_ch
    n_exp = rb.shape[0]
    L = SSM_CHUNK
    assert (n_tok // n_batch) % L == 0

    u = _ssm_in_proj(h, w_in, chunk=L)
    d_op, c_in, s_out, coef = _ssm_operators(a_re, a_im, log_dt, b_re, b_im, c_re, c_im, d_skip)
    y = _ssm_scan(u, d_op, c_in, s_out, coef, n_batch=n_batch, chunk=L)

    tm = min(TM_SSM_OUT, n_tok)
    assert n_tok % tm == 0
    r_shapes, r_specs = _router_out(n_tok, tm, n_exp)
    kern = functools.partial(_ssm_out_kernel, alpha=alpha)
    n_steps = n_tok // tm
    return pl.pallas_call(
        kern,
        grid=(n_steps + 1,),
        in_specs=[pl.BlockSpec((tm, width), lambda i: (jnp.minimum(i, n_steps - 1), 0)),
                  pl.BlockSpec((tm, d), lambda i: (jnp.minimum(i, n_steps - 1), 0)),
                  _const_spec((width, 2 * d)), _const_spec((1, d)), _const_spec((1, d)),
                  _const_spec((d, 256)), _const_spec((n_exp, 1))],
        out_specs=[pl.BlockSpec((tm, d), lambda i: (_prev(i), 0)),
                   pl.BlockSpec((tm * RECORD_ROWS, 128), lambda i: (_prev(i), 0))] + r_specs,
        out_shape=[jax.ShapeDtypeStruct((n_tok, d), F32),
                   jax.ShapeDtypeStruct((n_tok * RECORD_ROWS, 128), U32)] + r_shapes,
        scratch_shapes=[pltpu.VMEM((tm, d), F32)],
        compiler_params=pltpu.CompilerParams(dimension_semantics=("arbitrary",),
                                             vmem_limit_bytes=V7X_VMEM_LIMIT_BYTES),
        name="ssm_out_mixer",
    )(y, h, w_out.astype(BF16), l1g.reshape(1, -1), l1b.reshape(1, -1), rw2, rb)


def _moe_plan(idx, rank, cnt, tme, n_tiles):
    n_exp = cnt.shape[0]
    counts = cnt[:, 0].astype(I32)
    padded = ((counts + tme - 1) // tme) * tme
    ends = jnp.cumsum(padded)
    offs = ends - padded
    n_valid = ends[-1] // tme
    tile_start = jnp.arange(n_tiles, dtype=I32) * tme
    tile_expert = jnp.minimum(jnp.sum(tile_start[:, None] >= ends[None, :], axis=1), n_exp - 1).astype(I32)
    onehot = idx[:, :, None] == jnp.arange(n_exp, dtype=I32)[None, None, :]
    pos = jnp.sum(jnp.where(onehot, offs[None, None, :], 0), axis=-1) + rank
    meta = jnp.concatenate([counts, padded, offs, n_valid[None]]).astype(I32)
    return pos, tile_expert, meta


def _tile_pos(pos, tm):
    n_tok = pos.shape[1]
    return pos.reshape(2, n_tok // tm, tm).transpose(1, 0, 2).reshape(n_tok // tm, 1, 2 * tm)


def _record_slice(ref, row, n_rec=1):
    return ref.at[pl.ds(pl.multiple_of(row * RECORD_ROWS, RECORD_ROWS), n_rec * RECORD_ROWS), :]


def _dispatch_kernel(meta_ref, pos_ref, x_ref, xs_ref, zero_scr, sem, *, n_exp, tme, n_tiles):
    i = pl.program_id(0)
    tm = x_ref.shape[0] // RECORD_ROWS
    half = zero_scr.shape[0] // RECORD_ROWS

    def zero_fill(start, size):
        cp = pltpu.make_async_copy(_record_slice(zero_scr, 0, size), _record_slice(xs_ref, start, size), sem)
        cp.start()
        cp.wait()

    @pl.when(i == 0)
    def _():
        zero_scr[...] = jnp.zeros_like(zero_scr)
        for e in range(n_exp):
            pad = meta_ref[n_exp + e] - meta_ref[e]
            base = meta_ref[2 * n_exp + e] + meta_ref[e]
            for b in range(int(math.log2(tme))):
                size = 1 << b

                @pl.when(((pad >> b) & 1) == 1)
                def _():
                    zero_fill(base + (pad & (size - 1)), size)
        n_valid = meta_ref[3 * n_exp]
        for j in range(n_exp):
            @pl.when(n_valid + j < n_tiles)
            def _():
                for part in range(tme // half):
                    zero_fill((n_valid + j) * tme + part * half, half)

    def start(r, c):
        for slot in range(2):
            p = pos_ref[0, 0, slot * tm + r]
            pltpu.make_async_copy(_record_slice(x_ref, r), _record_slice(xs_ref, p), sem).start(priority=slot)
        return c

    lax.fori_loop(0, tm, start, 0, unroll=8)
    for slot in range(2):
        pltpu.make_async_copy(x_ref, _record_slice(xs_ref, 0, tm), sem).wait()


def _cast_specs(w_in, w_out, layer, n_steps, index):
    n_layers, n_exp, d, ff2 = w_in.shape
    ff = w_out.shape[2]
    assert (n_exp * d) % n_steps == 0 and (n_exp * ff) % n_steps == 0
    rin, rout = n_exp * d // n_steps, n_exp * ff // n_steps
    in_specs = [pl.BlockSpec((None, rin, ff2), lambda i, *_: (layer, index(i), 0)),
                pl.BlockSpec((None, rout, d), lambda i, *_: (layer, index(i), 0))]
    out_specs = [pl.BlockSpec((rin, ff2), lambda i, *_: (index(i), 0)),
                 pl.BlockSpec((rout, d), lambda i, *_: (index(i), 0))]
    out_shape = [jax.ShapeDtypeStruct((n_exp * d, ff2), BF16), jax.ShapeDtypeStruct((n_exp * ff, d), BF16)]
    operands = (w_in.reshape(n_layers, n_exp * d, ff2), w_out.reshape(n_layers, n_exp * ff, d))
    return in_specs, out_specs, out_shape, operands


def _moe_dispatch(h1p, pos, meta, *, n_exp, tme, n_tiles):
    n_tok = h1p.shape[0] // RECORD_ROWS
    tm = min(TM_DISPATCH, n_tok)
    assert n_tok % tm == 0
    half = max(tme // 2, 1)
    kern = functools.partial(_dispatch_kernel, n_exp=n_exp, tme=tme, n_tiles=n_tiles)
    return pl.pallas_call(
        kern,
        grid_spec=pltpu.PrefetchScalarGridSpec(
            num_scalar_prefetch=1,
            grid=(n_tok // tm,),
            in_specs=[pl.BlockSpec((1, 1, 2 * tm), lambda i, m: (i, 0, 0), memory_space=pltpu.SMEM),
                      pl.BlockSpec((tm * RECORD_ROWS, 128), lambda i, m: (i, 0))],
            out_specs=pl.BlockSpec(memory_space=pl.ANY),
            scratch_shapes=[pltpu.VMEM((half * RECORD_ROWS, 128), U32), pltpu.SemaphoreType.DMA]),
        out_shape=jax.ShapeDtypeStruct((n_tiles * tme * RECORD_ROWS, 128), U32),
        compiler_params=pltpu.CompilerParams(dimension_semantics=("arbitrary",),
                                             vmem_limit_bytes=V7X_VMEM_LIMIT_BYTES),
        name="moe_dispatch",
    )(meta, _tile_pos(pos, tm), h1p)


def _expert_kernel(te_ref, meta_ref, xs_ref, win_ref, wout_ref, *refs, n_exp, cast):
    if cast:
        nwin_ref, nwout_ref, ys_ref, nwin_b_ref, nwout_b_ref = refs
        nwin_b_ref[...] = nwin_ref[...].astype(BF16)
        nwout_b_ref[...] = nwout_ref[...].astype(BF16)
    else:
        (ys_ref,) = refs
    i = pl.program_id(0)
    n_valid = meta_ref[3 * n_exp]
    tme = xs_ref.shape[0] // RECORD_ROWS

    @pl.when(i < n_valid)
    def _():
        hi, lo = _load_records(xs_ref, tme)
        dh = hi.shape[1]
        ff = wout_ref.shape[0]
        h = _dot(hi.astype(BF16), win_ref[:dh, :]) + _dot(lo.astype(BF16), win_ref[dh:, :])
        act = (jax.nn.silu(h[:, :ff]) * h[:, ff:]).astype(BF16)
        _store_records(ys_ref, _dot(act, wout_ref[...]))

    @pl.when(i >= n_valid)
    def _():
        ys_ref[...] = jnp.zeros_like(ys_ref)


def _moe_experts(xs, tile_expert, meta, w_in, w_out, cast_weights, *, tme):
    n_rows = xs.shape[0] // RECORD_ROWS
    n_exp, d, ff2 = w_in.shape
    ff = w_out.shape[1]
    n_tiles = n_rows // tme

    def x_map(i, te, m):
        return (jnp.minimum(i, m[3 * n_exp] - 1), 0)

    cast = cast_weights is not None
    c_in, c_out, c_shape, c_ops = ([], [], [], ())
    if cast:
        n_cast = 1 << (n_tiles.bit_length() - 1)
        c_in, c_out, c_shape, c_ops = _cast_specs(*cast_weights, n_cast, lambda i: jnp.minimum(i, n_cast - 1))
    kern = functools.partial(_expert_kernel, n_exp=n_exp, cast=cast)
    return pl.pallas_call(
        kern,
        grid_spec=pltpu.PrefetchScalarGridSpec(
            num_scalar_prefetch=2,
            grid=(n_tiles,),
            in_specs=[pl.BlockSpec((tme * RECORD_ROWS, 128), x_map),
                      pl.BlockSpec((None, d, ff2), lambda i, te, m: (te[i], 0, 0)),
                      pl.BlockSpec((None, ff, d), lambda i, te, m: (te[i], 0, 0))] + c_in,
            out_specs=[pl.BlockSpec((tme * RECORD_ROWS, 128), lambda i, te, m: (i, 0))] + c_out),
        out_shape=[jax.ShapeDtypeStruct((n_rows * RECORD_ROWS, 128), U32)] + c_shape,
        compiler_params=pltpu.CompilerParams(dimension_semantics=("arbitrary",),
                                             vmem_limit_bytes=V7X_VMEM_LIMIT_BYTES),
        name="moe_experts",
    )(tile_expert, meta, xs, w_in, w_out, *c_ops)


def _combine_kernel(pos_ref, pos_next_ref, h1_ref, gate_ref, ys_ref, g_ref, b_ref, out_ref, buf, sems, *, alpha):
    i = pl.program_id(0)
    n_steps = pl.num_programs(0)
    tm = h1_ref.shape[0]

    def gather(p_ref, half):
        def start(r, c):
            for slot in range(2):
                p = p_ref[0, 0, slot * tm + r]
                pltpu.make_async_copy(_record_slice(ys_ref, p), _record_slice(buf.at[half, slot], r),
                                      sems.at[half]).start(priority=slot)
            return c

        lax.fori_loop(0, tm, start, 0, unroll=8)

    @pl.when(i == 0)
    def _():
        gather(pos_ref, 0)

    @pl.when(i + 1 < n_steps)
    def _():
        gather(pos_next_ref, (i + 1) % 2)

    cur = i % 2
    moe = None
    for slot in range(2):
        pltpu.make_async_copy(_record_slice(ys_ref, 0, tm), buf.at[cur, slot], sems.at[cur]).wait()
    for slot in range(2):
        hi, lo = _load_records(buf.at[cur, slot], tm)
        y = jnp.concatenate([hi, lo], axis=1) * gate_ref[:, slot:slot + 1]
        moe = y if moe is None else moe + y
    out_ref[...] = _layer_norm(alpha * h1_ref[...] + moe, g_ref[...], b_ref[...])


def _moe_combine(h1, ys, pos, gates, l2g, l2b, *, alpha):
    n_tok, d = h1.shape
    tm = min(TM_COMBINE, n_tok)
    assert n_tok % tm == 0
    n_steps = n_tok // tm
    kern = functools.partial(_combine_kernel, alpha=alpha)
    pos_tiles = _tile_pos(pos, tm)
    return pl.pallas_call(
        kern,
        grid=(n_steps,),
        in_specs=[pl.BlockSpec((1, 1, 2 * tm), lambda i: (i, 0, 0), memory_space=pltpu.SMEM),
                  pl.BlockSpec((1, 1, 2 * tm), lambda i: (jnp.minimum(i + 1, n_steps - 1), 0, 0),
                               memory_space=pltpu.SMEM),
                  pl.BlockSpec((tm, d), lambda i: (i, 0)),
                  pl.BlockSpec((tm, 2), lambda i: (i, 0)),
                  pl.BlockSpec(memory_space=pl.ANY),
                  _const_spec((1, d)), _const_spec((1, d))],
        out_specs=pl.BlockSpec((tm, d), lambda i: (i, 0)),
        out_shape=jax.ShapeDtypeStruct((n_tok, d), F32),
        scratch_shapes=[pltpu.VMEM((2, 2, tm * RECORD_ROWS, 128), U32), pltpu.SemaphoreType.DMA((2,))],
        compiler_params=pltpu.CompilerParams(dimension_semantics=("arbitrary",),
                                             vmem_limit_bytes=V7X_VMEM_LIMIT_BYTES),
        name="moe_combine",
    )(pos_tiles, pos_tiles, h1, gates.T, ys, l2g.reshape(1, -1), l2b.reshape(1, -1))


def _moe_layer(h1, h1p, idx, gates, rank, cnt, w_in_b, w_out_b, cast_next, l2g, l2b, *, alpha):
    n_tok, d = h1.shape
    n_exp = cnt.shape[0]
    ff = w_out_b.shape[0] // n_exp
    tme = min(TM_EXPERT, n_tok)
    n_tiles = (2 * n_tok) // tme + n_exp
    pos, tile_expert, meta = _moe_plan(idx, rank, cnt, tme, n_tiles)
    xs = _moe_dispatch(h1p, pos, meta, n_exp=n_exp, tme=tme, n_tiles=n_tiles)
    ys, *cast = _moe_experts(xs, tile_expert, meta, w_in_b.reshape(n_exp, d, 2 * ff), w_out_b.reshape(n_exp, ff, d),
                             cast_next, tme=tme)
    return _moe_combine(h1, ys, pos, gates, l2g, l2b, alpha=alpha), cast


def kernel(x, a_w_in, a_b_in, a_ln_g, a_ln_b, a_w_s, a_b_s, a_w_out, b_w_in, b_a_re, b_a_im, b_log_dt, b_b_re, b_b_im, b_c_re, b_c_im, b_d, b_w_out, ln1_g, ln1_b, router_w, router_b, e_w_in, e_w_out, ln2_g, ln2_b):
    n_batch, seq, d = x.shape
    depth = ln1_g.shape[0]
    alpha = float((2 * depth) ** 0.25)
    n_exp = router_w.shape[1]
    rw = jnp.pad(router_w.astype(F32), ((0, 0), (0, 128 - n_exp)))
    rw_hi = lax.reduce_precision(rw, exponent_bits=8, mantissa_bits=7)
    rw2 = jnp.concatenate([rw_hi, rw - rw_hi], axis=1).astype(BF16)
    rb = router_b.astype(F32).reshape(-1, 1)
    h = x.reshape(n_batch * seq, d)
    e_bf16 = None
    for i in range(depth):
        j = i // 2
        if i % 2 == 0:
            *mixed, e_in_b, e_out_b = _gmlp_layer(h, a_w_in[j], a_b_in[j], a_ln_g[j], a_ln_b[j], a_w_s[j], a_b_s[j],
                                                  a_w_out[j], ln1_g[i], ln1_b[i], rw2, rb, (e_w_in, e_w_out, i),
                                                  alpha=alpha)
            e_bf16 = [e_in_b, e_out_b]
        else:
            mixed = _ssm_layer(h, n_batch, b_w_in[j], b_a_re[j], b_a_im[j], b_log_dt[j], b_b_re[j],
                               b_b_im[j], b_c_re[j], b_c_im[j], b_d[j], b_w_out[j],
                               ln1_g[i], ln1_b[i], rw2, rb, alpha=alpha)
        cast_next = (e_w_in, e_w_out, i + 1) if (i % 2 == 0 and i + 1 < depth) else None
        h, e_bf16 = _moe_layer(*mixed, *e_bf16, cast_next, ln2_g[i], ln2_b[i], alpha=alpha)
    return h.reshape(n_batch, seq, d)
```

```python
import functools
import math

import jax
import jax.numpy as jnp
from jax import lax
from jax.experimental import pallas as pl
from jax.experimental.pallas import tpu as pltpu

F32 = jnp.float32
BF16 = jnp.bfloat16
I32 = jnp.int32
U32 = jnp.uint32

LN_EPS = 1e-5
N_EXPERT_GROUPS = 4
SSM_CHUNK = 16

TM_MIX = 256
TM_SSM_OUT = 512
TM_PROJ = 1024
TM_DISPATCH = 512
TM_EXPERT = 512
TM_COMBINE = 256
COMBINE_GROUPS = 4

V7X_VMEM_LIMIT_BYTES = 56 * 1024 * 1024


def _dot(a, b):
    return jnp.dot(a, b, preferred_element_type=F32)


def _layer_norm(x, g, b):
    mu = jnp.mean(x, axis=-1, keepdims=True)
    xc = x - mu
    var = jnp.sum(xc * xc, axis=-1, keepdims=True) * (1.0 / x.shape[-1])
    return xc * lax.rsqrt(var + LN_EPS) * g + b


def _const_spec(shape):
    return pl.BlockSpec(shape, lambda *_: (0,) * len(shape), pipeline_mode=pl.Buffered(1))


def _split_bf16(h1):
    h_top = lax.bitcast_convert_type(lax.bitcast_convert_type(h1, U32) & jnp.uint32(0xFFFF0000), F32)
    return h_top.astype(BF16), (h1 - h_top).astype(BF16)


def _route(h_hi, h_mid, rw_ref, rb_ref, idx_ref, gate_ref, rank_ref, cnt_ref, live):
    n_exp = rb_ref.shape[0]
    per_group = n_exp // N_EXPERT_GROUPS
    tm = h_hi.shape[0]
    first = _dot(h_hi, rw_ref[...])
    logits_t = first[:, :128] + first[:, 128:] + _dot(h_mid, rw_ref[:, :128])
    logits = logits_t.T[:n_exp, :] + rb_ref[...]
    m = jnp.max(logits, axis=0, keepdims=True)
    ex = jnp.exp(logits - m)
    probs = ex / jnp.sum(ex, axis=0, keepdims=True)

    best_score = None
    best_group = None
    for g in range(N_EXPERT_GROUPS):
        t1 = t2 = None
        for e in range(g * per_group, (g + 1) * per_group):
            v = probs[e:e + 1, :]
            if t1 is None:
                t1 = v
            elif t2 is None:
                t1, t2 = jnp.maximum(t1, v), jnp.minimum(t1, v)
            else:
                t2 = jnp.maximum(t2, jnp.minimum(t1, v))
                t1 = jnp.maximum(t1, v)
        score = t1 + t2
        if best_score is None:
            best_score, best_group = score, jnp.zeros((1, tm), I32)
        else:
            better = score > best_score
            best_group = jnp.where(better, g, best_group)
            best_score = jnp.where(better, score, best_score)

    eidx = lax.broadcasted_iota(I32, (n_exp, tm), 0)
    egrp = jnp.zeros((n_exp, tm), I32)
    for g in range(1, N_EXPERT_GROUPS):
        egrp = egrp + jnp.where(eidx >= g * per_group, 1, 0)
    in_group = egrp == best_group
    masked = jnp.where(in_group, probs, -1.0)
    m1 = jnp.max(masked, axis=0, keepdims=True)
    i1 = jnp.min(jnp.where(masked == m1, eidx, n_exp), axis=0, keepdims=True)
    sel1 = eidx == i1
    masked2 = jnp.where(sel1, -2.0, masked)
    m2 = jnp.max(masked2, axis=0, keepdims=True)
    i2 = jnp.min(jnp.where(masked2 == m2, eidx, n_exp), axis=0, keepdims=True)
    sel2 = eidx == i2
    den = m1 + m2

    cnt = jnp.where(sel1 | sel2, live, 0.0)
    r_i = lax.broadcasted_iota(I32, (tm, tm), 0)
    c_i = lax.broadcasted_iota(I32, (tm, tm), 1)
    strict_upper = jnp.where(r_i < c_i, 1.0, 0.0).astype(BF16)
    prefix = _dot(cnt.astype(BF16), strict_upper) + cnt_ref[:, 0:1]
    r1 = jnp.sum(jnp.where(sel1, prefix, 0.0), axis=0, keepdims=True)
    r2 = jnp.sum(jnp.where(sel2, prefix, 0.0), axis=0, keepdims=True)
    cnt_ref[...] += jnp.sum(cnt, axis=1, keepdims=True)

    idx_ref[...] = jnp.concatenate([i1, i2], axis=0)
    gate_ref[...] = jnp.concatenate([m1 / den, m2 / den], axis=0)
    rank_ref[...] = jnp.concatenate([r1, r2], axis=0).astype(I32)


RECORD_ROWS = 8


def _store_records(ref, h):
    tm, d = h.shape
    half = d // 2
    assert half == RECORD_ROWS * 128
    bits = lax.bitcast_convert_type(h.astype(BF16).astype(F32), U32)
    packed = (bits[:, :half] & jnp.uint32(0xFFFF0000)) | (bits[:, half:] >> 16)
    for s in range(RECORD_ROWS):
        ref[pl.ds(s, tm, stride=RECORD_ROWS), :] = packed[:, s * 128:(s + 1) * 128]


def _load_records(ref, tm):
    w = jnp.concatenate([ref[pl.ds(s, tm, stride=RECORD_ROWS), :] for s in range(RECORD_ROWS)], axis=1)
    hi = lax.bitcast_convert_type(w & jnp.uint32(0xFFFF0000), F32)
    lo = lax.bitcast_convert_type(w << 16, F32)
    return hi, lo


def _router_out(n_tok, tm, n_exp):
    shapes = [jax.ShapeDtypeStruct((2, n_tok), I32),
              jax.ShapeDtypeStruct((2, n_tok), F32),
              jax.ShapeDtypeStruct((2, n_tok), I32),
              jax.ShapeDtypeStruct((n_exp, 128), F32)]
    specs = [pl.BlockSpec((2, tm), lambda i: (0, _prev(i))),
             pl.BlockSpec((2, tm), lambda i: (0, _prev(i))),
             pl.BlockSpec((2, tm), lambda i: (0, _prev(i))),
             pl.BlockSpec((n_exp, 128), lambda i: (0, 0))]
    return shapes, specs


def _prev(i):
    return jnp.maximum(i - 1, 0)


def _tail_norm(res_scr, l1g_ref, l1b_ref, h1_ref, h1p_ref):
    h1 = _layer_norm(res_scr[...], l1g_ref[...], l1b_ref[...])
    h1_ref[...] = h1
    _store_records(h1p_ref, h1)
    return _split_bf16(h1)


def _tail_route(h_split, rw_ref, rb_ref, idx_ref, gate_ref, rank_ref, cnt_ref):
    live = jnp.where(pl.program_id(0) > 0, 1.0, 0.0)
    _route(*h_split, rw_ref, rb_ref, idx_ref, gate_ref, rank_ref, cnt_ref, live)


def _gmlp_kernel(x_ref, win_ref, bin_ref, lng_ref, lnb_ref, ws_ref, bst_ref, wout_ref,
                 l1g_ref, l1b_ref, rw_ref, rb_ref, ew_in_ref, ew_out_ref,
                 h1_ref, h1p_ref, idx_ref, gate_ref, rank_ref, cnt_ref, ew_in_b_ref, ew_out_b_ref,
                 p_scr, res_scr, *, alpha, chunk, groups):
    @pl.when(pl.program_id(0) == 0)
    def _():
        cnt_ref[...] = jnp.zeros_like(cnt_ref)
        res_scr[...] = jnp.zeros_like(res_scr)

    ew_in_b_ref[...] = ew_in_ref[...].astype(BF16)
    ew_out_b_ref[...] = ew_out_ref[...].astype(BF16)

    _tail_route(_tail_norm(res_scr, l1g_ref, l1b_ref, h1_ref, h1p_ref), rw_ref, rb_ref, idx_ref, gate_ref, rank_ref,
                cnt_ref)

    x = x_ref[...]
    xb = x.astype(BF16)
    width = win_ref.shape[1] // 2
    gdim = width // groups
    tm = x.shape[0]

    zv = jax.nn.gelu(_dot(xb, win_ref[:, width:]) + bin_ref[:, width:])
    vn = _layer_norm(zv, lng_ref[...], lnb_ref[...]).astype(BF16)
    zu = jax.nn.gelu(_dot(xb, win_ref[:, :width]) + bin_ref[:, :width])

    r_i = lax.broadcasted_iota(I32, (chunk, chunk), 0)
    c_i = lax.broadcasted_iota(I32, (chunk, chunk), 1)
    causal = r_i >= c_i
    for g in range(groups):
        w = jnp.where(causal, ws_ref[g], 0.0).astype(BF16)
        bias = bst_ref[:, g:g + 1]
        for c in range(tm // chunk):
            rows = slice(c * chunk, (c + 1) * chunk)
            cols = slice(g * gdim, (g + 1) * gdim)
            s = _dot(w, vn[rows, cols]) + bias
            p_scr[rows, cols] = (zu[rows, cols] * s).astype(BF16)

    res_scr[...] = alpha * x + _dot(p_scr[...], wout_ref[...])


def _gmlp_layer(h, w_in, b_in, ln_g, ln_b, w_s, b_s, w_out, l1g, l1b, rw2, rb, expert_weights, *, alpha):
    n_tok, d = h.shape
    groups, chunk, _ = w_s.shape
    width = w_in.shape[1] // 2
    n_exp = rb.shape[0]
    tm = min(TM_MIX, n_tok)
    assert n_tok % tm == 0 and tm % chunk == 0
    r_shapes, r_specs = _router_out(n_tok, tm, n_exp)
    kern = functools.partial(_gmlp_kernel, alpha=alpha, chunk=chunk, groups=groups)
    n_steps = n_tok // tm
    c_in, c_out, c_shape, c_ops = _cast_specs(*expert_weights, n_steps, lambda i: jnp.minimum(i, n_steps - 1))
    return pl.pallas_call(
        kern,
        grid=(n_steps + 1,),
        in_specs=[pl.BlockSpec((tm, d), lambda i: (jnp.minimum(i, n_steps - 1), 0)),
                  _const_spec((d, 2 * width)), _const_spec((1, 2 * width)),
                  _const_spec((1, width)), _const_spec((1, width)),
                  _const_spec((groups, chunk, chunk)), _const_spec((chunk, groups)),
                  _const_spec((width, d)), _const_spec((1, d)), _const_spec((1, d)),
                  _const_spec((d, 256)), _const_spec((n_exp, 1))] + c_in,
        out_specs=[pl.BlockSpec((tm, d), lambda i: (_prev(i), 0)),
                   pl.BlockSpec((tm * RECORD_ROWS, 128), lambda i: (_prev(i), 0))] + r_specs + c_out,
        out_shape=[jax.ShapeDtypeStruct((n_tok, d), F32),
                   jax.ShapeDtypeStruct((n_tok * RECORD_ROWS, 128), U32)] + r_shapes + c_shape,
        scratch_shapes=[pltpu.VMEM((tm, width), BF16), pltpu.VMEM((tm, d), F32)],
        compiler_params=pltpu.CompilerParams(dimension_semantics=("arbitrary",),
                                             vmem_limit_bytes=V7X_VMEM_LIMIT_BYTES),
        name="gmlp_mixer",
    )(h, w_in.astype(BF16), b_in.reshape(1, -1), ln_g.reshape(1, -1), ln_b.reshape(1, -1),
      w_s, b_s.T, w_out.astype(BF16), l1g.reshape(1, -1), l1b.reshape(1, -1), rw2, rb, *c_ops)


SSM_LANE_GROUPS = 8
SSM_BATCH_PER_STEP = 4


def _proj_kernel(x_ref, w_ref, o_ref, res_scr, *, chunk):
    res = _dot(x_ref[...].astype(BF16), w_ref[...])
    n_slab = res_scr.shape[0]
    kb = o_ref.shape[0]
    for j in range(n_slab):
        res_scr[j] = res[:, j * 128:(j + 1) * 128]
    for t in range(chunk):
        for j in range(n_slab):
            o_ref[:, (t * n_slab + j) * 128:(t * n_slab + j + 1) * 128] = (
                res_scr[j, pl.ds(t, kb, stride=chunk), :].astype(o_ref.dtype))


def _ssm_in_proj(h, w, *, chunk):
    n_tok, d = h.shape
    width = w.shape[1]
    tm = min(TM_PROJ, n_tok)
    assert n_tok % tm == 0 and tm % chunk == 0 and width % 128 == 0
    kern = functools.partial(_proj_kernel, chunk=chunk)
    return pl.pallas_call(
        kern,
        grid=(n_tok // tm,),
        in_specs=[pl.BlockSpec((tm, d), lambda i: (i, 0)), _const_spec((d, width))],
        out_specs=pl.BlockSpec((tm // chunk, chunk * width), lambda i: (i, 0)),
        out_shape=jax.ShapeDtypeStruct((n_tok // chunk, chunk * width), BF16),
        scratch_shapes=[pltpu.VMEM((width // 128, tm, 128), F32)],
        compiler_params=pltpu.CompilerParams(dimension_semantics=("arbitrary",),
                                             vmem_limit_bytes=V7X_VMEM_LIMIT_BYTES),
        name="ssm_in_proj",
    )(h, w.astype(BF16))


def _ssm_kernel(*refs, chunk, n_batch, n_chunks):
    u_refs = refs[:chunk]
    d_ref, cin_ref, sout_ref, coef_ref, y_ref, t_ref, sin_ref, w_scr, x_scr, ytok_scr = refs[chunk:]
    bblk = pl.program_id(1)
    rows = n_batch * n_chunks
    n_x = x_scr.shape[0]

    @pl.when(bblk == 0)
    def _():
        for tp in range(chunk):
            for t in range(chunk):
                blk = d_ref[t - tp] if t >= tp else jnp.zeros((128, 128), BF16)
                t_ref[tp * 128:(tp + 1) * 128, t * 128:(t + 1) * 128] = blk
        ch_bits = int(math.log2(128 // SSM_LANE_GROUPS))
        row = lax.broadcasted_iota(I32, (chunk * 128, 128), 0)
        row_group = lax.shift_right_logical(row, ch_bits) & (SSM_LANE_GROUPS - 1)
        for part in range(2 * n_x // SSM_LANE_GROUPS):
            src = cin_ref[:, part * 128:(part + 1) * 128]
            for g in range(SSM_LANE_GROUPS):
                col = (part * SSM_LANE_GROUPS + g) * 128
                sin_ref[:, col:col + 128] = jnp.where(row_group == g, src, jnp.zeros_like(src))

    pitch = w_scr.shape[1] // n_batch

    u = jnp.concatenate([r[...] for r in u_refs], axis=1)
    for jb in range(n_x):
        w = _dot(u, sin_ref[:, jb * 256:(jb + 1) * 256])
        for b in range(n_batch):
            seq = slice(b * n_chunks, (b + 1) * n_chunks)
            w_scr[2 * jb, b * pitch:b * pitch + n_chunks, :] = w[seq, :128]
            w_scr[2 * jb + 1, b * pitch:b * pitch + n_chunks, :] = w[seq, 128:]

    ar = [jnp.broadcast_to(coef_ref[0:1, j * 128:(j + 1) * 128], (n_batch, 128)) for j in range(n_x)]
    ai_a = [jnp.broadcast_to(coef_ref[1:2, j * 128:(j + 1) * 128], (n_batch, 128)) for j in range(n_x)]
    ai_b = [jnp.broadcast_to(coef_ref[2:3, j * 128:(j + 1) * 128], (n_batch, 128)) for j in range(n_x)]

    def step(k, carry):
        x, xsw = carry
        rows_k = pl.ds(k, n_batch, stride=pitch)
        new_x, new_xsw = [], []
        for j in range(n_x):
            x_scr[j, rows_k, :] = x[j]
            new_x.append(ar[j] * x[j] + ai_a[j] * xsw[j] + w_scr[j, rows_k, :])
            new_xsw.append(ar[j] * xsw[j] + ai_b[j] * x[j] + w_scr[n_x + j, rows_k, :])
        return tuple(new_x), tuple(new_xsw)

    zero = tuple(jnp.zeros((n_batch, 128), F32) for _ in range(n_x))
    lax.fori_loop(0, n_chunks, step, (zero, zero))

    xs = jnp.concatenate(
        [jnp.concatenate([x_scr[j, b * pitch:b * pitch + n_chunks, :] for b in range(n_batch)], axis=0)
         for j in range(n_x)], axis=1).astype(BF16)
    for jb in range(chunk // 2):
        kk = (jb + 1) * 256
        cols = slice(jb * 256, (jb + 1) * 256)
        y = jax.nn.gelu(_dot(u[:, :kk], t_ref[:kk, cols]) + _dot(xs, sout_ref[:, cols]))
        for half in range(2):
            ytok_scr[pl.ds(2 * jb + half, rows, stride=chunk), :] = y[:, half * 128:(half + 1) * 128]
    y_ref[...] = ytok_scr[...].astype(y_ref.dtype)


def _ssm_operators(a_re, a_im, log_dt, b_re, b_im, c_re, c_im, d_skip):
    hp = lax.Precision.HIGHEST
    n_grp, n_state, n_ch = b_re.shape
    L = SSM_CHUNK
    nb = SSM_LANE_GROUPS
    assert n_ch * nb == 128 and 2 * n_state == 128 and n_grp % nb == 0
    n_gb = n_grp // nb
    a = lax.complex(a_re.astype(F32), a_im.astype(F32))
    lam = a * jnp.exp(log_dt.astype(F32))[:, None]
    a_bar = jnp.exp(lam)
    b_bar = ((a_bar - 1.0) / a)[..., None] * lax.complex(b_re.astype(F32), b_im.astype(F32))
    cc = lax.complex(c_re.astype(F32), c_im.astype(F32))
    steps = jnp.arange(L + 1, dtype=F32)
    apow = jnp.exp(steps[:, None, None] * lam[None])
    lane = jnp.arange(128)
    put = ((lane[None, None, :] // n_ch == jnp.arange(nb)[:, None, None])
           & (lane[None, None, :] % n_ch == jnp.arange(n_ch)[None, :, None])).astype(F32)

    kern = jnp.real(jnp.einsum('gcp,lgp,gpd->glcd', cc, apow[:L], b_bar, precision=hp))
    d_op = jnp.einsum('Bgtcd,gdr,gcl->Btrl', kern.reshape(n_gb, nb, L, n_ch, n_ch), put, put, precision=hp)
    skip = jnp.eye(128, dtype=F32)[None] * d_skip.astype(F32).reshape(n_gb, 1, 128)
    d_op = d_op.at[:, 0].add(skip).astype(BF16)

    s_in = apow[:L][::-1].transpose(1, 0, 2)[:, :, None, :] * b_bar.transpose(0, 2, 1)[:, None, :, :]
    s_in = jnp.concatenate([s_in.real, s_in.imag, s_in.imag, s_in.real], axis=-1).astype(BF16)
    c_in = s_in.reshape(n_gb, nb, L, n_ch, 4 * n_state).transpose(0, 2, 1, 3, 4).reshape(n_gb, L * 128, 4 * n_state)

    s_out = cc.transpose(0, 2, 1)[:, :, None, :] * apow[1:].transpose(1, 2, 0)[:, :, :, None]
    s_out = jnp.concatenate([s_out.real, -s_out.imag], axis=1)
    x_idx = jnp.arange(L * n_ch)
    l_idx = jnp.arange(L * 128)
    place = ((x_idx[:, None] // n_ch == l_idx[None, :] // 128) & (x_idx[:, None] % n_ch == l_idx[None, :] % n_ch))
    place = place[None] & ((l_idx[None, None, :] // n_ch) % nb == jnp.arange(nb)[:, None, None])
    s_out = jnp.einsum('Bgqx,gxl->Bgql', s_out.reshape(n_gb, nb, 2 * n_state, L * n_ch), place.astype(F32),
                       precision=hp).astype(BF16).reshape(n_gb, nb * 2 * n_state, L * 128)

    a_l = apow[L]
    coef = jnp.stack([jnp.concatenate([a_l.real, a_l.real], -1),
                      jnp.concatenate([-a_l.imag, a_l.imag], -1),
                      jnp.concatenate([a_l.imag, -a_l.imag], -1)], axis=1)
    coef = coef.reshape(n_gb, nb, 3, 2 * n_state).transpose(0, 2, 1, 3).reshape(n_gb, 3, nb * 2 * n_state)
    coef = jnp.pad(coef, ((0, 0), (0, 5), (0, 0)))
    return d_op, c_in, s_out, coef


def _ssm_scan(u, d_op, c_in, s_out, coef, *, n_batch, chunk):
    rows_all, lanes_all = u.shape
    n_blk, lc, _ = c_in.shape
    n_chunks = rows_all // n_batch
    nb = min(SSM_BATCH_PER_STEP, n_batch)
    assert n_batch % nb == 0 and lanes_all == chunk * n_blk * 128
    rows = nb * n_chunks
    pitch = n_chunks + 4 if n_chunks % 8 == 0 else n_chunks
    n_x = s_out.shape[1] // 128
    kern = functools.partial(_ssm_kernel, chunk=chunk, n_batch=nb, n_chunks=n_chunks)
    u_specs = [pl.BlockSpec((rows, 128), lambda g, b, t=t: (b, t * n_blk + g)) for t in range(chunk)]

    def op_spec(shape):
        return pl.BlockSpec((None,) + shape, lambda g, b: (g,) + (0,) * len(shape))

    return pl.pallas_call(
        kern,
        grid=(n_blk, n_batch // nb),
        in_specs=u_specs + [op_spec((chunk, 128, 128)), op_spec((lc, c_in.shape[2])),
                            op_spec((n_x * 128, lc)), op_spec((8, n_x * 128))],
        out_specs=pl.BlockSpec((rows * chunk, 128), lambda g, b: (b, g)),
        out_shape=jax.ShapeDtypeStruct((rows_all * chunk, n_blk * 128), BF16),
        scratch_shapes=[pltpu.VMEM((lc, lc), BF16), pltpu.VMEM((lc, 2 * n_x * 128), BF16),
                        pltpu.VMEM((2 * n_x, nb * pitch, 128), F32), pltpu.VMEM((n_x, nb * pitch, 128), F32),
                        pltpu.VMEM((rows * chunk, 128), F32)],
        compiler_params=pltpu.CompilerParams(dimension_semantics=("arbitrary", "arbitrary"),
                                             vmem_limit_bytes=V7X_VMEM_LIMIT_BYTES),
        name="ssm_scan",
    )(*([u] * chunk), d_op, c_in, s_out, coef)


def _ssm_out_kernel(y_ref, h_ref, wout_ref, l1g_ref, l1b_ref, rw_ref, rb_ref,
                    h1_ref, h1p_ref, idx_ref, gate_ref, rank_ref, cnt_ref, res_scr, *, alpha):
    @pl.when(pl.program_id(0) == 0)
    def _():
        cnt_ref[...] = jnp.zeros_like(cnt_ref)
        res_scr[...] = jnp.zeros_like(res_scr)

    h_split = _tail_norm(res_scr, l1g_ref, l1b_ref, h1_ref, h1p_ref)

    y = y_ref[...]
    d = h_ref.shape[1]
    val = _dot(y, wout_ref[:, :d])
    gate = _dot(y, wout_ref[:, d:])
    res_scr[...] = alpha * h_ref[...] + val * jax.nn.sigmoid(gate)
    _tail_route(h_split, rw_ref, rb_ref, idx_ref, gate_ref, rank_ref, cnt_ref)


def _ssm_layer(h, n_batch, w_in, a_re, a_im, log_dt, b_re, b_im, c_re, c_im, d_skip, w_out,
               l1g, l1b, rw2, rb, *, alpha):
    n_tok, d = h.shape
    n_grp, n_state, n_ch = b_re.shape
    width = n_grp * n_ch
    n_exp = rb.shape[0]
    L = SSM_CHUNK
    assert (n_tok // n_batch) % L == 0

    u = _ssm_in_proj(h, w_in, chunk=L)
    d_op, c_in, s_out, coef = _ssm_operators(a_re, a_im, log_dt, b_re, b_im, c_re, c_im, d_skip)
    y = _ssm_scan(u, d_op, c_in, s_out, coef, n_batch=n_batch, chunk=L)

    tm = min(TM_SSM_OUT, n_tok)
    assert n_tok % tm == 0
    r_shapes, r_specs = _router_out(n_tok, tm, n_exp)
    kern = functools.partial(_ssm_out_kernel, alpha=alpha)
    n_steps = n_tok // tm
    return pl.pallas_call(
        kern,
        grid=(n_steps + 1,),
        in_specs=[pl.BlockSpec((tm, width), lambda i: (jnp.minimum(i, n_steps - 1), 0)),
                  pl.BlockSpec((tm, d), lambda i: (jnp.minimum(i, n_steps - 1), 0)),
                  _const_spec((width, 2 * d)), _const_spec((1, d)), _const_spec((1, d)),
                  _const_spec((d, 256)), _const_spec((n_exp, 1))],
        out_specs=[pl.BlockSpec((tm, d), lambda i: (_prev(i), 0)),
                   pl.BlockSpec((tm * RECORD_ROWS, 128), lambda i: (_prev(i), 0))] + r_specs,
        out_shape=[jax.ShapeDtypeStruct((n_tok, d), F32),
                   jax.ShapeDtypeStruct((n_tok * RECORD_ROWS, 128), U32)] + r_shapes,
        scratch_shapes=[pltpu.VMEM((tm, d), F32)],
        compiler_params=pltpu.CompilerParams(dimension_semantics=("arbitrary",),
                                             vmem_limit_bytes=V7X_VMEM_LIMIT_BYTES),
        name="ssm_out_mixer",
    )(y, h, w_out.astype(BF16), l1g.reshape(1, -1), l1b.reshape(1, -1), rw2, rb)


def _moe_plan(idx, rank, cnt, tme, n_tiles):
    n_exp = cnt.shape[0]
    counts = cnt[:, 0].astype(I32)
    padded = ((counts + tme - 1) // tme) * tme
    ends = jnp.cumsum(padded)
    offs = ends - padded
    n_valid = ends[-1] // tme
    tile_start = jnp.arange(n_tiles, dtype=I32) * tme
    tile_expert = jnp.minimum(jnp.sum(tile_start[:, None] >= ends[None, :], axis=1), n_exp - 1).astype(I32)
    onehot = idx[:, :, None] == jnp.arange(n_exp, dtype=I32)[None, None, :]
    pos = jnp.sum(jnp.where(onehot, offs[None, None, :], 0), axis=-1) + rank
    meta = jnp.concatenate([counts, padded, offs, n_valid[None]]).astype(I32)
    return pos, tile_expert, meta


def _tile_pos(pos, tm):
    n_tok = pos.shape[1]
    return pos.reshape(2, n_tok // tm, tm).transpose(1, 0, 2).reshape(n_tok // tm, 1, 2 * tm)


def _record_slice(ref, row, n_rec=1):
    return ref.at[pl.ds(pl.multiple_of(row * RECORD_ROWS, RECORD_ROWS), n_rec * RECORD_ROWS), :]


def _dispatch_kernel(meta_ref, pos_ref, x_ref, xs_ref, zero_scr, sem, *, n_exp, tme, n_tiles):
    i = pl.program_id(0)
    tm = x_ref.shape[0] // RECORD_ROWS
    half = zero_scr.shape[0] // RECORD_ROWS

    def zero_fill(start, size):
        cp = pltpu.make_async_copy(_record_slice(zero_scr, 0, size), _record_slice(xs_ref, start, size), sem)
        cp.start()
        cp.wait()

    @pl.when(i == 0)
    def _():
        zero_scr[...] = jnp.zeros_like(zero_scr)
        for e in range(n_exp):
            pad = meta_ref[n_exp + e] - meta_ref[e]
            base = meta_ref[2 * n_exp + e] + meta_ref[e]
            for b in range(int(math.log2(tme))):
                size = 1 << b

                @pl.when(((pad >> b) & 1) == 1)
                def _():
                    zero_fill(base + (pad & (size - 1)), size)
        n_valid = meta_ref[3 * n_exp]
        for j in range(n_exp):
            @pl.when(n_valid + j < n_tiles)
            def _():
                for part in range(tme // half):
                    zero_fill((n_valid + j) * tme + part * half, half)

    def start(r, c):
        for slot in range(2):
            p = pos_ref[0, 0, slot * tm + r]
            pltpu.make_async_copy(_record_slice(x_ref, r), _record_slice(xs_ref, p), sem).start(priority=slot)
        return c

    lax.fori_loop(0, tm, start, 0, unroll=8)
    for slot in range(2):
        pltpu.make_async_copy(x_ref, _record_slice(xs_ref, 0, tm), sem).wait()


def _cast_specs(w_in, w_out, layer, n_steps, index):
    n_layers, n_exp, d, ff2 = w_in.shape
    ff = w_out.shape[2]
    assert (n_exp * d) % n_steps == 0 and (n_exp * ff) % n_steps == 0
    rin, rout = n_exp * d // n_steps, n_exp * ff // n_steps
    in_specs = [pl.BlockSpec((None, rin, ff2), lambda i, *_: (layer, index(i), 0)),
                pl.BlockSpec((None, rout, d), lambda i, *_: (layer, index(i), 0))]
    out_specs = [pl.BlockSpec((rin, ff2), lambda i, *_: (index(i), 0)),
                 pl.BlockSpec((rout, d), lambda i, *_: (index(i), 0))]
    out_shape = [jax.ShapeDtypeStruct((n_exp * d, ff2), BF16), jax.ShapeDtypeStruct((n_exp * ff, d), BF16)]
    operands = (w_in.reshape(n_layers, n_exp * d, ff2), w_out.reshape(n_layers, n_exp * ff, d))
    return in_specs, out_specs, out_shape, operands


def _moe_dispatch(h1p, pos, meta, *, n_exp, tme, n_tiles):
    n_tok = h1p.shape[0] // RECORD_ROWS
    tm = min(TM_DISPATCH, n_tok)
    assert n_tok % tm == 0
    half = max(tme // 2, 1)
    kern = functools.partial(_dispatch_kernel, n_exp=n_exp, tme=tme, n_tiles=n_tiles)
    return pl.pallas_call(
        kern,
        grid_spec=pltpu.PrefetchScalarGridSpec(
            num_scalar_prefetch=1,
            grid=(n_tok // tm,),
            in_specs=[pl.BlockSpec((1, 1, 2 * tm), lambda i, m: (i, 0, 0), memory_space=pltpu.SMEM),
                      pl.BlockSpec((tm * RECORD_ROWS, 128), lambda i, m: (i, 0))],
            out_specs=pl.BlockSpec(memory_space=pl.ANY),
            scratch_shapes=[pltpu.VMEM((half * RECORD_ROWS, 128), U32), pltpu.SemaphoreType.DMA]),
        out_shape=jax.ShapeDtypeStruct((n_tiles * tme * RECORD_ROWS, 128), U32),
        compiler_params=pltpu.CompilerParams(dimension_semantics=("arbitrary",),
                                             vmem_limit_bytes=V7X_VMEM_LIMIT_BYTES),
        name="moe_dispatch",
    )(meta, _tile_pos(pos, tm), h1p)


def _expert_kernel(te_ref, meta_ref, xs_ref, win_ref, wout_ref, *refs, n_exp, cast):
    if cast:
        nwin_ref, nwout_ref, ys_ref, nwin_b_ref, nwout_b_ref = refs
        nwin_b_ref[...] = nwin_ref[...].astype(BF16)
        nwout_b_ref[...] = nwout_ref[...].astype(BF16)
    else:
        (ys_ref,) = refs
    i = pl.program_id(0)
    n_valid = meta_ref[3 * n_exp]
    tme = xs_ref.shape[0] // RECORD_ROWS

    @pl.when(i < n_valid)
    def _():
        hi, lo = _load_records(xs_ref, tme)
        dh = hi.shape[1]
        ff = wout_ref.shape[0]
        h = _dot(hi.astype(BF16), win_ref[:dh, :]) + _dot(lo.astype(BF16), win_ref[dh:, :])
        act = (jax.nn.silu(h[:, :ff]) * h[:, ff:]).astype(BF16)
        _store_records(ys_ref, _dot(act, wout_ref[...]))

    @pl.when(i >= n_valid)
    def _():
        ys_ref[...] = jnp.zeros_like(ys_ref)


def _moe_experts(xs, tile_expert, meta, w_in, w_out, cast_weights, *, tme):
    n_rows = xs.shape[0] // RECORD_ROWS
    n_exp, d, ff2 = w_in.shape
    ff = w_out.shape[1]
    n_tiles = n_rows // tme

    def x_map(i, te, m):
        return (jnp.minimum(i, m[3 * n_exp] - 1), 0)

    cast = cast_weights is not None
    c_in, c_out, c_shape, c_ops = ([], [], [], ())
    if cast:
        n_cast = 1 << (n_tiles.bit_length() - 1)
        c_in, c_out, c_shape, c_ops = _cast_specs(*cast_weights, n_cast, lambda i: jnp.minimum(i, n_cast - 1))
    kern = functools.partial(_expert_kernel, n_exp=n_exp, cast=cast)
    return pl.pallas_call(
        kern,
        grid_spec=pltpu.PrefetchScalarGridSpec(
            num_scalar_prefetch=2,
            grid=(n_tiles,),
            in_specs=[pl.BlockSpec((tme * RECORD_ROWS, 128), x_map),
                      pl.BlockSpec((None, d, ff2), lambda i, te, m: (te[i], 0, 0)),
                      pl.BlockSpec((None, ff, d), lambda i, te, m: (te[i], 0, 0))] + c_in,
            out_specs=[pl.BlockSpec((tme * RECORD_ROWS, 128), lambda i, te, m: (i, 0))] + c_out),
        out_shape=[jax.ShapeDtypeStruct((n_rows * RECORD_ROWS, 128), U32)] + c_shape,
        compiler_params=pltpu.CompilerParams(dimension_semantics=("arbitrary",),
                                             vmem_limit_bytes=V7X_VMEM_LIMIT_BYTES),
        name="moe_experts",
    )(tile_expert, meta, xs, w_in, w_out, *c_ops)


def _combine_kernel(pos_ref, pos_next_ref, h1_ref, gate_ref, ys_ref, g_ref, b_ref, out_ref, buf, sems, *, alpha):
    i = pl.program_id(0)
    n_steps = pl.num_programs(0)
    tm = h1_ref.shape[0]

    def gather(p_ref, half):
        def start(r, c):
            for slot in range(2):
                p = p_ref[0, 0, slot * tm + r]
                pltpu.make_async_copy(_record_slice(ys_ref, p), _record_slice(buf.at[half, slot], r),
                                      sems.at[half]).start(priority=slot)
            return c

        lax.fori_loop(0, tm, start, 0, unroll=8)

    @pl.when(i == 0)
    def _():
        gather(pos_ref, 0)

    cur = i % 2
    nxt = 1 - cur

    def wait_half(half):
        for slot in range(2):
            pltpu.make_async_copy(_record_slice(ys_ref, 0, tm), buf.at[half, slot], sems.at[half]).wait()

    wait_half(cur)
    group = tm // COMBINE_GROUPS
    for c in range(COMBINE_GROUPS):
        for r in range(c * group, (c + 1) * group):
            for slot in range(2):
                p = pos_next_ref[0, 0, slot * tm + r]
                pltpu.make_async_copy(_record_slice(ys_ref, p), _record_slice(buf.at[nxt, slot], r),
                                      sems.at[nxt]).start(priority=slot)
        rows = slice(c * group, (c + 1) * group)
        moe = None
        for slot in range(2):
            recs = buf.at[cur, slot]
            w = jnp.concatenate([recs[pl.ds(c * group * RECORD_ROWS + s, group, stride=RECORD_ROWS), :]
                                 for s in range(RECORD_ROWS)], axis=1)
            y = jnp.concatenate([lax.bitcast_convert_type(w & jnp.uint32(0xFFFF0000), F32),
                                 lax.bitcast_convert_type(w << 16, F32)], axis=1) * gate_ref[rows, slot:slot + 1]
            moe = y if moe is None else moe + y
        out_ref[rows, :] = _layer_norm(alpha * h1_ref[rows, :] + moe, g_ref[...], b_ref[...])

    @pl.when(i + 1 == n_steps)
    def _():
        wait_half(nxt)


def _moe_combine(h1, ys, pos, gates, l2g, l2b, *, alpha):
    n_tok, d = h1.shape
    tm = min(TM_COMBINE, n_tok)
    assert n_tok % tm == 0
    n_steps = n_tok // tm
    kern = functools.partial(_combine_kernel, alpha=alpha)
    pos_tiles = _tile_pos(pos, tm)
    return pl.pallas_call(
        kern,
        grid=(n_steps,),
        in_specs=[pl.BlockSpec((1, 1, 2 * tm), lambda i: (i, 0, 0), memory_space=pltpu.SMEM),
                  pl.BlockSpec((1, 1, 2 * tm), lambda i: (jnp.minimum(i + 1, n_steps - 1), 0, 0),
                               memory_space=pltpu.SMEM),
                  pl.BlockSpec((tm, d), lambda i: (i, 0)),
                  pl.BlockSpec((tm, 2), lambda i: (i, 0)),
                  pl.BlockSpec(memory_space=pl.ANY),
                  _const_spec((1, d)), _const_spec((1, d))],
        out_specs=pl.BlockSpec((tm, d), lambda i: (i, 0)),
        out_shape=jax.ShapeDtypeStruct((n_tok, d), F32),
        scratch_shapes=[pltpu.VMEM((2, 2, tm * RECORD_ROWS, 128), U32), pltpu.SemaphoreType.DMA((2,))],
        compiler_params=pltpu.CompilerParams(dimension_semantics=("arbitrary",),
                                             vmem_limit_bytes=V7X_VMEM_LIMIT_BYTES),
        name="moe_combine",
    )(pos_tiles, pos_tiles, h1, gates.T, ys, l2g.reshape(1, -1), l2b.reshape(1, -1))


def _moe_layer(h1, h1p, idx, gates, rank, cnt, w_in_b, w_out_b, cast_next, l2g, l2b, *, alpha):
    n_tok, d = h1.shape
    n_exp = cnt.shape[0]
    ff = w_out_b.shape[0] // n_exp
    tme = min(TM_EXPERT, n_tok)
    n_tiles = (2 * n_tok) // tme + n_exp
    pos, tile_expert, meta = _moe_plan(idx, rank, cnt, tme, n_tiles)
    xs = _moe_dispatch(h1p, pos, meta, n_exp=n_exp, tme=tme, n_tiles=n_tiles)
    ys, *cast = _moe_experts(xs, tile_expert, meta, w_in_b.reshape(n_exp, d, 2 * ff), w_out_b.reshape(n_exp, ff, d),
                             cast_next, tme=tme)
    return _moe_combine(h1, ys, pos, gates, l2g, l2b, alpha=alpha), cast


def kernel(x, a_w_in, a_b_in, a_ln_g, a_ln_b, a_w_s, a_b_s, a_w_out, b_w_in, b_a_re, b_a_im, b_log_dt, b_b_re, b_b_im, b_c_re, b_c_im, b_d, b_w_out, ln1_g, ln1_b, router_w, router_b, e_w_in, e_w_out, ln2_g, ln2_b):
    n_batch, seq, d = x.shape
    depth = ln1_g.shape[0]
    alpha = float((2 * depth) ** 0.25)
    n_exp = router_w.shape[1]
    rw = jnp.pad(router_w.astype(F32), ((0, 0), (0, 128 - n_exp)))
    rw_hi = lax.reduce_precision(rw, exponent_bits=8, mantissa_bits=7)
    rw2 = jnp.concatenate([rw_hi, rw - rw_hi], axis=1).astype(BF16)
    rb = router_b.astype(F32).reshape(-1, 1)
    h = x.reshape(n_batch * seq, d)
    e_bf16 = None
    for i in range(depth):
        j = i // 2
        if i % 2 == 0:
            *mixed, e_in_b, e_out_b = _gmlp_layer(h, a_w_in[j], a_b_in[j], a_ln_g[j], a_ln_b[j], a_w_s[j], a_b_s[j],
                                                  a_w_out[j], ln1_g[i], ln1_b[i], rw2, rb, (e_w_in, e_w_out, i),
                                                  alpha=alpha)
            e_bf16 = [e_in_b, e_out_b]
        else:
            mixed = _ssm_layer(h, n_batch, b_w_in[j], b_a_re[j], b_a_im[j], b_log_dt[j], b_b_re[j],
                               b_b_im[j], b_c_re[j], b_c_im[j], b_d[j], b_w_out[j],
                               ln1_g[i], ln1_b[i], rw2, rb, alpha=alpha)
        cast_next = (e_w_in, e_w_out, i + 1) if (i % 2 == 0 and i + 1 < depth) else None
        h, e_bf16 = _moe_layer(*mixed, *e_bf16, cast_next, ln2_g[i], ln2_b[i], alpha=alpha)
    return h.reshape(n_batch, seq, d)
```

```python
import functools
import math

import jax
import jax.numpy as jnp
from jax import lax
from jax.experimental import pallas as pl
from jax.experimental.pallas import tpu as pltpu

F32 = jnp.float32
BF16 = jnp.bfloat16
I32 = jnp.int32
U32 = jnp.uint32

LN_EPS = 1e-5
N_EXPERT_GROUPS = 4
SSM_CHUNK = 16

TM_MIX = 256
TM_SSM_OUT = 512
TM_PROJ = 1024
TM_DISPATCH = 256
TM_EXPERT = 512
TM_COMBINE = 128

V7X_VMEM_LIMIT_BYTES = 56 * 1024 * 1024


def _dot(a, b):
    return jnp.dot(a, b, preferred_element_type=F32)


def _layer_norm(x, g, b):
    mu = jnp.mean(x, axis=-1, keepdims=True)
    xc = x - mu
    var = jnp.sum(xc * xc, axis=-1, keepdims=True) * (1.0 / x.shape[-1])
    return xc * lax.rsqrt(var + LN_EPS) * g + b


def _const_spec(shape):
    return pl.BlockSpec(shape, lambda *_: (0,) * len(shape), pipeline_mode=pl.Buffered(1))


def _split_bf16(h1):
    h_top = lax.bitcast_convert_type(lax.bitcast_convert_type(h1, U32) & jnp.uint32(0xFFFF0000), F32)
    return h_top.astype(BF16), (h1 - h_top).astype(BF16)


def _route(h_hi, h_mid, rw_ref, rb_ref, idx_ref, gate_ref, rank_ref, cnt_ref, live):
    n_exp = rb_ref.shape[0]
    per_group = n_exp // N_EXPERT_GROUPS
    tm = h_hi.shape[0]
    first = _dot(h_hi, rw_ref[...])
    logits_t = first[:, :128] + first[:, 128:] + _dot(h_mid, rw_ref[:, :128])
    logits = logits_t.T[:n_exp, :] + rb_ref[...]
    m = jnp.max(logits, axis=0, keepdims=True)
    ex = jnp.exp(logits - m)
    probs = ex / jnp.sum(ex, axis=0, keepdims=True)

    best_score = None
    best_group = None
    for g in range(N_EXPERT_GROUPS):
        t1 = t2 = None
        for e in range(g * per_group, (g + 1) * per_group):
            v = probs[e:e + 1, :]
            if t1 is None:
                t1 = v
            elif t2 is None:
                t1, t2 = jnp.maximum(t1, v), jnp.minimum(t1, v)
            else:
                t2 = jnp.maximum(t2, jnp.minimum(t1, v))
                t1 = jnp.maximum(t1, v)
        score = t1 + t2
        if best_score is None:
            best_score, best_group = score, jnp.zeros((1, tm), I32)
        else:
            better = score > best_score
            best_group = jnp.where(better, g, best_group)
            best_score = jnp.where(better, score, best_score)

    eidx = lax.broadcasted_iota(I32, (n_exp, tm), 0)
    egrp = jnp.zeros((n_exp, tm), I32)
    for g in range(1, N_EXPERT_GROUPS):
        egrp = egrp + jnp.where(eidx >= g * per_group, 1, 0)
    in_group = egrp == best_group
    masked = jnp.where(in_group, probs, -1.0)
    m1 = jnp.max(masked, axis=0, keepdims=True)
    i1 = jnp.min(jnp.where(masked == m1, eidx, n_exp), axis=0, keepdims=True)
    sel1 = eidx == i1
    masked2 = jnp.where(sel1, -2.0, masked)
    m2 = jnp.max(masked2, axis=0, keepdims=True)
    i2 = jnp.min(jnp.where(masked2 == m2, eidx, n_exp), axis=0, keepdims=True)
    sel2 = eidx == i2
    den = m1 + m2

    cnt = jnp.where(sel1 | sel2, live, 0.0)
    r_i = lax.broadcasted_iota(I32, (tm, tm), 0)
    c_i = lax.broadcasted_iota(I32, (tm, tm), 1)
    strict_upper = jnp.where(r_i < c_i, 1.0, 0.0).astype(BF16)
    prefix = _dot(cnt.astype(BF16), strict_upper) + cnt_ref[:, 0:1]
    r1 = jnp.sum(jnp.where(sel1, prefix, 0.0), axis=0, keepdims=True)
    r2 = jnp.sum(jnp.where(sel2, prefix, 0.0), axis=0, keepdims=True)
    cnt_ref[...] += jnp.sum(cnt, axis=1, keepdims=True)

    idx_ref[...] = jnp.concatenate([i1, i2], axis=0)
    gate_ref[...] = jnp.concatenate([m1 / den, m2 / den], axis=0)
    rank_ref[...] = jnp.concatenate([r1, r2], axis=0).astype(I32)


RECORD_ROWS = 8


def _store_records(ref, h):
    tm, d = h.shape
    half = d // 2
    assert half == RECORD_ROWS * 128
    bits = lax.bitcast_convert_type(h.astype(BF16).astype(F32), U32)
    packed = (bits[:, :half] & jnp.uint32(0xFFFF0000)) | (bits[:, half:] >> 16)
    for s in range(RECORD_ROWS):
        ref[pl.ds(s, tm, stride=RECORD_ROWS), :] = packed[:, s * 128:(s + 1) * 128]


def _load_records(ref, tm):
    w = jnp.concatenate([ref[pl.ds(s, tm, stride=RECORD_ROWS), :] for s in range(RECORD_ROWS)], axis=1)
    hi = lax.bitcast_convert_type(w & jnp.uint32(0xFFFF0000), F32)
    lo = lax.bitcast_convert_type(w << 16, F32)
    return hi, lo


def _router_out(n_tok, tm, n_exp):
    shapes = [jax.ShapeDtypeStruct((2, n_tok), I32),
              jax.ShapeDtypeStruct((2, n_tok), F32),
              jax.ShapeDtypeStruct((2, n_tok), I32),
              jax.ShapeDtypeStruct((n_exp, 128), F32)]
    specs = [pl.BlockSpec((2, tm), lambda i: (0, _prev(i))),
             pl.BlockSpec((2, tm), lambda i: (0, _prev(i))),
             pl.BlockSpec((2, tm), lambda i: (0, _prev(i))),
             pl.BlockSpec((n_exp, 128), lambda i: (0, 0))]
    return shapes, specs


def _prev(i):
    return jnp.maximum(i - 1, 0)


def _tail_norm(res_scr, l1g_ref, l1b_ref, h1_ref, h1p_ref):
    h1 = _layer_norm(res_scr[...], l1g_ref[...], l1b_ref[...])
    h1_ref[...] = h1
    _store_records(h1p_ref, h1)
    return _split_bf16(h1)


def _tail_route(h_split, rw_ref, rb_ref, idx_ref, gate_ref, rank_ref, cnt_ref):
    live = jnp.where(pl.program_id(0) > 0, 1.0, 0.0)
    _route(*h_split, rw_ref, rb_ref, idx_ref, gate_ref, rank_ref, cnt_ref, live)


def _gmlp_kernel(x_ref, win_ref, bin_ref, lng_ref, lnb_ref, ws_ref, bst_ref, wout_ref,
                 l1g_ref, l1b_ref, rw_ref, rb_ref, ew_in_ref, ew_out_ref,
                 h1_ref, h1p_ref, idx_ref, gate_ref, rank_ref, cnt_ref, ew_in_b_ref, ew_out_b_ref,
                 p_scr, res_scr, *, alpha, chunk, groups):
    @pl.when(pl.program_id(0) == 0)
    def _():
        cnt_ref[...] = jnp.zeros_like(cnt_ref)
        res_scr[...] = jnp.zeros_like(res_scr)

    ew_in_b_ref[...] = ew_in_ref[...].astype(BF16)
    ew_out_b_ref[...] = ew_out_ref[...].astype(BF16)

    _tail_route(_tail_norm(res_scr, l1g_ref, l1b_ref, h1_ref, h1p_ref), rw_ref, rb_ref, idx_ref, gate_ref, rank_ref,
                cnt_ref)

    x = x_ref[...]
    xb = x.astype(BF16)
    width = win_ref.shape[1] // 2
    gdim = width // groups
    tm = x.shape[0]

    zv = jax.nn.gelu(_dot(xb, win_ref[:, width:]) + bin_ref[:, width:])
    vn = _layer_norm(zv, lng_ref[...], lnb_ref[...]).astype(BF16)
    zu = jax.nn.gelu(_dot(xb, win_ref[:, :width]) + bin_ref[:, :width])

    r_i = lax.broadcasted_iota(I32, (chunk, chunk), 0)
    c_i = lax.broadcasted_iota(I32, (chunk, chunk), 1)
    causal = r_i >= c_i
    for g in range(groups):
        w = jnp.where(causal, ws_ref[g], 0.0).astype(BF16)
        bias = bst_ref[:, g:g + 1]
        for c in range(tm // chunk):
            rows = slice(c * chunk, (c + 1) * chunk)
            cols = slice(g * gdim, (g + 1) * gdim)
            s = _dot(w, vn[rows, cols]) + bias
            p_scr[rows, cols] = (zu[rows, cols] * s).astype(BF16)

    res_scr[...] = alpha * x + _dot(p_scr[...], wout_ref[...])


def _gmlp_layer(h, w_in, b_in, ln_g, ln_b, w_s, b_s, w_out, l1g, l1b, rw2, rb, expert_weights, *, alpha):
    n_tok, d = h.shape
    groups, chunk, _ = w_s.shape
    width = w_in.shape[1] // 2
    n_exp = rb.shape[0]
    tm = min(TM_MIX, n_tok)
    assert n_tok % tm == 0 and tm % chunk == 0
    r_shapes, r_specs = _router_out(n_tok, tm, n_exp)
    kern = functools.partial(_gmlp_kernel, alpha=alpha, chunk=chunk, groups=groups)
    n_steps = n_tok // tm
    c_in, c_out, c_shape, c_ops = _cast_specs(*expert_weights, n_steps, lambda i: jnp.minimum(i, n_steps - 1))
    return pl.pallas_call(
        kern,
        grid=(n_steps + 1,),
        in_specs=[pl.BlockSpec((tm, d), lambda i: (jnp.minimum(i, n_steps - 1), 0)),
                  _const_spec((d, 2 * width)), _const_spec((1, 2 * width)),
                  _const_spec((1, width)), _const_spec((1, width)),
                  _const_spec((groups, chunk, chunk)), _const_spec((chunk, groups)),
                  _const_spec((width, d)), _const_spec((1, d)), _const_spec((1, d)),
                  _const_spec((d, 256)), _const_spec((n_exp, 1))] + c_in,
        out_specs=[pl.BlockSpec((tm, d), lambda i: (_prev(i), 0)),
                   pl.BlockSpec((tm * RECORD_ROWS, 128), lambda i: (_prev(i), 0))] + r_specs + c_out,
        out_shape=[jax.ShapeDtypeStruct((n_tok, d), F32),
                   jax.ShapeDtypeStruct((n_tok * RECORD_ROWS, 128), U32)] + r_shapes + c_shape,
        scratch_shapes=[pltpu.VMEM((tm, width), BF16), pltpu.VMEM((tm, d), F32)],
        compiler_params=pltpu.CompilerParams(dimension_semantics=("arbitrary",),
                                             vmem_limit_bytes=V7X_VMEM_LIMIT_BYTES),
        name="gmlp_mixer",
    )(h, w_in.astype(BF16), b_in.reshape(1, -1), ln_g.reshape(1, -1), ln_b.reshape(1, -1),
      w_s, b_s.T, w_out.astype(BF16), l1g.reshape(1, -1), l1b.reshape(1, -1), rw2, rb, *c_ops)


SSM_LANE_GROUPS = 8
SSM_BATCH_PER_STEP = 4


def _proj_kernel(x_ref, w_ref, o_ref, res_scr, *, chunk):
    res = _dot(x_ref[...].astype(BF16), w_ref[...])
    n_slab = res_scr.shape[0]
    kb = o_ref.shape[0]
    for j in range(n_slab):
        res_scr[j] = res[:, j * 128:(j + 1) * 128]
    for t in range(chunk):
        for j in range(n_slab):
            o_ref[:, (t * n_slab + j) * 128:(t * n_slab + j + 1) * 128] = (
                res_scr[j, pl.ds(t, kb, stride=chunk), :].astype(o_ref.dtype))


def _ssm_in_proj(h, w, *, chunk):
    n_tok, d = h.shape
    width = w.shape[1]
    tm = min(TM_PROJ, n_tok)
    assert n_tok % tm == 0 and tm % chunk == 0 and width % 128 == 0
    kern = functools.partial(_proj_kernel, chunk=chunk)
    return pl.pallas_call(
        kern,
        grid=(n_tok // tm,),
        in_specs=[pl.BlockSpec((tm, d), lambda i: (i, 0)), _const_spec((d, width))],
        out_specs=pl.BlockSpec((tm // chunk, chunk * width), lambda i: (i, 0)),
        out_shape=jax.ShapeDtypeStruct((n_tok // chunk, chunk * width), BF16),
        scratch_shapes=[pltpu.VMEM((width // 128, tm, 128), F32)],
        compiler_params=pltpu.CompilerParams(dimension_semantics=("arbitrary",),
                                             vmem_limit_bytes=V7X_VMEM_LIMIT_BYTES),
        name="ssm_in_proj",
    )(h, w.astype(BF16))


def _ssm_kernel(*refs, chunk, n_batch, n_chunks):
    u_refs = refs[:chunk]
    d_ref, cin_ref, sout_ref, coef_ref, y_ref, t_ref, sin_ref, w_scr, x_scr, ytok_scr = refs[chunk:]
    bblk = pl.program_id(1)
    rows = n_batch * n_chunks
    n_x = x_scr.shape[0]

    @pl.when(bblk == 0)
    def _():
        for tp in range(chunk):
            for t in range(chunk):
                blk = d_ref[t - tp] if t >= tp else jnp.zeros((128, 128), BF16)
                t_ref[tp * 128:(tp + 1) * 128, t * 128:(t + 1) * 128] = blk
        ch_bits = int(math.log2(128 // SSM_LANE_GROUPS))
        row = lax.broadcasted_iota(I32, (chunk * 128, 128), 0)
        row_group = lax.shift_right_logical(row, ch_bits) & (SSM_LANE_GROUPS - 1)
        for part in range(2 * n_x // SSM_LANE_GROUPS):
            src = cin_ref[:, part * 128:(part + 1) * 128]
            for g in range(SSM_LANE_GROUPS):
                col = (part * SSM_LANE_GROUPS + g) * 128
                sin_ref[:, col:col + 128] = jnp.where(row_group == g, src, jnp.zeros_like(src))

    pitch = w_scr.shape[1] // n_batch

    u = jnp.concatenate([r[...] for r in u_refs], axis=1)
    for jb in range(n_x):
        w = _dot(u, sin_ref[:, jb * 256:(jb + 1) * 256])
        for b in range(n_batch):
            seq = slice(b * n_chunks, (b + 1) * n_chunks)
            w_scr[2 * jb, b * pitch:b * pitch + n_chunks, :] = w[seq, :128]
            w_scr[2 * jb + 1, b * pitch:b * pitch + n_chunks, :] = w[seq, 128:]

    ar = [jnp.broadcast_to(coef_ref[0:1, j * 128:(j + 1) * 128], (n_batch, 128)) for j in range(n_x)]
    ai_a = [jnp.broadcast_to(coef_ref[1:2, j * 128:(j + 1) * 128], (n_batch, 128)) for j in range(n_x)]
    ai_b = [jnp.broadcast_to(coef_ref[2:3, j * 128:(j + 1) * 128], (n_batch, 128)) for j in range(n_x)]

    def step(k, carry):
        x, xsw = carry
        rows_k = pl.ds(k, n_batch, stride=pitch)
        new_x, new_xsw = [], []
        for j in range(n_x):
            x_scr[j, rows_k, :] = x[j]
            new_x.append(ar[j] * x[j] + ai_a[j] * xsw[j] + w_scr[j, rows_k, :])
            new_xsw.append(ar[j] * xsw[j] + ai_b[j] * x[j] + w_scr[n_x + j, rows_k, :])
        return tuple(new_x), tuple(new_xsw)

    zero = tuple(jnp.zeros((n_batch, 128), F32) for _ in range(n_x))
    lax.fori_loop(0, n_chunks, step, (zero, zero))

    xs = jnp.concatenate(
        [jnp.concatenate([x_scr[j, b * pitch:b * pitch + n_chunks, :] for b in range(n_batch)], axis=0)
         for j in range(n_x)], axis=1).astype(BF16)
    for jb in range(chunk // 2):
        kk = (jb + 1) * 256
        cols = slice(jb * 256, (jb + 1) * 256)
        y = jax.nn.gelu(_dot(u[:, :kk], t_ref[:kk, cols]) + _dot(xs, sout_ref[:, cols]))
        for half in range(2):
            ytok_scr[pl.ds(2 * jb + half, rows, stride=chunk), :] = y[:, half * 128:(half + 1) * 128]
    y_ref[...] = ytok_scr[...].astype(y_ref.dtype)


def _ssm_operators(a_re, a_im, log_dt, b_re, b_im, c_re, c_im, d_skip):
    hp = lax.Precision.HIGHEST
    n_grp, n_state, n_ch = b_re.shape
    L = SSM_CHUNK
    nb = SSM_LANE_GROUPS
    assert n_ch * nb == 128 and 2 * n_state == 128 and n_grp % nb == 0
    n_gb = n_grp // nb
    a = lax.complex(a_re.astype(F32), a_im.astype(F32))
    lam = a * jnp.exp(log_dt.astype(F32))[:, None]
    a_bar = jnp.exp(lam)
    b_bar = ((a_bar - 1.0) / a)[..., None] * lax.complex(b_re.astype(F32), b_im.astype(F32))
    cc = lax.complex(c_re.astype(F32), c_im.astype(F32))
    steps = jnp.arange(L + 1, dtype=F32)
    apow = jnp.exp(steps[:, None, None] * lam[None])
    lane = jnp.arange(128)
    put = ((lane[None, None, :] // n_ch == jnp.arange(nb)[:, None, None])
           & (lane[None, None, :] % n_ch == jnp.arange(n_ch)[None, :, None])).astype(F32)

    kern = jnp.real(jnp.einsum('gcp,lgp,gpd->glcd', cc, apow[:L], b_bar, precision=hp))
    d_op = jnp.einsum('Bgtcd,gdr,gcl->Btrl', kern.reshape(n_gb, nb, L, n_ch, n_ch), put, put, precision=hp)
    skip = jnp.eye(128, dtype=F32)[None] * d_skip.astype(F32).reshape(n_gb, 1, 128)
    d_op = d_op.at[:, 0].add(skip).astype(BF16)

    s_in = apow[:L][::-1].transpose(1, 0, 2)[:, :, None, :] * b_bar.transpose(0, 2, 1)[:, None, :, :]
    s_in = jnp.concatenate([s_in.real, s_in.imag, s_in.imag, s_in.real], axis=-1).astype(BF16)
    c_in = s_in.reshape(n_gb, nb, L, n_ch, 4 * n_state).transpose(0, 2, 1, 3, 4).reshape(n_gb, L * 128, 4 * n_state)

    s_out = cc.transpose(0, 2, 1)[:, :, None, :] * apow[1:].transpose(1, 2, 0)[:, :, :, None]
    s_out = jnp.concatenate([s_out.real, -s_out.imag], axis=1)
    x_idx = jnp.arange(L * n_ch)
    l_idx = jnp.arange(L * 128)
    place = ((x_idx[:, None] // n_ch == l_idx[None, :] // 128) & (x_idx[:, None] % n_ch == l_idx[None, :] % n_ch))
    place = place[None] & ((l_idx[None, None, :] // n_ch) % nb == jnp.arange(nb)[:, None, None])
    s_out = jnp.einsum('Bgqx,gxl->Bgql', s_out.reshape(n_gb, nb, 2 * n_state, L * n_ch), place.astype(F32),
                       precision=hp).astype(BF16).reshape(n_gb, nb * 2 * n_state, L * 128)

    a_l = apow[L]
    coef = jnp.stack([jnp.concatenate([a_l.real, a_l.real], -1),
                      jnp.concatenate([-a_l.imag, a_l.imag], -1),
                      jnp.concatenate([a_l.imag, -a_l.imag], -1)], axis=1)
    coef = coef.reshape(n_gb, nb, 3, 2 * n_state).transpose(0, 2, 1, 3).reshape(n_gb, 3, nb * 2 * n_state)
    coef = jnp.pad(coef, ((0, 0), (0, 5), (0, 0)))
    return d_op, c_in, s_out, coef


def _ssm_scan(u, d_op, c_in, s_out, coef, *, n_batch, chunk):
    rows_all, lanes_all = u.shape
    n_blk, lc, _ = c_in.shape
    n_chunks = rows_all // n_batch
    nb = min(SSM_BATCH_PER_STEP, n_batch)
    assert n_batch % nb == 0 and lanes_all == chunk * n_blk * 128
    rows = nb * n_chunks
    pitch = n_chunks + 4 if n_chunks % 8 == 0 else n_chunks
    n_x = s_out.shape[1] // 128
    kern = functools.partial(_ssm_kernel, chunk=chunk, n_batch=nb, n_chunks=n_chunks)
    u_specs = [pl.BlockSpec((rows, 128), lambda g, b, t=t: (b, t * n_blk + g)) for t in range(chunk)]

    def op_spec(shape):
        return pl.BlockSpec((None,) + shape, lambda g, b: (g,) + (0,) * len(shape))

    return pl.pallas_call(
        kern,
        grid=(n_blk, n_batch // nb),
        in_specs=u_specs + [op_spec((chunk, 128, 128)), op_spec((lc, c_in.shape[2])),
                            op_spec((n_x * 128, lc)), op_spec((8, n_x * 128))],
        out_specs=pl.BlockSpec((rows * chunk, 128), lambda g, b: (b, g)),
        out_shape=jax.ShapeDtypeStruct((rows_all * chunk, n_blk * 128), BF16),
        scratch_shapes=[pltpu.VMEM((lc, lc), BF16), pltpu.VMEM((lc, 2 * n_x * 128), BF16),
                        pltpu.VMEM((2 * n_x, nb * pitch, 128), F32), pltpu.VMEM((n_x, nb * pitch, 128), F32),
                        pltpu.VMEM((rows * chunk, 128), F32)],
        compiler_params=pltpu.CompilerParams(dimension_semantics=("arbitrary", "arbitrary"),
                                             vmem_limit_bytes=V7X_VMEM_LIMIT_BYTES),
        name="ssm_scan",
    )(*([u] * chunk), d_op, c_in, s_out, coef)


def _ssm_out_kernel(y_ref, h_ref, wout_ref, l1g_ref, l1b_ref, rw_ref, rb_ref,
                    h1_ref, h1p_ref, idx_ref, gate_ref, rank_ref, cnt_ref, res_scr, *, alpha):
    @pl.when(pl.program_id(0) == 0)
    def _():
        cnt_ref[...] = jnp.zeros_like(cnt_ref)
        res_scr[...] = jnp.zeros_like(res_scr)

    h_split = _tail_norm(res_scr, l1g_ref, l1b_ref, h1_ref, h1p_ref)

    y = y_ref[...]
    d = h_ref.shape[1]
    val = _dot(y, wout_ref[:, :d])
    gate = _dot(y, wout_ref[:, d:])
    res_scr[...] = alpha * h_ref[...] + val * jax.nn.sigmoid(gate)
    _tail_route(h_split, rw_ref, rb_ref, idx_ref, gate_ref, rank_ref, cnt_ref)


def _ssm_layer(h, n_batch, w_in, a_re, a_im, log_dt, b_re, b_im, c_re, c_im, d_skip, w_out,
               l1g, l1b, rw2, rb, *, alpha):
    n_tok, d = h.shape
    n_grp, n_state, n_ch = b_re.shape
    width = n_grp * n_ch
    n_exp = rb.shape[0]
    L = SSM_CHUNK
    assert (n_tok // n_batch) % L == 0

    u = _ssm_in_proj(h, w_in, chunk=L)
    d_op, c_in, s_out, coef = _ssm_operators(a_re, a_im, log_dt, b_re, b_im, c_re, c_im, d_skip)
    y = _ssm_scan(u, d_op, c_in, s_out, coef, n_batch=n_batch, chunk=L)

    tm = min(TM_SSM_OUT, n_tok)
    assert n_tok % tm == 0
    r_shapes, r_specs = _router_out(n_tok, tm, n_exp)
    kern = functools.partial(_ssm_out_kernel, alpha=alpha)
    n_steps = n_tok // tm
    return pl.pallas_call(
        kern,
        grid=(n_steps + 1,),
        in_specs=[pl.BlockSpec((tm, width), lambda i: (jnp.minimum(i, n_steps - 1), 0)),
                  pl.BlockSpec((tm, d), lambda i: (jnp.minimum(i, n_steps - 1), 0)),
                  _const_spec((width, 2 * d)), _const_spec((1, d)), _const_spec((1, d)),
                  _const_spec((d, 256)), _const_spec((n_exp, 1))],
        out_specs=[pl.BlockSpec((tm, d), lambda i: (_prev(i), 0)),
                   pl.BlockSpec((tm * RECORD_ROWS, 128), lambda i: (_prev(i), 0))] + r_specs,
        out_shape=[jax.ShapeDtypeStruct((n_tok, d), F32),
                   jax.ShapeDtypeStruct((n_tok * RECORD_ROWS, 128), U32)] + r_shapes,
        scratch_shapes=[pltpu.VMEM((tm, d), F32)],
        compiler_params=pltpu.CompilerParams(dimension_semantics=("arbitrary",),
                                             vmem_limit_bytes=V7X_VMEM_LIMIT_BYTES),
        name="ssm_out_mixer",
    )(y, h, w_out.astype(BF16), l1g.reshape(1, -1), l1b.reshape(1, -1), rw2, rb)


def _moe_plan(idx, rank, cnt, tme, n_tiles):
    n_exp = cnt.shape[0]
    counts = cnt[:, 0].astype(I32)
    padded = ((counts + tme - 1) // tme) * tme
    ends = jnp.cumsum(padded)
    offs = ends - padded
    n_valid = ends[-1] // tme
    tile_start = jnp.arange(n_tiles, dtype=I32) * tme
    tile_expert = jnp.minimum(jnp.sum(tile_start[:, None] >= ends[None, :], axis=1), n_exp - 1).astype(I32)
    onehot = idx[:, :, None] == jnp.arange(n_exp, dtype=I32)[None, None, :]
    pos = jnp.sum(jnp.where(onehot, offs[None, None, :], 0), axis=-1) + rank
    meta = jnp.concatenate([counts, padded, offs, n_valid[None]]).astype(I32)
    return pos, tile_expert, meta


def _tile_pos(pos, tm):
    n_tok = pos.shape[1]
    return pos.reshape(2, n_tok // tm, tm).transpose(1, 0, 2).reshape(n_tok // tm, 1, 2 * tm)


def _record_slice(ref, row, n_rec=1):
    return ref.at[pl.ds(pl.multiple_of(row * RECORD_ROWS, RECORD_ROWS), n_rec * RECORD_ROWS), :]


def _dispatch_kernel(meta_ref, pos_ref, x_ref, xs_ref, zero_scr, sem, *, n_exp, tme, n_tiles):
    i = pl.program_id(0)
    tm = x_ref.shape[0] // RECORD_ROWS
    half = zero_scr.shape[0] // RECORD_ROWS

    def zero_fill(start, size):
        cp = pltpu.make_async_copy(_record_slice(zero_scr, 0, size), _record_slice(xs_ref, start, size), sem)
        cp.start()
        cp.wait()

    @pl.when(i == 0)
    def _():
        zero_scr[...] = jnp.zeros_like(zero_scr)
        for e in range(n_exp):
            pad = meta_ref[n_exp + e] - meta_ref[e]
            base = meta_ref[2 * n_exp + e] + meta_ref[e]
            for b in range(int(math.log2(tme))):
                size = 1 << b

                @pl.when(((pad >> b) & 1) == 1)
                def _():
                    zero_fill(base + (pad & (size - 1)), size)
        n_valid = meta_ref[3 * n_exp]
        for j in range(n_exp):
            @pl.when(n_valid + j < n_tiles)
            def _():
                for part in range(tme // half):
                    zero_fill((n_valid + j) * tme + part * half, half)

    def start(r, c):
        for slot in range(2):
            p = pos_ref[0, 0, slot * tm + r]
            pltpu.make_async_copy(_record_slice(x_ref, r), _record_slice(xs_ref, p), sem).start(priority=slot)
        return c

    lax.fori_loop(0, tm, start, 0, unroll=8)
    for slot in range(2):
        pltpu.make_async_copy(x_ref, _record_slice(xs_ref, 0, tm), sem).wait()


def _cast_specs(w_in, w_out, layer, n_steps, index):
    n_layers, n_exp, d, ff2 = w_in.shape
    ff = w_out.shape[2]
    assert (n_exp * d) % n_steps == 0 and (n_exp * ff) % n_steps == 0
    rin, rout = n_exp * d // n_steps, n_exp * ff // n_steps
    in_specs = [pl.BlockSpec((None, rin, ff2), lambda i, *_: (layer, index(i), 0)),
                pl.BlockSpec((None, rout, d), lambda i, *_: (layer, index(i), 0))]
    out_specs = [pl.BlockSpec((rin, ff2), lambda i, *_: (index(i), 0)),
                 pl.BlockSpec((rout, d), lambda i, *_: (index(i), 0))]
    out_shape = [jax.ShapeDtypeStruct((n_exp * d, ff2), BF16), jax.ShapeDtypeStruct((n_exp * ff, d), BF16)]
    operands = (w_in.reshape(n_layers, n_exp * d, ff2), w_out.reshape(n_layers, n_exp * ff, d))
    return in_specs, out_specs, out_shape, operands


def _moe_dispatch(h1p, pos, meta, *, n_exp, tme, n_tiles):
    n_tok = h1p.shape[0] // RECORD_ROWS
    tm = min(TM_DISPATCH, n_tok)
    assert n_tok % tm == 0
    half = max(tme // 2, 1)
    kern = functools.partial(_dispatch_kernel, n_exp=n_exp, tme=tme, n_tiles=n_tiles)
    return pl.pallas_call(
        kern,
        grid_spec=pltpu.PrefetchScalarGridSpec(
            num_scalar_prefetch=1,
            grid=(n_tok // tm,),
            in_specs=[pl.BlockSpec((1, 1, 2 * tm), lambda i, m: (i, 0, 0), memory_space=pltpu.SMEM),
                      pl.BlockSpec((tm * RECORD_ROWS, 128), lambda i, m: (i, 0))],
            out_specs=pl.BlockSpec(memory_space=pl.ANY),
            scratch_shapes=[pltpu.VMEM((half * RECORD_ROWS, 128), U32), pltpu.SemaphoreType.DMA]),
        out_shape=jax.ShapeDtypeStruct((n_tiles * tme * RECORD_ROWS, 128), U32),
        compiler_params=pltpu.CompilerParams(dimension_semantics=("arbitrary",),
                                             vmem_limit_bytes=V7X_VMEM_LIMIT_BYTES),
        name="moe_dispatch",
    )(meta, _tile_pos(pos, tm), h1p)


def _expert_kernel(te_ref, meta_ref, xs_ref, win_ref, wout_ref, *refs, n_exp, cast):
    if cast:
        nwin_ref, nwout_ref, ys_ref, nwin_b_ref, nwout_b_ref = refs
        nwin_b_ref[...] = nwin_ref[...].astype(BF16)
        nwout_b_ref[...] = nwout_ref[...].astype(BF16)
    else:
        (ys_ref,) = refs
    i = pl.program_id(0)
    n_valid = meta_ref[3 * n_exp]
    tme = xs_ref.shape[0] // RECORD_ROWS

    @pl.when(i < n_valid)
    def _():
        hi, lo = _load_records(xs_ref, tme)
        dh = hi.shape[1]
        ff = wout_ref.shape[0]
        h = _dot(hi.astype(BF16), win_ref[:dh, :]) + _dot(lo.astype(BF16), win_ref[dh:, :])
        act = (jax.nn.silu(h[:, :ff]) * h[:, ff:]).astype(BF16)
        _store_records(ys_ref, _dot(act, wout_ref[...]))

    @pl.when(i >= n_valid)
    def _():
        ys_ref[...] = jnp.zeros_like(ys_ref)


def _moe_experts(xs, tile_expert, meta, w_in, w_out, cast_weights, *, tme):
    n_rows = xs.shape[0] // RECORD_ROWS
    n_exp, d, ff2 = w_in.shape
    ff = w_out.shape[1]
    n_tiles = n_rows // tme

    def x_map(i, te, m):
        return (jnp.minimum(i, m[3 * n_exp] - 1), 0)

    cast = cast_weights is not None
    c_in, c_out, c_shape, c_ops = ([], [], [], ())
    if cast:
        n_cast = 1 << (n_tiles.bit_length() - 1)
        c_in, c_out, c_shape, c_ops = _cast_specs(*cast_weights, n_cast, lambda i: jnp.minimum(i, n_cast - 1))
    kern = functools.partial(_expert_kernel, n_exp=n_exp, cast=cast)
    return pl.pallas_call(
        kern,
        grid_spec=pltpu.PrefetchScalarGridSpec(
            num_scalar_prefetch=2,
            grid=(n_tiles,),
            in_specs=[pl.BlockSpec((tme * RECORD_ROWS, 128), x_map),
                      pl.BlockSpec((None, d, ff2), lambda i, te, m: (te[i], 0, 0)),
                      pl.BlockSpec((None, ff, d), lambda i, te, m: (te[i], 0, 0))] + c_in,
            out_specs=[pl.BlockSpec((tme * RECORD_ROWS, 128), lambda i, te, m: (i, 0))] + c_out),
        out_shape=[jax.ShapeDtypeStruct((n_rows * RECORD_ROWS, 128), U32)] + c_shape,
        compiler_params=pltpu.CompilerParams(dimension_semantics=("arbitrary",),
                                             vmem_limit_bytes=V7X_VMEM_LIMIT_BYTES),
        name="moe_experts",
    )(tile_expert, meta, xs, w_in, w_out, *c_ops)


def _combine_kernel(pos_ref, pos_next_ref, h1_ref, gate_ref, ys_ref, g_ref, b_ref, out_ref, buf, sems, *, alpha):
    i = pl.program_id(0)
    n_steps = pl.num_programs(0)
    tm = h1_ref.shape[0]

    def gather(p_ref, half):
        def start(r, c):
            for slot in range(2):
                p = p_ref[0, 0, slot * tm + r]
                pltpu.make_async_copy(_record_slice(ys_ref, p), _record_slice(buf.at[half, slot], r),
                                      sems.at[half]).start(priority=slot)
            return c

        lax.fori_loop(0, tm, start, 0, unroll=8)

    @pl.when(i == 0)
    def _():
        gather(pos_ref, 0)

    @pl.when(i + 1 < n_steps)
    def _():
        gather(pos_next_ref, (i + 1) % 2)

    cur = i % 2
    moe = None
    for slot in range(2):
        pltpu.make_async_copy(_record_slice(ys_ref, 0, tm), buf.at[cur, slot], sems.at[cur]).wait()
    for slot in range(2):
        hi, lo = _load_records(buf.at[cur, slot], tm)
        y = jnp.concatenate([hi, lo], axis=1) * gate_ref[:, slot:slot + 1]
        moe = y if moe is None else moe + y
    out_ref[...] = _layer_norm(alpha * h1_ref[...] + moe, g_ref[...], b_ref[...])


def _moe_combine(h1, ys, pos, gates, l2g, l2b, *, alpha):
    n_tok, d = h1.shape
    tm = min(TM_COMBINE, n_tok)
    assert n_tok % tm == 0
    n_steps = n_tok // tm
    kern = functools.partial(_combine_kernel, alpha=alpha)
    pos_tiles = _tile_pos(pos, tm)
    return pl.pallas_call(
        kern,
        grid=(n_steps,),
        in_specs=[pl.BlockSpec((1, 1, 2 * tm), lambda i: (i, 0, 0), memory_space=pltpu.SMEM),
                  pl.BlockSpec((1, 1, 2 * tm), lambda i: (jnp.minimum(i + 1, n_steps - 1), 0, 0),
                               memory_space=pltpu.SMEM),
                  pl.BlockSpec((tm, d), lambda i: (i, 0)),
                  pl.BlockSpec((tm, 2), lambda i: (i, 0)),
                  pl.BlockSpec(memory_space=pl.ANY),
                  _const_spec((1, d)), _const_spec((1, d))],
        out_specs=pl.BlockSpec((tm, d), lambda i: (i, 0)),
        out_shape=jax.ShapeDtypeStruct((n_tok, d), F32),
        scratch_shapes=[pltpu.VMEM((2, 2, tm * RECORD_ROWS, 128), U32), pltpu.SemaphoreType.DMA((2,))],
        compiler_params=pltpu.CompilerParams(dimension_semantics=("arbitrary",),
                                             vmem_limit_bytes=V7X_VMEM_LIMIT_BYTES),
        name="moe_combine",
    )(pos_tiles, pos_tiles, h1, gates.T, ys, l2g.reshape(1, -1), l2b.reshape(1, -1))


def _moe_layer(h1, h1p, idx, gates, rank, cnt, w_in_b, w_out_b, cast_next, l2g, l2b, *, alpha):
    n_tok, d = h1.shape
    n_exp = cnt.shape[0]
    ff = w_out_b.shape[0] // n_exp
    tme = min(TM_EXPERT, n_tok)
    n_tiles = (2 * n_tok) // tme + n_exp
    pos, tile_expert, meta = _moe_plan(idx, rank, cnt, tme, n_tiles)
    xs = _moe_dispatch(h1p, pos, meta, n_exp=n_exp, tme=tme, n_tiles=n_tiles)
    ys, *cast = _moe_experts(xs, tile_expert, meta, w_in_b.reshape(n_exp, d, 2 * ff), w_out_b.reshape(n_exp, ff, d),
                             cast_next, tme=tme)
    return _moe_combine(h1, ys, pos, gates, l2g, l2b, alpha=alpha), cast


def kernel(x, a_w_in, a_b_in, a_ln_g, a_ln_b, a_w_s, a_b_s, a_w_out, b_w_in, b_a_re, b_a_im, b_log_dt, b_b_re, b_b_im, b_c_re, b_c_im, b_d, b_w_out, ln1_g, ln1_b, router_w, router_b, e_w_in, e_w_out, ln2_g, ln2_b):
    n_batch, seq, d = x.shape
    depth = ln1_g.shape[0]
    alpha = float((2 * depth) ** 0.25)
    n_exp = router_w.shape[1]
    rw = jnp.pad(router_w.astype(F32), ((0, 0), (0, 128 - n_exp)))
    rw_hi = lax.reduce_precision(rw, exponent_bits=8, mantissa_bits=7)
    rw2 = jnp.concatenate([rw_hi, rw - rw_hi], axis=1).astype(BF16)
    rb = router_b.astype(F32).reshape(-1, 1)
    h = x.reshape(n_batch * seq, d)
    e_bf16 = None
    for i in range(depth):
        j = i // 2
        if i % 2 == 0:
            *mixed, e_in_b, e_out_b = _gmlp_layer(h, a_w_in[j], a_b_in[j], a_ln_g[j], a_ln_b[j], a_w_s[j], a_b_s[j],
                                                  a_w_out[j], ln1_g[i], ln1_b[i], rw2, rb, (e_w_in, e_w_out, i),
                                                  alpha=alpha)
            e_bf16 = [e_in_b, e_out_b]
        else:
            mixed = _ssm_layer(h, n_batch, b_w_in[j], b_a_re[j], b_a_im[j], b_log_dt[j], b_b_re[j],
                               b_b_im[j], b_c_re[j], b_c_im[j], b_d[j], b_w_out[j],
                               ln1_g[i], ln1_b[i], rw2, rb, alpha=alpha)
        cast_next = (e_w_in, e_w_out, i + 1) if (i % 2 == 0 and i + 1 < depth) else None
        h, e_bf16 = _moe_layer(*mixed, *e_bf16, cast_next, ln2_g[i], ln2_b[i], alpha=alpha)
    return h.reshape(n_batch, seq, d)
```

```python
import functools
import math

import jax
import jax.numpy as jnp
from jax import lax
from jax.experimental import pallas as pl
from jax.experimental.pallas import tpu as pltpu

F32 = jnp.float32
BF16 = jnp.bfloat16
I32 = jnp.int32
U32 = jnp.uint32

LN_EPS = 1e-5
N_EXPERT_GROUPS = 4
SSM_CHUNK = 16

TM_MIX = 256
TM_SSM_OUT = 512
TM_PROJ = 1024
TM_DISPATCH = 1024
TM_EXPERT = 512
TM_COMBINE = 256

V7X_VMEM_LIMIT_BYTES = 56 * 1024 * 1024


def _dot(a, b):
    return jnp.dot(a, b, preferred_element_type=F32)


def _layer_norm(x, g, b):
    mu = jnp.mean(x, axis=-1, keepdims=True)
    xc = x - mu
    var = jnp.sum(xc * xc, axis=-1, keepdims=True) * (1.0 / x.shape[-1])
    return xc * lax.rsqrt(var + LN_EPS) * g + b


def _const_spec(shape):
    return pl.BlockSpec(shape, lambda *_: (0,) * len(shape), pipeline_mode=pl.Buffered(1))


def _split_bf16(h1):
    h_top = lax.bitcast_convert_type(lax.bitcast_convert_type(h1, U32) & jnp.uint32(0xFFFF0000), F32)
    return h_top.astype(BF16), (h1 - h_top).astype(BF16)


def _route(h_hi, h_mid, rw_ref, rb_ref, idx_ref, gate_ref, rank_ref, cnt_ref, live):
    n_exp = rb_ref.shape[0]
    per_group = n_exp // N_EXPERT_GROUPS
    tm = h_hi.shape[0]
    first = _dot(h_hi, rw_ref[...])
    logits_t = first[:, :128] + first[:, 128:] + _dot(h_mid, rw_ref[:, :128])
    logits = logits_t.T[:n_exp, :] + rb_ref[...]
    m = jnp.max(logits, axis=0, keepdims=True)
    ex = jnp.exp(logits - m)
    probs = ex / jnp.sum(ex, axis=0, keepdims=True)

    best_score = None
    best_group = None
    for g in range(N_EXPERT_GROUPS):
        t1 = t2 = None
        for e in range(g * per_group, (g + 1) * per_group):
            v = probs[e:e + 1, :]
            if t1 is None:
                t1 = v
            elif t2 is None:
                t1, t2 = jnp.maximum(t1, v), jnp.minimum(t1, v)
            else:
                t2 = jnp.maximum(t2, jnp.minimum(t1, v))
                t1 = jnp.maximum(t1, v)
        score = t1 + t2
        if best_score is None:
            best_score, best_group = score, jnp.zeros((1, tm), I32)
        else:
            better = score > best_score
            best_group = jnp.where(better, g, best_group)
            best_score = jnp.where(better, score, best_score)

    eidx = lax.broadcasted_iota(I32, (n_exp, tm), 0)
    egrp = jnp.zeros((n_exp, tm), I32)
    for g in range(1, N_EXPERT_GROUPS):
        egrp = egrp + jnp.where(eidx >= g * per_group, 1, 0)
    in_group = egrp == best_group
    masked = jnp.where(in_group, probs, -1.0)
    m1 = jnp.max(masked, axis=0, keepdims=True)
    i1 = jnp.min(jnp.where(masked == m1, eidx, n_exp), axis=0, keepdims=True)
    sel1 = eidx == i1
    masked2 = jnp.where(sel1, -2.0, masked)
    m2 = jnp.max(masked2, axis=0, keepdims=True)
    i2 = jnp.min(jnp.where(masked2 == m2, eidx, n_exp), axis=0, keepdims=True)
    sel2 = eidx == i2
    den = m1 + m2

    cnt = jnp.where(sel1 | sel2, live, 0.0)
    r_i = lax.broadcasted_iota(I32, (tm, tm), 0)
    c_i = lax.broadcasted_iota(I32, (tm, tm), 1)
    strict_upper = jnp.where(r_i < c_i, 1.0, 0.0).astype(BF16)
    prefix = _dot(cnt.astype(BF16), strict_upper) + cnt_ref[:, 0:1]
    r1 = jnp.sum(jnp.where(sel1, prefix, 0.0), axis=0, keepdims=True)
    r2 = jnp.sum(jnp.where(sel2, prefix, 0.0), axis=0, keepdims=True)
    cnt_ref[...] += jnp.sum(cnt, axis=1, keepdims=True)

    idx_ref[...] = jnp.concatenate([i1, i2], axis=0)
    gate_ref[...] = jnp.concatenate([m1 / den, m2 / den], axis=0)
    rank_ref[...] = jnp.concatenate([r1, r2], axis=0).astype(I32)


RECORD_ROWS = 8


def _store_records(ref, h):
    tm, d = h.shape
    half = d // 2
    assert half == RECORD_ROWS * 128
    bits = lax.bitcast_convert_type(h.astype(BF16).astype(F32), U32)
    packed = (bits[:, :half] & jnp.uint32(0xFFFF0000)) | (bits[:, half:] >> 16)
    for s in range(RECORD_ROWS):
        ref[pl.ds(s, tm, stride=RECORD_ROWS), :] = packed[:, s * 128:(s + 1) * 128]


def _load_records(ref, tm):
    w = jnp.concatenate([ref[pl.ds(s, tm, stride=RECORD_ROWS), :] for s in range(RECORD_ROWS)], axis=1)
    hi = lax.bitcast_convert_type(w & jnp.uint32(0xFFFF0000), F32)
    lo = lax.bitcast_convert_type(w << 16, F32)
    return hi, lo


def _router_out(n_tok, tm, n_exp):
    shapes = [jax.ShapeDtypeStruct((2, n_tok), I32),
              jax.ShapeDtypeStruct((2, n_tok), F32),
              jax.ShapeDtypeStruct((2, n_tok), I32),
              jax.ShapeDtypeStruct((n_exp, 128), F32)]
    specs = [pl.BlockSpec((2, tm), lambda i: (0, _prev(i))),
             pl.BlockSpec((2, tm), lambda i: (0, _prev(i))),
             pl.BlockSpec((2, tm), lambda i: (0, _prev(i))),
             pl.BlockSpec((n_exp, 128), lambda i: (0, 0))]
    return shapes, specs


def _prev(i):
    return jnp.maximum(i - 1, 0)


def _tail_norm(res_scr, l1g_ref, l1b_ref, h1_ref, h1p_ref):
    h1 = _layer_norm(res_scr[...], l1g_ref[...], l1b_ref[...])
    h1_ref[...] = h1
    _store_records(h1p_ref, h1)
    return _split_bf16(h1)


def _tail_route(h_split, rw_ref, rb_ref, idx_ref, gate_ref, rank_ref, cnt_ref):
    live = jnp.where(pl.program_id(0) > 0, 1.0, 0.0)
    _route(*h_split, rw_ref, rb_ref, idx_ref, gate_ref, rank_ref, cnt_ref, live)


def _gmlp_kernel(x_ref, win_ref, bin_ref, lng_ref, lnb_ref, ws_ref, bst_ref, wout_ref,
                 l1g_ref, l1b_ref, rw_ref, rb_ref, ew_in_ref, ew_out_ref,
                 h1_ref, h1p_ref, idx_ref, gate_ref, rank_ref, cnt_ref, ew_in_b_ref, ew_out_b_ref,
                 p_scr, res_scr, *, alpha, chunk, groups):
    @pl.when(pl.program_id(0) == 0)
    def _():
        cnt_ref[...] = jnp.zeros_like(cnt_ref)
        res_scr[...] = jnp.zeros_like(res_scr)

    ew_in_b_ref[...] = ew_in_ref[...].astype(BF16)
    ew_out_b_ref[...] = ew_out_ref[...].astype(BF16)

    _tail_route(_tail_norm(res_scr, l1g_ref, l1b_ref, h1_ref, h1p_ref), rw_ref, rb_ref, idx_ref, gate_ref, rank_ref,
                cnt_ref)

    x = x_ref[...]
    xb = x.astype(BF16)
    width = win_ref.shape[1] // 2
    gdim = width // groups
    tm = x.shape[0]

    zv = jax.nn.gelu(_dot(xb, win_ref[:, width:]) + bin_ref[:, width:])
    vn = _layer_norm(zv, lng_ref[...], lnb_ref[...]).astype(BF16)
    zu = jax.nn.gelu(_dot(xb, win_ref[:, :width]) + bin_ref[:, :width])

    r_i = lax.broadcasted_iota(I32, (chunk, chunk), 0)
    c_i = lax.broadcasted_iota(I32, (chunk, chunk), 1)
    causal = r_i >= c_i
    for g in range(groups):
        w = jnp.where(causal, ws_ref[g], 0.0).astype(BF16)
        bias = bst_ref[:, g:g + 1]
        for c in range(tm // chunk):
            rows = slice(c * chunk, (c + 1) * chunk)
            cols = slice(g * gdim, (g + 1) * gdim)
            s = _dot(w, vn[rows, cols]) + bias
            p_scr[rows, cols] = (zu[rows, cols] * s).astype(BF16)

    res_scr[...] = alpha * x + _dot(p_scr[...], wout_ref[...])


def _gmlp_layer(h, w_in, b_in, ln_g, ln_b, w_s, b_s, w_out, l1g, l1b, rw2, rb, expert_weights, *, alpha):
    n_tok, d = h.shape
    groups, chunk, _ = w_s.shape
    width = w_in.shape[1] // 2
    n_exp = rb.shape[0]
    tm = min(TM_MIX, n_tok)
    assert n_tok % tm == 0 and tm % chunk == 0
    r_shapes, r_specs = _router_out(n_tok, tm, n_exp)
    kern = functools.partial(_gmlp_kernel, alpha=alpha, chunk=chunk, groups=groups)
    n_steps = n_tok // tm
    c_in, c_out, c_shape, c_ops = _cast_specs(*expert_weights, n_steps, lambda i: jnp.minimum(i, n_steps - 1))
    return pl.pallas_call(
        kern,
        grid=(n_steps + 1,),
        in_specs=[pl.BlockSpec((tm, d), lambda i: (jnp.minimum(i, n_steps - 1), 0)),
                  _const_spec((d, 2 * width)), _const_spec((1, 2 * width)),
                  _const_spec((1, width)), _const_spec((1, width)),
                  _const_spec((groups, chunk, chunk)), _const_spec((chunk, groups)),
                  _const_spec((width, d)), _const_spec((1, d)), _const_spec((1, d)),
                  _const_spec((d, 256)), _const_spec((n_exp, 1))] + c_in,
        out_specs=[pl.BlockSpec((tm, d), lambda i: (_prev(i), 0)),
                   pl.BlockSpec((tm * RECORD_ROWS, 128), lambda i: (_prev(i), 0))] + r_specs + c_out,
        out_shape=[jax.ShapeDtypeStruct((n_tok, d), F32),
                   jax.ShapeDtypeStruct((n_tok * RECORD_ROWS, 128), U32)] + r_shapes + c_shape,
        scratch_shapes=[pltpu.VMEM((tm, width), BF16), pltpu.VMEM((tm, d), F32)],
        compiler_params=pltpu.CompilerParams(dimension_semantics=("arbitrary",),
                                             vmem_limit_bytes=V7X_VMEM_LIMIT_BYTES),
        name="gmlp_mixer",
    )(h, w_in.astype(BF16), b_in.reshape(1, -1), ln_g.reshape(1, -1), ln_b.reshape(1, -1),
      w_s, b_s.T, w_out.astype(BF16), l1g.reshape(1, -1), l1b.reshape(1, -1), rw2, rb, *c_ops)


SSM_LANE_GROUPS = 8
SSM_BATCH_PER_STEP = 4


def _proj_kernel(x_ref, w_ref, o_ref, res_scr, *, chunk):
    res = _dot(x_ref[...].astype(BF16), w_ref[...])
    n_slab = res_scr.shape[0]
    kb = o_ref.shape[0]
    for j in range(n_slab):
        res_scr[j] = res[:, j * 128:(j + 1) * 128]
    for t in range(chunk):
        for j in range(n_slab):
            o_ref[:, (t * n_slab + j) * 128:(t * n_slab + j + 1) * 128] = (
                res_scr[j, pl.ds(t, kb, stride=chunk), :].astype(o_ref.dtype))


def _ssm_in_proj(h, w, *, chunk):
    n_tok, d = h.shape
    width = w.shape[1]
    tm = min(TM_PROJ, n_tok)
    assert n_tok % tm == 0 and tm % chunk == 0 and width % 128 == 0
    kern = functools.partial(_proj_kernel, chunk=chunk)
    return pl.pallas_call(
        kern,
        grid=(n_tok // tm,),
        in_specs=[pl.BlockSpec((tm, d), lambda i: (i, 0)), _const_spec((d, width))],
        out_specs=pl.BlockSpec((tm // chunk, chunk * width), lambda i: (i, 0)),
        out_shape=jax.ShapeDtypeStruct((n_tok // chunk, chunk * width), BF16),
        scratch_shapes=[pltpu.VMEM((width // 128, tm, 128), F32)],
        compiler_params=pltpu.CompilerParams(dimension_semantics=("arbitrary",),
                                             vmem_limit_bytes=V7X_VMEM_LIMIT_BYTES),
        name="ssm_in_proj",
    )(h, w.astype(BF16))


def _ssm_kernel(*refs, chunk, n_batch, n_chunks):
    u_refs = refs[:chunk]
    d_ref, cin_ref, sout_ref, coef_ref, y_ref, t_ref, sin_ref, w_scr, x_scr, ytok_scr = refs[chunk:]
    bblk = pl.program_id(1)
    rows = n_batch * n_chunks
    n_x = x_scr.shape[0]

    @pl.when(bblk == 0)
    def _():
        for tp in range(chunk):
            for t in range(chunk):
                blk = d_ref[t - tp] if t >= tp else jnp.zeros((128, 128), BF16)
                t_ref[tp * 128:(tp + 1) * 128, t * 128:(t + 1) * 128] = blk
        ch_bits = int(math.log2(128 // SSM_LANE_GROUPS))
        row = lax.broadcasted_iota(I32, (chunk * 128, 128), 0)
        row_group = lax.shift_right_logical(row, ch_bits) & (SSM_LANE_GROUPS - 1)
        for part in range(2 * n_x // SSM_LANE_GROUPS):
            src = cin_ref[:, part * 128:(part + 1) * 128]
            for g in range(SSM_LANE_GROUPS):
                col = (part * SSM_LANE_GROUPS + g) * 128
                sin_ref[:, col:col + 128] = jnp.where(row_group == g, src, jnp.zeros_like(src))

    pitch = w_scr.shape[1] // n_batch

    u = jnp.concatenate([r[...] for r in u_refs], axis=1)
    for jb in range(n_x):
        w = _dot(u, sin_ref[:, jb * 256:(jb + 1) * 256])
        for b in range(n_batch):
            seq = slice(b * n_chunks, (b + 1) * n_chunks)
            w_scr[2 * jb, b * pitch:b * pitch + n_chunks, :] = w[seq, :128]
            w_scr[2 * jb + 1, b * pitch:b * pitch + n_chunks, :] = w[seq, 128:]

    ar = [jnp.broadcast_to(coef_ref[0:1, j * 128:(j + 1) * 128], (n_batch, 128)) for j in range(n_x)]
    ai_a = [jnp.broadcast_to(coef_ref[1:2, j * 128:(j + 1) * 128], (n_batch, 128)) for j in range(n_x)]
    ai_b = [jnp.broadcast_to(coef_ref[2:3, j * 128:(j + 1) * 128], (n_batch, 128)) for j in range(n_x)]

    def step(k, carry):
        x, xsw = carry
        rows_k = pl.ds(k, n_batch, stride=pitch)
        new_x, new_xsw = [], []
        for j in range(n_x):
            x_scr[j, rows_k, :] = x[j]
            new_x.append(ar[j] * x[j] + ai_a[j] * xsw[j] + w_scr[j, rows_k, :])
            new_xsw.append(ar[j] * xsw[j] + ai_b[j] * x[j] + w_scr[n_x + j, rows_k, :])
        return tuple(new_x), tuple(new_xsw)

    zero = tuple(jnp.zeros((n_batch, 128), F32) for _ in range(n_x))
    lax.fori_loop(0, n_chunks, step, (zero, zero))

    xs = jnp.concatenate(
        [jnp.concatenate([x_scr[j, b * pitch:b * pitch + n_chunks, :] for b in range(n_batch)], axis=0)
         for j in range(n_x)], axis=1).astype(BF16)
    for jb in range(chunk // 2):
        kk = (jb + 1) * 256
        cols = slice(jb * 256, (jb + 1) * 256)
        y = jax.nn.gelu(_dot(u[:, :kk], t_ref[:kk, cols]) + _dot(xs, sout_ref[:, cols]))
        for half in range(2):
            ytok_scr[pl.ds(2 * jb + half, rows, stride=chunk), :] = y[:, half * 128:(half + 1) * 128]
    y_ref[...] = ytok_scr[...].astype(y_ref.dtype)


def _ssm_operators(a_re, a_im, log_dt, b_re, b_im, c_re, c_im, d_skip):
    hp = lax.Precision.HIGHEST
    n_grp, n_state, n_ch = b_re.shape
    L = SSM_CHUNK
    nb = SSM_LANE_GROUPS
    assert n_ch * nb == 128 and 2 * n_state == 128 and n_grp % nb == 0
    n_gb = n_grp // nb
    a = lax.complex(a_re.astype(F32), a_im.astype(F32))
    lam = a * jnp.exp(log_dt.astype(F32))[:, None]
    a_bar = jnp.exp(lam)
    b_bar = ((a_bar - 1.0) / a)[..., None] * lax.complex(b_re.astype(F32), b_im.astype(F32))
    cc = lax.complex(c_re.astype(F32), c_im.astype(F32))
    steps = jnp.arange(L + 1, dtype=F32)
    apow = jnp.exp(steps[:, None, None] * lam[None])
    lane = jnp.arange(128)
    put = ((lane[None, None, :] // n_ch == jnp.arange(nb)[:, None, None])
           & (lane[None, None, :] % n_ch == jnp.arange(n_ch)[None, :, None])).astype(F32)

    kern = jnp.real(jnp.einsum('gcp,lgp,gpd->glcd', cc, apow[:L], b_bar, precision=hp))
    d_op = jnp.einsum('Bgtcd,gdr,gcl->Btrl', kern.reshape(n_gb, nb, L, n_ch, n_ch), put, put, precision=hp)
    skip = jnp.eye(128, dtype=F32)[None] * d_skip.astype(F32).reshape(n_gb, 1, 128)
    d_op = d_op.at[:, 0].add(skip).astype(BF16)

    s_in = apow[:L][::-1].transpose(1, 0, 2)[:, :, None, :] * b_bar.transpose(0, 2, 1)[:, None, :, :]
    s_in = jnp.concatenate([s_in.real, s_in.imag, s_in.imag, s_in.real], axis=-1).astype(BF16)
    c_in = s_in.reshape(n_gb, nb, L, n_ch, 4 * n_state).transpose(0, 2, 1, 3, 4).reshape(n_gb, L * 128, 4 * n_state)

    s_out = cc.transpose(0, 2, 1)[:, :, None, :] * apow[1:].transpose(1, 2, 0)[:, :, :, None]
    s_out = jnp.concatenate([s_out.real, -s_out.imag], axis=1)
    x_idx = jnp.arange(L * n_ch)
    l_idx = jnp.arange(L * 128)
    place = ((x_idx[:, None] // n_ch == l_idx[None, :] // 128) & (x_idx[:, None] % n_ch == l_idx[None, :] % n_ch))
    place = place[None] & ((l_idx[None, None, :] // n_ch) % nb == jnp.arange(nb)[:, None, None])
    s_out = jnp.einsum('Bgqx,gxl->Bgql', s_out.reshape(n_gb, nb, 2 * n_state, L * n_ch), place.astype(F32),
                       precision=hp).astype(BF16).reshape(n_gb, nb * 2 * n_state, L * 128)

    a_l = apow[L]
    coef = jnp.stack([jnp.concatenate([a_l.real, a_l.real], -1),
                      jnp.concatenate([-a_l.imag, a_l.imag], -1),
                      jnp.concatenate([a_l.imag, -a_l.imag], -1)], axis=1)
    coef = coef.reshape(n_gb, nb, 3, 2 * n_state).transpose(0, 2, 1, 3).reshape(n_gb, 3, nb * 2 * n_state)
    coef = jnp.pad(coef, ((0, 0), (0, 5), (0, 0)))
    return d_op, c_in, s_out, coef


def _ssm_scan(u, d_op, c_in, s_out, coef, *, n_batch, chunk):
    rows_all, lanes_all = u.shape
    n_blk, lc, _ = c_in.shape
    n_chunks = rows_all // n_batch
    nb = min(SSM_BATCH_PER_STEP, n_batch)
    assert n_batch % nb == 0 and lanes_all == chunk * n_blk * 128
    rows = nb * n_chunks
    pitch = n_chunks + 4 if n_chunks % 8 == 0 else n_chunks
    n_x = s_out.shape[1] // 128
    kern = functools.partial(_ssm_kernel, chunk=chunk, n_batch=nb, n_chunks=n_chunks)
    u_specs = [pl.BlockSpec((rows, 128), lambda g, b, t=t: (b, t * n_blk + g)) for t in range(chunk)]

    def op_spec(shape):
        return pl.BlockSpec((None,) + shape, lambda g, b: (g,) + (0,) * len(shape))

    return pl.pallas_call(
        kern,
        grid=(n_blk, n_batch // nb),
        in_specs=u_specs + [op_spec((chunk, 128, 128)), op_spec((lc, c_in.shape[2])),
                            op_spec((n_x * 128, lc)), op_spec((8, n_x * 128))],
        out_specs=pl.BlockSpec((rows * chunk, 128), lambda g, b: (b, g)),
        out_shape=jax.ShapeDtypeStruct((rows_all * chunk, n_blk * 128), BF16),
        scratch_shapes=[pltpu.VMEM((lc, lc), BF16), pltpu.VMEM((lc, 2 * n_x * 128), BF16),
                        pltpu.VMEM((2 * n_x, nb * pitch, 128), F32), pltpu.VMEM((n_x, nb * pitch, 128), F32),
                        pltpu.VMEM((rows * chunk, 128), F32)],
        compiler_params=pltpu.CompilerParams(dimension_semantics=("arbitrary", "arbitrary"),
                                             vmem_limit_bytes=V7X_VMEM_LIMIT_BYTES),
        name="ssm_scan",
    )(*([u] * chunk), d_op, c_in, s_out, coef)


def _ssm_out_kernel(y_ref, h_ref, wout_ref, l1g_ref, l1b_ref, rw_ref, rb_ref,
                    h1_ref, h1p_ref, idx_ref, gate_ref, rank_ref, cnt_ref, res_scr, *, alpha):
    @pl.when(pl.program_id(0) == 0)
    def _():
        cnt_ref[...] = jnp.zeros_like(cnt_ref)
        res_scr[...] = jnp.zeros_like(res_scr)

    h_split = _tail_norm(res_scr, l1g_ref, l1b_ref, h1_ref, h1p_ref)

    y = y_ref[...]
    d = h_ref.shape[1]
    val = _dot(y, wout_ref[:, :d])
    gate = _dot(y, wout_ref[:, d:])
    res_scr[...] = alpha * h_ref[...] + val * jax.nn.sigmoid(gate)
    _tail_route(h_split, rw_ref, rb_ref, idx_ref, gate_ref, rank_ref, cnt_ref)


def _ssm_layer(h, n_batch, w_in, a_re, a_im, log_dt, b_re, b_im, c_re, c_im, d_skip, w_out,
               l1g, l1b, rw2, rb, *, alpha):
    n_tok, d = h.shape
    n_grp, n_state, n_ch = b_re.shape
    width = n_grp * n_ch
    n_exp = rb.shape[0]
    L = SSM_CHUNK
    assert (n_tok // n_batch) % L == 0

    u = _ssm_in_proj(h, w_in, chunk=L)
    d_op, c_in, s_out, coef = _ssm_operators(a_re, a_im, log_dt, b_re, b_im, c_re, c_im, d_skip)
    y = _ssm_scan(u, d_op, c_in, s_out, coef, n_batch=n_batch, chunk=L)

    tm = min(TM_SSM_OUT, n_tok)
    assert n_tok % tm == 0
    r_shapes, r_specs = _router_out(n_tok, tm, n_exp)
    kern = functools.partial(_ssm_out_kernel, alpha=alpha)
    n_steps = n_tok // tm
    return pl.pallas_call(
        kern,
        grid=(n_steps + 1,),
        in_specs=[pl.BlockSpec((tm, width), lambda i: (jnp.minimum(i, n_steps - 1), 0)),
                  pl.BlockSpec((tm, d), lambda i: (jnp.minimum(i, n_steps - 1), 0)),
                  _const_spec((width, 2 * d)), _const_spec((1, d)), _const_spec((1, d)),
                  _const_spec((d, 256)), _const_spec((n_exp, 1))],
        out_specs=[pl.BlockSpec((tm, d), lambda i: (_prev(i), 0)),
                   pl.BlockSpec((tm * RECORD_ROWS, 128), lambda i: (_prev(i), 0))] + r_specs,
        out_shape=[jax.ShapeDtypeStruct((n_tok, d), F32),
                   jax.ShapeDtypeStruct((n_tok * RECORD_ROWS, 128), U32)] + r_shapes,
        scratch_shapes=[pltpu.VMEM((tm, d), F32)],
        compiler_params=pltpu.CompilerParams(dimension_semantics=("arbitrary",),
                                             vmem_limit_bytes=V7X_VMEM_LIMIT_BYTES),
        name="ssm_out_mixer",
    )(y, h, w_out.astype(BF16), l1g.reshape(1, -1), l1b.reshape(1, -1), rw2, rb)


def _moe_plan(idx, rank, cnt, tme, n_tiles):
    n_exp = cnt.shape[0]
    counts = cnt[:, 0].astype(I32)
    padded = ((counts + tme - 1) // tme) * tme
    ends = jnp.cumsum(padded)
    offs = ends - padded
    n_valid = ends[-1] // tme
    tile_start = jnp.arange(n_tiles, dtype=I32) * tme
    tile_expert = jnp.minimum(jnp.sum(tile_start[:, None] >= ends[None, :], axis=1), n_exp - 1).astype(I32)
    onehot = idx[:, :, None] == jnp.arange(n_exp, dtype=I32)[None, None, :]
    pos = jnp.sum(jnp.where(onehot, offs[None, None, :], 0), axis=-1) + rank
    meta = jnp.concatenate([counts, padded, offs, n_valid[None]]).astype(I32)
    return pos, tile_expert, meta


def _tile_pos(pos, tm):
    n_tok = pos.shape[1]
    return pos.reshape(2, n_tok // tm, tm).transpose(1, 0, 2).reshape(n_tok // tm, 1, 2 * tm)


def _record_slice(ref, row, n_rec=1):
    return ref.at[pl.ds(pl.multiple_of(row * RECORD_ROWS, RECORD_ROWS), n_rec * RECORD_ROWS), :]


def _dispatch_kernel(meta_ref, pos_ref, x_ref, xs_ref, zero_scr, sem, *, n_exp, tme, n_tiles):
    i = pl.program_id(0)
    tm = x_ref.shape[0] // RECORD_ROWS
    half = zero_scr.shape[0] // RECORD_ROWS

    def zero_fill(start, size):
        cp = pltpu.make_async_copy(_record_slice(zero_scr, 0, size), _record_slice(xs_ref, start, size), sem)
        cp.start()
        cp.wait()

    @pl.when(i == 0)
    def _():
        zero_scr[...] = jnp.zeros_like(zero_scr)
        for e in range(n_exp):
            pad = meta_ref[n_exp + e] - meta_ref[e]
            base = meta_ref[2 * n_exp + e] + meta_ref[e]
            for b in range(int(math.log2(tme))):
                size = 1 << b

                @pl.when(((pad >> b) & 1) == 1)
                def _():
                    zero_fill(base + (pad & (size - 1)), size)
        n_valid = meta_ref[3 * n_exp]
        for j in range(n_exp):
            @pl.when(n_valid + j < n_tiles)
            def _():
                for part in range(tme // half):
                    zero_fill((n_valid + j) * tme + part * half, half)

    def start(r, c):
        for slot in range(2):
            p = pos_ref[0, 0, slot * tm + r]
            pltpu.make_async_copy(_record_slice(x_ref, r), _record_slice(xs_ref, p), sem).start(priority=slot)
        return c

    lax.fori_loop(0, tm, start, 0, unroll=8)
    for slot in range(2):
        pltpu.make_async_copy(x_ref, _record_slice(xs_ref, 0, tm), sem).wait()


def _cast_specs(w_in, w_out, layer, n_steps, index):
    n_layers, n_exp, d, ff2 = w_in.shape
    ff = w_out.shape[2]
    assert (n_exp * d) % n_steps == 0 and (n_exp * ff) % n_steps == 0
    rin, rout = n_exp * d // n_steps, n_exp * ff // n_steps
    in_specs = [pl.BlockSpec((None, rin, ff2), lambda i, *_: (layer, index(i), 0)),
                pl.BlockSpec((None, rout, d), lambda i, *_: (layer, index(i), 0))]
    out_specs = [pl.BlockSpec((rin, ff2), lambda i, *_: (index(i), 0)),
                 pl.BlockSpec((rout, d), lambda i, *_: (index(i), 0))]
    out_shape = [jax.ShapeDtypeStruct((n_exp * d, ff2), BF16), jax.ShapeDtypeStruct((n_exp * ff, d), BF16)]
    operands = (w_in.reshape(n_layers, n_exp * d, ff2), w_out.reshape(n_layers, n_exp * ff, d))
    return in_specs, out_specs, out_shape, operands


def _moe_dispatch(h1p, pos, meta, *, n_exp, tme, n_tiles):
    n_tok = h1p.shape[0] // RECORD_ROWS
    tm = min(TM_DISPATCH, n_tok)
    assert n_tok % tm == 0
    half = max(tme // 2, 1)
    kern = functools.partial(_dispatch_kernel, n_exp=n_exp, tme=tme, n_tiles=n_tiles)
    return pl.pallas_call(
        kern,
        grid_spec=pltpu.PrefetchScalarGridSpec(
            num_scalar_prefetch=1,
            grid=(n_tok // tm,),
            in_specs=[pl.BlockSpec((1, 1, 2 * tm), lambda i, m: (i, 0, 0), memory_space=pltpu.SMEM),
                      pl.BlockSpec((tm * RECORD_ROWS, 128), lambda i, m: (i, 0))],
            out_specs=pl.BlockSpec(memory_space=pl.ANY),
            scratch_shapes=[pltpu.VMEM((half * RECORD_ROWS, 128), U32), pltpu.SemaphoreType.DMA]),
        out_shape=jax.ShapeDtypeStruct((n_tiles * tme * RECORD_ROWS, 128), U32),
        compiler_params=pltpu.CompilerParams(dimension_semantics=("arbitrary",),
                                             vmem_limit_bytes=V7X_VMEM_LIMIT_BYTES),
        name="moe_dispatch",
    )(meta, _tile_pos(pos, tm), h1p)


def _expert_kernel(te_ref, meta_ref, xs_ref, win_ref, wout_ref, *refs, n_exp, cast):
    if cast:
        nwin_ref, nwout_ref, ys_ref, nwin_b_ref, nwout_b_ref = refs
        nwin_b_ref[...] = nwin_ref[...].astype(BF16)
        nwout_b_ref[...] = nwout_ref[...].astype(BF16)
    else:
        (ys_ref,) = refs
    i = pl.program_id(0)
    n_valid = meta_ref[3 * n_exp]
    tme = xs_ref.shape[0] // RECORD_ROWS

    @pl.when(i < n_valid)
    def _():
        hi, lo = _load_records(xs_ref, tme)
        dh = hi.shape[1]
        ff = wout_ref.shape[0]
        h = _dot(hi.astype(BF16), win_ref[:dh, :]) + _dot(lo.astype(BF16), win_ref[dh:, :])
        act = (jax.nn.silu(h[:, :ff]) * h[:, ff:]).astype(BF16)
        _store_records(ys_ref, _dot(act, wout_ref[...]))

    @pl.when(i >= n_valid)
    def _():
        ys_ref[...] = jnp.zeros_like(ys_ref)


def _moe_experts(xs, tile_expert, meta, w_in, w_out, cast_weights, *, tme):
    n_rows = xs.shape[0] // RECORD_ROWS
    n_exp, d, ff2 = w_in.shape
    ff = w_out.shape[1]
    n_tiles = n_rows // tme

    def x_map(i, te, m):
        return (jnp.minimum(i, m[3 * n_exp] - 1), 0)

    cast = cast_weights is not None
    c_in, c_out, c_shape, c_ops = ([], [], [], ())
    if cast:
        n_cast = 1 << (n_tiles.bit_length() - 1)
        c_in, c_out, c_shape, c_ops = _cast_specs(*cast_weights, n_cast, lambda i: jnp.minimum(i, n_cast - 1))
    kern = functools.partial(_expert_kernel, n_exp=n_exp, cast=cast)
    return pl.pallas_call(
        kern,
        grid_spec=pltpu.PrefetchScalarGridSpec(
            num_scalar_prefetch=2,
            grid=(n_tiles,),
            in_specs=[pl.BlockSpec((tme * RECORD_ROWS, 128), x_map),
                      pl.BlockSpec((None, d, ff2), lambda i, te, m: (te[i], 0, 0)),
                      pl.BlockSpec((None, ff, d), lambda i, te, m: (te[i], 0, 0))] + c_in,
            out_specs=[pl.BlockSpec((tme * RECORD_ROWS, 128), lambda i, te, m: (i, 0))] + c_out),
        out_shape=[jax.ShapeDtypeStruct((n_rows * RECORD_ROWS, 128), U32)] + c_shape,
        compiler_params=pltpu.CompilerParams(dimension_semantics=("arbitrary",),
                                             vmem_limit_bytes=V7X_VMEM_LIMIT_BYTES),
        name="moe_experts",
    )(tile_expert, meta, xs, w_in, w_out, *c_ops)


def _combine_kernel(pos_ref, pos_next_ref, h1_ref, gate_ref, ys_ref, g_ref, b_ref, out_ref, buf, sems, *, alpha):
    i = pl.program_id(0)
    n_steps = pl.num_programs(0)
    tm = h1_ref.shape[0]

    def gather(p_ref, half):
        def start(r, c):
            for slot in range(2):
                p = p_ref[0, 0, slot * tm + r]
                pltpu.make_async_copy(_record_slice(ys_ref, p), _record_slice(buf.at[half, slot], r),
                                      sems.at[half]).start(priority=slot)
            return c

        lax.fori_loop(0, tm, start, 0, unroll=8)

    @pl.when(i == 0)
    def _():
        gather(pos_ref, 0)

    @pl.when(i + 1 < n_steps)
    def _():
        gather(pos_next_ref, (i + 1) % 2)

    cur = i % 2
    moe = None
    for slot in range(2):
        pltpu.make_async_copy(_record_slice(ys_ref, 0, tm), buf.at[cur, slot], sems.at[cur]).wait()
    for slot in range(2):
        hi, lo = _load_records(buf.at[cur, slot], tm)
        y = jnp.concatenate([hi, lo], axis=1) * gate_ref[:, slot:slot + 1]
        moe = y if moe is None else moe + y
    out_ref[...] = _layer_norm(alpha * h1_ref[...] + moe, g_ref[...], b_ref[...])


def _moe_combine(h1, ys, pos, gates, l2g, l2b, *, alpha):
    n_tok, d = h1.shape
    tm = min(TM_COMBINE, n_tok)
    assert n_tok % tm == 0
    n_steps = n_tok // tm
    kern = functools.partial(_combine_kernel, alpha=alpha)
    pos_tiles = _tile_pos(pos, tm)
    return pl.pallas_call(
        kern,
        grid=(n_steps,),
        in_specs=[pl.BlockSpec((1, 1, 2 * tm), lambda i: (i, 0, 0), memory_space=pltpu.SMEM),
                  pl.BlockSpec((1, 1, 2 * tm), lambda i: (jnp.minimum(i + 1, n_steps - 1), 0, 0),
                               memory_space=pltpu.SMEM),
                  pl.BlockSpec((tm, d), lambda i: (i, 0)),
                  pl.BlockSpec((tm, 2), lambda i: (i, 0)),
                  pl.BlockSpec(memory_space=pl.ANY),
                  _const_spec((1, d)), _const_spec((1, d))],
        out_specs=pl.BlockSpec((tm, d), lambda i: (i, 0)),
        out_shape=jax.ShapeDtypeStruct((n_tok, d), F32),
        scratch_shapes=[pltpu.VMEM((2, 2, tm * RECORD_ROWS, 128), U32), pltpu.SemaphoreType.DMA((2,))],
        compiler_params=pltpu.CompilerParams(dimension_semantics=("arbitrary",),
                                             vmem_limit_bytes=V7X_VMEM_LIMIT_BYTES),
        name="moe_combine",
    )(pos_tiles, pos_tiles, h1, gates.T, ys, l2g.reshape(1, -1), l2b.reshape(1, -1))


def _moe_layer(h1, h1p, idx, gates, rank, cnt, w_in_b, w_out_b, cast_next, l2g, l2b, *, alpha):
    n_tok, d = h1.shape
    n_exp = cnt.shape[0]
    ff = w_out_b.shape[0] // n_exp
    tme = min(TM_EXPERT, n_tok)
    n_tiles = (2 * n_tok) // tme + n_exp
    pos, tile_expert, meta = _moe_plan(idx, rank, cnt, tme, n_tiles)
    xs = _moe_dispatch(h1p, pos, meta, n_exp=n_exp, tme=tme, n_tiles=n_tiles)
    ys, *cast = _moe_experts(xs, tile_expert, meta, w_in_b.reshape(n_exp, d, 2 * ff), w_out_b.reshape(n_exp, ff, d),
                             cast_next, tme=tme)
    return _moe_combine(h1, ys, pos, gates, l2g, l2b, alpha=alpha), cast


def kernel(x, a_w_in, a_b_in, a_ln_g, a_ln_b, a_w_s, a_b_s, a_w_out, b_w_in, b_a_re, b_a_im, b_log_dt, b_b_re, b_b_im, b_c_re, b_c_im, b_d, b_w_out, ln1_g, ln1_b, router_w, router_b, e_w_in, e_w_out, ln2_g, ln2_b):
    n_batch, seq, d = x.shape
    depth = ln1_g.shape[0]
    alpha = float((2 * depth) ** 0.25)
    n_exp = router_w.shape[1]
    rw = jnp.pad(router_w.astype(F32), ((0, 0), (0, 128 - n_exp)))
    rw_hi = lax.reduce_precision(rw, exponent_bits=8, mantissa_bits=7)
    rw2 = jnp.concatenate([rw_hi, rw - rw_hi], axis=1).astype(BF16)
    rb = router_b.astype(F32).reshape(-1, 1)
    h = x.reshape(n_batch * seq, d)
    e_bf16 = None
    for i in range(depth):
        j = i // 2
        if i % 2 == 0:
            *mixed, e_in_b, e_out_b = _gmlp_layer(h, a_w_in[j], a_b_in[j], a_ln_g[j], a_ln_b[j], a_w_s[j], a_b_s[j],
                                                  a_w_out[j], ln1_g[i], ln1_b[i], rw2, rb, (e_w_in, e_w_out, i),
                                                  alpha=alpha)
            e_bf16 = [e_in_b, e_out_b]
        else:
            mixed = _ssm_layer(h, n_batch, b_w_in[j], b_a_re[j], b_a_im[j], b_log_dt[j], b_b_re[j],
                               b_b_im[j], b_c_re[j], b_c_im[j], b_d[j], b_w_out[j],
                               ln1_g[i], ln1_b[i], rw2, rb, alpha=alpha)
        cast_next = (e_w_in, e_w_out, i + 1) if (i % 2 == 0 and i + 1 < depth) else None
        h, e_bf16 = _moe_layer(*mixed, *e_bf16, cast_next, ln2_g[i], ln2_b[i], alpha=alpha)
    return h.reshape(n_batch, seq, d)
```

```python
import functools
import math

import jax
import jax.numpy as jnp
from jax import lax
from jax.experimental import pallas as pl
from jax.experimental.pallas import tpu as pltpu

F32 = jnp.float32
BF16 = jnp.bfloat16
I32 = jnp.int32
U32 = jnp.uint32

LN_EPS = 1e-5
N_EXPERT_GROUPS = 4
SSM_CHUNK = 16

TM_MIX = 256
TM_SSM_OUT = 512
TM_PROJ = 1024
TM_DISPATCH = 2048
TM_EXPERT = 512
TM_COMBINE = 256

V7X_VMEM_LIMIT_BYTES = 56 * 1024 * 1024


def _dot(a, b):
    return jnp.dot(a, b, preferred_element_type=F32)


def _layer_norm(x, g, b):
    mu = jnp.mean(x, axis=-1, keepdims=True)
    xc = x - mu
    var = jnp.sum(xc * xc, axis=-1, keepdims=True) * (1.0 / x.shape[-1])
    return xc * lax.rsqrt(var + LN_EPS) * g + b


def _const_spec(shape):
    return pl.BlockSpec(shape, lambda *_: (0,) * len(shape), pipeline_mode=pl.Buffered(1))


def _split_bf16(h1):
    h_top = lax.bitcast_convert_type(lax.bitcast_convert_type(h1, U32) & jnp.uint32(0xFFFF0000), F32)
    return h_top.astype(BF16), (h1 - h_top).astype(BF16)


def _route(h_hi, h_mid, rw_ref, rb_ref, idx_ref, gate_ref, rank_ref, cnt_ref, live):
    n_exp = rb_ref.shape[0]
    per_group = n_exp // N_EXPERT_GROUPS
    tm = h_hi.shape[0]
    first = _dot(h_hi, rw_ref[...])
    logits_t = first[:, :128] + first[:, 128:] + _dot(h_mid, rw_ref[:, :128])
    logits = logits_t.T[:n_exp, :] + rb_ref[...]
    m = jnp.max(logits, axis=0, keepdims=True)
    ex = jnp.exp(logits - m)
    probs = ex / jnp.sum(ex, axis=0, keepdims=True)

    best_score = None
    best_group = None
    for g in range(N_EXPERT_GROUPS):
        t1 = t2 = None
        for e in range(g * per_group, (g + 1) * per_group):
            v = probs[e:e + 1, :]
            if t1 is None:
                t1 = v
            elif t2 is None:
                t1, t2 = jnp.maximum(t1, v), jnp.minimum(t1, v)
            else:
                t2 = jnp.maximum(t2, jnp.minimum(t1, v))
                t1 = jnp.maximum(t1, v)
        score = t1 + t2
        if best_score is None:
            best_score, best_group = score, jnp.zeros((1, tm), I32)
        else:
            better = score > best_score
            best_group = jnp.where(better, g, best_group)
            best_score = jnp.where(better, score, best_score)

    eidx = lax.broadcasted_iota(I32, (n_exp, tm), 0)
    egrp = jnp.zeros((n_exp, tm), I32)
    for g in range(1, N_EXPERT_GROUPS):
        egrp = egrp + jnp.where(eidx >= g * per_group, 1, 0)
    in_group = egrp == best_group
    masked = jnp.where(in_group, probs, -1.0)
    m1 = jnp.max(masked, axis=0, keepdims=True)
    i1 = jnp.min(jnp.where(masked == m1, eidx, n_exp), axis=0, keepdims=True)
    sel1 = eidx == i1
    masked2 = jnp.where(sel1, -2.0, masked)
    m2 = jnp.max(masked2, axis=0, keepdims=True)
    i2 = jnp.min(jnp.where(masked2 == m2, eidx, n_exp), axis=0, keepdims=True)
    sel2 = eidx == i2
    den = m1 + m2

    cnt = jnp.where(sel1 | sel2, live, 0.0)
    r_i = lax.broadcasted_iota(I32, (tm, tm), 0)
    c_i = lax.broadcasted_iota(I32, (tm, tm), 1)
    strict_upper = jnp.where(r_i < c_i, 1.0, 0.0).astype(BF16)
    prefix = _dot(cnt.astype(BF16), strict_upper) + cnt_ref[:, 0:1]
    r1 = jnp.sum(jnp.where(sel1, prefix, 0.0), axis=0, keepdims=True)
    r2 = jnp.sum(jnp.where(sel2, prefix, 0.0), axis=0, keepdims=True)
    cnt_ref[...] += jnp.sum(cnt, axis=1, keepdims=True)

    idx_ref[...] = jnp.concatenate([i1, i2], axis=0)
    gate_ref[...] = jnp.concatenate([m1 / den, m2 / den], axis=0)
    rank_ref[...] = jnp.concatenate([r1, r2], axis=0).astype(I32)


RECORD_ROWS = 8


def _store_records(ref, h):
    tm, d = h.shape
    half = d // 2
    assert half == RECORD_ROWS * 128
    bits = lax.bitcast_convert_type(h.astype(BF16).astype(F32), U32)
    packed = (bits[:, :half] & jnp.uint32(0xFFFF0000)) | (bits[:, half:] >> 16)
    for s in range(RECORD_ROWS):
        ref[pl.ds(s, tm, stride=RECORD_ROWS), :] = packed[:, s * 128:(s + 1) * 128]


def _load_records(ref, tm):
    w = jnp.concatenate([ref[pl.ds(s, tm, stride=RECORD_ROWS), :] for s in range(RECORD_ROWS)], axis=1)
    hi = lax.bitcast_convert_type(w & jnp.uint32(0xFFFF0000), F32)
    lo = lax.bitcast_convert_type(w << 16, F32)
    return hi, lo


def _router_out(n_tok, tm, n_exp):
    shapes = [jax.ShapeDtypeStruct((2, n_tok), I32),
              jax.ShapeDtypeStruct((2, n_tok), F32),
              jax.ShapeDtypeStruct((2, n_tok), I32),
              jax.ShapeDtypeStruct((n_exp, 128), F32)]
    specs = [pl.BlockSpec((2, tm), lambda i: (0, _prev(i))),
             pl.BlockSpec((2, tm), lambda i: (0, _prev(i))),
             pl.BlockSpec((2, tm), lambda i: (0, _prev(i))),
             pl.BlockSpec((n_exp, 128), lambda i: (0, 0))]
    return shapes, specs


def _prev(i):
    return jnp.maximum(i - 1, 0)


def _tail_norm(res_scr, l1g_ref, l1b_ref, h1_ref, h1p_ref):
    h1 = _layer_norm(res_scr[...], l1g_ref[...], l1b_ref[...])
    h1_ref[...] = h1
    _store_records(h1p_ref, h1)
    return _split_bf16(h1)


def _tail_route(h_split, rw_ref, rb_ref, idx_ref, gate_ref, rank_ref, cnt_ref):
    live = jnp.where(pl.program_id(0) > 0, 1.0, 0.0)
    _route(*h_split, rw_ref, rb_ref, idx_ref, gate_ref, rank_ref, cnt_ref, live)


def _gmlp_kernel(x_ref, win_ref, bin_ref, lng_ref, lnb_ref, ws_ref, bst_ref, wout_ref,
                 l1g_ref, l1b_ref, rw_ref, rb_ref, ew_in_ref, ew_out_ref,
                 h1_ref, h1p_ref, idx_ref, gate_ref, rank_ref, cnt_ref, ew_in_b_ref, ew_out_b_ref,
                 p_scr, res_scr, *, alpha, chunk, groups):
    @pl.when(pl.program_id(0) == 0)
    def _():
        cnt_ref[...] = jnp.zeros_like(cnt_ref)
        res_scr[...] = jnp.zeros_like(res_scr)

    ew_in_b_ref[...] = ew_in_ref[...].astype(BF16)
    ew_out_b_ref[...] = ew_out_ref[...].astype(BF16)

    _tail_route(_tail_norm(res_scr, l1g_ref, l1b_ref, h1_ref, h1p_ref), rw_ref, rb_ref, idx_ref, gate_ref, rank_ref,
                cnt_ref)

    x = x_ref[...]
    xb = x.astype(BF16)
    width = win_ref.shape[1] // 2
    gdim = width // groups
    tm = x.shape[0]

    zv = jax.nn.gelu(_dot(xb, win_ref[:, width:]) + bin_ref[:, width:])
    vn = _layer_norm(zv, lng_ref[...], lnb_ref[...]).astype(BF16)
    zu = jax.nn.gelu(_dot(xb, win_ref[:, :width]) + bin_ref[:, :width])

    r_i = lax.broadcasted_iota(I32, (chunk, chunk), 0)
    c_i = lax.broadcasted_iota(I32, (chunk, chunk), 1)
    causal = r_i >= c_i
    for g in range(groups):
        w = jnp.where(causal, ws_ref[g], 0.0).astype(BF16)
        bias = bst_ref[:, g:g + 1]
        for c in range(tm // chunk):
            rows = slice(c * chunk, (c + 1) * chunk)
            cols = slice(g * gdim, (g + 1) * gdim)
            s = _dot(w, vn[rows, cols]) + bias
            p_scr[rows, cols] = (zu[rows, cols] * s).astype(BF16)

    res_scr[...] = alpha * x + _dot(p_scr[...], wout_ref[...])


def _gmlp_layer(h, w_in, b_in, ln_g, ln_b, w_s, b_s, w_out, l1g, l1b, rw2, rb, expert_weights, *, alpha):
    n_tok, d = h.shape
    groups, chunk, _ = w_s.shape
    width = w_in.shape[1] // 2
    n_exp = rb.shape[0]
    tm = min(TM_MIX, n_tok)
    assert n_tok % tm == 0 and tm % chunk == 0
    r_shapes, r_specs = _router_out(n_tok, tm, n_exp)
    kern = functools.partial(_gmlp_kernel, alpha=alpha, chunk=chunk, groups=groups)
    n_steps = n_tok // tm
    c_in, c_out, c_shape, c_ops = _cast_specs(*expert_weights, n_steps, lambda i: jnp.minimum(i, n_steps - 1))
    return pl.pallas_call(
        kern,
        grid=(n_steps + 1,),
        in_specs=[pl.BlockSpec((tm, d), lambda i: (jnp.minimum(i, n_steps - 1), 0)),
                  _const_spec((d, 2 * width)), _const_spec((1, 2 * width)),
                  _const_spec((1, width)), _const_spec((1, width)),
                  _const_spec((groups, chunk, chunk)), _const_spec((chunk, groups)),
                  _const_spec((width, d)), _const_spec((1, d)), _const_spec((1, d)),
                  _const_spec((d, 256)), _const_spec((n_exp, 1))] + c_in,
        out_specs=[pl.BlockSpec((tm, d), lambda i: (_prev(i), 0)),
                   pl.BlockSpec((tm * RECORD_ROWS, 128), lambda i: (_prev(i), 0))] + r_specs + c_out,
        out_shape=[jax.ShapeDtypeStruct((n_tok, d), F32),
                   jax.ShapeDtypeStruct((n_tok * RECORD_ROWS, 128), U32)] + r_shapes + c_shape,
        scratch_shapes=[pltpu.VMEM((tm, width), BF16), pltpu.VMEM((tm, d), F32)],
        compiler_params=pltpu.CompilerParams(dimension_semantics=("arbitrary",),
                                             vmem_limit_bytes=V7X_VMEM_LIMIT_BYTES),
        name="gmlp_mixer",
    )(h, w_in.astype(BF16), b_in.reshape(1, -1), ln_g.reshape(1, -1), ln_b.reshape(1, -1),
      w_s, b_s.T, w_out.astype(BF16), l1g.reshape(1, -1), l1b.reshape(1, -1), rw2, rb, *c_ops)


SSM_LANE_GROUPS = 8
SSM_BATCH_PER_STEP = 4


def _proj_kernel(x_ref, w_ref, o_ref, res_scr, *, chunk):
    res = _dot(x_ref[...].astype(BF16), w_ref[...])
    n_slab = res_scr.shape[0]
    kb = o_ref.shape[0]
    for j in range(n_slab):
        res_scr[j] = res[:, j * 128:(j + 1) * 128]
    for t in range(chunk):
        for j in range(n_slab):
            o_ref[:, (t * n_slab + j) * 128:(t * n_slab + j + 1) * 128] = (
                res_scr[j, pl.ds(t, kb, stride=chunk), :].astype(o_ref.dtype))


def _ssm_in_proj(h, w, *, chunk):
    n_tok, d = h.shape
    width = w.shape[1]
    tm = min(TM_PROJ, n_tok)
    assert n_tok % tm == 0 and tm % chunk == 0 and width % 128 == 0
    kern = functools.partial(_proj_kernel, chunk=chunk)
    return pl.pallas_call(
        kern,
        grid=(n_tok // tm,),
        in_specs=[pl.BlockSpec((tm, d), lambda i: (i, 0)), _const_spec((d, width))],
        out_specs=pl.BlockSpec((tm // chunk, chunk * width), lambda i: (i, 0)),
        out_shape=jax.ShapeDtypeStruct((n_tok // chunk, chunk * width), BF16),
        scratch_shapes=[pltpu.VMEM((width // 128, tm, 128), F32)],
        compiler_params=pltpu.CompilerParams(dimension_semantics=("arbitrary",),
                                             vmem_limit_bytes=V7X_VMEM_LIMIT_BYTES),
        name="ssm_in_proj",
    )(h, w.astype(BF16))


def _ssm_kernel(*refs, chunk, n_batch, n_chunks):
    u_refs = refs[:chunk]
    d_ref, cin_ref, sout_ref, coef_ref, y_ref, t_ref, sin_ref, w_scr, x_scr, ytok_scr = refs[chunk:]
    bblk = pl.program_id(1)
    rows = n_batch * n_chunks
    n_x = x_scr.shape[0]

    @pl.when(bblk == 0)
    def _():
        for tp in range(chunk):
            for t in range(chunk):
                blk = d_ref[t - tp] if t >= tp else jnp.zeros((128, 128), BF16)
                t_ref[tp * 128:(tp + 1) * 128, t * 128:(t + 1) * 128] = blk
        ch_bits = int(math.log2(128 // SSM_LANE_GROUPS))
        row = lax.broadcasted_iota(I32, (chunk * 128, 128), 0)
        row_group = lax.shift_right_logical(row, ch_bits) & (SSM_LANE_GROUPS - 1)
        for part in range(2 * n_x // SSM_LANE_GROUPS):
            src = cin_ref[:, part * 128:(part + 1) * 128]
            for g in range(SSM_LANE_GROUPS):
                col = (part * SSM_LANE_GROUPS + g) * 128
                sin_ref[:, col:col + 128] = jnp.where(row_group == g, src, jnp.zeros_like(src))

    pitch = w_scr.shape[1] // n_batch

    u = jnp.concatenate([r[...] for r in u_refs], axis=1)
    for jb in range(n_x):
        w = _dot(u, sin_ref[:, jb * 256:(jb + 1) * 256])
        for b in range(n_batch):
            seq = slice(b * n_chunks, (b + 1) * n_chunks)
            w_scr[2 * jb, b * pitch:b * pitch + n_chunks, :] = w[seq, :128]
            w_scr[2 * jb + 1, b * pitch:b * pitch + n_chunks, :] = w[seq, 128:]

    ar = [jnp.broadcast_to(coef_ref[0:1, j * 128:(j + 1) * 128], (n_batch, 128)) for j in range(n_x)]
    ai_a = [jnp.broadcast_to(coef_ref[1:2, j * 128:(j + 1) * 128], (n_batch, 128)) for j in range(n_x)]
    ai_b = [jnp.broadcast_to(coef_ref[2:3, j * 128:(j + 1) * 128], (n_batch, 128)) for j in range(n_x)]

    def step(k, carry):
        x, xsw = carry
        rows_k = pl.ds(k, n_batch, stride=pitch)
        new_x, new_xsw = [], []
        for j in range(n_x):
            x_scr[j, rows_k, :] = x[j]
            new_x.append(ar[j] * x[j] + ai_a[j] * xsw[j] + w_scr[j, rows_k, :])
            new_xsw.append(ar[j] * xsw[j] + ai_b[j] * x[j] + w_scr[n_x + j, rows_k, :])
        return tuple(new_x), tuple(new_xsw)

    zero = tuple(jnp.zeros((n_batch, 128), F32) for _ in range(n_x))
    lax.fori_loop(0, n_chunks, step, (zero, zero))

    xs = jnp.concatenate(
        [jnp.concatenate([x_scr[j, b * pitch:b * pitch + n_chunks, :] for b in range(n_batch)], axis=0)
         for j in range(n_x)], axis=1).astype(BF16)
    for jb in range(chunk // 2):
        kk = (jb + 1) * 256
        cols = slice(jb * 256, (jb + 1) * 256)
        y = jax.nn.gelu(_dot(u[:, :kk], t_ref[:kk, cols]) + _dot(xs, sout_ref[:, cols]))
        for half in range(2):
            ytok_scr[pl.ds(2 * jb + half, rows, stride=chunk), :] = y[:, half * 128:(half + 1) * 128]
    y_ref[...] = ytok_scr[...].astype(y_ref.dtype)


def _ssm_operators(a_re, a_im, log_dt, b_re, b_im, c_re, c_im, d_skip):
    hp = lax.Precision.HIGHEST
    n_grp, n_state, n_ch = b_re.shape
    L = SSM_CHUNK
    nb = SSM_LANE_GROUPS
    assert n_ch * nb == 128 and 2 * n_state == 128 and n_grp % nb == 0
    n_gb = n_grp // nb
    a = lax.complex(a_re.astype(F32), a_im.astype(F32))
    lam = a * jnp.exp(log_dt.astype(F32))[:, None]
    a_bar = jnp.exp(lam)
    b_bar = ((a_bar - 1.0) / a)[..., None] * lax.complex(b_re.astype(F32), b_im.astype(F32))
    cc = lax.complex(c_re.astype(F32), c_im.astype(F32))
    steps = jnp.arange(L + 1, dtype=F32)
    apow = jnp.exp(steps[:, None, None] * lam[None])
    lane = jnp.arange(128)
    put = ((lane[None, None, :] // n_ch == jnp.arange(nb)[:, None, None])
           & (lane[None, None, :] % n_ch == jnp.arange(n_ch)[None, :, None])).astype(F32)

    kern = jnp.real(jnp.einsum('gcp,lgp,gpd->glcd', cc, apow[:L], b_bar, precision=hp))
    d_op = jnp.einsum('Bgtcd,gdr,gcl->Btrl', kern.reshape(n_gb, nb, L, n_ch, n_ch), put, put, precision=hp)
    skip = jnp.eye(128, dtype=F32)[None] * d_skip.astype(F32).reshape(n_gb, 1, 128)
    d_op = d_op.at[:, 0].add(skip).astype(BF16)

    s_in = apow[:L][::-1].transpose(1, 0, 2)[:, :, None, :] * b_bar.transpose(0, 2, 1)[:, None, :, :]
    s_in = jnp.concatenate([s_in.real, s_in.imag, s_in.imag, s_in.real], axis=-1).astype(BF16)
    c_in = s_in.reshape(n_gb, nb, L, n_ch, 4 * n_state).transpose(0, 2, 1, 3, 4).reshape(n_gb, L * 128, 4 * n_state)

    s_out = cc.transpose(0, 2, 1)[:, :, None, :] * apow[1:].transpose(1, 2, 0)[:, :, :, None]
    s_out = jnp.concatenate([s_out.real, -s_out.imag], axis=1)
    x_idx = jnp.arange(L * n_ch)
    l_idx = jnp.arange(L * 128)
    place = ((x_idx[:, None] // n_ch == l_idx[None, :] // 128) & (x_idx[:, None] % n_ch == l_idx[None, :] % n_ch))
    place = place[None] & ((l_idx[None, None, :] // n_ch) % nb == jnp.arange(nb)[:, None, None])
    s_out = jnp.einsum('Bgqx,gxl->Bgql', s_out.reshape(n_gb, nb, 2 * n_state, L * n_ch), place.astype(F32),
                       precision=hp).astype(BF16).reshape(n_gb, nb * 2 * n_state, L * 128)

    a_l = apow[L]
    coef = jnp.stack([jnp.concatenate([a_l.real, a_l.real], -1),
                      jnp.concatenate([-a_l.imag, a_l.imag], -1),
                      jnp.concatenate([a_l.imag, -a_l.imag], -1)], axis=1)
    coef = coef.reshape(n_gb, nb, 3, 2 * n_state).transpose(0, 2, 1, 3).reshape(n_gb, 3, nb * 2 * n_state)
    coef = jnp.pad(coef, ((0, 0), (0, 5), (0, 0)))
    return d_op, c_in, s_out, coef


def _ssm_scan(u, d_op, c_in, s_out, coef, *, n_batch, chunk):
    rows_all, lanes_all = u.shape
    n_blk, lc, _ = c_in.shape
    n_chunks = rows_all // n_batch
    nb = min(SSM_BATCH_PER_STEP, n_batch)
    assert n_batch % nb == 0 and lanes_all == chunk * n_blk * 128
    rows = nb * n_chunks
    pitch = n_chunks + 4 if n_chunks % 8 == 0 else n_chunks
    n_x = s_out.shape[1] // 128
    kern = functools.partial(_ssm_kernel, chunk=chunk, n_batch=nb, n_chunks=n_chunks)
    u_specs = [pl.BlockSpec((rows, 128), lambda g, b, t=t: (b, t * n_blk + g)) for t in range(chunk)]

    def op_spec(shape):
        return pl.BlockSpec((None,) + shape, lambda g, b: (g,) + (0,) * len(shape))

    return pl.pallas_call(
        kern,
        grid=(n_blk, n_batch // nb),
        in_specs=u_specs + [op_spec((chunk, 128, 128)), op_spec((lc, c_in.shape[2])),
                            op_spec((n_x * 128, lc)), op_spec((8, n_x * 128))],
        out_specs=pl.BlockSpec((rows * chunk, 128), lambda g, b: (b, g)),
        out_shape=jax.ShapeDtypeStruct((rows_all * chunk, n_blk * 128), BF16),
        scratch_shapes=[pltpu.VMEM((lc, lc), BF16), pltpu.VMEM((lc, 2 * n_x * 128), BF16),
                        pltpu.VMEM((2 * n_x, nb * pitch, 128), F32), pltpu.VMEM((n_x, nb * pitch, 128), F32),
                        pltpu.VMEM((rows * chunk, 128), F32)],
        compiler_params=pltpu.CompilerParams(dimension_semantics=("arbitrary", "arbitrary"),
                                             vmem_limit_bytes=V7X_VMEM_LIMIT_BYTES),
        name="ssm_scan",
    )(*([u] * chunk), d_op, c_in, s_out, coef)


def _ssm_out_kernel(y_ref, h_ref, wout_ref, l1g_ref, l1b_ref, rw_ref, rb_ref,
                    h1_ref, h1p_ref, idx_ref, gate_ref, rank_ref, cnt_ref, res_scr, *, alpha):
    @pl.when(pl.program_id(0) == 0)
    def _():
        cnt_ref[...] = jnp.zeros_like(cnt_ref)
        res_scr[...] = jnp.zeros_like(res_scr)

    h_split = _tail_norm(res_scr, l1g_ref, l1b_ref, h1_ref, h1p_ref)

    y = y_ref[...]
    d = h_ref.shape[1]
    val = _dot(y, wout_ref[:, :d])
    gate = _dot(y, wout_ref[:, d:])
    res_scr[...] = alpha * h_ref[...] + val * jax.nn.sigmoid(gate)
    _tail_route(h_split, rw_ref, rb_ref, idx_ref, gate_ref, rank_ref, cnt_ref)


def _ssm_layer(h, n_batch, w_in, a_re, a_im, log_dt, b_re, b_im, c_re, c_im, d_skip, w_out,
               l1g, l1b, rw2, rb, *, alpha):
    n_tok, d = h.shape
    n_grp, n_state, n_ch = b_re.shape
    width = n_grp * n_ch
    n_exp = rb.shape[0]
    L = SSM_CHUNK
    assert (n_tok // n_batch) % L == 0

    u = _ssm_in_proj(h, w_in, chunk=L)
    d_op, c_in, s_out, coef = _ssm_operators(a_re, a_im, log_dt, b_re, b_im, c_re, c_im, d_skip)
    y = _ssm_scan(u, d_op, c_in, s_out, coef, n_batch=n_batch, chunk=L)

    tm = min(TM_SSM_OUT, n_tok)
    assert n_tok % tm == 0
    r_shapes, r_specs = _router_out(n_tok, tm, n_exp)
    kern = functools.partial(_ssm_out_kernel, alpha=alpha)
    n_steps = n_tok // tm
    return pl.pallas_call(
        kern,
        grid=(n_steps + 1,),
        in_specs=[pl.BlockSpec((tm, width), lambda i: (jnp.minimum(i, n_steps - 1), 0)),
                  pl.BlockSpec((tm, d), lambda i: (jnp.minimum(i, n_steps - 1), 0)),
                  _const_spec((width, 2 * d)), _const_spec((1, d)), _const_spec((1, d)),
                  _const_spec((d, 256)), _const_spec((n_exp, 1))],
        out_specs=[pl.BlockSpec((tm, d), lambda i: (_prev(i), 0)),
                   pl.BlockSpec((tm * RECORD_ROWS, 128), lambda i: (_prev(i), 0))] + r_specs,
        out_shape=[jax.ShapeDtypeStruct((n_tok, d), F32),
                   jax.ShapeDtypeStruct((n_tok * RECORD_ROWS, 128), U32)] + r_shapes,
        scratch_shapes=[pltpu.VMEM((tm, d), F32)],
        compiler_params=pltpu.CompilerParams(dimension_semantics=("arbitrary",),
                                             vmem_limit_bytes=V7X_VMEM_LIMIT_BYTES),
        name="ssm_out_mixer",
    )(y, h, w_out.astype(BF16), l1g.reshape(1, -1), l1b.reshape(1, -1), rw2, rb)


def _moe_plan(idx, rank, cnt, tme, n_tiles):
    n_exp = cnt.shape[0]
    counts = cnt[:, 0].astype(I32)
    padded = ((counts + tme - 1) // tme) * tme
    ends = jnp.cumsum(padded)
    offs = ends - padded
    n_valid = ends[-1] // tme
    tile_start = jnp.arange(n_tiles, dtype=I32) * tme
    tile_expert = jnp.minimum(jnp.sum(tile_start[:, None] >= ends[None, :], axis=1), n_exp - 1).astype(I32)
    onehot = idx[:, :, None] == jnp.arange(n_exp, dtype=I32)[None, None, :]
    pos = jnp.sum(jnp.where(onehot, offs[None, None, :], 0), axis=-1) + rank
    meta = jnp.concatenate([counts, padded, offs, n_valid[None]]).astype(I32)
    return pos, tile_expert, meta


def _tile_pos(pos, tm):
    n_tok = pos.shape[1]
    return pos.reshape(2, n_tok // tm, tm).transpose(1, 0, 2).reshape(n_tok // tm, 1, 2 * tm)


def _record_slice(ref, row, n_rec=1):
    return ref.at[pl.ds(pl.multiple_of(row * RECORD_ROWS, RECORD_ROWS), n_rec * RECORD_ROWS), :]


def _dispatch_kernel(meta_ref, pos_ref, x_ref, xs_ref, zero_scr, sem, *, n_exp, tme, n_tiles):
    i = pl.program_id(0)
    tm = x_ref.shape[0] // RECORD_ROWS
    half = zero_scr.shape[0] // RECORD_ROWS

    def zero_fill(start, size):
        cp = pltpu.make_async_copy(_record_slice(zero_scr, 0, size), _record_slice(xs_ref, start, size), sem)
        cp.start()
        cp.wait()

    @pl.when(i == 0)
    def _():
        zero_scr[...] = jnp.zeros_like(zero_scr)
        for e in range(n_exp):
            pad = meta_ref[n_exp + e] - meta_ref[e]
            base = meta_ref[2 * n_exp + e] + meta_ref[e]
            for b in range(int(math.log2(tme))):
                size = 1 << b

                @pl.when(((pad >> b) & 1) == 1)
                def _():
                    zero_fill(base + (pad & (size - 1)), size)
        n_valid = meta_ref[3 * n_exp]
        for j in range(n_exp):
            @pl.when(n_valid + j < n_tiles)
            def _():
                for part in range(tme // half):
                    zero_fill((n_valid + j) * tme + part * half, half)

    def start(r, c):
        for slot in range(2):
            p = pos_ref[0, 0, slot * tm + r]
            pltpu.make_async_copy(_record_slice(x_ref, r), _record_slice(xs_ref, p), sem).start(priority=slot)
        return c

    lax.fori_loop(0, tm, start, 0, unroll=8)
    for slot in range(2):
        pltpu.make_async_copy(x_ref, _record_slice(xs_ref, 0, tm), sem).wait()


def _cast_specs(w_in, w_out, layer, n_steps, index):
    n_layers, n_exp, d, ff2 = w_in.shape
    ff = w_out.shape[2]
    assert (n_exp * d) % n_steps == 0 and (n_exp * ff) % n_steps == 0
    rin, rout = n_exp * d // n_steps, n_exp * ff // n_steps
    in_specs = [pl.BlockSpec((None, rin, ff2), lambda i, *_: (layer, index(i), 0)),
                pl.BlockSpec((None, rout, d), lambda i, *_: (layer, index(i), 0))]
    out_specs = [pl.BlockSpec((rin, ff2), lambda i, *_: (index(i), 0)),
                 pl.BlockSpec((rout, d), lambda i, *_: (index(i), 0))]
    out_shape = [jax.ShapeDtypeStruct((n_exp * d, ff2), BF16), jax.ShapeDtypeStruct((n_exp * ff, d), BF16)]
    operands = (w_in.reshape(n_layers, n_exp * d, ff2), w_out.reshape(n_layers, n_exp * ff, d))
    return in_specs, out_specs, out_shape, operands


def _moe_dispatch(h1p, pos, meta, *, n_exp, tme, n_tiles):
    n_tok = h1p.shape[0] // RECORD_ROWS
    tm = min(TM_DISPATCH, n_tok)
    assert n_tok % tm == 0
    half = max(tme // 2, 1)
    kern = functools.partial(_dispatch_kernel, n_exp=n_exp, tme=tme, n_tiles=n_tiles)
    return pl.pallas_call(
        kern,
        grid_spec=pltpu.PrefetchScalarGridSpec(
            num_scalar_prefetch=1,
            grid=(n_tok // tm,),
            in_specs=[pl.BlockSpec((1, 1, 2 * tm), lambda i, m: (i, 0, 0), memory_space=pltpu.SMEM),
                      pl.BlockSpec((tm * RECORD_ROWS, 128), lambda i, m: (i, 0))],
            out_specs=pl.BlockSpec(memory_space=pl.ANY),
            scratch_shapes=[pltpu.VMEM((half * RECORD_ROWS, 128), U32), pltpu.SemaphoreType.DMA]),
        out_shape=jax.ShapeDtypeStruct((n_tiles * tme * RECORD_ROWS, 128), U32),
        compiler_params=pltpu.CompilerParams(dimension_semantics=("arbitrary",),
                                             vmem_limit_bytes=V7X_VMEM_LIMIT_BYTES),
        name="moe_dispatch",
    )(meta, _tile_pos(pos, tm), h1p)


def _expert_kernel(te_ref, meta_ref, xs_ref, win_ref, wout_ref, *refs, n_exp, cast):
    if cast:
        nwin_ref, nwout_ref, ys_ref, nwin_b_ref, nwout_b_ref = refs
        nwin_b_ref[...] = nwin_ref[...].astype(BF16)
        nwout_b_ref[...] = nwout_ref[...].astype(BF16)
    else:
        (ys_ref,) = refs
    i = pl.program_id(0)
    n_valid = meta_ref[3 * n_exp]
    tme = xs_ref.shape[0] // RECORD_ROWS

    @pl.when(i < n_valid)
    def _():
        hi, lo = _load_records(xs_ref, tme)
        dh = hi.shape[1]
        ff = wout_ref.shape[0]
        h = _dot(hi.astype(BF16), win_ref[:dh, :]) + _dot(lo.astype(BF16), win_ref[dh:, :])
        act = (jax.nn.silu(h[:, :ff]) * h[:, ff:]).astype(BF16)
        _store_records(ys_ref, _dot(act, wout_ref[...]))

    @pl.when(i >= n_valid)
    def _():
        ys_ref[...] = jnp.zeros_like(ys_ref)


def _moe_experts(xs, tile_expert, meta, w_in, w_out, cast_weights, *, tme):
    n_rows = xs.shape[0] // RECORD_ROWS
    n_exp, d, ff2 = w_in.shape
    ff = w_out.shape[1]
    n_tiles = n_rows // tme

    def x_map(i, te, m):
        return (jnp.minimum(i, m[3 * n_exp] - 1), 0)

    cast = cast_weights is not None
    c_in, c_out, c_shape, c_ops = ([], [], [], ())
    if cast:
        n_cast = 1 << (n_tiles.bit_length() - 1)
        c_in, c_out, c_shape, c_ops = _cast_specs(*cast_weights, n_cast, lambda i: jnp.minimum(i, n_cast - 1))
    kern = functools.partial(_expert_kernel, n_exp=n_exp, cast=cast)
    return pl.pallas_call(
        kern,
        grid_spec=pltpu.PrefetchScalarGridSpec(
            num_scalar_prefetch=2,
            grid=(n_tiles,),
            in_specs=[pl.BlockSpec((tme * RECORD_ROWS, 128), x_map),
                      pl.BlockSpec((None, d, ff2), lambda i, te, m: (te[i], 0, 0)),
                      pl.BlockSpec((None, ff, d), lambda i, te, m: (te[i], 0, 0))] + c_in,
            out_specs=[pl.BlockSpec((tme * RECORD_ROWS, 128), lambda i, te, m: (i, 0))] + c_out),
        out_shape=[jax.ShapeDtypeStruct((n_rows * RECORD_ROWS, 128), U32)] + c_shape,
        compiler_params=pltpu.CompilerParams(dimension_semantics=("arbitrary",),
                                             vmem_limit_bytes=V7X_VMEM_LIMIT_BYTES),
        name="moe_experts",
    )(tile_expert, meta, xs, w_in, w_out, *c_ops)


def _combine_kernel(pos_ref, pos_next_ref, h1_ref, gate_ref, ys_ref, g_ref, b_ref, out_ref, buf, sems, *, alpha):
    i = pl.program_id(0)
    n_steps = pl.num_programs(0)
    tm = h1_ref.shape[0]

    def gather(p_ref, half):
        def start(r, c):
            for slot in range(2):
                p = p_ref[0, 0, slot * tm + r]
                pltpu.make_async_copy(_record_slice(ys_ref, p), _record_slice(buf.at[half, slot], r),
                                      sems.at[half]).start(priority=slot)
            return c

        lax.fori_loop(0, tm, start, 0, unroll=8)

    @pl.when(i == 0)
    def _():
        gather(pos_ref, 0)

    @pl.when(i + 1 < n_steps)
    def _():
        gather(pos_next_ref, (i + 1) % 2)

    cur = i % 2
    moe = None
    for slot in range(2):
        pltpu.make_async_copy(_record_slice(ys_ref, 0, tm), buf.at[cur, slot], sems.at[cur]).wait()
    for slot in range(2):
        hi, lo = _load_records(buf.at[cur, slot], tm)
        y = jnp.concatenate([hi, lo], axis=1) * gate_ref[:, slot:slot + 1]
        moe = y if moe is None else moe + y
    out_ref[...] = _layer_norm(alpha * h1_ref[...] + moe, g_ref[...], b_ref[...])


def _moe_combine(h1, ys, pos, gates, l2g, l2b, *, alpha):
    n_tok, d = h1.shape
    tm = min(TM_COMBINE, n_tok)
    assert n_tok % tm == 0
    n_steps = n_tok // tm
    kern = functools.partial(_combine_kernel, alpha=alpha)
    pos_tiles = _tile_pos(pos, tm)
    return pl.pallas_call(
        kern,
        grid=(n_steps,),
        in_specs=[pl.BlockSpec((1, 1, 2 * tm), lambda i: (i, 0, 0), memory_space=pltpu.SMEM),
                  pl.BlockSpec((1, 1, 2 * tm), lambda i: (jnp.minimum(i + 1, n_steps - 1), 0, 0),
                               memory_space=pltpu.SMEM),
                  pl.BlockSpec((tm, d), lambda i: (i, 0)),
                  pl.BlockSpec((tm, 2), lambda i: (i, 0)),
                  pl.BlockSpec(memory_space=pl.ANY),
                  _const_spec((1, d)), _const_spec((1, d))],
        out_specs=pl.BlockSpec((tm, d), lambda i: (i, 0)),
        out_shape=jax.ShapeDtypeStruct((n_tok, d), F32),
        scratch_shapes=[pltpu.VMEM((2, 2, tm * RECORD_ROWS, 128), U32), pltpu.SemaphoreType.DMA((2,))],
        compiler_params=pltpu.CompilerParams(dimension_semantics=("arbitrary",),
                                             vmem_limit_bytes=V7X_VMEM_LIMIT_BYTES),
        name="moe_combine",
    )(pos_tiles, pos_tiles, h1, gates.T, ys, l2g.reshape(1, -1), l2b.reshape(1, -1))


def _moe_layer(h1, h1p, idx, gates, rank, cnt, w_in_b, w_out_b, cast_next, l2g, l2b, *, alpha):
    n_tok, d = h1.shape
    n_exp = cnt.shape[0]
    ff = w_out_b.shape[0] // n_exp
    tme = min(TM_EXPERT, n_tok)
    n_tiles = (2 * n_tok) // tme + n_exp
    pos, tile_expert, meta = _moe_plan(idx, rank, cnt, tme, n_tiles)
    xs = _moe_dispatch(h1p, pos, meta, n_exp=n_exp, tme=tme, n_tiles=n_tiles)
    ys, *cast = _moe_experts(xs, tile_expert, meta, w_in_b.reshape(n_exp, d, 2 * ff), w_out_b.reshape(n_exp, ff, d),
                             cast_next, tme=tme)
    return _moe_combine(h1, ys, pos, gates, l2g, l2b, alpha=alpha), cast


def kernel(x, a_w_in, a_b_in, a_ln_g, a_ln_b, a_w_s, a_b_s, a_w_out, b_w_in, b_a_re, b_a_im, b_log_dt, b_b_re, b_b_im, b_c_re, b_c_im, b_d, b_w_out, ln1_g, ln1_b, router_w, router_b, e_w_in, e_w_out, ln2_g, ln2_b):
    n_batch, seq, d = x.shape
    depth = ln1_g.shape[0]
    alpha = float((2 * depth) ** 0.25)
    n_exp = router_w.shape[1]
    rw = jnp.pad(router_w.astype(F32), ((0, 0), (0, 128 - n_exp)))
    rw_hi = lax.reduce_precision(rw, exponent_bits=8, mantissa_bits=7)
    rw2 = jnp.concatenate([rw_hi, rw - rw_hi], axis=1).astype(BF16)
    rb = router_b.astype(F32).reshape(-1, 1)
    h = x.reshape(n_batch * seq, d)
    e_bf16 = None
    for i in range(depth):
        j = i // 2
        if i % 2 == 0:
            *mixed, e_in_b, e_out_b = _gmlp_layer(h, a_w_in[j], a_b_in[j], a_ln_g[j], a_ln_b[j], a_w_s[j], a_b_s[j],
                                                  a_w_out[j], ln1_g[i], ln1_b[i], rw2, rb, (e_w_in, e_w_out, i),
                                                  alpha=alpha)
            e_bf16 = [e_in_b, e_out_b]
        else:
            mixed = _ssm_layer(h, n_batch, b_w_in[j], b_a_re[j], b_a_im[j], b_log_dt[j], b_b_re[j],
                               b_b_im[j], b_c_re[j], b_c_im[j], b_d[j], b_w_out[j],
                               ln1_g[i], ln1_b[i], rw2, rb, alpha=alpha)
        cast_next = (e_w_in, e_w_out, i + 1) if (i % 2 == 0 and i + 1 < depth) else None
        h, e_bf16 = _moe_layer(*mixed, *e_bf16, cast_next, ln2_g[i], ln2_b[i], alpha=alpha)
    return h.reshape(n_batch, seq, d)
```

```python
import functools
import math

import jax
import jax.numpy as jnp
from jax import lax
from jax.experimental import pallas as pl
from jax.experimental.pallas import tpu as pltpu

F32 = jnp.float32
BF16 = jnp.bfloat16
I32 = jnp.int32
U32 = jnp.uint32

LN_EPS = 1e-5
N_EXPERT_GROUPS = 4
SSM_CHUNK = 16

TM_MIX = 256
TM_SSM_OUT = 512
TM_PROJ = 1024
TM_DISPATCH = 4096
TM_EXPERT = 512
TM_COMBINE = 256

V7X_VMEM_LIMIT_BYTES = 56 * 1024 * 1024


def _dot(a, b):
    return jnp.dot(a, b, preferred_element_type=F32)


def _layer_norm(x, g, b):
    mu = jnp.mean(x, axis=-1, keepdims=True)
    xc = x - mu
    var = jnp.sum(xc * xc, axis=-1, keepdims=True) * (1.0 / x.shape[-1])
    return xc * lax.rsqrt(var + LN_EPS) * g + b


def _const_spec(shape):
    return pl.BlockSpec(shape, lambda *_: (0,) * len(shape), pipeline_mode=pl.Buffered(1))


def _split_bf16(h1):
    h_top = lax.bitcast_convert_type(lax.bitcast_convert_type(h1, U32) & jnp.uint32(0xFFFF0000), F32)
    return h_top.astype(BF16), (h1 - h_top).astype(BF16)


def _route(h_hi, h_mid, rw_ref, rb_ref, idx_ref, gate_ref, rank_ref, cnt_ref, live):
    n_exp = rb_ref.shape[0]
    per_group = n_exp // N_EXPERT_GROUPS
    tm = h_hi.shape[0]
    first = _dot(h_hi, rw_ref[...])
    logits_t = first[:, :128] + first[:, 128:] + _dot(h_mid, rw_ref[:, :128])
    logits = logits_t.T[:n_exp, :] + rb_ref[...]
    m = jnp.max(logits, axis=0, keepdims=True)
    ex = jnp.exp(logits - m)
    probs = ex / jnp.sum(ex, axis=0, keepdims=True)

    best_score = None
    best_group = None
    for g in range(N_EXPERT_GROUPS):
        t1 = t2 = None
        for e in range(g * per_group, (g + 1) * per_group):
            v = probs[e:e + 1, :]
            if t1 is None:
                t1 = v
            elif t2 is None:
                t1, t2 = jnp.maximum(t1, v), jnp.minimum(t1, v)
            else:
                t2 = jnp.maximum(t2, jnp.minimum(t1, v))
                t1 = jnp.maximum(t1, v)
        score = t1 + t2
        if best_score is None:
            best_score, best_group = score, jnp.zeros((1, tm), I32)
        else:
            better = score > best_score
            best_group = jnp.where(better, g, best_group)
            best_score = jnp.where(better, score, best_score)

    eidx = lax.broadcasted_iota(I32, (n_exp, tm), 0)
    egrp = jnp.zeros((n_exp, tm), I32)
    for g in range(1, N_EXPERT_GROUPS):
        egrp = egrp + jnp.where(eidx >= g * per_group, 1, 0)
    in_group = egrp == best_group
    masked = jnp.where(in_group, probs, -1.0)
    m1 = jnp.max(masked, axis=0, keepdims=True)
    i1 = jnp.min(jnp.where(masked == m1, eidx, n_exp), axis=0, keepdims=True)
    sel1 = eidx == i1
    masked2 = jnp.where(sel1, -2.0, masked)
    m2 = jnp.max(masked2, axis=0, keepdims=True)
    i2 = jnp.min(jnp.where(masked2 == m2, eidx, n_exp), axis=0, keepdims=True)
    sel2 = eidx == i2
    den = m1 + m2

    cnt = jnp.where(sel1 | sel2, live, 0.0)
    r_i = lax.broadcasted_iota(I32, (tm, tm), 0)
    c_i = lax.broadcasted_iota(I32, (tm, tm), 1)
    strict_upper = jnp.where(r_i < c_i, 1.0, 0.0).astype(BF16)
    prefix = _dot(cnt.astype(BF16), strict_upper) + cnt_ref[:, 0:1]
    r1 = jnp.sum(jnp.where(sel1, prefix, 0.0), axis=0, keepdims=True)
    r2 = jnp.sum(jnp.where(sel2, prefix, 0.0), axis=0, keepdims=True)
    cnt_ref[...] += jnp.sum(cnt, axis=1, keepdims=True)

    idx_ref[...] = jnp.concatenate([i1, i2], axis=0)
    gate_ref[...] = jnp.concatenate([m1 / den, m2 / den], axis=0)
    rank_ref[...] = jnp.concatenate([r1, r2], axis=0).astype(I32)


RECORD_ROWS = 8


def _store_records(ref, h):
    tm, d = h.shape
    half = d // 2
    assert half == RECORD_ROWS * 128
    bits = lax.bitcast_convert_type(h.astype(BF16).astype(F32), U32)
    packed = (bits[:, :half] & jnp.uint32(0xFFFF0000)) | (bits[:, half:] >> 16)
    for s in range(RECORD_ROWS):
        ref[pl.ds(s, tm, stride=RECORD_ROWS), :] = packed[:, s * 128:(s + 1) * 128]


def _load_records(ref, tm):
    w = jnp.concatenate([ref[pl.ds(s, tm, stride=RECORD_ROWS), :] for s in range(RECORD_ROWS)], axis=1)
    hi = lax.bitcast_convert_type(w & jnp.uint32(0xFFFF0000), F32)
    lo = lax.bitcast_convert_type(w << 16, F32)
    return hi, lo


def _router_out(n_tok, tm, n_exp):
    shapes = [jax.ShapeDtypeStruct((2, n_tok), I32),
              jax.ShapeDtypeStruct((2, n_tok), F32),
              jax.ShapeDtypeStruct((2, n_tok), I32),
              jax.ShapeDtypeStruct((n_exp, 128), F32)]
    specs = [pl.BlockSpec((2, tm), lambda i: (0, _prev(i))),
             pl.BlockSpec((2, tm), lambda i: (0, _prev(i))),
             pl.BlockSpec((2, tm), lambda i: (0, _prev(i))),
             pl.BlockSpec((n_exp, 128), lambda i: (0, 0))]
    return shapes, specs


def _prev(i):
    return jnp.maximum(i - 1, 0)


def _tail_norm(res_scr, l1g_ref, l1b_ref, h1_ref, h1p_ref):
    h1 = _layer_norm(res_scr[...], l1g_ref[...], l1b_ref[...])
    h1_ref[...] = h1
    _store_records(h1p_ref, h1)
    return _split_bf16(h1)


def _tail_route(h_split, rw_ref, rb_ref, idx_ref, gate_ref, rank_ref, cnt_ref):
    live = jnp.where(pl.program_id(0) > 0, 1.0, 0.0)
    _route(*h_split, rw_ref, rb_ref, idx_ref, gate_ref, rank_ref, cnt_ref, live)


def _gmlp_kernel(x_ref, win_ref, bin_ref, lng_ref, lnb_ref, ws_ref, bst_ref, wout_ref,
                 l1g_ref, l1b_ref, rw_ref, rb_ref, ew_in_ref, ew_out_ref,
                 h1_ref, h1p_ref, idx_ref, gate_ref, rank_ref, cnt_ref, ew_in_b_ref, ew_out_b_ref,
                 p_scr, res_scr, *, alpha, chunk, groups):
    @pl.when(pl.program_id(0) == 0)
    def _():
        cnt_ref[...] = jnp.zeros_like(cnt_ref)
        res_scr[...] = jnp.zeros_like(res_scr)

    ew_in_b_ref[...] = ew_in_ref[...].astype(BF16)
    ew_out_b_ref[...] = ew_out_ref[...].astype(BF16)

    _tail_route(_tail_norm(res_scr, l1g_ref, l1b_ref, h1_ref, h1p_ref), rw_ref, rb_ref, idx_ref, gate_ref, rank_ref,
                cnt_ref)

    x = x_ref[...]
    xb = x.astype(BF16)
    width = win_ref.shape[1] // 2
    gdim = width // groups
    tm = x.shape[0]

    zv = jax.nn.gelu(_dot(xb, win_ref[:, width:]) + bin_ref[:, width:])
    vn = _layer_norm(zv, lng_ref[...], lnb_ref[...]).astype(BF16)
    zu = jax.nn.gelu(_dot(xb, win_ref[:, :width]) + bin_ref[:, :width])

    r_i = lax.broadcasted_iota(I32, (chunk, chunk), 0)
    c_i = lax.broadcasted_iota(I32, (chunk, chunk), 1)
    causal = r_i >= c_i
    for g in range(groups):
        w = jnp.where(causal, ws_ref[g], 0.0).astype(BF16)
        bias = bst_ref[:, g:g + 1]
        for c in range(tm // chunk):
            rows = slice(c * chunk, (c + 1) * chunk)
            cols = slice(g * gdim, (g + 1) * gdim)
            s = _dot(w, vn[rows, cols]) + bias
            p_scr[rows, cols] = (zu[rows, cols] * s).astype(BF16)

    res_scr[...] = alpha * x + _dot(p_scr[...], wout_ref[...])


def _gmlp_layer(h, w_in, b_in, ln_g, ln_b, w_s, b_s, w_out, l1g, l1b, rw2, rb, expert_weights, *, alpha):
    n_tok, d = h.shape
    groups, chunk, _ = w_s.shape
    width = w_in.shape[1] // 2
    n_exp = rb.shape[0]
    tm = min(TM_MIX, n_tok)
    assert n_tok % tm == 0 and tm % chunk == 0
    r_shapes, r_specs = _router_out(n_tok, tm, n_exp)
    kern = functools.partial(_gmlp_kernel, alpha=alpha, chunk=chunk, groups=groups)
    n_steps = n_tok // tm
    c_in, c_out, c_shape, c_ops = _cast_specs(*expert_weights, n_steps, lambda i: jnp.minimum(i, n_steps - 1))
    return pl.pallas_call(
        kern,
        grid=(n_steps + 1,),
        in_specs=[pl.BlockSpec((tm, d), lambda i: (jnp.minimum(i, n_steps - 1), 0)),
                  _const_spec((d, 2 * width)), _const_spec((1, 2 * width)),
                  _const_spec((1, width)), _const_spec((1, width)),
                  _const_spec((groups, chunk, chunk)), _const_spec((chunk, groups)),
                  _const_spec((width, d)), _const_spec((1, d)), _const_spec((1, d)),
                  _const_spec((d, 256)), _const_spec((n_exp, 1))] + c_in,
        out_specs=[pl.BlockSpec((tm, d), lambda i: (_prev(i), 0)),
                   pl.BlockSpec((tm * RECORD_ROWS, 128), lambda i: (_prev(i), 0))] + r_specs + c_out,
        out_shape=[jax.ShapeDtypeStruct((n_tok, d), F32),
                   jax.ShapeDtypeStruct((n_tok * RECORD_ROWS, 128), U32)] + r_shapes + c_shape,
        scratch_shapes=[pltpu.VMEM((tm, width), BF16), pltpu.VMEM((tm, d), F32)],
        compiler_params=pltpu.CompilerParams(dimension_semantics=("arbitrary",),
                                             vmem_limit_bytes=V7X_VMEM_LIMIT_BYTES),
        name="gmlp_mixer",
    )(h, w_in.astype(BF16), b_in.reshape(1, -1), ln_g.reshape(1, -1), ln_b.reshape(1, -1),
      w_s, b_s.T, w_out.astype(BF16), l1g.reshape(1, -1), l1b.reshape(1, -1), rw2, rb, *c_ops)


SSM_LANE_GROUPS = 8
SSM_BATCH_PER_STEP = 4


def _proj_kernel(x_ref, w_ref, o_ref, res_scr, *, chunk):
    res = _dot(x_ref[...].astype(BF16), w_ref[...])
    n_slab = res_scr.shape[0]
    kb = o_ref.shape[0]
    for j in range(n_slab):
        res_scr[j] = res[:, j * 128:(j + 1) * 128]
    for t in range(chunk):
        for j in range(n_slab):
            o_ref[:, (t * n_slab + j) * 128:(t * n_slab + j + 1) * 128] = (
                res_scr[j, pl.ds(t, kb, stride=chunk), :].astype(o_ref.dtype))


def _ssm_in_proj(h, w, *, chunk):
    n_tok, d = h.shape
    width = w.shape[1]
    tm = min(TM_PROJ, n_tok)
    assert n_tok % tm == 0 and tm % chunk == 0 and width % 128 == 0
    kern = functools.partial(_proj_kernel, chunk=chunk)
    return pl.pallas_call(
        kern,
        grid=(n_tok // tm,),
        in_specs=[pl.BlockSpec((tm, d), lambda i: (i, 0)), _const_spec((d, width))],
        out_specs=pl.BlockSpec((tm // chunk, chunk * width), lambda i: (i, 0)),
        out_shape=jax.ShapeDtypeStruct((n_tok // chunk, chunk * width), BF16),
        scratch_shapes=[pltpu.VMEM((width // 128, tm, 128), F32)],
        compiler_params=pltpu.CompilerParams(dimension_semantics=("arbitrary",),
                                             vmem_limit_bytes=V7X_VMEM_LIMIT_BYTES),
        name="ssm_in_proj",
    )(h, w.astype(BF16))


def _ssm_kernel(*refs, chunk, n_batch, n_chunks):
    u_refs = refs[:chunk]
    d_ref, cin_ref, sout_ref, coef_ref, y_ref, t_ref, sin_ref, w_scr, x_scr, ytok_scr = refs[chunk:]
    bblk = pl.program_id(1)
    rows = n_batch * n_chunks
    n_x = x_scr.shape[0]

    @pl.when(bblk == 0)
    def _():
        for tp in range(chunk):
            for t in range(chunk):
                blk = d_ref[t - tp] if t >= tp else jnp.zeros((128, 128), BF16)
                t_ref[tp * 128:(tp + 1) * 128, t * 128:(t + 1) * 128] = blk
        ch_bits = int(math.log2(128 // SSM_LANE_GROUPS))
        row = lax.broadcasted_iota(I32, (chunk * 128, 128), 0)
        row_group = lax.shift_right_logical(row, ch_bits) & (SSM_LANE_GROUPS - 1)
        for part in range(2 * n_x // SSM_LANE_GROUPS):
            src = cin_ref[:, part * 128:(part + 1) * 128]
            for g in range(SSM_LANE_GROUPS):
                col = (part * SSM_LANE_GROUPS + g) * 128
                sin_ref[:, col:col + 128] = jnp.where(row_group == g, src, jnp.zeros_like(src))

    pitch = w_scr.shape[1] // n_batch

    u = jnp.concatenate([r[...] for r in u_refs], axis=1)
    for jb in range(n_x):
        w = _dot(u, sin_ref[:, jb * 256:(jb + 1) * 256])
        for b in range(n_batch):
            seq = slice(b * n_chunks, (b + 1) * n_chunks)
            w_scr[2 * jb, b * pitch:b * pitch + n_chunks, :] = w[seq, :128]
            w_scr[2 * jb + 1, b * pitch:b * pitch + n_chunks, :] = w[seq, 128:]

    ar = [jnp.broadcast_to(coef_ref[0:1, j * 128:(j + 1) * 128], (n_batch, 128)) for j in range(n_x)]
    ai_a = [jnp.broadcast_to(coef_ref[1:2, j * 128:(j + 1) * 128], (n_batch, 128)) for j in range(n_x)]
    ai_b = [jnp.broadcast_to(coef_ref[2:3, j * 128:(j + 1) * 128], (n_batch, 128)) for j in range(n_x)]

    def step(k, carry):
        x, xsw = carry
        rows_k = pl.ds(k, n_batch, stride=pitch)
        new_x, new_xsw = [], []
        for j in range(n_x):
            x_scr[j, rows_k, :] = x[j]
            new_x.append(ar[j] * x[j] + ai_a[j] * xsw[j] + w_scr[j, rows_k, :])
            new_xsw.append(ar[j] * xsw[j] + ai_b[j] * x[j] + w_scr[n_x + j, rows_k, :])
        return tuple(new_x), tuple(new_xsw)

    zero = tuple(jnp.zeros((n_batch, 128), F32) for _ in range(n_x))
    lax.fori_loop(0, n_chunks, step, (zero, zero))

    xs = jnp.concatenate(
        [jnp.concatenate([x_scr[j, b * pitch:b * pitch + n_chunks, :] for b in range(n_batch)], axis=0)
         for j in range(n_x)], axis=1).astype(BF16)
    for jb in range(chunk // 2):
        kk = (jb + 1) * 256
        cols = slice(jb * 256, (jb + 1) * 256)
        y = jax.nn.gelu(_dot(u[:, :kk], t_ref[:kk, cols]) + _dot(xs, sout_ref[:, cols]))
        for half in range(2):
            ytok_scr[pl.ds(2 * jb + half, rows, stride=chunk), :] = y[:, half * 128:(half + 1) * 128]
    y_ref[...] = ytok_scr[...].astype(y_ref.dtype)


def _ssm_operators(a_re, a_im, log_dt, b_re, b_im, c_re, c_im, d_skip):
    hp = lax.Precision.HIGHEST
    n_grp, n_state, n_ch = b_re.shape
    L = SSM_CHUNK
    nb = SSM_LANE_GROUPS
    assert n_ch * nb == 128 and 2 * n_state == 128 and n_grp % nb == 0
    n_gb = n_grp // nb
    a = lax.complex(a_re.astype(F32), a_im.astype(F32))
    lam = a * jnp.exp(log_dt.astype(F32))[:, None]
    a_bar = jnp.exp(lam)
    b_bar = ((a_bar - 1.0) / a)[..., None] * lax.complex(b_re.astype(F32), b_im.astype(F32))
    cc = lax.complex(c_re.astype(F32), c_im.astype(F32))
    steps = jnp.arange(L + 1, dtype=F32)
    apow = jnp.exp(steps[:, None, None] * lam[None])
    lane = jnp.arange(128)
    put = ((lane[None, None, :] // n_ch == jnp.arange(nb)[:, None, None])
           & (lane[None, None, :] % n_ch == jnp.arange(n_ch)[None, :, None])).astype(F32)

    kern = jnp.real(jnp.einsum('gcp,lgp,gpd->glcd', cc, apow[:L], b_bar, precision=hp))
    d_op = jnp.einsum('Bgtcd,gdr,gcl->Btrl', kern.reshape(n_gb, nb, L, n_ch, n_ch), put, put, precision=hp)
    skip = jnp.eye(128, dtype=F32)[None] * d_skip.astype(F32).reshape(n_gb, 1, 128)
    d_op = d_op.at[:, 0].add(skip).astype(BF16)

    s_in = apow[:L][::-1].transpose(1, 0, 2)[:, :, None, :] * b_bar.transpose(0, 2, 1)[:, None, :, :]
    s_in = jnp.concatenate([s_in.real, s_in.imag, s_in.imag, s_in.real], axis=-1).astype(BF16)
    c_in = s_in.reshape(n_gb, nb, L, n_ch, 4 * n_state).transpose(0, 2, 1, 3, 4).reshape(n_gb, L * 128, 4 * n_state)

    s_out = cc.transpose(0, 2, 1)[:, :, None, :] * apow[1:].transpose(1, 2, 0)[:, :, :, None]
    s_out = jnp.concatenate([s_out.real, -s_out.imag], axis=1)
    s_out = lax.reduce_precision(s_out, exponent_bits=8, mantissa_bits=7)
    x_idx = jnp.arange(L * n_ch)
    l_idx = jnp.arange(L * 128)
    place = ((x_idx[:, None] // n_ch == l_idx[None, :] // 128) & (x_idx[:, None] % n_ch == l_idx[None, :] % n_ch))
    place = place[None] & ((l_idx[None, None, :] // n_ch) % nb == jnp.arange(nb)[:, None, None])
    s_out = jnp.einsum('Bgqx,gxl->Bgql', s_out.reshape(n_gb, nb, 2 * n_state, L * n_ch), place.astype(F32))
    s_out = s_out.astype(BF16).reshape(n_gb, nb * 2 * n_state, L * 128)

    a_l = apow[L]
    coef = jnp.stack([jnp.concatenate([a_l.real, a_l.real], -1),
                      jnp.concatenate([-a_l.imag, a_l.imag], -1),
                      jnp.concatenate([a_l.imag, -a_l.imag], -1)], axis=1)
    coef = coef.reshape(n_gb, nb, 3, 2 * n_state).transpose(0, 2, 1, 3).reshape(n_gb, 3, nb * 2 * n_state)
    coef = jnp.pad(coef, ((0, 0), (0, 5), (0, 0)))
    return d_op, c_in, s_out, coef


def _ssm_scan(u, d_op, c_in, s_out, coef, *, n_batch, chunk):
    rows_all, lanes_all = u.shape
    n_blk, lc, _ = c_in.shape
    n_chunks = rows_all // n_batch
    nb = min(SSM_BATCH_PER_STEP, n_batch)
    assert n_batch % nb == 0 and lanes_all == chunk * n_blk * 128
    rows = nb * n_chunks
    pitch = n_chunks + 4 if n_chunks % 8 == 0 else n_chunks
    n_x = s_out.shape[1] // 128
    kern = functools.partial(_ssm_kernel, chunk=chunk, n_batch=nb, n_chunks=n_chunks)
    u_specs = [pl.BlockSpec((rows, 128), lambda g, b, t=t: (b, t * n_blk + g)) for t in range(chunk)]

    def op_spec(shape):
        return pl.BlockSpec((None,) + shape, lambda g, b: (g,) + (0,) * len(shape))

    return pl.pallas_call(
        kern,
        grid=(n_blk, n_batch // nb),
        in_specs=u_specs + [op_spec((chunk, 128, 128)), op_spec((lc, c_in.shape[2])),
                            op_spec((n_x * 128, lc)), op_spec((8, n_x * 128))],
        out_specs=pl.BlockSpec((rows * chunk, 128), lambda g, b: (b, g)),
        out_shape=jax.ShapeDtypeStruct((rows_all * chunk, n_blk * 128), BF16),
        scratch_shapes=[pltpu.VMEM((lc, lc), BF16), pltpu.VMEM((lc, 2 * n_x * 128), BF16),
                        pltpu.VMEM((2 * n_x, nb * pitch, 128), F32), pltpu.VMEM((n_x, nb * pitch, 128), F32),
                        pltpu.VMEM((rows * chunk, 128), F32)],
        compiler_params=pltpu.CompilerParams(dimension_semantics=("arbitrary", "arbitrary"),
                                             vmem_limit_bytes=V7X_VMEM_LIMIT_BYTES),
        name="ssm_scan",
    )(*([u] * chunk), d_op, c_in, s_out, coef)


def _ssm_out_kernel(y_ref, h_ref, wout_ref, l1g_ref, l1b_ref, rw_ref, rb_ref,
                    h1_ref, h1p_ref, idx_ref, gate_ref, rank_ref, cnt_ref, res_scr, *, alpha):
    @pl.when(pl.program_id(0) == 0)
    def _():
        cnt_ref[...] = jnp.zeros_like(cnt_ref)
        res_scr[...] = jnp.zeros_like(res_scr)

    h_split = _tail_norm(res_scr, l1g_ref, l1b_ref, h1_ref, h1p_ref)

    y = y_ref[...]
    d = h_ref.shape[1]
    val = _dot(y, wout_ref[:, :d])
    gate = _dot(y, wout_ref[:, d:])
    res_scr[...] = alpha * h_ref[...] + val * jax.nn.sigmoid(gate)
    _tail_route(h_split, rw_ref, rb_ref, idx_ref, gate_ref, rank_ref, cnt_ref)


def _ssm_layer(h, n_batch, w_in, a_re, a_im, log_dt, b_re, b_im, c_re, c_im, d_skip, w_out,
               l1g, l1b, rw2, rb, *, alpha):
    n_tok, d = h.shape
    n_grp, n_state, n_ch = b_re.shape
    width = n_grp * n_ch
    n_exp = rb.shape[0]
    L = SSM_CHUNK
    assert (n_tok // n_batch) % L == 0

    u = _ssm_in_proj(h, w_in, chunk=L)
    d_op, c_in, s_out, coef = _ssm_operators(a_re, a_im, log_dt, b_re, b_im, c_re, c_im, d_skip)
    y = _ssm_scan(u, d_op, c_in, s_out, coef, n_batch=n_batch, chunk=L)

    tm = min(TM_SSM_OUT, n_tok)
    assert n_tok % tm == 0
    r_shapes, r_specs = _router_out(n_tok, tm, n_exp)
    kern = functools.partial(_ssm_out_kernel, alpha=alpha)
    n_steps = n_tok // tm
    return pl.pallas_call(
        kern,
        grid=(n_steps + 1,),
        in_specs=[pl.BlockSpec((tm, width), lambda i: (jnp.minimum(i, n_steps - 1), 0)),
                  pl.BlockSpec((tm, d), lambda i: (jnp.minimum(i, n_steps - 1), 0)),
                  _const_spec((width, 2 * d)), _const_spec((1, d)), _const_spec((1, d)),
                  _const_spec((d, 256)), _const_spec((n_exp, 1))],
        out_specs=[pl.BlockSpec((tm, d), lambda i: (_prev(i), 0)),
                   pl.BlockSpec((tm * RECORD_ROWS, 128), lambda i: (_prev(i), 0))] + r_specs,
        out_shape=[jax.ShapeDtypeStruct((n_tok, d), F32),
                   jax.ShapeDtypeStruct((n_tok * RECORD_ROWS, 128), U32)] + r_shapes,
        scratch_shapes=[pltpu.VMEM((tm, d), F32)],
        compiler_params=pltpu.CompilerParams(dimension_semantics=("arbitrary",),
                                             vmem_limit_bytes=V7X_VMEM_LIMIT_BYTES),
        name="ssm_out_mixer",
    )(y, h, w_out.astype(BF16), l1g.reshape(1, -1), l1b.reshape(1, -1), rw2, rb)


def _moe_plan(idx, rank, cnt, tme, n_tiles):
    n_exp = cnt.shape[0]
    counts = cnt[:, 0].astype(I32)
    padded = ((counts + tme - 1) // tme) * tme
    ends = jnp.cumsum(padded)
    offs = ends - padded
    n_valid = ends[-1] // tme
    tile_start = jnp.arange(n_tiles, dtype=I32) * tme
    tile_expert = jnp.minimum(jnp.sum(tile_start[:, None] >= ends[None, :], axis=1), n_exp - 1).astype(I32)
    onehot = idx[:, :, None] == jnp.arange(n_exp, dtype=I32)[None, None, :]
    pos = jnp.sum(jnp.where(onehot, offs[None, None, :], 0), axis=-1) + rank
    meta = jnp.concatenate([counts, padded, offs, n_valid[None]]).astype(I32)
    return pos, tile_expert, meta


def _tile_pos(pos, tm):
    n_tok = pos.shape[1]
    return pos.reshape(2, n_tok // tm, tm).transpose(1, 0, 2).reshape(n_tok // tm, 1, 2 * tm)


def _record_slice(ref, row, n_rec=1):
    return ref.at[pl.ds(pl.multiple_of(row * RECORD_ROWS, RECORD_ROWS), n_rec * RECORD_ROWS), :]


def _dispatch_kernel(meta_ref, pos_ref, x_ref, xs_ref, zero_scr, sem, *, n_exp, tme, n_tiles):
    i = pl.program_id(0)
    tm = x_ref.shape[0] // RECORD_ROWS
    half = zero_scr.shape[0] // RECORD_ROWS

    def zero_fill(start, size):
        cp = pltpu.make_async_copy(_record_slice(zero_scr, 0, size), _record_slice(xs_ref, start, size), sem)
        cp.start()
        cp.wait()

    @pl.when(i == 0)
    def _():
        zero_scr[...] = jnp.zeros_like(zero_scr)
        for e in range(n_exp):
            pad = meta_ref[n_exp + e] - meta_ref[e]
            base = meta_ref[2 * n_exp + e] + meta_ref[e]
            for b in range(int(math.log2(tme))):
                size = 1 << b

                @pl.when(((pad >> b) & 1) == 1)
                def _():
                    zero_fill(base + (pad & (size - 1)), size)
        n_valid = meta_ref[3 * n_exp]
        for j in range(n_exp):
            @pl.when(n_valid + j < n_tiles)
            def _():
                for part in range(tme // half):
                    zero_fill((n_valid + j) * tme + part * half, half)

    def start(r, c):
        for slot in range(2):
            p = pos_ref[0, 0, slot * tm + r]
            pltpu.make_async_copy(_record_slice(x_ref, r), _record_slice(xs_ref, p), sem).start(priority=slot)
        return c

    lax.fori_loop(0, tm, start, 0, unroll=8)
    for slot in range(2):
        pltpu.make_async_copy(x_ref, _record_slice(xs_ref, 0, tm), sem).wait()


def _cast_specs(w_in, w_out, layer, n_steps, index):
    n_layers, n_exp, d, ff2 = w_in.shape
    ff = w_out.shape[2]
    assert (n_exp * d) % n_steps == 0 and (n_exp * ff) % n_steps == 0
    rin, rout = n_exp * d // n_steps, n_exp * ff // n_steps
    in_specs = [pl.BlockSpec((None, rin, ff2), lambda i, *_: (layer, index(i), 0)),
                pl.BlockSpec((None, rout, d), lambda i, *_: (layer, index(i), 0))]
    out_specs = [pl.BlockSpec((rin, ff2), lambda i, *_: (index(i), 0)),
                 pl.BlockSpec((rout, d), lambda i, *_: (index(i), 0))]
    out_shape = [jax.ShapeDtypeStruct((n_exp * d, ff2), BF16), jax.ShapeDtypeStruct((n_exp * ff, d), BF16)]
    operands = (w_in.reshape(n_layers, n_exp * d, ff2), w_out.reshape(n_layers, n_exp * ff, d))
    return in_specs, out_specs, out_shape, operands


def _moe_dispatch(h1p, pos, meta, *, n_exp, tme, n_tiles):
    n_tok = h1p.shape[0] // RECORD_ROWS
    tm = min(TM_DISPATCH, n_tok)
    assert n_tok % tm == 0
    half = max(tme // 2, 1)
    kern = functools.partial(_dispatch_kernel, n_exp=n_exp, tme=tme, n_tiles=n_tiles)
    return pl.pallas_call(
        kern,
        grid_spec=pltpu.PrefetchScalarGridSpec(
            num_scalar_prefetch=1,
            grid=(n_tok // tm,),
            in_specs=[pl.BlockSpec((1, 1, 2 * tm), lambda i, m: (i, 0, 0), memory_space=pltpu.SMEM),
                      pl.BlockSpec((tm * RECORD_ROWS, 128), lambda i, m: (i, 0))],
            out_specs=pl.BlockSpec(memory_space=pl.ANY),
            scratch_shapes=[pltpu.VMEM((half * RECORD_ROWS, 128), U32), pltpu.SemaphoreType.DMA]),
        out_shape=jax.ShapeDtypeStruct((n_tiles * tme * RECORD_ROWS, 128), U32),
        compiler_params=pltpu.CompilerParams(dimension_semantics=("arbitrary",),
                                             vmem_limit_bytes=V7X_VMEM_LIMIT_BYTES),
        name="moe_dispatch",
    )(meta, _tile_pos(pos, tm), h1p)


def _expert_kernel(te_ref, meta_ref, xs_ref, win_ref, wout_ref, *refs, n_exp, cast):
    if cast:
        nwin_ref, nwout_ref, ys_ref, nwin_b_ref, nwout_b_ref = refs
        nwin_b_ref[...] = nwin_ref[...].astype(BF16)
        nwout_b_ref[...] = nwout_ref[...].astype(BF16)
    else:
        (ys_ref,) = refs
    i = pl.program_id(0)
    n_valid = meta_ref[3 * n_exp]
    tme = xs_ref.shape[0] // RECORD_ROWS

    @pl.when(i < n_valid)
    def _():
        hi, lo = _load_records(xs_ref, tme)
        dh = hi.shape[1]
        ff = wout_ref.shape[0]
        h = _dot(hi.astype(BF16), win_ref[:dh, :]) + _dot(lo.astype(BF16), win_ref[dh:, :])
        act = (jax.nn.silu(h[:, :ff]) * h[:, ff:]).astype(BF16)
        _store_records(ys_ref, _dot(act, wout_ref[...]))

    @pl.when(i >= n_valid)
    def _():
        ys_ref[...] = jnp.zeros_like(ys_ref)


def _moe_experts(xs, tile_expert, meta, w_in, w_out, cast_weights, *, tme):
    n_rows = xs.shape[0] // RECORD_ROWS
    n_exp, d, ff2 = w_in.shape
    ff = w_out.shape[1]
    n_tiles = n_rows // tme

    def x_map(i, te, m):
        return (jnp.minimum(i, m[3 * n_exp] - 1), 0)

    cast = cast_weights is not None
    c_in, c_out, c_shape, c_ops = ([], [], [], ())
    if cast:
        n_cast = 1 << (n_tiles.bit_length() - 1)
        c_in, c_out, c_shape, c_ops = _cast_specs(*cast_weights, n_cast, lambda i: jnp.minimum(i, n_cast - 1))
    kern = functools.partial(_expert_kernel, n_exp=n_exp, cast=cast)
    return pl.pallas_call(
        kern,
        grid_spec=pltpu.PrefetchScalarGridSpec(
            num_scalar_prefetch=2,
            grid=(n_tiles,),
            in_specs=[pl.BlockSpec((tme * RECORD_ROWS, 128), x_map),
                      pl.BlockSpec((None, d, ff2), lambda i, te, m: (te[i], 0, 0)),
                      pl.BlockSpec((None, ff, d), lambda i, te, m: (te[i], 0, 0))] + c_in,
            out_specs=[pl.BlockSpec((tme * RECORD_ROWS, 128), lambda i, te, m: (i, 0))] + c_out),
        out_shape=[jax.ShapeDtypeStruct((n_rows * RECORD_ROWS, 128), U32)] + c_shape,
        compiler_params=pltpu.CompilerParams(dimension_semantics=("arbitrary",),
                                             vmem_limit_bytes=V7X_VMEM_LIMIT_BYTES),
        name="moe_experts",
    )(tile_expert, meta, xs, w_in, w_out, *c_ops)


def _combine_kernel(pos_ref, pos_next_ref, h1_ref, gate_ref, ys_ref, g_ref, b_ref, out_ref, buf, sems, *, alpha):
    i = pl.program_id(0)
    n_steps = pl.num_programs(0)
    tm = h1_ref.shape[0]

    def gather(p_ref, half):
        def start(r, c):
            for slot in range(2):
                p = p_ref[0, 0, slot * tm + r]
                pltpu.make_async_copy(_record_slice(ys_ref, p), _record_slice(buf.at[half, slot], r),
                                      sems.at[half]).start(priority=slot)
            return c

        lax.fori_loop(0, tm, start, 0, unroll=8)

    @pl.when(i == 0)
    def _():
        gather(pos_ref, 0)

    @pl.when(i + 1 < n_steps)
    def _():
        gather(pos_next_ref, (i + 1) % 2)

    cur = i % 2
    moe = None
    for slot in range(2):
        pltpu.make_async_copy(_record_slice(ys_ref, 0, tm), buf.at[cur, slot], sems.at[cur]).wait()
    for slot in range(2):
        hi, lo = _load_records(buf.at[cur, slot], tm)
        y = jnp.concatenate([hi, lo], axis=1) * gate_ref[:, slot:slot + 1]
        moe = y if moe is None else moe + y
    out_ref[...] = _layer_norm(alpha * h1_ref[...] + moe, g_ref[...], b_ref[...])


def _moe_combine(h1, ys, pos, gates, l2g, l2b, *, alpha):
    n_tok, d = h1.shape
    tm = min(TM_COMBINE, n_tok)
    assert n_tok % tm == 0
    n_steps = n_tok // tm
    kern = functools.partial(_combine_kernel, alpha=alpha)
    pos_tiles = _tile_pos(pos, tm)
    return pl.pallas_call(
        kern,
        grid=(n_steps,),
        in_specs=[pl.BlockSpec((1, 1, 2 * tm), lambda i: (i, 0, 0), memory_space=pltpu.SMEM),
                  pl.BlockSpec((1, 1, 2 * tm), lambda i: (jnp.minimum(i + 1, n_steps - 1), 0, 0),
                               memory_space=pltpu.SMEM),
                  pl.BlockSpec((tm, d), lambda i: (i, 0)),
                  pl.BlockSpec((tm, 2), lambda i: (i, 0)),
                  pl.BlockSpec(memory_space=pl.ANY),
                  _const_spec((1, d)), _const_spec((1, d))],
        out_specs=pl.BlockSpec((tm, d), lambda i: (i, 0)),
        out_shape=jax.ShapeDtypeStruct((n_tok, d), F32),
        scratch_shapes=[pltpu.VMEM((2, 2, tm * RECORD_ROWS, 128), U32), pltpu.SemaphoreType.DMA((2,))],
        compiler_params=pltpu.CompilerParams(dimension_semantics=("arbitrary",),
                                             vmem_limit_bytes=V7X_VMEM_LIMIT_BYTES),
        name="moe_combine",
    )(pos_tiles, pos_tiles, h1, gates.T, ys, l2g.reshape(1, -1), l2b.reshape(1, -1))


def _moe_layer(h1, h1p, idx, gates, rank, cnt, w_in_b, w_out_b, cast_next, l2g, l2b, *, alpha):
    n_tok, d = h1.shape
    n_exp = cnt.shape[0]
    ff = w_out_b.shape[0] // n_exp
    tme = min(TM_EXPERT, n_tok)
    n_tiles = (2 * n_tok) // tme + n_exp
    pos, tile_expert, meta = _moe_plan(idx, rank, cnt, tme, n_tiles)
    xs = _moe_dispatch(h1p, pos, meta, n_exp=n_exp, tme=tme, n_tiles=n_tiles)
    ys, *cast = _moe_experts(xs, tile_expert, meta, w_in_b.reshape(n_exp, d, 2 * ff), w_out_b.reshape(n_exp, ff, d),
                             cast_next, tme=tme)
    return _moe_combine(h1, ys, pos, gates, l2g, l2b, alpha=alpha), cast


def kernel(x, a_w_in, a_b_in, a_ln_g, a_ln_b, a_w_s, a_b_s, a_w_out, b_w_in, b_a_re, b_a_im, b_log_dt, b_b_re, b_b_im, b_c_re, b_c_im, b_d, b_w_out, ln1_g, ln1_b, router_w, router_b, e_w_in, e_w_out, ln2_g, ln2_b):
    n_batch, seq, d = x.shape
    depth = ln1_g.shape[0]
    alpha = float((2 * depth) ** 0.25)
    n_exp = router_w.shape[1]
    rw = jnp.pad(router_w.astype(F32), ((0, 0), (0, 128 - n_exp)))
    rw_hi = lax.reduce_precision(rw, exponent_bits=8, mantissa_bits=7)
    rw2 = jnp.concatenate([rw_hi, rw - rw_hi], axis=1).astype(BF16)
    rb = router_b.astype(F32).reshape(-1, 1)
    h = x.reshape(n_batch * seq, d)
    e_bf16 = None
    for i in range(depth):
        j = i // 2
        if i % 2 == 0:
            *mixed, e_in_b, e_out_b = _gmlp_layer(h, a_w_in[j], a_b_in[j], a_ln_g[j], a_ln_b[j], a_w_s[j], a_b_s[j],
                                                  a_w_out[j], ln1_g[i], ln1_b[i], rw2, rb, (e_w_in, e_w_out, i),
                                                  alpha=alpha)
            e_bf16 = [e_in_b, e_out_b]
        else:
            mixed = _ssm_layer(h, n_batch, b_w_in[j], b_a_re[j], b_a_im[j], b_log_dt[j], b_b_re[j],
                               b_b_im[j], b_c_re[j], b_c_im[j], b_d[j], b_w_out[j],
                               ln1_g[i], ln1_b[i], rw2, rb, alpha=alpha)
        cast_next = (e_w_in, e_w_out, i + 1) if (i % 2 == 0 and i + 1 < depth) else None
        h, e_bf16 = _moe_layer(*mixed, *e_bf16, cast_next, ln2_g[i], ln2_b[i], alpha=alpha)
    return h.reshape(n_batch, seq, d)
```
